```python
import math
import jax, jax.numpy as jnp
from jax import lax
import numpy as np

D_MODEL = 1024
BATCH = 8
SEQ = 2048
DEPTH = 2
DEC_BATCH = 128
DEC_SEQ = 8
PAST_LEN = 16384
PAGE_SIZE = 128

HEAD_DIM = 64
N_HEADS_A = (D_MODEL // 2) // HEAD_DIM
N_KV_A = max(1, N_HEADS_A // 4)
GQA_GROUP = N_HEADS_A // N_KV_A
WINDOW = 128
BLOCK = WINDOW
Q_COLS = N_HEADS_A * HEAD_DIM
KV_COLS = N_KV_A * HEAD_DIM
D_A_IN = Q_COLS + 2 * KV_COLS
HEAD_DIM_B = 64
N_HEADS_B = (D_MODEL // 2) // HEAD_DIM_B
D_B = N_HEADS_B * HEAD_DIM_B
D_LORA_W = 64
D_LORA_A = 64
D_LORA_G = 128
D_B_IN = 3 * D_B + D_LORA_W + D_LORA_A + D_LORA_G
D_IN0 = D_A_IN + D_B_IN
RWKV_GN_EPS = 64e-5
S5_CH = 16
S5_GROUPS = D_MODEL // S5_CH
S5_P = 64
N_EGROUPS = 4
EXP_PER_GROUP = 4
N_EXPERTS = N_EGROUPS * EXP_PER_GROUP
TOP_K_FINE = 2
D_FF_E = 512
RMS_EPS = 1e-5
NEG_BIG = -1e30
F32 = jnp.float32

kernel_name = 'hybrid_swa_rwkv7_s5_hmoe_step'


def rmsnorm(x, g):
    xf = x.astype(F32)
    y = xf * lax.rsqrt(jnp.mean(xf * xf, axis=-1, keepdims=True) + RMS_EPS)
    return (y * g.astype(F32)).astype(x.dtype)


def alibi_slopes():
    h = jnp.arange(1, N_HEADS_A + 1, dtype=F32)
    return jnp.exp2(-8.0 * h / N_HEADS_A)


def sink_attention(q, k, v, q_pos, k_pos, sinks):
    s = jnp.einsum('...qngd,...knd->...ngqk', q.astype(F32), k.astype(F32)) * (HEAD_DIM ** -0.5)
    dist = q_pos[..., :, None] - k_pos[..., None, :]
    valid = (dist >= 0) & (dist < WINDOW) & (k_pos >= 0)[..., None, :]
    slopes = alibi_slopes().reshape(N_KV_A, GQA_GROUP, 1, 1)
    s = s - slopes * dist[..., None, None, :, :].astype(F32)
    s = jnp.where(valid[..., None, None, :, :], s, NEG_BIG)
    sink = jnp.broadcast_to(sinks.astype(F32).reshape(N_KV_A, GQA_GROUP, 1, 1), s.shape[:-1] + (1,))
    p = jax.nn.softmax(jnp.concatenate([s, sink], axis=-1), axis=-1)[..., :-1]
    o = jnp.einsum('...ngqk,...knd->...qngd', p, v.astype(F32))
    return o.astype(v.dtype)


def window_attention_prompt(q, k, v, sinks):
    b, s = q.shape[:2]
    nb = s // BLOCK
    qb = q.reshape(b, nb, BLOCK, N_KV_A, GQA_GROUP, HEAD_DIM)

    def band(t):
        pad = jnp.zeros((b, BLOCK) + t.shape[2:], t.dtype)
        tb = jnp.concatenate([pad, t], axis=1).reshape(b, nb + 1, BLOCK, N_KV_A, HEAD_DIM)
        return jnp.concatenate([tb[:, :-1], tb[:, 1:]], axis=2)

    q_pos = jnp.arange(s, dtype=jnp.int32).reshape(nb, BLOCK)
    k_pos = (jnp.arange(nb, dtype=jnp.int32)[:, None] - 1) * BLOCK + jnp.arange(2 * BLOCK, dtype=jnp.int32)[None, :]
    o = sink_attention(qb, band(k), band(v), q_pos, k_pos, sinks)
    return o.reshape(b, s, Q_COLS)


def window_attention_sample(q, k, v, cache_k, cache_v, sinks):
    bsz, t = q.shape[:2]
    w = cache_k.shape[1]
    kc = jnp.concatenate([cache_k.astype(k.dtype), k], axis=1)
    vc = jnp.concatenate([cache_v.astype(v.dtype), v], axis=1)
    q_pos = PAST_LEN + jnp.arange(t, dtype=jnp.int32)
    k_pos = PAST_LEN - w + jnp.arange(w + t, dtype=jnp.int32)
    o = sink_attention(q.reshape(bsz, t, N_KV_A, GQA_GROUP, HEAD_DIM), kc, vc, q_pos, k_pos, sinks)
    return o.reshape(bsz, t, Q_COLS), kc[:, t:], vc[:, t:]


def wkv_scan(r, w, k, v, kk, a, s0):
    def step(s, inp):
        r_t, w_t, k_t, v_t, kk_t, a_t = inp
        sa = jnp.einsum('bhvk,bhk->bhv', s, -kk_t)
        s = s * w_t[:, :, None, :] + sa[..., None] * (kk_t * a_t)[:, :, None, :] + v_t[..., None] * k_t[:, :, None, :]
        return s, jnp.einsum('bhvk,bhk->bhv', s, r_t)

    xs = tuple(jnp.moveaxis(z, 1, 0) for z in (r, w, k, v, kk, a))
    s_fin, o = lax.scan(step, s0, xs)
    return jnp.moveaxis(o, 0, 1), s_fin


def rwkv7_mixer(pb, pb_prev, s0, mu, w0, w_up, a0, a_up, g_up, k_k, k_a, r_k, ln_w, ln_b):
    bsz, t = pb.shape[:2]
    xs = pb + (pb_prev - pb) * mu.astype(pb.dtype)
    splits = [D_B, 2 * D_B, 3 * D_B, 3 * D_B + D_LORA_W, 3 * D_B + D_LORA_W + D_LORA_A]
    r, k, v, wd, ad, gd = jnp.split(xs, splits, axis=-1)
    w = -jax.nn.softplus(-(w0 + jnp.tanh(wd) @ w_up).astype(F32)) - 0.5
    decay = jnp.exp(-jnp.exp(w))
    a = jax.nn.sigmoid((a0 + ad @ a_up).astype(F32))
    g = (jax.nn.sigmoid(gd) @ g_up).astype(F32)

    def heads(z):
        return z.astype(F32).reshape(bsz, t, N_HEADS_B, HEAD_DIM_B)

    def hvec(z):
        return z.astype(F32).reshape(N_HEADS_B, HEAD_DIM_B)

    r, k, v, a, decay = heads(r), heads(k), heads(v), heads(a), heads(decay)
    kk = k * hvec(k_k)
    kk = kk * lax.rsqrt(jnp.maximum(jnp.sum(kk * kk, axis=-1, keepdims=True), 1e-24))
    k = k * (1.0 + (a - 1.0) * hvec(k_a))
    o, s_fin = wkv_scan(r, decay, k, v, kk, a, s0.astype(F32))
    mean = jnp.mean(o, axis=-1, keepdims=True)
    var = jnp.mean(jnp.square(o - mean), axis=-1, keepdims=True)
    o = (o - mean) * lax.rsqrt(var + RWKV_GN_EPS) * hvec(ln_w) + hvec(ln_b)
    o = o + jnp.sum(r * k * hvec(r_k), axis=-1, keepdims=True) * v
    o = o.reshape(bsz, t, D_B) * g
    return o.astype(pb.dtype), s_fin


def mixer_ab(x, cache_k, cache_v, shift_prev, wkv0, win_buf, norm_mix, w_in, sinks, rw, w_out):
    bsz, t = x.shape[:2]
    proj = rmsnorm(x, norm_mix) @ w_in
    q = proj[..., :Q_COLS].reshape(bsz, t, N_HEADS_A, HEAD_DIM)
    k = proj[..., Q_COLS:Q_COLS + KV_COLS].reshape(bsz, t, N_KV_A, HEAD_DIM)
    v = proj[..., Q_COLS + KV_COLS:D_A_IN].reshape(bsz, t, N_KV_A, HEAD_DIM)
    pb = proj[..., D_A_IN:]
    if cache_k is None:
        attn = window_attention_prompt(q, k, v, sinks)
        new_k, new_v = k[:, -win_buf:], v[:, -win_buf:]
        prev0 = jnp.zeros_like(pb[:, :1])
    else:
        attn, new_k, new_v = window_attention_sample(q, k, v, cache_k, cache_v, sinks)
        prev0 = shift_prev[:, None].astype(pb.dtype)
    pb_prev = jnp.concatenate([prev0, pb[:, :-1]], axis=1)
    rout, wkv_new = rwkv7_mixer(pb, pb_prev, wkv0, *rw)
    x = x + jnp.concatenate([attn, rout], axis=-1) @ w_out
    return x, new_k, new_v, pb[:, -1], wkv_new.astype(x.dtype)


def complex_affine_combine(e1, e2):
    a1r, a1i, b1r, b1i = e1
    a2r, a2i, b2r, b2i = e2
    return (a2r * a1r - a2i * a1i, a2r * a1i + a2i * a1r,
            a2r * b1r - a2i * b1i + b2r, a2r * b1i + a2i * b1r + b2i)


def s5_mixer(u, h0_re, h0_im, a_re, a_im, log_dt, b_re, b_im, c_re, c_im, d_skip):
    bsz, t = u.shape[:2]
    a_re, a_im = a_re.astype(F32), a_im.astype(F32)
    dt = jnp.exp(log_dt.astype(F32))
    mag = jnp.exp(dt * a_re)
    ab_re, ab_im = mag * jnp.cos(dt * a_im), mag * jnp.sin(dt * a_im)
    den = a_re * a_re + a_im * a_im
    f_re = ((ab_re - 1.0) * a_re + ab_im * a_im) / den
    f_im = (ab_im * a_re - (ab_re - 1.0) * a_im) / den
    br, bi = b_re.astype(F32), b_im.astype(F32)
    bb_re = f_re[..., None] * br - f_im[..., None] * bi
    bb_im = f_re[..., None] * bi + f_im[..., None] * br
    ug = u.astype(F32).reshape(bsz, t, S5_GROUPS, S5_CH)
    bu_re = jnp.einsum('gpc,btgc->btgp', bb_re, ug)
    bu_im = jnp.einsum('gpc,btgc->btgp', bb_im, ug)
    h0r, h0i = h0_re.astype(F32), h0_im.astype(F32)
    bu_re = bu_re.at[:, 0].add(ab_re * h0r - ab_im * h0i)
    bu_im = bu_im.at[:, 0].add(ab_re * h0i + ab_im * h0r)
    ar = jnp.broadcast_to(ab_re, bu_re.shape)
    ai = jnp.broadcast_to(ab_im, bu_im.shape)
    _, _, h_re, h_im = lax.associative_scan(complex_affine_combine, (ar, ai, bu_re, bu_im), axis=1)
    y = (jnp.einsum('gcp,btgp->btgc', c_re.astype(F32), h_re)
         - jnp.einsum('gcp,btgp->btgc', c_im.astype(F32), h_im)
         + d_skip.astype(F32).reshape(S5_GROUPS, S5_CH) * ug)
    return y.reshape(bsz, t, D_MODEL).astype(u.dtype), h_re[:, -1], h_im[:, -1]


def mixer_c(x, h0_re, h0_im, norm_mix, s5p, w_out):
    y, hr, hi = s5_mixer(rmsnorm(x, norm_mix), h0_re, h0_im, *s5p)
    z = jax.nn.gelu(y) @ w_out
    x = x + z[..., :D_MODEL] * jax.nn.sigmoid(z[..., D_MODEL:])
    return x, hr.astype(x.dtype), hi.astype(x.dtype)


def hier_moe(x, w_rc, b_rc, w_rf, b_rf, wg, wu, wd):
    shp = x.shape
    t = x.reshape(-1, D_MODEL)
    n = t.shape[0]
    lc = (t @ w_rc).astype(F32) + b_rc.astype(F32)
    pg, gi = lax.top_k(jax.nn.softmax(lc, axis=-1), 1)
    lf = jnp.einsum('td,dge->tge', t, w_rf).astype(F32) + b_rf.astype(F32)
    lf_sel = jnp.take_along_axis(lf, jnp.broadcast_to(gi[:, :, None], (n, 1, EXP_PER_GROUP)), axis=1)[:, 0]
    vf, ei = lax.top_k(lf_sel, TOP_K_FINE)
    weights = pg * jax.nn.softmax(vf, axis=-1)
    eidx = gi * EXP_PER_GROUP + ei
    gates = jnp.sum(jax.nn.one_hot(eidx, N_EXPERTS, dtype=F32) * weights[..., None], axis=1)
    y = jnp.zeros((n, D_MODEL), F32)
    for e in range(N_EXPERTS):
        hdn = jax.nn.silu(t @ wg[e]) * (t @ wu[e])
        y = y + gates[:, e:e + 1] * (hdn @ wd[e]).astype(F32)
    return y.astype(x.dtype).reshape(shp)


def setup_inputs(seed: int = 0) -> dict:
    key = jax.random.key(seed)
    keys = iter(jax.random.split(key, 64))

    def nrm(shape, scale):
        return jax.random.normal(next(keys), shape, F32) * scale

    def gain(n):
        return 1.0 + nrm((n,), 0.02)

    win_buf = min(WINDOW, PAST_LEN)
    inv_d = D_MODEL ** -0.5
    w0_base = jnp.tile(jnp.linspace(-6.5, -1.5, HEAD_DIM_B, dtype=F32), N_HEADS_B)
    a_im_base = jnp.pi * jnp.broadcast_to(jnp.arange(S5_P, dtype=F32), (S5_GROUPS, S5_P))

    def moe_params(prefix):
        return {
            prefix + 'norm_ffn': gain(D_MODEL),
            prefix + 'router_coarse': nrm((D_MODEL, N_EGROUPS), inv_d),
            prefix + 'bias_coarse': nrm((N_EGROUPS,), 0.01),
            prefix + 'router_fine': nrm((D_MODEL, N_EGROUPS, EXP_PER_GROUP), inv_d),
            prefix + 'bias_fine': nrm((N_EGROUPS, EXP_PER_GROUP), 0.01),
            prefix + 'exp_gate': nrm((N_EXPERTS, D_MODEL, D_FF_E), inv_d),
            prefix + 'exp_up': nrm((N_EXPERTS, D_MODEL, D_FF_E), inv_d),
            prefix + 'exp_down': nrm((N_EXPERTS, D_FF_E, D_MODEL), D_FF_E ** -0.5),
        }

    d = {
        'x_prompt': nrm((BATCH, SEQ, D_MODEL), 1.0),
        'x_sample': nrm((DEC_BATCH, DEC_SEQ, D_MODEL), 1.0),
        'cache_win_k': nrm((DEC_BATCH, win_buf, N_KV_A, HEAD_DIM), 1.0),
        'cache_win_v': nrm((DEC_BATCH, win_buf, N_KV_A, HEAD_DIM), 1.0),
        'state_shift': nrm((DEC_BATCH, D_B_IN), 1.0),
        'state_wkv': nrm((DEC_BATCH, N_HEADS_B, HEAD_DIM_B, HEAD_DIM_B), 0.3),
        'state_s5_re': nrm((DEC_BATCH, S5_GROUPS, S5_P), 0.1),
        'state_s5_im': nrm((DEC_BATCH, S5_GROUPS, S5_P), 0.1),
        'l0_norm_mix': gain(D_MODEL),
        'l0_w_in': nrm((D_MODEL, D_IN0), inv_d),
        'l0_sinks': nrm((N_HEADS_A,), 0.5),
        'l0_mu': jax.random.uniform(next(keys), (D_B_IN,), F32),
        'l0_w0': w0_base + nrm((D_B,), 0.1),
        'l0_w_lora_up': nrm((D_LORA_W, D_B), 0.1),
        'l0_a0': nrm((D_B,), 0.1),
        'l0_a_lora_up': nrm((D_LORA_A, D_B), 0.1),
        'l0_g_lora_up': nrm((D_LORA_G, D_B), D_LORA_G ** -0.5),
        'l0_k_k': 0.85 + nrm((D_B,), 0.02),
        'l0_k_a': 1.0 + nrm((D_B,), 0.02),
        'l0_r_k': nrm((D_B,), 0.1),
        'l0_ln_w': gain(D_B),
        'l0_ln_b': nrm((D_B,), 0.02),
        'l0_w_out': nrm((D_MODEL, D_MODEL), inv_d),
    }
    d.update(moe_params('l0_'))
    d.update({
        'l1_norm_mix': gain(D_MODEL),
        'l1_s5_a_re': -0.5 + nrm((S5_GROUPS, S5_P), 0.01),
        'l1_s5_a_im': a_im_base + nrm((S5_GROUPS, S5_P), 0.01),
        'l1_s5_log_dt': jax.random.uniform(next(keys), (S5_GROUPS, S5_P), F32, minval=math.log(1e-3), maxval=math.log(1e-1)),
        'l1_s5_b_re': nrm((S5_GROUPS, S5_P, S5_CH), (2 * S5_CH) ** -0.5),
        'l1_s5_b_im': nrm((S5_GROUPS, S5_P, S5_CH), (2 * S5_CH) ** -0.5),
        'l1_s5_c_re': nrm((S5_GROUPS, S5_CH, S5_P), S5_P ** -0.5),
        'l1_s5_c_im': nrm((S5_GROUPS, S5_CH, S5_P), S5_P ** -0.5),
        'l1_s5_d': nrm((D_MODEL,), 1.0),
        'l1_w_out': nrm((D_MODEL, 2 * D_MODEL), inv_d),
    })
    d.update(moe_params('l1_'))
    d['final_norm'] = gain(D_MODEL)
    return d


def reference(x_prompt, x_sample, cache_win_k, cache_win_v, state_shift, state_wkv, state_s5_re, state_s5_im,
              l0_norm_mix, l0_w_in, l0_sinks, l0_mu, l0_w0, l0_w_lora_up, l0_a0, l0_a_lora_up, l0_g_lora_up,
              l0_k_k, l0_k_a, l0_r_k, l0_ln_w, l0_ln_b, l0_w_out,
              l0_norm_ffn, l0_router_coarse, l0_bias_coarse, l0_router_fine, l0_bias_fine,
              l0_exp_gate, l0_exp_up, l0_exp_down,
              l1_norm_mix, l1_s5_a_re, l1_s5_a_im, l1_s5_log_dt, l1_s5_b_re, l1_s5_b_im, l1_s5_c_re, l1_s5_c_im,
              l1_s5_d, l1_w_out,
              l1_norm_ffn, l1_router_coarse, l1_bias_coarse, l1_router_fine, l1_bias_fine,
              l1_exp_gate, l1_exp_up, l1_exp_down,
              final_norm):
    win_buf = cache_win_k.shape[1]
    rw = (l0_mu, l0_w0, l0_w_lora_up, l0_a0, l0_a_lora_up, l0_g_lora_up, l0_k_k, l0_k_a, l0_r_k, l0_ln_w, l0_ln_b)
    s5p = (l1_s5_a_re, l1_s5_a_im, l1_s5_log_dt, l1_s5_b_re, l1_s5_b_im, l1_s5_c_re, l1_s5_c_im, l1_s5_d)
    layer_mix = [(l0_norm_mix, l0_w_in, l0_sinks, rw, l0_w_out), (l1_norm_mix, s5p, l1_w_out)]
    layer_ffn = [
        (l0_norm_ffn, l0_router_coarse, l0_bias_coarse, l0_router_fine, l0_bias_fine, l0_exp_gate, l0_exp_up, l0_exp_down),
        (l1_norm_ffn, l1_router_coarse, l1_bias_coarse, l1_router_fine, l1_bias_fine, l1_exp_gate, l1_exp_up, l1_exp_down),
    ]

    def run(x, win_k, win_v, shift, wkv, s5_re, s5_im):
        new = []
        for i in range(DEPTH):
            if i % 2 == 0:
                norm_mix, w_in, sinks, rwp, w_out = layer_mix[i]
                x, nk, nv, nsh, nwkv = mixer_ab(x, win_k, win_v, shift, wkv, win_buf, norm_mix, w_in, sinks, rwp, w_out)
                new += [nk, nv, nsh, nwkv]
            else:
                norm_mix, sp, w_out = layer_mix[i]
                x, nr, ni = mixer_c(x, s5_re, s5_im, norm_mix, sp, w_out)
                new += [nr, ni]
            ffn = layer_ffn[i]
            x = x + hier_moe(rmsnorm(x, ffn[0]), *ffn[1:])
        return rmsnorm(x, final_norm), new

    wkv_zero = jnp.zeros((x_prompt.shape[0], N_HEADS_B, HEAD_DIM_B, HEAD_DIM_B), F32)
    s5_zero = jnp.zeros((x_prompt.shape[0], S5_GROUPS, S5_P), F32)
    y_prompt, pst = run(x_prompt, None, None, None, wkv_zero, s5_zero, s5_zero)
    p_win_k, p_win_v, p_shift, p_wkv, p_s5_re, p_s5_im = pst
    y_sample, sst = run(x_sample, cache_win_k, cache_win_v, state_shift, state_wkv, state_s5_re, state_s5_im)
    s_win_k, s_win_v, s_shift, s_wkv, s_s5_re, s_s5_im = sst
    return (y_prompt, y_sample,
            p_win_k, p_win_v, p_shift, p_wkv, p_s5_re, p_s5_im,
            s_win_k, s_win_v, s_shift, s_wkv, s_s5_re, s_s5_im)
```

```python
import functools

import jax
import jax.numpy as jnp
from jax import lax
from jax.experimental import pallas as pl
from jax.experimental.pallas import tpu as pltpu

F32 = jnp.float32
BF16 = jnp.bfloat16
I32 = jnp.int32

D_MODEL = 1024
HEAD_DIM = 64
N_HEADS_A = 8
N_KV_A = 2
GQA_GROUP = 4
WINDOW = 128
Q_COLS = 512
KV_COLS = 128
D_A_IN = 768
N_HEADS_B = 8
D_B = 512
D_LORA_W = 64
D_LORA_A = 64
D_LORA_G = 128
D_B_IN = 1792
D_IN0 = 2560
RWKV_GN_EPS = 64e-5
S5_CH = 16
S5_GROUPS = 64
S5_P = 64
S5_STATE = S5_GROUPS * S5_P
N_EGROUPS = 4
EXP_PER_GROUP = 4
N_EXPERTS = 16
D_FF_E = 512
RMS_EPS = 1e-5
NEG_BIG = -1e30
PAIRS = 64
MOE_TILE = 256
VMEM_LIMIT = 56 * 1024 * 1024


def _cparams(*sem):
    return pltpu.CompilerParams(dimension_semantics=sem, vmem_limit_bytes=VMEM_LIMIT)


def _dot(a, b):
    return jnp.dot(a, b, preferred_element_type=F32)


def _split_bf16(x):
    hi = x.astype(BF16)
    lo = (x - hi.astype(F32)).astype(BF16)
    return hi, lo


def _dot2(x, w):
    hi, lo = _split_bf16(x)
    return _dot(hi, w) + _dot(lo, w)


def _rms(x, g):
    return x * lax.rsqrt(jnp.mean(x * x, axis=-1, keepdims=True) + RMS_EPS) * g


def _sigmoid(x):
    return 1.0 / (1.0 + jnp.exp(-x))


def _in_proj_kernel(x_ref, g_ref, w_ref, q_ref, kv_ref, pb_ref):
    xn = _rms(x_ref[...], g_ref[...]).astype(BF16)
    q_ref[...] = _dot(xn, w_ref[:, :Q_COLS]).astype(BF16)
    kv_ref[...] = _dot(xn, w_ref[:, Q_COLS:D_A_IN])
    pb_ref[...] = _dot(xn, w_ref[:, D_A_IN:])


def _in_proj(x, g, w_bf16, tm):
    n = x.shape[0]
    return pl.pallas_call(
        _in_proj_kernel,
        grid=(n // tm,),
        in_specs=[pl.BlockSpec((tm, D_MODEL), lambda i: (i, 0)),
                  pl.BlockSpec((1, D_MODEL), lambda i: (0, 0)),
                  pl.BlockSpec((D_MODEL, D_IN0), lambda i: (0, 0))],
        out_specs=[pl.BlockSpec((tm, Q_COLS), lambda i: (i, 0)),
                   pl.BlockSpec((tm, 2 * KV_COLS), lambda i: (i, 0)),
                   pl.BlockSpec((tm, D_B_IN), lambda i: (i, 0))],
        out_shape=[jax.ShapeDtypeStruct((n, Q_COLS), BF16),
                   jax.ShapeDtypeStruct((n, 2 * KV_COLS), F32),
                   jax.ShapeDtypeStruct((n, D_B_IN), F32)],
        compiler_params=_cparams("parallel"),
        name="in_proj",
    )(x, g, w_bf16)


def _attn_prompt_kernel(sinks_ref, q_ref, kc_ref, kp_ref, vc_ref, vp_ref, o_ref):
    j = pl.program_id(1)
    qi = lax.broadcasted_iota(I32, (WINDOW, 2 * WINDOW), 0)
    kj = lax.broadcasted_iota(I32, (WINDOW, 2 * WINDOW), 1)
    valid = jnp.logical_and(kj > qi, kj <= qi + WINDOW)
    valid = jnp.logical_and(valid, jnp.logical_or(kj >= WINDOW, j > 0))
    dist = (WINDOW + qi - kj).astype(F32)
    for n in range(N_KV_A):
        cs = slice(n * HEAD_DIM, (n + 1) * HEAD_DIM)
        kb = jnp.concatenate([kp_ref[:, cs], kc_ref[:, cs]], axis=0).astype(BF16)
        vb = jnp.concatenate([vp_ref[:, cs], vc_ref[:, cs]], axis=0).astype(BF16)
        for g in range(GQA_GROUP):
            h = n * GQA_GROUP + g
            hs = slice(h * HEAD_DIM, (h + 1) * HEAD_DIM)
            s = lax.dot_general(q_ref[:, hs], kb, (((1,), (1,)), ((), ())), preferred_element_type=F32)
            s = s * (HEAD_DIM ** -0.5) - (2.0 ** -(h + 1)) * dist
            s = jnp.where(valid, s, NEG_BIG)
            sink = sinks_ref[h]
            m = jnp.maximum(jnp.max(s, axis=1, keepdims=True), sink)
            p = jnp.exp(s - m)
            l = jnp.sum(p, axis=1, keepdims=True) + jnp.exp(sink - m)
            o = _dot(p.astype(BF16), vb) / l
            o_ref[:, hs] = o.astype(BF16)


def _attn_prompt(q, kv, sinks, b, t):
    q2 = q.reshape(t, b * Q_COLS)
    kv2 = kv.reshape(t, b * 2 * KV_COLS)
    prev = lambda bi, j: jnp.maximum(j - 1, 0)
    out = pl.pallas_call(
        _attn_prompt_kernel,
        grid=(b, t // WINDOW),
        in_specs=[pl.BlockSpec(memory_space=pltpu.SMEM),
                  pl.BlockSpec((WINDOW, Q_COLS), lambda bi, j: (j, bi)),
                  pl.BlockSpec((WINDOW, KV_COLS), lambda bi, j: (j, 2 * bi)),
                  pl.BlockSpec((WINDOW, KV_COLS), lambda bi, j: (prev(bi, j), 2 * bi)),
                  pl.BlockSpec((WINDOW, KV_COLS), lambda bi, j: (j, 2 * bi + 1)),
                  pl.BlockSpec((WINDOW, KV_COLS), lambda bi, j: (prev(bi, j), 2 * bi + 1))],
        out_specs=pl.BlockSpec((WINDOW, Q_COLS), lambda bi, j: (j, bi)),
        out_shape=jax.ShapeDtypeStruct((t, b * Q_COLS), BF16),
        compiler_params=_cparams("parallel", "parallel"),
        name="attn_prompt",
    )(sinks, q2, kv2, kv2, kv2, kv2)
    return out.reshape(t * b, Q_COLS)


def _attn_sample_kernel(sinks_ref, q_ref, kn_ref, vn_ref, ck_ref, cv_ref, o_ref):
    bs, t = q_ref.shape[0], q_ref.shape[1]
    assert t & (t - 1) == 0
    nq, nk = GQA_GROUP * t, 2 * WINDOW
    r = lax.broadcasted_iota(I32, (nq, nk), 0)
    kj = lax.broadcasted_iota(I32, (nq, nk), 1)
    tq = jnp.bitwise_and(r, t - 1)
    valid = jnp.logical_and(kj > tq, kj <= tq + WINDOW)
    dist = (WINDOW + tq - kj).astype(F32)
    grp = jnp.right_shift(lax.broadcasted_iota(I32, (nq, 1), 0), t.bit_length() - 1)
    pad = jnp.zeros((bs, nk - WINDOW - t, HEAD_DIM), F32)
    for n in range(N_KV_A):
        cs = slice(n * HEAD_DIM, (n + 1) * HEAD_DIM)
        kb = jnp.concatenate([ck_ref[:, :, cs], kn_ref[:, :, cs], pad], axis=1).astype(BF16)
        vb = jnp.concatenate([cv_ref[:, :, cs], vn_ref[:, :, cs], pad], axis=1).astype(BF16)
        qn = jnp.concatenate([q_ref[:, :, (n * GQA_GROUP + g) * HEAD_DIM:(n * GQA_GROUP + g + 1) * HEAD_DIM]
                              for g in range(GQA_GROUP)], axis=1).astype(BF16)
        slope = jnp.zeros((nq, 1), F32)
        sink = jnp.zeros((nq, 1), F32)
        for g in range(GQA_GROUP):
            h = n * GQA_GROUP + g
            slope = jnp.where(grp == g, 2.0 ** -(h + 1), slope)
            sink = jnp.where(grp == g, sinks_ref[h], sink)
        s = jnp.einsum('bqd,bkd->bqk', qn, kb, preferred_element_type=F32)
        s = s * (HEAD_DIM ** -0.5) - (slope * dist)[None]
        s = jnp.where(valid[None], s, NEG_BIG)
        m = jnp.maximum(jnp.max(s, axis=2, keepdims=True), sink[None])
        p = jnp.exp(s - m)
        l = jnp.sum(p, axis=2, keepdims=True) + jnp.exp(sink[None] - m)
        o = jnp.einsum('bqk,bkd->bqd', p.astype(BF16), vb, preferred_element_type=F32) / l
        for g in range(GQA_GROUP):
            h = n * GQA_GROUP + g
            o_ref[:, :, h * HEAD_DIM:(h + 1) * HEAD_DIM] = o[:, g * t:(g + 1) * t, :]


def _attn_sample(q, kn, vn, ck, cv, sinks, bs=16):
    db, t = q.shape[0], q.shape[1]
    seq3 = lambda w: pl.BlockSpec((bs, t, w), lambda i: (i, 0, 0))
    cache = pl.BlockSpec((bs, WINDOW, KV_COLS), lambda i: (i, 0, 0))
    return pl.pallas_call(
        _attn_sample_kernel,
        grid=(db // bs,),
        in_specs=[pl.BlockSpec(memory_space=pltpu.SMEM), seq3(Q_COLS), seq3(KV_COLS), seq3(KV_COLS), cache, cache],
        out_specs=seq3(Q_COLS),
        out_shape=jax.ShapeDtypeStruct((db, t, Q_COLS), F32),
        compiler_params=_cparams("parallel"),
        name="attn_sample",
    )(sinks, q, kn, vn, ck, cv)


def _rwkv_prep_kernel(pb_ref, halo_ref, init_ref, mu_ref, w0_ref, wup_ref, a0_ref, aup_ref, gup_ref,
                      kk_ref, ka_ref, rk_ref, ones_ref,
                      r_o, w_o, k_o, v_o, kk_o, nkka_o, bonus_o, g_o, *, b):
    i = pl.program_id(0)
    pb = pb_ref[...]
    tm = pb.shape[0]
    halo = jnp.where(i == 0, init_ref[...], halo_ref[...])
    prev = halo if tm == b else jnp.concatenate([halo, pb[:tm - b]], axis=0)
    xs = pb + (prev - pb) * mu_ref[...]
    r = xs[:, :D_B]
    k = xs[:, D_B:2 * D_B]
    v = xs[:, 2 * D_B:3 * D_B]
    o1 = 3 * D_B
    wd = xs[:, o1:o1 + D_LORA_W]
    ad = xs[:, o1 + D_LORA_W:o1 + D_LORA_W + D_LORA_A]
    gd = xs[:, o1 + D_LORA_W + D_LORA_A:]
    z = -(w0_ref[...] + _dot(jnp.tanh(wd).astype(BF16), wup_ref[...]))
    softplus = jnp.maximum(z, 0.0) + jnp.log(1.0 + jnp.exp(-jnp.abs(z)))
    decay = jnp.exp(-jnp.exp(-softplus - 0.5))
    a = _sigmoid(a0_ref[...] + _dot(ad.astype(BF16), aup_ref[...]))
    g_o[...] = _dot(_sigmoid(gd).astype(BF16), gup_ref[...])
    ones = ones_ref[...]
    kk = k * kk_ref[...]
    kk = kk * lax.rsqrt(jnp.maximum(_dot2(kk * kk, ones), 1e-24))
    k2 = k * (1.0 + (a - 1.0) * ka_ref[...])
    bonus_o[...] = _dot2(r * k2 * rk_ref[...], ones) * v
    r_o[...] = r
    w_o[...] = decay
    k_o[...] = k2
    v_o[...] = v
    kk_o[...] = kk
    nkka_o[...] = -(kk * a)


def _rwkv_prep(pb, init, p, b, tm):
    n = pb.shape[0]
    row = lambda w: pl.BlockSpec((1, w), lambda i: (0, 0))
    full = lambda r, c: pl.BlockSpec((r, c), lambda i: (0, 0))
    tile = pl.BlockSpec((tm, D_B), lambda i: (i, 0))
    halo_blocks = tm // b
    return pl.pallas_call(
        functools.partial(_rwkv_prep_kernel, b=b),
        grid=(n // tm,),
        in_specs=[pl.BlockSpec((tm, D_B_IN), lambda i: (i, 0)),
                  pl.BlockSpec((b, D_B_IN), lambda i: (jnp.maximum(i * halo_blocks - 1, 0), 0)),
                  full(b, D_B_IN), row(D_B_IN), row(D_B), full(D_LORA_W, D_B), row(D_B), full(D_LORA_A, D_B),
                  full(D_LORA_G, D_B), row(D_B), row(D_B), row(D_B), full(D_B, D_B)],
        out_specs=[tile] * 8,
        out_shape=[jax.ShapeDtypeStruct((n, D_B), F32)] * 8,
        compiler_params=_cparams("arbitrary"),
        name="rwkv_prep",
    )(pb, pb, init, p['mu'], p['w0'], p['w_up'], p['a0'], p['a_up'], p['g_up'], p['k_k'], p['k_a'], p['r_k'],
      p['head_ones'])


def _wkv_scan_kernel(r_ref, w_ref, k_ref, kk_ref, nkka_ref, v_ref, s0_ref, o_ref, st_ref, s_scr):
    j = pl.program_id(1)

    @pl.when(j == 0)
    def _():
        s_scr[...] = s0_ref[0]

    tc = r_ref.shape[1]
    nacc = 4

    def step(s, carry):
        acc = [jnp.zeros((HEAD_DIM // 2, 2 * PAIRS), F32) for _ in range(nacc)]
        for kx in range(HEAD_DIM):
            acc[kx % nacc] = acc[kx % nacc] + s_scr[kx] * kk_ref[0, s, pl.ds(kx, 1), :]
        sa = (acc[0] + acc[1]) + (acc[2] + acc[3])
        vv = v_ref[0, s]
        oacc = [jnp.zeros((HEAD_DIM // 2, 2 * PAIRS), F32) for _ in range(nacc)]
        for kx in range(HEAD_DIM):
            row = pl.ds(kx, 1)
            sk = s_scr[kx] * w_ref[0, s, row, :] + sa * nkka_ref[0, s, row, :] + vv * k_ref[0, s, row, :]
            s_scr[kx] = sk
            oacc[kx % nacc] = oacc[kx % nacc] + sk * r_ref[0, s, row, :]
        o_ref[0, s] = (oacc[0] + oacc[1]) + (oacc[2] + oacc[3])
        return carry

    lax.fori_loop(0, tc, step, 0)

    @pl.when(j == pl.num_programs(1) - 1)
    def _():
        st_ref[0] = s_scr[...]


def _wkv_scan(r, w, k, kk, nkka, v, s0, tc):
    g, t = r.shape[0], r.shape[1]
    kspec = pl.BlockSpec((1, tc, HEAD_DIM, 2 * PAIRS), lambda gi, j: (gi, j, 0, 0))
    vspec = pl.BlockSpec((1, tc, HEAD_DIM // 2, 2 * PAIRS), lambda gi, j: (gi, j, 0, 0))
    sspec = pl.BlockSpec((1, HEAD_DIM, HEAD_DIM // 2, 2 * PAIRS), lambda gi, j: (gi, 0, 0, 0))
    return pl.pallas_call(
        _wkv_scan_kernel,
        grid=(g, t // tc),
        in_specs=[kspec] * 5 + [vspec, sspec],
        out_specs=[vspec, sspec],
        out_shape=[jax.ShapeDtypeStruct((g, t, HEAD_DIM // 2, 2 * PAIRS), F32),
                   jax.ShapeDtypeStruct((g, HEAD_DIM, HEAD_DIM // 2, 2 * PAIRS), F32)],
        scratch_shapes=[pltpu.VMEM((HEAD_DIM, HEAD_DIM // 2, 2 * PAIRS), F32)],
        compiler_params=_cparams("parallel", "arbitrary"),
        name="wkv_scan",
    )(r, w, k, kk, nkka, v, s0)


def _to_key_rows(x, b, t):
    g = b // 8
    x = x.reshape(t, g, 8, N_HEADS_B, HEAD_DIM).transpose(1, 0, 4, 2, 3).reshape(g, t, HEAD_DIM, PAIRS)
    return jnp.concatenate([x, x], axis=-1)


def _to_val_rows(x, b, t):
    g = b // 8
    x = x.reshape(t, g, 8, N_HEADS_B, 2, HEAD_DIM // 2).transpose(1, 0, 5, 4, 2, 3)
    return x.reshape(g, t, HEAD_DIM // 2, 2 * PAIRS)


def _from_val_rows(o, b, t):
    g = b // 8
    o = o.reshape(g, t, HEAD_DIM // 2, 2, 8, N_HEADS_B).transpose(1, 0, 4, 5, 3, 2)
    return o.reshape(t * b, D_B)


def _state_to_scan(s, b):
    g = b // 8
    s = s.reshape(g, 8, N_HEADS_B, 2, HEAD_DIM // 2, HEAD_DIM).transpose(0, 5, 4, 3, 1, 2)
    return s.reshape(g, HEAD_DIM, HEAD_DIM // 2, 2 * PAIRS)


def _state_from_scan(s, b):
    g = b // 8
    s = s.reshape(g, HEAD_DIM, HEAD_DIM // 2, 2, 8, N_HEADS_B).transpose(0, 4, 5, 3, 2, 1)
    return s.reshape(b, N_HEADS_B, HEAD_DIM, HEAD_DIM)


def _route_tile(x, nrm_ref, wr_ref, br_ref, cnt_scr, xn_o, idx_o, wts_o):
    tm = x.shape[0]
    xn = _rms(x, nrm_ref[...])
    hi, lo = _split_bf16(xn)
    xn_o[...] = hi
    wr = wr_ref[...]
    pa = _dot(hi, wr)
    pb = _dot(lo, wr)
    lg = pa[:, 0:32] + pa[:, 32:64] + pb[:, 0:32] + pb[:, 32:64] + br_ref[...]
    col = lambda c: lg[:, c:c + 1]
    c = [col(gx) for gx in range(N_EGROUPS)]
    m = jnp.maximum(jnp.maximum(c[0], c[1]), jnp.maximum(c[2], c[3]))
    den = jnp.exp(c[0] - m) + jnp.exp(c[1] - m) + jnp.exp(c[2] - m) + jnp.exp(c[3] - m)
    pg = 1.0 / den
    gi = jnp.where(c[0] >= m, 0, jnp.where(c[1] >= m, 1, jnp.where(c[2] >= m, 2, 3))).astype(I32)
    sel = []
    for e in range(EXP_PER_GROUP):
        sel.append(jnp.where(gi == 0, col(4 + e), jnp.where(gi == 1, col(8 + e),
                                                            jnp.where(gi == 2, col(12 + e), col(16 + e)))))
    v1 = jnp.maximum(jnp.maximum(sel[0], sel[1]), jnp.maximum(sel[2], sel[3]))
    i1 = jnp.where(sel[0] >= v1, 0, jnp.where(sel[1] >= v1, 1, jnp.where(sel[2] >= v1, 2, 3))).astype(I32)
    rest = [jnp.where(i1 == e, -jnp.inf, sel[e]) for e in range(EXP_PER_GROUP)]
    v2 = jnp.maximum(jnp.maximum(rest[0], rest[1]), jnp.maximum(rest[2], rest[3]))
    i2 = jnp.where(rest[0] >= v2, 0, jnp.where(rest[1] >= v2, 1, jnp.where(rest[2] >= v2, 2, 3))).astype(I32)
    tt = jnp.exp(v2 - v1)
    w1 = pg / (1.0 + tt)
    w2 = pg * tt / (1.0 + tt)
    e1 = gi * EXP_PER_GROUP + i1
    e2 = gi * EXP_PER_GROUP + i2
    lane = lax.broadcasted_iota(I32, (tm, N_EXPERTS), 1)
    oh1 = lane == e1
    oh2 = lane == e2
    oh = jnp.where(jnp.logical_or(oh1, oh2), 1.0, 0.0)
    ri = lax.broadcasted_iota(I32, (tm, tm), 0)
    ci = lax.broadcasted_iota(I32, (tm, tm), 1)
    ltri = jnp.where(ri > ci, 1.0, 0.0).astype(BF16)
    cnt = cnt_scr[0:1, 0:N_EXPERTS]
    pre = _dot(ltri, oh.astype(BF16)) + cnt
    rank1 = jnp.sum(jnp.where(oh1, pre, 0.0), axis=1, keepdims=True)
    rank2 = jnp.sum(jnp.where(oh2, pre, 0.0), axis=1, keepdims=True)
    cnt_scr[0:1, 0:N_EXPERTS] = cnt + jnp.sum(oh, axis=0, keepdims=True)
    idx_o[:, 0:1] = e1
    idx_o[:, 1:2] = e2
    idx_o[:, 2:3] = rank1.astype(I32)
    idx_o[:, 3:4] = rank2.astype(I32)
    lw = lax.broadcasted_iota(I32, (tm, 128), 1)
    wts_o[...] = jnp.where(lw == 0, w1, jnp.where(lw == 1, w2, 0.0))


def _route_out_specs(tm):
    return [pl.BlockSpec((tm, D_MODEL), lambda i: (i, 0)),
            pl.BlockSpec((tm, 4), lambda i: (i, 0)),
            pl.BlockSpec((tm, 128), lambda i: (i, 0)),
            pl.BlockSpec((8, 128), lambda i: (0, 0))]


def _route_out_shapes(n):
    return [jax.ShapeDtypeStruct((n, D_MODEL), BF16),
            jax.ShapeDtypeStruct((n, 4), I32),
            jax.ShapeDtypeStruct((n, 128), F32),
            jax.ShapeDtypeStruct((8, 128), F32)]


def _route_in_specs():
    return [pl.BlockSpec((1, D_MODEL), lambda i: (0, 0)),
            pl.BlockSpec((D_MODEL, 128), lambda i: (0, 0)),
            pl.BlockSpec((1, 32), lambda i: (0, 0))]


def _combine(x_ref, g1_ref, g2_ref, wts_ref):
    wts = wts_ref[...]
    return x_ref[...] + wts[:, 0:1] * g1_ref[...] + wts[:, 1:2] * g2_ref[...]


def _mix0_out_kernel(o_ref, bonus_ref, g_ref, attn_ref, x_ref, lnw_ref, lnb_ref, avg_ref, wo_ref,
                     nrm_ref, wr_ref, br_ref,
                     x1_o, xn_o, idx_o, wts_o, cnt_o, cnt_scr):
    @pl.when(pl.program_id(0) == 0)
    def _():
        cnt_scr[...] = jnp.zeros_like(cnt_scr)

    avg = avg_ref[...]
    o = o_ref[...]
    d = o - _dot2(o, avg)
    var = _dot2(d * d, avg)
    on = d * lax.rsqrt(var + RWKV_GN_EPS) * lnw_ref[...] + lnb_ref[...]
    rout = ((on + bonus_ref[...]) * g_ref[...]).astype(BF16)
    y = _dot(attn_ref[...], wo_ref[:Q_COLS, :]) + _dot(rout, wo_ref[Q_COLS:, :]) + x_ref[...]
    x1_o[...] = y
    _route_tile(y, nrm_ref, wr_ref, br_ref, cnt_scr, xn_o, idx_o, wts_o)
    cnt_o[...] = cnt_scr[...]


def _mix0_out(o, bonus, g, attn, x, p, rp, tm):
    n = x.shape[0]
    tile = lambda w: pl.BlockSpec((tm, w), lambda i: (i, 0))
    row = lambda w: pl.BlockSpec((1, w), lambda i: (0, 0))
    full = lambda r, c: pl.BlockSpec((r, c), lambda i: (0, 0))
    return pl.pallas_call(
        _mix0_out_kernel,
        grid=(n // tm,),
        in_specs=[tile(D_B), tile(D_B), tile(D_B), tile(Q_COLS), tile(D_MODEL), row(D_B), row(D_B),
                  full(D_B, D_B), full(D_MODEL, D_MODEL)] + _route_in_specs(),
        out_specs=[tile(D_MODEL)] + _route_out_specs(tm),
        out_shape=[jax.ShapeDtypeStruct((n, D_MODEL), F32)] + _route_out_shapes(n),
        scratch_shapes=[pltpu.VMEM((8, 128), F32)],
        compiler_params=_cparams("arbitrary"),
        name="mix0_out",
    )(o, bonus, g, attn, x, p['ln_w'], p['ln_b'], p['head_avg'], p['w_out'], rp['norm'], rp['w'], rp['b'])


def _expert_kernel(te_ref, nu_ref, x_ref, wg_ref, wu_ref, wd_ref, o_ref):
    i = pl.program_id(0)

    @pl.when(i < nu_ref[0])
    def _():
        x = x_ref[...]
        hg = _dot(x, wg_ref[0])
        hu = _dot(x, wu_ref[0])
        h = (hg * _sigmoid(hg)) * hu
        o_ref[...] = _dot(h.astype(BF16), wd_ref[0])

    @pl.when(i >= nu_ref[0])
    def _():
        o_ref[...] = jnp.zeros_like(o_ref)


def _experts(xs, tile_expert, n_used, wg, wu, wd):
    r = xs.shape[0]
    grid_spec = pltpu.PrefetchScalarGridSpec(
        num_scalar_prefetch=2,
        grid=(r // MOE_TILE,),
        in_specs=[pl.BlockSpec((MOE_TILE, D_MODEL), lambda i, te, nu: (i, 0)),
                  pl.BlockSpec((1, D_MODEL, D_FF_E), lambda i, te, nu: (te[i], 0, 0)),
                  pl.BlockSpec((1, D_MODEL, D_FF_E), lambda i, te, nu: (te[i], 0, 0)),
                  pl.BlockSpec((1, D_FF_E, D_MODEL), lambda i, te, nu: (te[i], 0, 0))],
        out_specs=pl.BlockSpec((MOE_TILE, D_MODEL), lambda i, te, nu: (i, 0)),
    )
    return pl.pallas_call(
        _expert_kernel,
        grid_spec=grid_spec,
        out_shape=jax.ShapeDtypeStruct((r, D_MODEL), F32),
        compiler_params=_cparams("arbitrary"),
        name="moe_experts",
    )(tile_expert, n_used, xs, wg, wu, wd)


def _moe(xn, idx, cnt, ep):
    n = xn.shape[0]
    rows = 2 * n + N_EXPERTS * MOE_TILE
    counts = cnt[0, :N_EXPERTS].astype(I32)
    padded = ((counts + MOE_TILE - 1) // MOE_TILE) * MOE_TILE
    ends = jnp.cumsum(padded)
    offs = ends - padded
    pos1 = offs[idx[:, 0]] + idx[:, 2]
    pos2 = offs[idx[:, 1]] + idx[:, 3]
    tok = jnp.arange(n, dtype=I32)
    src = jnp.zeros((rows,), I32).at[pos1].set(tok).at[pos2].set(tok)
    n_used = (ends[-1] // MOE_TILE).astype(I32)
    starts = jnp.arange(rows // MOE_TILE, dtype=I32) * MOE_TILE
    starts = jnp.minimum(starts, ends[-1] - 1)
    tile_expert = jnp.sum((starts[:, None] >= ends[None, :]).astype(I32), axis=1)
    tile_expert = jnp.minimum(tile_expert, N_EXPERTS - 1).astype(I32)
    xs = jnp.take(xn, src, axis=0)
    out = _experts(xs, tile_expert, n_used.reshape(1), ep['wg'], ep['wu'], ep['wd'])
    return jnp.take(out, pos1, axis=0), jnp.take(out, pos2, axis=0)


def _gelu_tanh(x):
    return 0.5 * x * (1.0 + jnp.tanh(0.7978845608028654 * (x + 0.044715 * (x * x * x))))


def _mix1_kernel(x_ref, g1_ref, g2_ref, wts_ref, nmix_ref, bre_ref, bim_ref, are_ref, aim_ref, cre_ref, cim_ref,
                 dsk_ref, wo_ref, h0r_ref, h0i_ref, nrm_ref, wr_ref, br_ref,
                 x2_o, xn_o, idx_o, wts_o, cnt_o, hr_o, hi_o,
                 bur, bui, hr_scr, hi_scr, cnt_scr, *, b, cw):
    @pl.when(pl.program_id(0) == 0)
    def _():
        cnt_scr[...] = jnp.zeros_like(cnt_scr)
        hr_scr[...] = h0r_ref[...]
        hi_scr[...] = h0i_ref[...]

    x = _combine(x_ref, g1_ref, g2_ref, wts_ref)
    u = _rms(x, nmix_ref[...])
    ub = u.astype(BF16)
    nblk = bre_ref.shape[0]
    kin = D_MODEL // nblk
    kst = S5_STATE // nblk
    for cb in range(nblk):
        ucb = ub[:, cb * kin:(cb + 1) * kin]
        bur[:, cb * kst:(cb + 1) * kst] = _dot(ucb, bre_ref[cb])
        bui[:, cb * kst:(cb + 1) * kst] = _dot(ucb, bim_ref[cb])

    tc = x.shape[0] // b
    for c0 in range(0, S5_STATE, cw):
        cs = slice(c0, c0 + cw)
        ar = jnp.broadcast_to(are_ref[:, cs], (b, cw))
        ai = jnp.broadcast_to(aim_ref[:, cs], (b, cw))

        def step(s, carry, cs=cs, ar=ar, ai=ai):
            hr, hi = carry
            rows = pl.ds(pl.multiple_of(s * b, b), b)
            nr = ar * hr - ai * hi + bur[rows, cs]
            ni = ar * hi + ai * hr + bui[rows, cs]
            bur[rows, cs] = nr
            bui[rows, cs] = ni
            return nr, ni

        hr, hi = lax.fori_loop(0, tc, step, (hr_scr[:, cs], hi_scr[:, cs]))
        hr_scr[:, cs] = hr
        hi_scr[:, cs] = hi

    ych = []
    for cb in range(nblk):
        ss = slice(cb * kst, (cb + 1) * kst)
        ych.append(_dot(bur[:, ss].astype(BF16), cre_ref[cb]) - _dot(bui[:, ss].astype(BF16), cim_ref[cb]))
    y = jnp.concatenate(ych, axis=1) + dsk_ref[...] * u
    z = _dot(_gelu_tanh(y).astype(BF16), wo_ref[...])
    x2 = x + z[:, :D_MODEL] * _sigmoid(z[:, D_MODEL:])
    x2_o[...] = x2
    _route_tile(x2, nrm_ref, wr_ref, br_ref, cnt_scr, xn_o, idx_o, wts_o)
    cnt_o[...] = cnt_scr[...]
    hr_o[...] = hr_scr[...]
    hi_o[...] = hi_scr[...]


def _mix1(x, g1, g2, wts, sp, rp, h0r, h0i, b, tr):
    n = x.shape[0]
    cw = 1024 if b == 8 else 128
    tile = lambda w: pl.BlockSpec((tr, w), lambda i: (i, 0))
    row = lambda w: pl.BlockSpec((1, w), lambda i: (0, 0))
    full = lambda *s: pl.BlockSpec(s, lambda i: (0,) * len(s))
    nblk = sp['b_re'].shape[0]
    return pl.pallas_call(
        functools.partial(_mix1_kernel, b=b, cw=cw),
        grid=(n // tr,),
        in_specs=[tile(D_MODEL), tile(D_MODEL), tile(D_MODEL), tile(128), row(D_MODEL),
                  full(nblk, D_MODEL // nblk, S5_STATE // nblk), full(nblk, D_MODEL // nblk, S5_STATE // nblk),
                  row(S5_STATE), row(S5_STATE),
                  full(nblk, S5_STATE // nblk, D_MODEL // nblk), full(nblk, S5_STATE // nblk, D_MODEL // nblk),
                  row(D_MODEL), full(D_MODEL, 2 * D_MODEL), full(b, S5_STATE), full(b, S5_STATE)] + _route_in_specs(),
        out_specs=[tile(D_MODEL)] + _route_out_specs(tr) + [full(b, S5_STATE), full(b, S5_STATE)],
        out_shape=[jax.ShapeDtypeStruct((n, D_MODEL), F32)] + _route_out_shapes(n)
                  + [jax.ShapeDtypeStruct((b, S5_STATE), F32)] * 2,
        scratch_shapes=[pltpu.VMEM((tr, S5_STATE), F32), pltpu.VMEM((tr, S5_STATE), F32),
                        pltpu.VMEM((b, S5_STATE), F32), pltpu.VMEM((b, S5_STATE), F32),
                        pltpu.VMEM((8, 128), F32)],
        compiler_params=_cparams("arbitrary"),
        name="mix1",
    )(x, g1, g2, wts, sp['norm'], sp['b_re'], sp['b_im'], sp['a_re'], sp['a_im'], sp['c_re'], sp['c_im'],
      sp['d'], sp['w_out'], h0r, h0i, rp['norm'], rp['w'], rp['b'])


def _final_kernel(x_ref, g1_ref, g2_ref, wts_ref, nrm_ref, y_o):
    y_o[...] = _rms(_combine(x_ref, g1_ref, g2_ref, wts_ref), nrm_ref[...])


def _final(x, g1, g2, wts, nrm, tm):
    n = x.shape[0]
    tile = lambda w: pl.BlockSpec((tm, w), lambda i: (i, 0))
    return pl.pallas_call(
        _final_kernel,
        grid=(n // tm,),
        in_specs=[tile(D_MODEL), tile(D_MODEL), tile(D_MODEL), tile(128), pl.BlockSpec((1, D_MODEL), lambda i: (0, 0))],
        out_specs=tile(D_MODEL),
        out_shape=jax.ShapeDtypeStruct((n, D_MODEL), F32),
        compiler_params=_cparams("parallel"),
        name="final_norm",
    )(x, g1, g2, wts, nrm)


def _router_params(norm, w_rc, b_rc, w_rf, b_rf):
    w = jnp.concatenate([w_rc, w_rf.reshape(D_MODEL, N_EXPERTS), jnp.zeros((D_MODEL, 12), F32)], axis=1)
    hi = w.astype(BF16)
    lo = (w - hi.astype(F32)).astype(BF16)
    wcat = jnp.concatenate([hi, lo, jnp.zeros((D_MODEL, 64), BF16)], axis=1)
    bias = jnp.concatenate([b_rc, b_rf.reshape(-1), jnp.zeros((12,), F32)]).reshape(1, 32)
    return {'norm': norm.reshape(1, D_MODEL), 'w': wcat, 'b': bias}


def _expert_params(wg, wu, wd):
    return {'wg': wg.astype(BF16), 'wu': wu.astype(BF16), 'wd': wd.astype(BF16)}


def _s5_params(norm, a_re, a_im, log_dt, b_re, b_im, c_re, c_im, d_skip, w_out, nblk=8):
    dt = jnp.exp(log_dt)
    mag = jnp.exp(dt * a_re)
    ab_re, ab_im = mag * jnp.cos(dt * a_im), mag * jnp.sin(dt * a_im)
    den = a_re * a_re + a_im * a_im
    f_re = ((ab_re - 1.0) * a_re + ab_im * a_im) / den
    f_im = (ab_im * a_re - (ab_re - 1.0) * a_im) / den
    bb_re = f_re[..., None] * b_re - f_im[..., None] * b_im
    bb_im = f_re[..., None] * b_im + f_im[..., None] * b_re
    gpb = S5_GROUPS // nblk
    eye = jnp.eye(gpb, dtype=F32)

    def in_blocks(bb):
        bb = bb.reshape(nblk, gpb, S5_P, S5_CH)
        w = jnp.einsum('ngpc,gh->ngchp', bb, eye)
        return w.reshape(nblk, gpb * S5_CH, gpb * S5_P).astype(BF16)

    def out_blocks(cc):
        cc = cc.reshape(nblk, gpb, S5_CH, S5_P)
        w = jnp.einsum('ngcp,gh->ngphc', cc, eye)
        return w.reshape(nblk, gpb * S5_P, gpb * S5_CH).astype(BF16)

    return {'norm': norm.reshape(1, D_MODEL), 'b_re': in_blocks(bb_re), 'b_im': in_blocks(bb_im),
            'a_re': ab_re.reshape(1, S5_STATE), 'a_im': ab_im.reshape(1, S5_STATE),
            'c_re': out_blocks(c_re), 'c_im': out_blocks(c_im), 'd': d_skip.reshape(1, D_MODEL),
            'w_out': w_out.astype(BF16)}


def _head_block(value):
    hid = jnp.arange(D_B, dtype=I32) // HEAD_DIM
    return jnp.where(hid[:, None] == hid[None, :], value, 0.0).astype(BF16)


def _run_group(x_tm, b, t, cache_k, cache_v, shift0, wkv0, h0r, h0i, pr):
    n = b * t
    prompt = cache_k is None
    tm = 256 if prompt else 128
    q, kv, pb = _in_proj(x_tm, pr['l0_norm'], pr['l0_w_in'], tm)

    if prompt:
        attn = _attn_prompt(q, kv, pr['sinks'], b, t)
        kv3 = kv.reshape(t, b, 2, N_KV_A, HEAD_DIM)
        new_k = kv3[t - WINDOW:, :, 0].transpose(1, 0, 2, 3)
        new_v = kv3[t - WINDOW:, :, 1].transpose(1, 0, 2, 3)
        init = jnp.zeros((b, D_B_IN), F32)
    else:
        qs = q.astype(F32).reshape(t, b, Q_COLS).transpose(1, 0, 2)
        kvs = kv.reshape(t, b, 2 * KV_COLS).transpose(1, 0, 2)
        kn, vn = kvs[..., :KV_COLS], kvs[..., KV_COLS:]
        ck = cache_k.reshape(b, WINDOW, KV_COLS)
        cv = cache_v.reshape(b, WINDOW, KV_COLS)
        attn = _attn_sample(qs, kn, vn, ck, cv, pr['sinks'])
        attn = attn.transpose(1, 0, 2).reshape(n, Q_COLS).astype(BF16)
        new_k = jnp.concatenate([ck[:, t:], kn], axis=1).reshape(b, WINDOW, N_KV_A, HEAD_DIM)
        new_v = jnp.concatenate([cv[:, t:], vn], axis=1).reshape(b, WINDOW, N_KV_A, HEAD_DIM)
        init = shift0
    new_shift = pb[n - b:]

    r, w, k, v, kk, nkka, bonus, g = _rwkv_prep(pb, init, pr['rw'], b, tm)
    tc = 32 if prompt else t
    s0 = jnp.zeros((b // 8, HEAD_DIM, HEAD_DIM // 2, 2 * PAIRS), F32) if prompt else _state_to_scan(wkv0, b)
    o, s_fin = _wkv_scan(_to_key_rows(r, b, t), _to_key_rows(w, b, t), _to_key_rows(k, b, t),
                         _to_key_rows(kk, b, t), _to_key_rows(nkka, b, t), _to_val_rows(v, b, t), s0, tc)
    o = _from_val_rows(o, b, t)
    new_wkv = _state_from_scan(s_fin, b)

    x1, xn, idx, wts, cnt = _mix0_out(o, bonus, g, attn, x_tm, pr['rw'], pr['l0_route'], 256)
    g1, g2 = _moe(xn, idx, cnt, pr['l0_exp'])

    x2, xn, idx, wts2, cnt, hr, hi = _mix1(x1, g1, g2, wts, pr['s5'], pr['l1_route'], h0r, h0i, b, 256)
    g1, g2 = _moe(xn, idx, cnt, pr['l1_exp'])
    y = _final(x2, g1, g2, wts2, pr['final_norm'], 256)
    return (y, new_k, new_v, new_shift, new_wkv,
            hr.reshape(b, S5_GROUPS, S5_P), hi.reshape(b, S5_GROUPS, S5_P))


def kernel(x_prompt, x_sample, cache_win_k, cache_win_v, state_shift, state_wkv, state_s5_re, state_s5_im,
           l0_norm_mix, l0_w_in, l0_sinks, l0_mu, l0_w0, l0_w_lora_up, l0_a0, l0_a_lora_up, l0_g_lora_up,
           l0_k_k, l0_k_a, l0_r_k, l0_ln_w, l0_ln_b, l0_w_out,
           l0_norm_ffn, l0_router_coarse, l0_bias_coarse, l0_router_fine, l0_bias_fine,
           l0_exp_gate, l0_exp_up, l0_exp_down,
           l1_norm_mix, l1_s5_a_re, l1_s5_a_im, l1_s5_log_dt, l1_s5_b_re, l1_s5_b_im, l1_s5_c_re, l1_s5_c_im,
           l1_s5_d, l1_w_out,
           l1_norm_ffn, l1_router_coarse, l1_bias_coarse, l1_router_fine, l1_bias_fine,
           l1_exp_gate, l1_exp_up, l1_exp_down,
           final_norm):
    row = lambda z: z.reshape(1, -1)
    pr = {
        'l0_norm': row(l0_norm_mix), 'l0_w_in': l0_w_in.astype(BF16), 'sinks': l0_sinks,
        'rw': {'mu': row(l0_mu), 'w0': row(l0_w0), 'w_up': l0_w_lora_up.astype(BF16), 'a0': row(l0_a0),
               'a_up': l0_a_lora_up.astype(BF16), 'g_up': l0_g_lora_up.astype(BF16), 'k_k': row(l0_k_k),
               'k_a': row(l0_k_a), 'r_k': row(l0_r_k), 'ln_w': row(l0_ln_w), 'ln_b': row(l0_ln_b),
               'head_ones': _head_block(1.0), 'head_avg': _head_block(1.0 / HEAD_DIM),
               'w_out': l0_w_out.astype(BF16)},
        'l0_route': _router_params(l0_norm_ffn, l0_router_coarse, l0_bias_coarse, l0_router_fine, l0_bias_fine),
        'l0_exp': _expert_params(l0_exp_gate, l0_exp_up, l0_exp_down),
        's5': _s5_params(l1_norm_mix, l1_s5_a_re, l1_s5_a_im, l1_s5_log_dt, l1_s5_b_re, l1_s5_b_im,
                         l1_s5_c_re, l1_s5_c_im, l1_s5_d, l1_w_out),
        'l1_route': _router_params(l1_norm_ffn, l1_router_coarse, l1_bias_coarse, l1_router_fine, l1_bias_fine),
        'l1_exp': _expert_params(l1_exp_gate, l1_exp_up, l1_exp_down),
        'final_norm': row(final_norm),
    }
    bp, tp = x_prompt.shape[0], x_prompt.shape[1]
    bs, ts = x_sample.shape[0], x_sample.shape[1]
    xp = x_prompt.transpose(1, 0, 2).reshape(bp * tp, D_MODEL)
    xs = x_sample.transpose(1, 0, 2).reshape(bs * ts, D_MODEL)
    zero_state = jnp.zeros((bp, S5_STATE), F32)
    yp, pk, pv, psh, pwkv, pre, pim = _run_group(xp, bp, tp, None, None, None, None, zero_state, zero_state, pr)
    ys, sk, sv, ssh, swkv, sre, sim = _run_group(
        xs, bs, ts, cache_win_k, cache_win_v, state_shift, state_wkv,
        state_s5_re.reshape(bs, S5_STATE), state_s5_im.reshape(bs, S5_STATE), pr)
    y_prompt = yp.reshape(tp, bp, D_MODEL).transpose(1, 0, 2)
    y_sample = ys.reshape(ts, bs, D_MODEL).transpose(1, 0, 2)
    return (y_prompt, y_sample, pk, pv, psh, pwkv, pre, pim, sk, sv, ssh, swkv, sre, sim)
```

```python
import functools

import jax
import jax.numpy as jnp
from jax import lax
from jax.experimental import pallas as pl
from jax.experimental.pallas import tpu as pltpu

F32 = jnp.float32
BF16 = jnp.bfloat16
I32 = jnp.int32

D_MODEL = 1024
HEAD_DIM = 64
N_HEADS_A = 8
N_KV_A = 2
GQA_GROUP = 4
WINDOW = 128
Q_COLS = 512
KV_COLS = 128
D_A_IN = 768
N_HEADS_B = 8
D_B = 512
D_LORA_W = 64
D_LORA_A = 64
D_LORA_G = 128
D_B_IN = 1792
D_IN0 = 2560
RWKV_GN_EPS = 64e-5
S5_CH = 16
S5_GROUPS = 64
S5_P = 64
S5_STATE = S5_GROUPS * S5_P
N_EGROUPS = 4
EXP_PER_GROUP = 4
N_EXPERTS = 16
D_FF_E = 512
RMS_EPS = 1e-5
NEG_BIG = -1e30
PAIRS = 64
MOE_TILE = 256
VMEM_LIMIT = 56 * 1024 * 1024


def _cparams(*sem):
    return pltpu.CompilerParams(dimension_semantics=sem, vmem_limit_bytes=VMEM_LIMIT)


def _dot(a, b):
    return jnp.dot(a, b, preferred_element_type=F32)


def _split_bf16(x):
    hi = x.astype(BF16)
    lo = (x - hi.astype(F32)).astype(BF16)
    return hi, lo


def _dot2(x, w):
    hi, lo = _split_bf16(x)
    return _dot(hi, w) + _dot(lo, w)


def _rms(x, g):
    return x * lax.rsqrt(jnp.mean(x * x, axis=-1, keepdims=True) + RMS_EPS) * g


def _sigmoid(x):
    return 1.0 / (1.0 + jnp.exp(-x))


def _in_proj_kernel(x_ref, g_ref, w_ref, q_ref, kv_ref, pb_ref):
    xn = _rms(x_ref[...], g_ref[...]).astype(BF16)
    q_ref[...] = _dot(xn, w_ref[:, :Q_COLS]).astype(BF16)
    kv_ref[...] = _dot(xn, w_ref[:, Q_COLS:D_A_IN])
    pb_ref[...] = _dot(xn, w_ref[:, D_A_IN:])


def _in_proj(x, g, w_bf16, tm):
    n = x.shape[0]
    return pl.pallas_call(
        _in_proj_kernel,
        grid=(n // tm,),
        in_specs=[pl.BlockSpec((tm, D_MODEL), lambda i: (i, 0)),
                  pl.BlockSpec((1, D_MODEL), lambda i: (0, 0)),
                  pl.BlockSpec((D_MODEL, D_IN0), lambda i: (0, 0))],
        out_specs=[pl.BlockSpec((tm, Q_COLS), lambda i: (i, 0)),
                   pl.BlockSpec((tm, 2 * KV_COLS), lambda i: (i, 0)),
                   pl.BlockSpec((tm, D_B_IN), lambda i: (i, 0))],
        out_shape=[jax.ShapeDtypeStruct((n, Q_COLS), BF16),
                   jax.ShapeDtypeStruct((n, 2 * KV_COLS), F32),
                   jax.ShapeDtypeStruct((n, D_B_IN), F32)],
        compiler_params=_cparams("parallel"),
        name="in_proj",
    )(x, g, w_bf16)


def _attn_prompt_kernel(sinks_ref, q_ref, kc_ref, kp_ref, vc_ref, vp_ref, o_ref):
    j = pl.program_id(1)
    qi = lax.broadcasted_iota(I32, (WINDOW, 2 * WINDOW), 0)
    kj = lax.broadcasted_iota(I32, (WINDOW, 2 * WINDOW), 1)
    valid = jnp.logical_and(kj > qi, kj <= qi + WINDOW)
    valid = jnp.logical_and(valid, jnp.logical_or(kj >= WINDOW, j > 0))
    dist = (WINDOW + qi - kj).astype(F32)
    for n in range(N_KV_A):
        cs = slice(n * HEAD_DIM, (n + 1) * HEAD_DIM)
        kb = jnp.concatenate([kp_ref[:, cs], kc_ref[:, cs]], axis=0).astype(BF16)
        vb = jnp.concatenate([vp_ref[:, cs], vc_ref[:, cs]], axis=0).astype(BF16)
        for g in range(GQA_GROUP):
            h = n * GQA_GROUP + g
            hs = slice(h * HEAD_DIM, (h + 1) * HEAD_DIM)
            s = lax.dot_general(q_ref[:, hs], kb, (((1,), (1,)), ((), ())), preferred_element_type=F32)
            s = s * (HEAD_DIM ** -0.5) - (2.0 ** -(h + 1)) * dist
            s = jnp.where(valid, s, NEG_BIG)
            sink = sinks_ref[h]
            m = jnp.maximum(jnp.max(s, axis=1, keepdims=True), sink)
            p = jnp.exp(s - m)
            l = jnp.sum(p, axis=1, keepdims=True) + jnp.exp(sink - m)
            o = _dot(p.astype(BF16), vb) / l
            o_ref[:, hs] = o.astype(BF16)


def _attn_prompt(q, kv, sinks, b, t):
    q2 = q.reshape(t, b * Q_COLS)
    kv2 = kv.reshape(t, b * 2 * KV_COLS)
    prev = lambda bi, j: jnp.maximum(j - 1, 0)
    out = pl.pallas_call(
        _attn_prompt_kernel,
        grid=(b, t // WINDOW),
        in_specs=[pl.BlockSpec(memory_space=pltpu.SMEM),
                  pl.BlockSpec((WINDOW, Q_COLS), lambda bi, j: (j, bi)),
                  pl.BlockSpec((WINDOW, KV_COLS), lambda bi, j: (j, 2 * bi)),
                  pl.BlockSpec((WINDOW, KV_COLS), lambda bi, j: (prev(bi, j), 2 * bi)),
                  pl.BlockSpec((WINDOW, KV_COLS), lambda bi, j: (j, 2 * bi + 1)),
                  pl.BlockSpec((WINDOW, KV_COLS), lambda bi, j: (prev(bi, j), 2 * bi + 1))],
        out_specs=pl.BlockSpec((WINDOW, Q_COLS), lambda bi, j: (j, bi)),
        out_shape=jax.ShapeDtypeStruct((t, b * Q_COLS), BF16),
        compiler_params=_cparams("parallel", "parallel"),
        name="attn_prompt",
    )(sinks, q2, kv2, kv2, kv2, kv2)
    return out.reshape(t * b, Q_COLS)


def _attn_sample_kernel(sinks_ref, q_ref, kn_ref, vn_ref, ck_ref, cv_ref, o_ref):
    bs, t = q_ref.shape[0], q_ref.shape[1]
    assert t & (t - 1) == 0
    nq, nk = GQA_GROUP * t, 2 * WINDOW
    r = lax.broadcasted_iota(I32, (nq, nk), 0)
    kj = lax.broadcasted_iota(I32, (nq, nk), 1)
    tq = jnp.bitwise_and(r, t - 1)
    valid = jnp.logical_and(kj > tq, kj <= tq + WINDOW)
    dist = (WINDOW + tq - kj).astype(F32)
    grp = jnp.right_shift(lax.broadcasted_iota(I32, (nq, 1), 0), t.bit_length() - 1)
    pad = jnp.zeros((bs, nk - WINDOW - t, HEAD_DIM), F32)
    for n in range(N_KV_A):
        cs = slice(n * HEAD_DIM, (n + 1) * HEAD_DIM)
        kb = jnp.concatenate([ck_ref[:, :, cs], kn_ref[:, :, cs], pad], axis=1).astype(BF16)
        vb = jnp.concatenate([cv_ref[:, :, cs], vn_ref[:, :, cs], pad], axis=1).astype(BF16)
        qn = jnp.concatenate([q_ref[:, :, (n * GQA_GROUP + g) * HEAD_DIM:(n * GQA_GROUP + g + 1) * HEAD_DIM]
                              for g in range(GQA_GROUP)], axis=1).astype(BF16)
        slope = jnp.zeros((nq, 1), F32)
        sink = jnp.zeros((nq, 1), F32)
        for g in range(GQA_GROUP):
            h = n * GQA_GROUP + g
            slope = jnp.where(grp == g, 2.0 ** -(h + 1), slope)
            sink = jnp.where(grp == g, sinks_ref[h], sink)
        s = jnp.einsum('bqd,bkd->bqk', qn, kb, preferred_element_type=F32)
        s = s * (HEAD_DIM ** -0.5) - (slope * dist)[None]
        s = jnp.where(valid[None], s, NEG_BIG)
        m = jnp.maximum(jnp.max(s, axis=2, keepdims=True), sink[None])
        p = jnp.exp(s - m)
        l = jnp.sum(p, axis=2, keepdims=True) + jnp.exp(sink[None] - m)
        o = jnp.einsum('bqk,bkd->bqd', p.astype(BF16), vb, preferred_element_type=F32) / l
        for g in range(GQA_GROUP):
            h = n * GQA_GROUP + g
            o_ref[:, :, h * HEAD_DIM:(h + 1) * HEAD_DIM] = o[:, g * t:(g + 1) * t, :]


def _attn_sample(q, kn, vn, ck, cv, sinks, bs=16):
    db, t = q.shape[0], q.shape[1]
    seq3 = lambda w: pl.BlockSpec((bs, t, w), lambda i: (i, 0, 0))
    cache = pl.BlockSpec((bs, WINDOW, KV_COLS), lambda i: (i, 0, 0))
    return pl.pallas_call(
        _attn_sample_kernel,
        grid=(db // bs,),
        in_specs=[pl.BlockSpec(memory_space=pltpu.SMEM), seq3(Q_COLS), seq3(KV_COLS), seq3(KV_COLS), cache, cache],
        out_specs=seq3(Q_COLS),
        out_shape=jax.ShapeDtypeStruct((db, t, Q_COLS), F32),
        compiler_params=_cparams("parallel"),
        name="attn_sample",
    )(sinks, q, kn, vn, ck, cv)


def _rwkv_prep_kernel(pb_ref, halo_ref, init_ref, mu_ref, w0_ref, wup_ref, a0_ref, aup_ref, gup_ref,
                      kk_ref, ka_ref, rk_ref, ones_ref,
                      r_o, w_o, k_o, v_o, kk_o, nkka_o, bonus_o, g_o, *, b):
    i = pl.program_id(0)
    pb = pb_ref[...]
    tm = pb.shape[0]
    halo = jnp.where(i == 0, init_ref[...], halo_ref[...])
    prev = halo if tm == b else jnp.concatenate([halo, pb[:tm - b]], axis=0)
    xs = pb + (prev - pb) * mu_ref[...]
    r = xs[:, :D_B]
    k = xs[:, D_B:2 * D_B]
    v = xs[:, 2 * D_B:3 * D_B]
    o1 = 3 * D_B
    wd = xs[:, o1:o1 + D_LORA_W]
    ad = xs[:, o1 + D_LORA_W:o1 + D_LORA_W + D_LORA_A]
    gd = xs[:, o1 + D_LORA_W + D_LORA_A:]
    z = -(w0_ref[...] + _dot(jnp.tanh(wd).astype(BF16), wup_ref[...]))
    softplus = jnp.maximum(z, 0.0) + jnp.log(1.0 + jnp.exp(-jnp.abs(z)))
    decay = jnp.exp(-jnp.exp(-softplus - 0.5))
    a = _sigmoid(a0_ref[...] + _dot(ad.astype(BF16), aup_ref[...]))
    g_o[...] = _dot(_sigmoid(gd).astype(BF16), gup_ref[...])
    ones = ones_ref[...]
    kk = k * kk_ref[...]
    kk = kk * lax.rsqrt(jnp.maximum(_dot2(kk * kk, ones), 1e-24))
    k2 = k * (1.0 + (a - 1.0) * ka_ref[...])
    bonus_o[...] = _dot2(r * k2 * rk_ref[...], ones) * v
    r_o[...] = r
    w_o[...] = decay
    k_o[...] = k2
    v_o[...] = v
    kk_o[...] = kk
    nkka_o[...] = -(kk * a)


def _rwkv_prep(pb, init, p, b, tm):
    n = pb.shape[0]
    row = lambda w: pl.BlockSpec((1, w), lambda i: (0, 0))
    full = lambda r, c: pl.BlockSpec((r, c), lambda i: (0, 0))
    tile = pl.BlockSpec((tm, D_B), lambda i: (i, 0))
    halo_blocks = tm // b
    return pl.pallas_call(
        functools.partial(_rwkv_prep_kernel, b=b),
        grid=(n // tm,),
        in_specs=[pl.BlockSpec((tm, D_B_IN), lambda i: (i, 0)),
                  pl.BlockSpec((b, D_B_IN), lambda i: (jnp.maximum(i * halo_blocks - 1, 0), 0)),
                  full(b, D_B_IN), row(D_B_IN), row(D_B), full(D_LORA_W, D_B), row(D_B), full(D_LORA_A, D_B),
                  full(D_LORA_G, D_B), row(D_B), row(D_B), row(D_B), full(D_B, D_B)],
        out_specs=[tile] * 8,
        out_shape=[jax.ShapeDtypeStruct((n, D_B), F32)] * 8,
        compiler_params=_cparams("arbitrary"),
        name="rwkv_prep",
    )(pb, pb, init, p['mu'], p['w0'], p['w_up'], p['a0'], p['a_up'], p['g_up'], p['k_k'], p['k_a'], p['r_k'],
      p['head_ones'])


def _wkv_scan_kernel(r_ref, w_ref, k_ref, kk_ref, nkka_ref, v_ref, s0_ref, o_ref, st_ref, s_scr):
    j = pl.program_id(1)

    @pl.when(j == 0)
    def _():
        s_scr[...] = s0_ref[0]

    tc = r_ref.shape[1]
    nacc = 4

    def step(s, carry):
        acc = [jnp.zeros((HEAD_DIM // 2, 2 * PAIRS), F32) for _ in range(nacc)]
        for kx in range(HEAD_DIM):
            acc[kx % nacc] = acc[kx % nacc] + s_scr[kx] * kk_ref[0, s, pl.ds(kx, 1), :]
        sa = (acc[0] + acc[1]) + (acc[2] + acc[3])
        vv = v_ref[0, s]
        oacc = [jnp.zeros((HEAD_DIM // 2, 2 * PAIRS), F32) for _ in range(nacc)]
        for kx in range(HEAD_DIM):
            row = pl.ds(kx, 1)
            sk = s_scr[kx] * w_ref[0, s, row, :] + sa * nkka_ref[0, s, row, :] + vv * k_ref[0, s, row, :]
            s_scr[kx] = sk
            oacc[kx % nacc] = oacc[kx % nacc] + sk * r_ref[0, s, row, :]
        o_ref[0, s] = (oacc[0] + oacc[1]) + (oacc[2] + oacc[3])
        return carry

    lax.fori_loop(0, tc, step, 0)

    @pl.when(j == pl.num_programs(1) - 1)
    def _():
        st_ref[0] = s_scr[...]


def _wkv_scan(r, w, k, kk, nkka, v, s0, tc):
    g, t = r.shape[0], r.shape[1]
    kspec = pl.BlockSpec((1, tc, HEAD_DIM, 2 * PAIRS), lambda gi, j: (gi, j, 0, 0))
    vspec = pl.BlockSpec((1, tc, HEAD_DIM // 2, 2 * PAIRS), lambda gi, j: (gi, j, 0, 0))
    sspec = pl.BlockSpec((1, HEAD_DIM, HEAD_DIM // 2, 2 * PAIRS), lambda gi, j: (gi, 0, 0, 0))
    return pl.pallas_call(
        _wkv_scan_kernel,
        grid=(g, t // tc),
        in_specs=[kspec] * 5 + [vspec, sspec],
        out_specs=[vspec, sspec],
        out_shape=[jax.ShapeDtypeStruct((g, t, HEAD_DIM // 2, 2 * PAIRS), F32),
                   jax.ShapeDtypeStruct((g, HEAD_DIM, HEAD_DIM // 2, 2 * PAIRS), F32)],
        scratch_shapes=[pltpu.VMEM((HEAD_DIM, HEAD_DIM // 2, 2 * PAIRS), F32)],
        compiler_params=_cparams("parallel", "arbitrary"),
        name="wkv_scan",
    )(r, w, k, kk, nkka, v, s0)


def _to_key_rows(x, b, t):
    g = b // 8
    x = x.reshape(t, g, 8, N_HEADS_B, HEAD_DIM).transpose(1, 0, 4, 2, 3).reshape(g, t, HEAD_DIM, PAIRS)
    return jnp.concatenate([x, x], axis=-1)


def _to_val_rows(x, b, t):
    g = b // 8
    x = x.reshape(t, g, 8, N_HEADS_B, 2, HEAD_DIM // 2).transpose(1, 0, 5, 4, 2, 3)
    return x.reshape(g, t, HEAD_DIM // 2, 2 * PAIRS)


def _from_val_rows(o, b, t):
    g = b // 8
    o = o.reshape(g, t, HEAD_DIM // 2, 2, 8, N_HEADS_B).transpose(1, 0, 4, 5, 3, 2)
    return o.reshape(t * b, D_B)


def _state_to_scan(s, b):
    g = b // 8
    s = s.reshape(g, 8, N_HEADS_B, 2, HEAD_DIM // 2, HEAD_DIM).transpose(0, 5, 4, 3, 1, 2)
    return s.reshape(g, HEAD_DIM, HEAD_DIM // 2, 2 * PAIRS)


def _state_from_scan(s, b):
    g = b // 8
    s = s.reshape(g, HEAD_DIM, HEAD_DIM // 2, 2, 8, N_HEADS_B).transpose(0, 4, 5, 3, 2, 1)
    return s.reshape(b, N_HEADS_B, HEAD_DIM, HEAD_DIM)


def _route_tile(x, nrm_ref, wr_ref, br_ref, cnt_scr, xn_o, idx_o, wts_o):
    tm = x.shape[0]
    xn = _rms(x, nrm_ref[...])
    hi, lo = _split_bf16(xn)
    xn_o[...] = xn
    wr = wr_ref[...]
    pa = _dot(hi, wr)
    pb = _dot(lo, wr)
    lg = pa[:, 0:32] + pa[:, 32:64] + pb[:, 0:32] + pb[:, 32:64] + br_ref[...]
    col = lambda c: lg[:, c:c + 1]
    c = [col(gx) for gx in range(N_EGROUPS)]
    m = jnp.maximum(jnp.maximum(c[0], c[1]), jnp.maximum(c[2], c[3]))
    den = jnp.exp(c[0] - m) + jnp.exp(c[1] - m) + jnp.exp(c[2] - m) + jnp.exp(c[3] - m)
    pg = 1.0 / den
    gi = jnp.where(c[0] >= m, 0, jnp.where(c[1] >= m, 1, jnp.where(c[2] >= m, 2, 3))).astype(I32)
    sel = []
    for e in range(EXP_PER_GROUP):
        sel.append(jnp.where(gi == 0, col(4 + e), jnp.where(gi == 1, col(8 + e),
                                                            jnp.where(gi == 2, col(12 + e), col(16 + e)))))
    v1 = jnp.maximum(jnp.maximum(sel[0], sel[1]), jnp.maximum(sel[2], sel[3]))
    i1 = jnp.where(sel[0] >= v1, 0, jnp.where(sel[1] >= v1, 1, jnp.where(sel[2] >= v1, 2, 3))).astype(I32)
    rest = [jnp.where(i1 == e, -jnp.inf, sel[e]) for e in range(EXP_PER_GROUP)]
    v2 = jnp.maximum(jnp.maximum(rest[0], rest[1]), jnp.maximum(rest[2], rest[3]))
    i2 = jnp.where(rest[0] >= v2, 0, jnp.where(rest[1] >= v2, 1, jnp.where(rest[2] >= v2, 2, 3))).astype(I32)
    tt = jnp.exp(v2 - v1)
    w1 = pg / (1.0 + tt)
    w2 = pg * tt / (1.0 + tt)
    e1 = gi * EXP_PER_GROUP + i1
    e2 = gi * EXP_PER_GROUP + i2
    lane = lax.broadcasted_iota(I32, (tm, N_EXPERTS), 1)
    oh1 = lane == e1
    oh2 = lane == e2
    oh = jnp.where(jnp.logical_or(oh1, oh2), 1.0, 0.0)
    ri = lax.broadcasted_iota(I32, (tm, tm), 0)
    ci = lax.broadcasted_iota(I32, (tm, tm), 1)
    ltri = jnp.where(ri > ci, 1.0, 0.0).astype(BF16)
    cnt = cnt_scr[0:1, 0:N_EXPERTS]
    pre = _dot(ltri, oh.astype(BF16)) + cnt
    rank1 = jnp.sum(jnp.where(oh1, pre, 0.0), axis=1, keepdims=True)
    rank2 = jnp.sum(jnp.where(oh2, pre, 0.0), axis=1, keepdims=True)
    cnt_scr[0:1, 0:N_EXPERTS] = cnt + jnp.sum(oh, axis=0, keepdims=True)
    idx_o[:, 0:1] = e1
    idx_o[:, 1:2] = e2
    idx_o[:, 2:3] = rank1.astype(I32)
    idx_o[:, 3:4] = rank2.astype(I32)
    lw = lax.broadcasted_iota(I32, (tm, 128), 1)
    wts_o[...] = jnp.where(lw == 0, w1, jnp.where(lw == 1, w2, 0.0))


def _route_out_specs(tm):
    return [pl.BlockSpec((tm, D_MODEL), lambda i: (i, 0)),
            pl.BlockSpec((tm, 4), lambda i: (i, 0)),
            pl.BlockSpec((tm, 128), lambda i: (i, 0)),
            pl.BlockSpec((8, 128), lambda i: (0, 0))]


def _route_out_shapes(n):
    return [jax.ShapeDtypeStruct((n, D_MODEL), F32),
            jax.ShapeDtypeStruct((n, 4), I32),
            jax.ShapeDtypeStruct((n, 128), F32),
            jax.ShapeDtypeStruct((8, 128), F32)]


def _route_in_specs():
    return [pl.BlockSpec((1, D_MODEL), lambda i: (0, 0)),
            pl.BlockSpec((D_MODEL, 128), lambda i: (0, 0)),
            pl.BlockSpec((1, 32), lambda i: (0, 0))]


def _mix0_out_kernel(o_ref, bonus_ref, g_ref, attn_ref, x_ref, lnw_ref, lnb_ref, avg_ref, wo_ref,
                     nrm_ref, wr_ref, br_ref,
                     x1_o, xn_o, idx_o, wts_o, cnt_o, cnt_scr):
    @pl.when(pl.program_id(0) == 0)
    def _():
        cnt_scr[...] = jnp.zeros_like(cnt_scr)

    avg = avg_ref[...]
    o = o_ref[...]
    d = o - _dot2(o, avg)
    var = _dot2(d * d, avg)
    on = d * lax.rsqrt(var + RWKV_GN_EPS) * lnw_ref[...] + lnb_ref[...]
    rout = ((on + bonus_ref[...]) * g_ref[...]).astype(BF16)
    y = _dot(attn_ref[...], wo_ref[:Q_COLS, :]) + _dot(rout, wo_ref[Q_COLS:, :]) + x_ref[...]
    x1_o[...] = y
    _route_tile(y, nrm_ref, wr_ref, br_ref, cnt_scr, xn_o, idx_o, wts_o)
    cnt_o[...] = cnt_scr[...]


def _mix0_out(o, bonus, g, attn, x, p, rp, tm):
    n = x.shape[0]
    tile = lambda w: pl.BlockSpec((tm, w), lambda i: (i, 0))
    row = lambda w: pl.BlockSpec((1, w), lambda i: (0, 0))
    full = lambda r, c: pl.BlockSpec((r, c), lambda i: (0, 0))
    return pl.pallas_call(
        _mix0_out_kernel,
        grid=(n // tm,),
        in_specs=[tile(D_B), tile(D_B), tile(D_B), tile(Q_COLS), tile(D_MODEL), row(D_B), row(D_B),
                  full(D_B, D_B), full(D_MODEL, D_MODEL)] + _route_in_specs(),
        out_specs=[tile(D_MODEL)] + _route_out_specs(tm),
        out_shape=[jax.ShapeDtypeStruct((n, D_MODEL), F32)] + _route_out_shapes(n),
        scratch_shapes=[pltpu.VMEM((8, 128), F32)],
        compiler_params=_cparams("arbitrary"),
        name="mix0_out",
    )(o, bonus, g, attn, x, p['ln_w'], p['ln_b'], p['head_avg'], p['w_out'], rp['norm'], rp['w'], rp['b'])


def _row_gather_start(idx_ref, base, n_rows, src_hbm, dst, sem):
    def body(r, carry):
        pltpu.make_async_copy(src_hbm.at[pl.ds(idx_ref[base + r], 1)], dst.at[pl.ds(r, 1)], sem).start()
        return carry

    lax.fori_loop(0, n_rows, body, 0, unroll=8)


def _row_gather_wait(dst, sem):
    pltpu.make_async_copy(dst, dst, sem).wait()


def _expert_kernel(te_ref, nu_ref, src_ref, x_hbm, wg_ref, wu_ref, wd_ref, o_ref, xbuf, sem):
    i = pl.program_id(0)
    nu = nu_ref[0]

    @pl.when(i == 0)
    def _():
        _row_gather_start(src_ref, 0, MOE_TILE, x_hbm, xbuf.at[0], sem.at[0])

    @pl.when(i + 1 < nu)
    def _():
        nxt = (i + 1) % 2
        _row_gather_start(src_ref, (i + 1) * MOE_TILE, MOE_TILE, x_hbm, xbuf.at[nxt], sem.at[nxt])

    @pl.when(i < nu)
    def _():
        cur = i % 2
        _row_gather_wait(xbuf.at[cur], sem.at[cur])
        x = xbuf[cur].astype(BF16)
        hg = _dot(x, wg_ref[0])
        hu = _dot(x, wu_ref[0])
        h = (hg * _sigmoid(hg)) * hu
        o_ref[...] = _dot(h.astype(BF16), wd_ref[0])

    @pl.when(i >= nu)
    def _():
        o_ref[...] = jnp.zeros_like(o_ref)


def _experts(xn, src, tile_expert, n_used, wg, wu, wd):
    r = src.shape[0]
    grid_spec = pltpu.PrefetchScalarGridSpec(
        num_scalar_prefetch=3,
        grid=(r // MOE_TILE,),
        in_specs=[pl.BlockSpec(memory_space=pl.ANY),
                  pl.BlockSpec((1, D_MODEL, D_FF_E), lambda i, te, nu, sr: (te[i], 0, 0)),
                  pl.BlockSpec((1, D_MODEL, D_FF_E), lambda i, te, nu, sr: (te[i], 0, 0)),
                  pl.BlockSpec((1, D_FF_E, D_MODEL), lambda i, te, nu, sr: (te[i], 0, 0))],
        out_specs=pl.BlockSpec((MOE_TILE, D_MODEL), lambda i, te, nu, sr: (i, 0)),
        scratch_shapes=[pltpu.VMEM((2, MOE_TILE, D_MODEL), F32), pltpu.SemaphoreType.DMA((2,))],
    )
    return pl.pallas_call(
        _expert_kernel,
        grid_spec=grid_spec,
        out_shape=jax.ShapeDtypeStruct((r, D_MODEL), F32),
        compiler_params=_cparams("arbitrary"),
        name="moe_experts",
    )(tile_expert, n_used, src, xn, wg, wu, wd)


def _moe(xn, idx, cnt, ep):
    n = xn.shape[0]
    rows = 2 * n + N_EXPERTS * MOE_TILE
    counts = cnt[0, :N_EXPERTS].astype(I32)
    padded = ((counts + MOE_TILE - 1) // MOE_TILE) * MOE_TILE
    ends = jnp.cumsum(padded)
    offs = ends - padded
    pos1 = offs[idx[:, 0]] + idx[:, 2]
    pos2 = offs[idx[:, 1]] + idx[:, 3]
    tok = jnp.arange(n, dtype=I32)
    src = jnp.zeros((rows,), I32).at[pos1].set(tok).at[pos2].set(tok)
    n_used = (ends[-1] // MOE_TILE).astype(I32)
    starts = jnp.arange(rows // MOE_TILE, dtype=I32) * MOE_TILE
    starts = jnp.minimum(starts, ends[-1] - 1)
    tile_expert = jnp.sum((starts[:, None] >= ends[None, :]).astype(I32), axis=1)
    tile_expert = jnp.minimum(tile_expert, N_EXPERTS - 1).astype(I32)
    out = _experts(xn, src, tile_expert, n_used.reshape(1), ep['wg'], ep['wu'], ep['wd'])
    return out, pos1, pos2


def _combine_kernel(p1_ref, p2_ref, x_ref, wts_ref, nrm_ref, out_hbm, y_o, gbuf, sem, *, final):
    i = pl.program_id(0)
    tm = x_ref.shape[0]

    def start(tile, slot):
        _row_gather_start(p1_ref, tile * tm, tm, out_hbm, gbuf.at[slot, 0], sem.at[slot])
        _row_gather_start(p2_ref, tile * tm, tm, out_hbm, gbuf.at[slot, 1], sem.at[slot])

    @pl.when(i == 0)
    def _():
        start(0, 0)

    @pl.when(i + 1 < pl.num_programs(0))
    def _():
        start(i + 1, (i + 1) % 2)

    cur = i % 2
    _row_gather_wait(gbuf.at[cur], sem.at[cur])
    wts = wts_ref[...]
    y = x_ref[...] + wts[:, 0:1] * gbuf[cur, 0] + wts[:, 1:2] * gbuf[cur, 1]
    if final:
        y = _rms(y, nrm_ref[...])
    y_o[...] = y


def _moe_combine(x, wts, out, pos1, pos2, nrm, final, tm=256):
    n = x.shape[0]
    tile = lambda w: pl.BlockSpec((tm, w), lambda i, p1, p2: (i, 0))
    grid_spec = pltpu.PrefetchScalarGridSpec(
        num_scalar_prefetch=2,
        grid=(n // tm,),
        in_specs=[tile(D_MODEL), tile(128), pl.BlockSpec((1, D_MODEL), lambda i, p1, p2: (0, 0)),
                  pl.BlockSpec(memory_space=pl.ANY)],
        out_specs=tile(D_MODEL),
        scratch_shapes=[pltpu.VMEM((2, 2, tm, D_MODEL), F32), pltpu.SemaphoreType.DMA((2,))],
    )
    return pl.pallas_call(
        functools.partial(_combine_kernel, final=final),
        grid_spec=grid_spec,
        out_shape=jax.ShapeDtypeStruct((n, D_MODEL), F32),
        compiler_params=_cparams("arbitrary"),
        name="moe_combine_final" if final else "moe_combine",
    )(pos1, pos2, x, wts, nrm, out)


def _gelu_tanh(x):
    return 0.5 * x * (1.0 + jnp.tanh(0.7978845608028654 * (x + 0.044715 * (x * x * x))))


def _mix1_kernel(x_ref, nmix_ref, bre_ref, bim_ref, are_ref, aim_ref, cre_ref, cim_ref,
                 dsk_ref, wo_ref, h0r_ref, h0i_ref, nrm_ref, wr_ref, br_ref,
                 x2_o, xn_o, idx_o, wts_o, cnt_o, hr_o, hi_o,
                 bur, bui, hr_scr, hi_scr, cnt_scr, *, b, cw):
    @pl.when(pl.program_id(0) == 0)
    def _():
        cnt_scr[...] = jnp.zeros_like(cnt_scr)
        hr_scr[...] = h0r_ref[...]
        hi_scr[...] = h0i_ref[...]

    x = x_ref[...]
    u = _rms(x, nmix_ref[...])
    ub = u.astype(BF16)
    nblk = bre_ref.shape[0]
    kin = D_MODEL // nblk
    kst = S5_STATE // nblk
    for cb in range(nblk):
        ucb = ub[:, cb * kin:(cb + 1) * kin]
        bur[:, cb * kst:(cb + 1) * kst] = _dot(ucb, bre_ref[cb])
        bui[:, cb * kst:(cb + 1) * kst] = _dot(ucb, bim_ref[cb])

    tc = x.shape[0] // b
    for c0 in range(0, S5_STATE, cw):
        cs = slice(c0, c0 + cw)
        ar = jnp.broadcast_to(are_ref[:, cs], (b, cw))
        ai = jnp.broadcast_to(aim_ref[:, cs], (b, cw))

        def step(s, carry, cs=cs, ar=ar, ai=ai):
            hr, hi = carry
            rows = pl.ds(pl.multiple_of(s * b, b), b)
            nr = ar * hr - ai * hi + bur[rows, cs]
            ni = ar * hi + ai * hr + bui[rows, cs]
            bur[rows, cs] = nr
            bui[rows, cs] = ni
            return nr, ni

        hr, hi = lax.fori_loop(0, tc, step, (hr_scr[:, cs], hi_scr[:, cs]))
        hr_scr[:, cs] = hr
        hi_scr[:, cs] = hi

    ych = []
    for cb in range(nblk):
        ss = slice(cb * kst, (cb + 1) * kst)
        ych.append(_dot(bur[:, ss].astype(BF16), cre_ref[cb]) - _dot(bui[:, ss].astype(BF16), cim_ref[cb]))
    y = jnp.concatenate(ych, axis=1) + dsk_ref[...] * u
    z = _dot(_gelu_tanh(y).astype(BF16), wo_ref[...])
    x2 = x + z[:, :D_MODEL] * _sigmoid(z[:, D_MODEL:])
    x2_o[...] = x2
    _route_tile(x2, nrm_ref, wr_ref, br_ref, cnt_scr, xn_o, idx_o, wts_o)
    cnt_o[...] = cnt_scr[...]
    hr_o[...] = hr_scr[...]
    hi_o[...] = hi_scr[...]


def _mix1(x, sp, rp, h0r, h0i, b, tr):
    n = x.shape[0]
    cw = 1024 if b == 8 else 128
    tile = lambda w: pl.BlockSpec((tr, w), lambda i: (i, 0))
    row = lambda w: pl.BlockSpec((1, w), lambda i: (0, 0))
    full = lambda *s: pl.BlockSpec(s, lambda i: (0,) * len(s))
    nblk = sp['b_re'].shape[0]
    return pl.pallas_call(
        functools.partial(_mix1_kernel, b=b, cw=cw),
        grid=(n // tr,),
        in_specs=[tile(D_MODEL), row(D_MODEL),
                  full(nblk, D_MODEL // nblk, S5_STATE // nblk), full(nblk, D_MODEL // nblk, S5_STATE // nblk),
                  row(S5_STATE), row(S5_STATE),
                  full(nblk, S5_STATE // nblk, D_MODEL // nblk), full(nblk, S5_STATE // nblk, D_MODEL // nblk),
                  row(D_MODEL), full(D_MODEL, 2 * D_MODEL), full(b, S5_STATE), full(b, S5_STATE)] + _route_in_specs(),
        out_specs=[tile(D_MODEL)] + _route_out_specs(tr) + [full(b, S5_STATE), full(b, S5_STATE)],
        out_shape=[jax.ShapeDtypeStruct((n, D_MODEL), F32)] + _route_out_shapes(n)
                  + [jax.ShapeDtypeStruct((b, S5_STATE), F32)] * 2,
        scratch_shapes=[pltpu.VMEM((tr, S5_STATE), F32), pltpu.VMEM((tr, S5_STATE), F32),
                        pltpu.VMEM((b, S5_STATE), F32), pltpu.VMEM((b, S5_STATE), F32),
                        pltpu.VMEM((8, 128), F32)],
        compiler_params=_cparams("arbitrary"),
        name="mix1",
    )(x, sp['norm'], sp['b_re'], sp['b_im'], sp['a_re'], sp['a_im'], sp['c_re'], sp['c_im'],
      sp['d'], sp['w_out'], h0r, h0i, rp['norm'], rp['w'], rp['b'])


def _router_params(norm, w_rc, b_rc, w_rf, b_rf):
    w = jnp.concatenate([w_rc, w_rf.reshape(D_MODEL, N_EXPERTS), jnp.zeros((D_MODEL, 12), F32)], axis=1)
    hi = w.astype(BF16)
    lo = (w - hi.astype(F32)).astype(BF16)
    wcat = jnp.concatenate([hi, lo, jnp.zeros((D_MODEL, 64), BF16)], axis=1)
    bias = jnp.concatenate([b_rc, b_rf.reshape(-1), jnp.zeros((12,), F32)]).reshape(1, 32)
    return {'norm': norm.reshape(1, D_MODEL), 'w': wcat, 'b': bias}


def _expert_params(wg, wu, wd):
    return {'wg': wg.astype(BF16), 'wu': wu.astype(BF16), 'wd': wd.astype(BF16)}


def _s5_params(norm, a_re, a_im, log_dt, b_re, b_im, c_re, c_im, d_skip, w_out, nblk=8):
    dt = jnp.exp(log_dt)
    mag = jnp.exp(dt * a_re)
    ab_re, ab_im = mag * jnp.cos(dt * a_im), mag * jnp.sin(dt * a_im)
    den = a_re * a_re + a_im * a_im
    f_re = ((ab_re - 1.0) * a_re + ab_im * a_im) / den
    f_im = (ab_im * a_re - (ab_re - 1.0) * a_im) / den
    bb_re = f_re[..., None] * b_re - f_im[..., None] * b_im
    bb_im = f_re[..., None] * b_im + f_im[..., None] * b_re
    gpb = S5_GROUPS // nblk
    eye = jnp.eye(gpb, dtype=F32)

    def in_blocks(bb):
        bb = bb.reshape(nblk, gpb, S5_P, S5_CH)
        w = jnp.einsum('ngpc,gh->ngchp', bb, eye)
        return w.reshape(nblk, gpb * S5_CH, gpb * S5_P).astype(BF16)

    def out_blocks(cc):
        cc = cc.reshape(nblk, gpb, S5_CH, S5_P)
        w = jnp.einsum('ngcp,gh->ngphc', cc, eye)
        return w.reshape(nblk, gpb * S5_P, gpb * S5_CH).astype(BF16)

    return {'norm': norm.reshape(1, D_MODEL), 'b_re': in_blocks(bb_re), 'b_im': in_blocks(bb_im),
            'a_re': ab_re.reshape(1, S5_STATE), 'a_im': ab_im.reshape(1, S5_STATE),
            'c_re': out_blocks(c_re), 'c_im': out_blocks(c_im), 'd': d_skip.reshape(1, D_MODEL),
            'w_out': w_out.astype(BF16)}


def _head_block(value):
    hid = jnp.arange(D_B, dtype=I32) // HEAD_DIM
    return jnp.where(hid[:, None] == hid[None, :], value, 0.0).astype(BF16)


def _run_group(x_tm, b, t, cache_k, cache_v, shift0, wkv0, h0r, h0i, pr):
    n = b * t
    prompt = cache_k is None
    tm = 256 if prompt else 128
    q, kv, pb = _in_proj(x_tm, pr['l0_norm'], pr['l0_w_in'], tm)

    if prompt:
        attn = _attn_prompt(q, kv, pr['sinks'], b, t)
        kv3 = kv.reshape(t, b, 2, N_KV_A, HEAD_DIM)
        new_k = kv3[t - WINDOW:, :, 0].transpose(1, 0, 2, 3)
        new_v = kv3[t - WINDOW:, :, 1].transpose(1, 0, 2, 3)
        init = jnp.zeros((b, D_B_IN), F32)
    else:
        qs = q.astype(F32).reshape(t, b, Q_COLS).transpose(1, 0, 2)
        kvs = kv.reshape(t, b, 2 * KV_COLS).transpose(1, 0, 2)
        kn, vn = kvs[..., :KV_COLS], kvs[..., KV_COLS:]
        ck = cache_k.reshape(b, WINDOW, KV_COLS)
        cv = cache_v.reshape(b, WINDOW, KV_COLS)
        attn = _attn_sample(qs, kn, vn, ck, cv, pr['sinks'])
        attn = attn.transpose(1, 0, 2).reshape(n, Q_COLS).astype(BF16)
        new_k = jnp.concatenate([ck[:, t:], kn], axis=1).reshape(b, WINDOW, N_KV_A, HEAD_DIM)
        new_v = jnp.concatenate([cv[:, t:], vn], axis=1).reshape(b, WINDOW, N_KV_A, HEAD_DIM)
        init = shift0
    new_shift = pb[n - b:]

    r, w, k, v, kk, nkka, bonus, g = _rwkv_prep(pb, init, pr['rw'], b, tm)
    tc = 32 if prompt else t
    s0 = jnp.zeros((b // 8, HEAD_DIM, HEAD_DIM // 2, 2 * PAIRS), F32) if prompt else _state_to_scan(wkv0, b)
    o, s_fin = _wkv_scan(_to_key_rows(r, b, t), _to_key_rows(w, b, t), _to_key_rows(k, b, t),
                         _to_key_rows(kk, b, t), _to_key_rows(nkka, b, t), _to_val_rows(v, b, t), s0, tc)
    o = _from_val_rows(o, b, t)
    new_wkv = _state_from_scan(s_fin, b)

    x1, xn, idx, wts, cnt = _mix0_out(o, bonus, g, attn, x_tm, pr['rw'], pr['l0_route'], 256)
    out, pos1, pos2 = _moe(xn, idx, cnt, pr['l0_exp'])
    x1 = _moe_combine(x1, wts, out, pos1, pos2, pr['final_norm'], False)

    x2, xn, idx, wts, cnt, hr, hi = _mix1(x1, pr['s5'], pr['l1_route'], h0r, h0i, b, 256)
    out, pos1, pos2 = _moe(xn, idx, cnt, pr['l1_exp'])
    y = _moe_combine(x2, wts, out, pos1, pos2, pr['final_norm'], True)
    return (y, new_k, new_v, new_shift, new_wkv,
            hr.reshape(b, S5_GROUPS, S5_P), hi.reshape(b, S5_GROUPS, S5_P))


def kernel(x_prompt, x_sample, cache_win_k, cache_win_v, state_shift, state_wkv, state_s5_re, state_s5_im,
           l0_norm_mix, l0_w_in, l0_sinks, l0_mu, l0_w0, l0_w_lora_up, l0_a0, l0_a_lora_up, l0_g_lora_up,
           l0_k_k, l0_k_a, l0_r_k, l0_ln_w, l0_ln_b, l0_w_out,
           l0_norm_ffn, l0_router_coarse, l0_bias_coarse, l0_router_fine, l0_bias_fine,
           l0_exp_gate, l0_exp_up, l0_exp_down,
           l1_norm_mix, l1_s5_a_re, l1_s5_a_im, l1_s5_log_dt, l1_s5_b_re, l1_s5_b_im, l1_s5_c_re, l1_s5_c_im,
           l1_s5_d, l1_w_out,
           l1_norm_ffn, l1_router_coarse, l1_bias_coarse, l1_router_fine, l1_bias_fine,
           l1_exp_gate, l1_exp_up, l1_exp_down,
           final_norm):
    row = lambda z: z.reshape(1, -1)
    pr = {
        'l0_norm': row(l0_norm_mix), 'l0_w_in': l0_w_in.astype(BF16), 'sinks': l0_sinks,
        'rw': {'mu': row(l0_mu), 'w0': row(l0_w0), 'w_up': l0_w_lora_up.astype(BF16), 'a0': row(l0_a0),
               'a_up': l0_a_lora_up.astype(BF16), 'g_up': l0_g_lora_up.astype(BF16), 'k_k': row(l0_k_k),
               'k_a': row(l0_k_a), 'r_k': row(l0_r_k), 'ln_w': row(l0_ln_w), 'ln_b': row(l0_ln_b),
               'head_ones': _head_block(1.0), 'head_avg': _head_block(1.0 / HEAD_DIM),
               'w_out': l0_w_out.astype(BF16)},
        'l0_route': _router_params(l0_norm_ffn, l0_router_coarse, l0_bias_coarse, l0_router_fine, l0_bias_fine),
        'l0_exp': _expert_params(l0_exp_gate, l0_exp_up, l0_exp_down),
        's5': _s5_params(l1_norm_mix, l1_s5_a_re, l1_s5_a_im, l1_s5_log_dt, l1_s5_b_re, l1_s5_b_im,
                         l1_s5_c_re, l1_s5_c_im, l1_s5_d, l1_w_out),
        'l1_route': _router_params(l1_norm_ffn, l1_router_coarse, l1_bias_coarse, l1_router_fine, l1_bias_fine),
        'l1_exp': _expert_params(l1_exp_gate, l1_exp_up, l1_exp_down),
        'final_norm': row(final_norm),
    }
    bp, tp = x_prompt.shape[0], x_prompt.shape[1]
    bs, ts = x_sample.shape[0], x_sample.shape[1]
    xp = x_prompt.transpose(1, 0, 2).reshape(bp * tp, D_MODEL)
    xs = x_sample.transpose(1, 0, 2).reshape(bs * ts, D_MODEL)
    zero_state = jnp.zeros((bp, S5_STATE), F32)
    yp, pk, pv, psh, pwkv, pre, pim = _run_group(xp, bp, tp, None, None, None, None, zero_state, zero_state, pr)
    ys, sk, sv, ssh, swkv, sre, sim = _run_group(
        xs, bs, ts, cache_win_k, cache_win_v, state_shift, state_wkv,
        state_s5_re.reshape(bs, S5_STATE), state_s5_im.reshape(bs, S5_STATE), pr)
    y_prompt = yp.reshape(tp, bp, D_MODEL).transpose(1, 0, 2)
    y_sample = ys.reshape(ts, bs, D_MODEL).transpose(1, 0, 2)
    return (y_prompt, y_sample, pk, pv, psh, pwkv, pre, pim, sk, sv, ssh, swkv, sre, sim)
```

```python
import functools

import jax
import jax.numpy as jnp
from jax import lax
from jax.experimental import pallas as pl
from jax.experimental.pallas import tpu as pltpu

F32 = jnp.float32
BF16 = jnp.bfloat16
I32 = jnp.int32

D_MODEL = 1024
HEAD_DIM = 64
N_HEADS_A = 8
N_KV_A = 2
GQA_GROUP = 4
WINDOW = 128
Q_COLS = 512
KV_COLS = 128
D_A_IN = 768
N_HEADS_B = 8
D_B = 512
D_LORA_W = 64
D_LORA_A = 64
D_LORA_G = 128
D_B_IN = 1792
D_IN0 = 2560
RWKV_GN_EPS = 64e-5
S5_CH = 16
S5_GROUPS = 64
S5_P = 64
S5_STATE = S5_GROUPS * S5_P
N_EGROUPS = 4
EXP_PER_GROUP = 4
N_EXPERTS = 16
D_FF_E = 512
RMS_EPS = 1e-5
NEG_BIG = -1e30
PAIRS = 64
SEQ_PER_GROUP = PAIRS // N_HEADS_B
MOE_TILE = 256
VMEM_LIMIT = 56 * 1024 * 1024


def _cparams(*sem):
    return pltpu.CompilerParams(dimension_semantics=sem, vmem_limit_bytes=VMEM_LIMIT)


def _dot(a, b):
    return jnp.dot(a, b, preferred_element_type=F32)


def _split_bf16(x):
    hi = x.astype(BF16)
    lo = (x - hi.astype(F32)).astype(BF16)
    return hi, lo


def _dot2(x, w):
    hi, lo = _split_bf16(x)
    return _dot(hi, w) + _dot(lo, w)


def _rms(x, g):
    return x * lax.rsqrt(jnp.mean(x * x, axis=-1, keepdims=True) + RMS_EPS) * g


def _sigmoid(x):
    return 1.0 / (1.0 + jnp.exp(-x))


def _in_proj_kernel(x_ref, g_ref, w_ref, q_ref, kv_ref, pb_ref):
    xn = _rms(x_ref[...], g_ref[...]).astype(BF16)
    q_ref[...] = _dot(xn, w_ref[:, :Q_COLS]).astype(BF16)
    kv_ref[...] = _dot(xn, w_ref[:, Q_COLS:D_A_IN])
    pb_ref[...] = _dot(xn, w_ref[:, D_A_IN:])


def _in_proj(x, g, w_bf16, tm):
    n = x.shape[0]
    return pl.pallas_call(
        _in_proj_kernel,
        grid=(n // tm,),
        in_specs=[pl.BlockSpec((tm, D_MODEL), lambda i: (i, 0)),
                  pl.BlockSpec((1, D_MODEL), lambda i: (0, 0)),
                  pl.BlockSpec((D_MODEL, D_IN0), lambda i: (0, 0))],
        out_specs=[pl.BlockSpec((tm, Q_COLS), lambda i: (i, 0)),
                   pl.BlockSpec((tm, 2 * KV_COLS), lambda i: (i, 0)),
                   pl.BlockSpec((tm, D_B_IN), lambda i: (i, 0))],
        out_shape=[jax.ShapeDtypeStruct((n, Q_COLS), BF16),
                   jax.ShapeDtypeStruct((n, 2 * KV_COLS), F32),
                   jax.ShapeDtypeStruct((n, D_B_IN), F32)],
        compiler_params=_cparams("parallel"),
        name="in_proj",
    )(x, g, w_bf16)


def _attn_prompt_kernel(sinks_ref, q_ref, kc_ref, kp_ref, vc_ref, vp_ref, o_ref):
    j = pl.program_id(1)
    qi = lax.broadcasted_iota(I32, (WINDOW, 2 * WINDOW), 0)
    kj = lax.broadcasted_iota(I32, (WINDOW, 2 * WINDOW), 1)
    valid = jnp.logical_and(kj > qi, kj <= qi + WINDOW)
    valid = jnp.logical_and(valid, jnp.logical_or(kj >= WINDOW, j > 0))
    dist = (WINDOW + qi - kj).astype(F32)
    for n in range(N_KV_A):
        cs = slice(n * HEAD_DIM, (n + 1) * HEAD_DIM)
        kb = jnp.concatenate([kp_ref[:, cs], kc_ref[:, cs]], axis=0).astype(BF16)
        vb = jnp.concatenate([vp_ref[:, cs], vc_ref[:, cs]], axis=0).astype(BF16)
        for g in range(GQA_GROUP):
            h = n * GQA_GROUP + g
            hs = slice(h * HEAD_DIM, (h + 1) * HEAD_DIM)
            s = lax.dot_general(q_ref[:, hs], kb, (((1,), (1,)), ((), ())), preferred_element_type=F32)
            s = s * (HEAD_DIM ** -0.5) - (2.0 ** -(h + 1)) * dist
            s = jnp.where(valid, s, NEG_BIG)
            sink = sinks_ref[h]
            m = jnp.maximum(jnp.max(s, axis=1, keepdims=True), sink)
            p = jnp.exp(s - m)
            l = jnp.sum(p, axis=1, keepdims=True) + jnp.exp(sink - m)
            o = _dot(p.astype(BF16), vb) / l
            o_ref[:, hs] = o.astype(BF16)


def _attn_prompt(q, kv, sinks, b, t):
    q2 = q.reshape(t, b * Q_COLS)
    kv2 = kv.reshape(t, b * 2 * KV_COLS)
    prev = lambda bi, j: jnp.maximum(j - 1, 0)
    out = pl.pallas_call(
        _attn_prompt_kernel,
        grid=(b, t // WINDOW),
        in_specs=[pl.BlockSpec(memory_space=pltpu.SMEM),
                  pl.BlockSpec((WINDOW, Q_COLS), lambda bi, j: (j, bi)),
                  pl.BlockSpec((WINDOW, KV_COLS), lambda bi, j: (j, 2 * bi)),
                  pl.BlockSpec((WINDOW, KV_COLS), lambda bi, j: (prev(bi, j), 2 * bi)),
                  pl.BlockSpec((WINDOW, KV_COLS), lambda bi, j: (j, 2 * bi + 1)),
                  pl.BlockSpec((WINDOW, KV_COLS), lambda bi, j: (prev(bi, j), 2 * bi + 1))],
        out_specs=pl.BlockSpec((WINDOW, Q_COLS), lambda bi, j: (j, bi)),
        out_shape=jax.ShapeDtypeStruct((t, b * Q_COLS), BF16),
        compiler_params=_cparams("parallel", "parallel"),
        name="attn_prompt",
    )(sinks, q2, kv2, kv2, kv2, kv2)
    return out.reshape(t * b, Q_COLS)


def _attn_sample_kernel(sinks_ref, q_ref, kn_ref, vn_ref, ck_ref, cv_ref, o_ref):
    bs, t = q_ref.shape[0], q_ref.shape[1]
    assert t & (t - 1) == 0
    nq, nk = GQA_GROUP * t, 2 * WINDOW
    r = lax.broadcasted_iota(I32, (nq, nk), 0)
    kj = lax.broadcasted_iota(I32, (nq, nk), 1)
    tq = jnp.bitwise_and(r, t - 1)
    valid = jnp.logical_and(kj > tq, kj <= tq + WINDOW)
    dist = (WINDOW + tq - kj).astype(F32)
    grp = jnp.right_shift(lax.broadcasted_iota(I32, (nq, 1), 0), t.bit_length() - 1)
    pad = jnp.zeros((bs, nk - WINDOW - t, HEAD_DIM), F32)
    for n in range(N_KV_A):
        cs = slice(n * HEAD_DIM, (n + 1) * HEAD_DIM)
        kb = jnp.concatenate([ck_ref[:, :, cs], kn_ref[:, :, cs], pad], axis=1).astype(BF16)
        vb = jnp.concatenate([cv_ref[:, :, cs], vn_ref[:, :, cs], pad], axis=1).astype(BF16)
        qn = jnp.concatenate([q_ref[:, :, (n * GQA_GROUP + g) * HEAD_DIM:(n * GQA_GROUP + g + 1) * HEAD_DIM]
                              for g in range(GQA_GROUP)], axis=1).astype(BF16)
        slope = jnp.zeros((nq, 1), F32)
        sink = jnp.zeros((nq, 1), F32)
        for g in range(GQA_GROUP):
            h = n * GQA_GROUP + g
            slope = jnp.where(grp == g, 2.0 ** -(h + 1), slope)
            sink = jnp.where(grp == g, sinks_ref[h], sink)
        s = jnp.einsum('bqd,bkd->bqk', qn, kb, preferred_element_type=F32)
        s = s * (HEAD_DIM ** -0.5) - (slope * dist)[None]
        s = jnp.where(valid[None], s, NEG_BIG)
        m = jnp.maximum(jnp.max(s, axis=2, keepdims=True), sink[None])
        p = jnp.exp(s - m)
        l = jnp.sum(p, axis=2, keepdims=True) + jnp.exp(sink[None] - m)
        o = jnp.einsum('bqk,bkd->bqd', p.astype(BF16), vb, preferred_element_type=F32) / l
        for g in range(GQA_GROUP):
            h = n * GQA_GROUP + g
            o_ref[:, :, h * HEAD_DIM:(h + 1) * HEAD_DIM] = o[:, g * t:(g + 1) * t, :]


def _attn_sample(q, kn, vn, ck, cv, sinks, bs=16):
    db, t = q.shape[0], q.shape[1]
    seq3 = lambda w: pl.BlockSpec((bs, t, w), lambda i: (i, 0, 0))
    cache = pl.BlockSpec((bs, WINDOW, KV_COLS), lambda i: (i, 0, 0))
    return pl.pallas_call(
        _attn_sample_kernel,
        grid=(db // bs,),
        in_specs=[pl.BlockSpec(memory_space=pltpu.SMEM), seq3(Q_COLS), seq3(KV_COLS), seq3(KV_COLS), cache, cache],
        out_specs=seq3(Q_COLS),
        out_shape=jax.ShapeDtypeStruct((db, t, Q_COLS), F32),
        compiler_params=_cparams("parallel"),
        name="attn_sample",
    )(sinks, q, kn, vn, ck, cv)


def _rwkv_prep_kernel(pb_ref, halo_ref, init_ref, mu_ref, w0_ref, wup_ref, a0_ref, aup_ref, gup_ref,
                      kk_ref, ka_ref, rk_ref, ones_ref,
                      r_o, w_o, k_o, v_o, kk_o, nkka_o, bonus_o, g_o, *, b):
    i = pl.program_id(0)
    pb = pb_ref[...]
    tm = pb.shape[0]
    halo = jnp.where(i == 0, init_ref[...], halo_ref[...])
    prev = halo if tm == b else jnp.concatenate([halo, pb[:tm - b]], axis=0)
    xs = pb + (prev - pb) * mu_ref[...]
    r = xs[:, :D_B]
    k = xs[:, D_B:2 * D_B]
    v = xs[:, 2 * D_B:3 * D_B]
    o1 = 3 * D_B
    wd = xs[:, o1:o1 + D_LORA_W]
    ad = xs[:, o1 + D_LORA_W:o1 + D_LORA_W + D_LORA_A]
    gd = xs[:, o1 + D_LORA_W + D_LORA_A:]
    z = -(w0_ref[...] + _dot(jnp.tanh(wd).astype(BF16), wup_ref[...]))
    softplus = jnp.maximum(z, 0.0) + jnp.log(1.0 + jnp.exp(-jnp.abs(z)))
    decay = jnp.exp(-jnp.exp(-softplus - 0.5))
    a = _sigmoid(a0_ref[...] + _dot(ad.astype(BF16), aup_ref[...]))
    g_o[...] = _dot(_sigmoid(gd).astype(BF16), gup_ref[...])
    ones = ones_ref[...]
    kk = k * kk_ref[...]
    kk = kk * lax.rsqrt(jnp.maximum(_dot2(kk * kk, ones), 1e-24))
    k2 = k * (1.0 + (a - 1.0) * ka_ref[...])
    bonus_o[...] = _dot2(r * k2 * rk_ref[...], ones) * v

    half = HEAD_DIM // 2
    lane8 = lax.broadcasted_iota(I32, (SEQ_PER_GROUP, 128), 1)
    low8 = lane8 < HEAD_DIM
    first_copy = jnp.bitwise_and(lax.broadcasted_iota(I32, (half, 128), 1), SEQ_PER_GROUP) == 0
    pairs = ((r, decay, r_o, w_o), (k2, kk, k_o, kk_o), (-(kk * a), v, nkka_o, None))
    for u in range(tm // SEQ_PER_GROUP):
        rows = slice(u * SEQ_PER_GROUP, (u + 1) * SEQ_PER_GROUP)
        for x, y, x_o, y_o in pairs:
            xu, yu = x[rows], y[rows]
            pieces = []
            for h in range(N_HEADS_B):
                cs = slice((h // 2) * 128, (h // 2 + 1) * 128)
                if h % 2 == 0:
                    p = jnp.where(low8, xu[:, cs], pltpu.roll(yu[:, cs], HEAD_DIM, 1))
                else:
                    p = jnp.where(low8, pltpu.roll(xu[:, cs], HEAD_DIM, 1), yu[:, cs])
                pieces += [p, p]
            tr = jnp.concatenate(pieces, axis=0).T
            x_o[u] = tr[:HEAD_DIM]
            if y_o is not None:
                y_o[u] = tr[HEAD_DIM:]
            else:
                v_o[u] = jnp.where(first_copy, tr[HEAD_DIM:HEAD_DIM + half], tr[HEAD_DIM + half:])


def _rwkv_prep(pb, init, p, b, tm):
    n = pb.shape[0]
    units = n // SEQ_PER_GROUP
    tu = tm // SEQ_PER_GROUP
    half = HEAD_DIM // 2
    row = lambda w: pl.BlockSpec((1, w), lambda i: (0, 0))
    full = lambda r, c: pl.BlockSpec((r, c), lambda i: (0, 0))
    tile = pl.BlockSpec((tm, D_B), lambda i: (i, 0))
    kspec = pl.BlockSpec((tu, HEAD_DIM, 128), lambda i: (i, 0, 0))
    vspec = pl.BlockSpec((tu, half, 128), lambda i: (i, 0, 0))
    kshape = jax.ShapeDtypeStruct((units, HEAD_DIM, 128), F32)
    halo_blocks = tm // b
    return pl.pallas_call(
        functools.partial(_rwkv_prep_kernel, b=b),
        grid=(n // tm,),
        in_specs=[pl.BlockSpec((tm, D_B_IN), lambda i: (i, 0)),
                  pl.BlockSpec((b, D_B_IN), lambda i: (jnp.maximum(i * halo_blocks - 1, 0), 0)),
                  full(b, D_B_IN), row(D_B_IN), row(D_B), full(D_LORA_W, D_B), row(D_B), full(D_LORA_A, D_B),
                  full(D_LORA_G, D_B), row(D_B), row(D_B), row(D_B), full(D_B, D_B)],
        out_specs=[kspec, kspec, kspec, vspec, kspec, kspec, tile, tile],
        out_shape=[kshape, kshape, kshape, jax.ShapeDtypeStruct((units, half, 128), F32), kshape, kshape,
                   jax.ShapeDtypeStruct((n, D_B), F32), jax.ShapeDtypeStruct((n, D_B), F32)],
        compiler_params=_cparams("arbitrary"),
        name="rwkv_prep",
    )(pb, pb, init, p['mu'], p['w0'], p['w_up'], p['a0'], p['a_up'], p['g_up'], p['k_k'], p['k_a'], p['r_k'],
      p['head_ones'])


def _wkv_scan_kernel(r_ref, w_ref, k_ref, kk_ref, nkka_ref, v_ref, s0_ref, o_ref, st_ref, s_scr):
    j = pl.program_id(1)

    @pl.when(j == 0)
    def _():
        s_scr[...] = s0_ref[0]

    tc = r_ref.shape[0]
    nsub = (HEAD_DIM // 2) // 8

    def bcast(ref, s, kx):
        return jnp.broadcast_to(ref[s, pl.ds(kx, 1), :], (8, 128))

    acc0 = [[jnp.zeros((8, 128), F32) for _ in range(2)] for _ in range(nsub)]
    for kx in range(HEAD_DIM):
        kkr = bcast(kk_ref, 0, kx)
        for i in range(nsub):
            acc0[i][kx % 2] = acc0[i][kx % 2] + s_scr[kx, 8 * i:8 * i + 8, :] * kkr

    def step(s, sa):
        nxt = jnp.minimum(s + 1, tc - 1)
        vv = [v_ref[s, 8 * i:8 * i + 8, :] for i in range(nsub)]
        oacc = [[jnp.zeros((8, 128), F32) for _ in range(2)] for _ in range(nsub)]
        nacc = [[jnp.zeros((8, 128), F32) for _ in range(2)] for _ in range(nsub)]
        for kx in range(HEAD_DIM):
            rr, wr, kr = bcast(r_ref, s, kx), bcast(w_ref, s, kx), bcast(k_ref, s, kx)
            nk, kkn = bcast(nkka_ref, s, kx), bcast(kk_ref, nxt, kx)
            for i in range(nsub):
                rows = slice(8 * i, 8 * i + 8)
                sk = s_scr[kx, rows, :] * wr + sa[i] * nk + vv[i] * kr
                s_scr[kx, rows, :] = sk
                oacc[i][kx % 2] = oacc[i][kx % 2] + sk * rr
                nacc[i][kx % 2] = nacc[i][kx % 2] + sk * kkn
        o_ref[s] = jnp.concatenate([a[0] + a[1] for a in oacc], axis=0)
        return [a[0] + a[1] for a in nacc]

    lax.fori_loop(0, tc, step, [a[0] + a[1] for a in acc0])

    @pl.when(j == pl.num_programs(1) - 1)
    def _():
        st_ref[0] = s_scr[...]


def _wkv_scan(r, w, k, kk, nkka, v, s0, b, t, tc):
    g = b // SEQ_PER_GROUP
    half = HEAD_DIM // 2
    kview = lambda z: z.reshape(t, g, HEAD_DIM, 128)
    kspec = pl.BlockSpec((tc, None, HEAD_DIM, 128), lambda gi, j: (j, gi, 0, 0))
    vspec = pl.BlockSpec((tc, None, half, 128), lambda gi, j: (j, gi, 0, 0))
    sspec = pl.BlockSpec((1, HEAD_DIM, half, 128), lambda gi, j: (gi, 0, 0, 0))
    o, st = pl.pallas_call(
        _wkv_scan_kernel,
        grid=(g, t // tc),
        in_specs=[kspec] * 5 + [vspec, sspec],
        out_specs=[vspec, sspec],
        out_shape=[jax.ShapeDtypeStruct((t, g, half, 128), F32),
                   jax.ShapeDtypeStruct((g, HEAD_DIM, half, 128), F32)],
        scratch_shapes=[pltpu.VMEM((HEAD_DIM, half, 128), F32)],
        compiler_params=_cparams("parallel", "arbitrary"),
        name="wkv_scan",
    )(kview(r), kview(w), kview(k), kview(kk), kview(nkka), v.reshape(t, g, half, 128), s0)
    return o.reshape(t * g, half, 128), st


def _state_to_scan(s, b):
    g = b // SEQ_PER_GROUP
    s = s.reshape(g, SEQ_PER_GROUP, N_HEADS_B, 2, HEAD_DIM // 2, HEAD_DIM).transpose(0, 5, 4, 2, 3, 1)
    return s.reshape(g, HEAD_DIM, HEAD_DIM // 2, 128)


def _state_from_scan(s, b):
    g = b // SEQ_PER_GROUP
    s = s.reshape(g, HEAD_DIM, HEAD_DIM // 2, N_HEADS_B, 2, SEQ_PER_GROUP).transpose(0, 5, 3, 4, 2, 1)
    return s.reshape(b, N_HEADS_B, HEAD_DIM, HEAD_DIM)


def _route_tile(x, nrm_ref, wr_ref, br_ref, cnt_scr, xn_o, idx_o, wts_o):
    tm = x.shape[0]
    xn = _rms(x, nrm_ref[...])
    hi, lo = _split_bf16(xn)
    xn_o[...] = xn
    wr = wr_ref[...]
    pa = _dot(hi, wr)
    pb = _dot(lo, wr)
    lg = pa[:, 0:32] + pa[:, 32:64] + pb[:, 0:32] + pb[:, 32:64] + br_ref[...]
    col = lambda c: lg[:, c:c + 1]
    c = [col(gx) for gx in range(N_EGROUPS)]
    m = jnp.maximum(jnp.maximum(c[0], c[1]), jnp.maximum(c[2], c[3]))
    den = jnp.exp(c[0] - m) + jnp.exp(c[1] - m) + jnp.exp(c[2] - m) + jnp.exp(c[3] - m)
    pg = 1.0 / den
    gi = jnp.where(c[0] >= m, 0, jnp.where(c[1] >= m, 1, jnp.where(c[2] >= m, 2, 3))).astype(I32)
    sel = []
    for e in range(EXP_PER_GROUP):
        sel.append(jnp.where(gi == 0, col(4 + e), jnp.where(gi == 1, col(8 + e),
                                                            jnp.where(gi == 2, col(12 + e), col(16 + e)))))
    v1 = jnp.maximum(jnp.maximum(sel[0], sel[1]), jnp.maximum(sel[2], sel[3]))
    i1 = jnp.where(sel[0] >= v1, 0, jnp.where(sel[1] >= v1, 1, jnp.where(sel[2] >= v1, 2, 3))).astype(I32)
    rest = [jnp.where(i1 == e, -jnp.inf, sel[e]) for e in range(EXP_PER_GROUP)]
    v2 = jnp.maximum(jnp.maximum(rest[0], rest[1]), jnp.maximum(rest[2], rest[3]))
    i2 = jnp.where(rest[0] >= v2, 0, jnp.where(rest[1] >= v2, 1, jnp.where(rest[2] >= v2, 2, 3))).astype(I32)
    tt = jnp.exp(v2 - v1)
    w1 = pg / (1.0 + tt)
    w2 = pg * tt / (1.0 + tt)
    e1 = gi * EXP_PER_GROUP + i1
    e2 = gi * EXP_PER_GROUP + i2
    lane = lax.broadcasted_iota(I32, (tm, N_EXPERTS), 1)
    oh1 = lane == e1
    oh2 = lane == e2
    oh = jnp.where(jnp.logical_or(oh1, oh2), 1.0, 0.0)
    ri = lax.broadcasted_iota(I32, (tm, tm), 0)
    ci = lax.broadcasted_iota(I32, (tm, tm), 1)
    ltri = jnp.where(ri > ci, 1.0, 0.0).astype(BF16)
    cnt = cnt_scr[0:1, 0:N_EXPERTS]
    pre = _dot(ltri, oh.astype(BF16)) + cnt
    rank1 = jnp.sum(jnp.where(oh1, pre, 0.0), axis=1, keepdims=True)
    rank2 = jnp.sum(jnp.where(oh2, pre, 0.0), axis=1, keepdims=True)
    cnt_scr[0:1, 0:N_EXPERTS] = cnt + jnp.sum(oh, axis=0, keepdims=True)
    idx_o[:, 0:1] = e1
    idx_o[:, 1:2] = e2
    idx_o[:, 2:3] = rank1.astype(I32)
    idx_o[:, 3:4] = rank2.astype(I32)
    lw = lax.broadcasted_iota(I32, (tm, 128), 1)
    wts_o[...] = jnp.where(lw == 0, w1, jnp.where(lw == 1, w2, 0.0))


def _route_out_specs(tm):
    return [pl.BlockSpec((tm, D_MODEL), lambda i: (i, 0)),
            pl.BlockSpec((tm, 4), lambda i: (i, 0)),
            pl.BlockSpec((tm, 128), lambda i: (i, 0)),
            pl.BlockSpec((8, 128), lambda i: (0, 0))]


def _route_out_shapes(n):
    return [jax.ShapeDtypeStruct((n, D_MODEL), F32),
            jax.ShapeDtypeStruct((n, 4), I32),
            jax.ShapeDtypeStruct((n, 128), F32),
            jax.ShapeDtypeStruct((8, 128), F32)]


def _route_in_specs():
    return [pl.BlockSpec((1, D_MODEL), lambda i: (0, 0)),
            pl.BlockSpec((D_MODEL, 128), lambda i: (0, 0)),
            pl.BlockSpec((1, 32), lambda i: (0, 0))]


def _mix0_out_kernel(o_ref, bonus_ref, g_ref, attn_ref, x_ref, lnw_ref, lnb_ref, avg_ref, wo_ref,
                     nrm_ref, wr_ref, br_ref,
                     x1_o, xn_o, idx_o, wts_o, cnt_o, cnt_scr):
    @pl.when(pl.program_id(0) == 0)
    def _():
        cnt_scr[...] = jnp.zeros_like(cnt_scr)

    avg = avg_ref[...]
    half = HEAD_DIM // 2
    lane8 = lax.broadcasted_iota(I32, (SEQ_PER_GROUP, 128), 1)
    unit_rows = []
    for u in range(o_ref.shape[0]):
        ot = jnp.concatenate([o_ref[u], jnp.zeros((128 - half, 128), F32)], axis=0).T
        cols = []
        for jj in range(N_HEADS_B // 2):
            q = [ot[(4 * jj + i) * SEQ_PER_GROUP:(4 * jj + i + 1) * SEQ_PER_GROUP] for i in range(4)]
            c = jnp.where(lane8 < half, q[0], pltpu.roll(q[1], half, 1))
            c = jnp.where(lane8 < 2 * half, c, pltpu.roll(q[2], 2 * half, 1))
            c = jnp.where(lane8 < 3 * half, c, pltpu.roll(q[3], 3 * half, 1))
            cols.append(c)
        unit_rows.append(jnp.concatenate(cols, axis=1))
    o = jnp.concatenate(unit_rows, axis=0)
    d = o - _dot2(o, avg)
    var = _dot2(d * d, avg)
    on = d * lax.rsqrt(var + RWKV_GN_EPS) * lnw_ref[...] + lnb_ref[...]
    rout = ((on + bonus_ref[...]) * g_ref[...]).astype(BF16)
    y = _dot(attn_ref[...], wo_ref[:Q_COLS, :]) + _dot(rout, wo_ref[Q_COLS:, :]) + x_ref[...]
    x1_o[...] = y
    _route_tile(y, nrm_ref, wr_ref, br_ref, cnt_scr, xn_o, idx_o, wts_o)
    cnt_o[...] = cnt_scr[...]


def _mix0_out(o, bonus, g, attn, x, p, rp, tm):
    n = x.shape[0]
    tile = lambda w: pl.BlockSpec((tm, w), lambda i: (i, 0))
    row = lambda w: pl.BlockSpec((1, w), lambda i: (0, 0))
    full = lambda r, c: pl.BlockSpec((r, c), lambda i: (0, 0))
    return pl.pallas_call(
        _mix0_out_kernel,
        grid=(n // tm,),
        in_specs=[pl.BlockSpec((tm // SEQ_PER_GROUP, HEAD_DIM // 2, 128), lambda i: (i, 0, 0)),
                  tile(D_B), tile(D_B), tile(Q_COLS), tile(D_MODEL), row(D_B), row(D_B),
                  full(D_B, D_B), full(D_MODEL, D_MODEL)] + _route_in_specs(),
        out_specs=[tile(D_MODEL)] + _route_out_specs(tm),
        out_shape=[jax.ShapeDtypeStruct((n, D_MODEL), F32)] + _route_out_shapes(n),
        scratch_shapes=[pltpu.VMEM((8, 128), F32)],
        compiler_params=_cparams("arbitrary"),
        name="mix0_out",
    )(o, bonus, g, attn, x, p['ln_w'], p['ln_b'], p['head_avg'], p['w_out'], rp['norm'], rp['w'], rp['b'])


def _row_gather_start(idx_ref, base, n_rows, src_hbm, dst, sem):
    def body(r, carry):
        pltpu.make_async_copy(src_hbm.at[pl.ds(idx_ref[base + r], 1)], dst.at[pl.ds(r, 1)], sem).start()
        return carry

    lax.fori_loop(0, n_rows, body, 0, unroll=8)


def _row_gather_wait(dst, sem):
    pltpu.make_async_copy(dst, dst, sem).wait()


def _expert_kernel(te_ref, nu_ref, src_ref, x_hbm, wg_ref, wu_ref, wd_ref, o_ref, xbuf, sem):
    i = pl.program_id(0)
    nu = nu_ref[0]

    @pl.when(i == 0)
    def _():
        _row_gather_start(src_ref, 0, MOE_TILE, x_hbm, xbuf.at[0], sem.at[0])

    @pl.when(i + 1 < nu)
    def _():
        nxt = (i + 1) % 2
        _row_gather_start(src_ref, (i + 1) * MOE_TILE, MOE_TILE, x_hbm, xbuf.at[nxt], sem.at[nxt])

    @pl.when(i < nu)
    def _():
        cur = i % 2
        _row_gather_wait(xbuf.at[cur], sem.at[cur])
        x = xbuf[cur].astype(BF16)
        hg = _dot(x, wg_ref[0])
        hu = _dot(x, wu_ref[0])
        h = (hg * _sigmoid(hg)) * hu
        o_ref[...] = _dot(h.astype(BF16), wd_ref[0])

    @pl.when(i >= nu)
    def _():
        o_ref[...] = jnp.zeros_like(o_ref)


def _experts(xn, src, tile_expert, n_used, wg, wu, wd):
    r = src.shape[0]
    grid_spec = pltpu.PrefetchScalarGridSpec(
        num_scalar_prefetch=3,
        grid=(r // MOE_TILE,),
        in_specs=[pl.BlockSpec(memory_space=pl.ANY),
                  pl.BlockSpec((1, D_MODEL, D_FF_E), lambda i, te, nu, sr: (te[i], 0, 0)),
                  pl.BlockSpec((1, D_MODEL, D_FF_E), lambda i, te, nu, sr: (te[i], 0, 0)),
                  pl.BlockSpec((1, D_FF_E, D_MODEL), lambda i, te, nu, sr: (te[i], 0, 0))],
        out_specs=pl.BlockSpec((MOE_TILE, D_MODEL), lambda i, te, nu, sr: (i, 0)),
        scratch_shapes=[pltpu.VMEM((2, MOE_TILE, D_MODEL), F32), pltpu.SemaphoreType.DMA((2,))],
    )
    return pl.pallas_call(
        _expert_kernel,
        grid_spec=grid_spec,
        out_shape=jax.ShapeDtypeStruct((r, D_MODEL), F32),
        compiler_params=_cparams("arbitrary"),
        name="moe_experts",
    )(tile_expert, n_used, src, xn, wg, wu, wd)


def _moe(xn, idx, cnt, ep):
    n = xn.shape[0]
    rows = 2 * n + N_EXPERTS * MOE_TILE
    counts = cnt[0, :N_EXPERTS].astype(I32)
    padded = ((counts + MOE_TILE - 1) // MOE_TILE) * MOE_TILE
    ends = jnp.cumsum(padded)
    offs = ends - padded
    pos1 = offs[idx[:, 0]] + idx[:, 2]
    pos2 = offs[idx[:, 1]] + idx[:, 3]
    tok = jnp.arange(n, dtype=I32)
    src = jnp.zeros((rows,), I32).at[pos1].set(tok).at[pos2].set(tok)
    n_used = (ends[-1] // MOE_TILE).astype(I32)
    starts = jnp.arange(rows // MOE_TILE, dtype=I32) * MOE_TILE
    starts = jnp.minimum(starts, ends[-1] - 1)
    tile_expert = jnp.sum((starts[:, None] >= ends[None, :]).astype(I32), axis=1)
    tile_expert = jnp.minimum(tile_expert, N_EXPERTS - 1).astype(I32)
    out = _experts(xn, src, tile_expert, n_used.reshape(1), ep['wg'], ep['wu'], ep['wd'])
    return out, pos1, pos2


def _combine_kernel(p1_ref, p2_ref, x_ref, wts_ref, nrm_ref, out_hbm, y_o, gbuf, sem, *, final):
    i = pl.program_id(0)
    tm = x_ref.shape[0]

    def start(tile, slot):
        _row_gather_start(p1_ref, tile * tm, tm, out_hbm, gbuf.at[slot, 0], sem.at[slot])
        _row_gather_start(p2_ref, tile * tm, tm, out_hbm, gbuf.at[slot, 1], sem.at[slot])

    @pl.when(i == 0)
    def _():
        start(0, 0)

    @pl.when(i + 1 < pl.num_programs(0))
    def _():
        start(i + 1, (i + 1) % 2)

    cur = i % 2
    _row_gather_wait(gbuf.at[cur], sem.at[cur])
    wts = wts_ref[...]
    y = x_ref[...] + wts[:, 0:1] * gbuf[cur, 0] + wts[:, 1:2] * gbuf[cur, 1]
    if final:
        y = _rms(y, nrm_ref[...])
    y_o[...] = y


def _moe_combine(x, wts, out, pos1, pos2, nrm, final, tm=256):
    n = x.shape[0]
    tile = lambda w: pl.BlockSpec((tm, w), lambda i, p1, p2: (i, 0))
    grid_spec = pltpu.PrefetchScalarGridSpec(
        num_scalar_prefetch=2,
        grid=(n // tm,),
        in_specs=[tile(D_MODEL), tile(128), pl.BlockSpec((1, D_MODEL), lambda i, p1, p2: (0, 0)),
                  pl.BlockSpec(memory_space=pl.ANY)],
        out_specs=tile(D_MODEL),
        scratch_shapes=[pltpu.VMEM((2, 2, tm, D_MODEL), F32), pltpu.SemaphoreType.DMA((2,))],
    )
    return pl.pallas_call(
        functools.partial(_combine_kernel, final=final),
        grid_spec=grid_spec,
        out_shape=jax.ShapeDtypeStruct((n, D_MODEL), F32),
        compiler_params=_cparams("arbitrary"),
        name="moe_combine_final" if final else "moe_combine",
    )(pos1, pos2, x, wts, nrm, out)


def _gelu_tanh(x):
    return 0.5 * x * (1.0 + jnp.tanh(0.7978845608028654 * (x + 0.044715 * (x * x * x))))


def _mix1_kernel(x_ref, nmix_ref, bre_ref, bim_ref, are_ref, aim_ref, cre_ref, cim_ref,
                 dsk_ref, wo_ref, h0r_ref, h0i_ref, nrm_ref, wr_ref, br_ref,
                 x2_o, xn_o, idx_o, wts_o, cnt_o, hr_o, hi_o,
                 bur, bui, hr_scr, hi_scr, cnt_scr, *, b, cw):
    @pl.when(pl.program_id(0) == 0)
    def _():
        cnt_scr[...] = jnp.zeros_like(cnt_scr)
        hr_scr[...] = h0r_ref[...]
        hi_scr[...] = h0i_ref[...]

    x = x_ref[...]
    u = _rms(x, nmix_ref[...])
    ub = u.astype(BF16)
    nblk = bre_ref.shape[0]
    kin = D_MODEL // nblk
    kst = S5_STATE // nblk
    for cb in range(nblk):
        ucb = ub[:, cb * kin:(cb + 1) * kin]
        bur[:, cb * kst:(cb + 1) * kst] = _dot(ucb, bre_ref[cb])
        bui[:, cb * kst:(cb + 1) * kst] = _dot(ucb, bim_ref[cb])

    tc = x.shape[0] // b
    for c0 in range(0, S5_STATE, cw):
        cs = slice(c0, c0 + cw)
        ar = jnp.broadcast_to(are_ref[:, cs], (b, cw))
        ai = jnp.broadcast_to(aim_ref[:, cs], (b, cw))

        def step(s, carry, cs=cs, ar=ar, ai=ai):
            hr, hi = carry
            rows = pl.ds(pl.multiple_of(s * b, b), b)
            nr = ar * hr - ai * hi + bur[rows, cs]
            ni = ar * hi + ai * hr + bui[rows, cs]
            bur[rows, cs] = nr
            bui[rows, cs] = ni
            return nr, ni

        hr, hi = lax.fori_loop(0, tc, step, (hr_scr[:, cs], hi_scr[:, cs]))
        hr_scr[:, cs] = hr
        hi_scr[:, cs] = hi

    ych = []
    for cb in range(nblk):
        ss = slice(cb * kst, (cb + 1) * kst)
        ych.append(_dot(bur[:, ss].astype(BF16), cre_ref[cb]) - _dot(bui[:, ss].astype(BF16), cim_ref[cb]))
    y = jnp.concatenate(ych, axis=1) + dsk_ref[...] * u
    z = _dot(_gelu_tanh(y).astype(BF16), wo_ref[...])
    x2 = x + z[:, :D_MODEL] * _sigmoid(z[:, D_MODEL:])
    x2_o[...] = x2
    _route_tile(x2, nrm_ref, wr_ref, br_ref, cnt_scr, xn_o, idx_o, wts_o)
    cnt_o[...] = cnt_scr[...]
    hr_o[...] = hr_scr[...]
    hi_o[...] = hi_scr[...]


def _mix1(x, sp, rp, h0r, h0i, b, tr):
    n = x.shape[0]
    cw = 1024 if b == 8 else 128
    tile = lambda w: pl.BlockSpec((tr, w), lambda i: (i, 0))
    row = lambda w: pl.BlockSpec((1, w), lambda i: (0, 0))
    full = lambda *s: pl.BlockSpec(s, lambda i: (0,) * len(s))
    nblk = sp['b_re'].shape[0]
    return pl.pallas_call(
        functools.partial(_mix1_kernel, b=b, cw=cw),
        grid=(n // tr,),
        in_specs=[tile(D_MODEL), row(D_MODEL),
                  full(nblk, D_MODEL // nblk, S5_STATE // nblk), full(nblk, D_MODEL // nblk, S5_STATE // nblk),
                  row(S5_STATE), row(S5_STATE),
                  full(nblk, S5_STATE // nblk, D_MODEL // nblk), full(nblk, S5_STATE // nblk, D_MODEL // nblk),
                  row(D_MODEL), full(D_MODEL, 2 * D_MODEL), full(b, S5_STATE), full(b, S5_STATE)] + _route_in_specs(),
        out_specs=[tile(D_MODEL)] + _route_out_specs(tr) + [full(b, S5_STATE), full(b, S5_STATE)],
        out_shape=[jax.ShapeDtypeStruct((n, D_MODEL), F32)] + _route_out_shapes(n)
                  + [jax.ShapeDtypeStruct((b, S5_STATE), F32)] * 2,
        scratch_shapes=[pltpu.VMEM((tr, S5_STATE), F32), pltpu.VMEM((tr, S5_STATE), F32),
                        pltpu.VMEM((b, S5_STATE), F32), pltpu.VMEM((b, S5_STATE), F32),
                        pltpu.VMEM((8, 128), F32)],
        compiler_params=_cparams("arbitrary"),
        name="mix1",
    )(x, sp['norm'], sp['b_re'], sp['b_im'], sp['a_re'], sp['a_im'], sp['c_re'], sp['c_im'],
      sp['d'], sp['w_out'], h0r, h0i, rp['norm'], rp['w'], rp['b'])


def _router_params(norm, w_rc, b_rc, w_rf, b_rf):
    w = jnp.concatenate([w_rc, w_rf.reshape(D_MODEL, N_EXPERTS), jnp.zeros((D_MODEL, 12), F32)], axis=1)
    hi = w.astype(BF16)
    lo = (w - hi.astype(F32)).astype(BF16)
    wcat = jnp.concatenate([hi, lo, jnp.zeros((D_MODEL, 64), BF16)], axis=1)
    bias = jnp.concatenate([b_rc, b_rf.reshape(-1), jnp.zeros((12,), F32)]).reshape(1, 32)
    return {'norm': norm.reshape(1, D_MODEL), 'w': wcat, 'b': bias}


def _expert_params(wg, wu, wd):
    return {'wg': wg.astype(BF16), 'wu': wu.astype(BF16), 'wd': wd.astype(BF16)}


def _s5_params(norm, a_re, a_im, log_dt, b_re, b_im, c_re, c_im, d_skip, w_out, nblk=8):
    dt = jnp.exp(log_dt)
    mag = jnp.exp(dt * a_re)
    ab_re, ab_im = mag * jnp.cos(dt * a_im), mag * jnp.sin(dt * a_im)
    den = a_re * a_re + a_im * a_im
    f_re = ((ab_re - 1.0) * a_re + ab_im * a_im) / den
    f_im = (ab_im * a_re - (ab_re - 1.0) * a_im) / den
    bb_re = f_re[..., None] * b_re - f_im[..., None] * b_im
    bb_im = f_re[..., None] * b_im + f_im[..., None] * b_re
    gpb = S5_GROUPS // nblk
    eye = jnp.eye(gpb, dtype=F32)

    def in_blocks(bb):
        bb = bb.reshape(nblk, gpb, S5_P, S5_CH)
        w = jnp.einsum('ngpc,gh->ngchp', bb, eye)
        return w.reshape(nblk, gpb * S5_CH, gpb * S5_P).astype(BF16)

    def out_blocks(cc):
        cc = cc.reshape(nblk, gpb, S5_CH, S5_P)
        w = jnp.einsum('ngcp,gh->ngphc', cc, eye)
        return w.reshape(nblk, gpb * S5_P, gpb * S5_CH).astype(BF16)

    return {'norm': norm.reshape(1, D_MODEL), 'b_re': in_blocks(bb_re), 'b_im': in_blocks(bb_im),
            'a_re': ab_re.reshape(1, S5_STATE), 'a_im': ab_im.reshape(1, S5_STATE),
            'c_re': out_blocks(c_re), 'c_im': out_blocks(c_im), 'd': d_skip.reshape(1, D_MODEL),
            'w_out': w_out.astype(BF16)}


def _head_block(value):
    hid = jnp.arange(D_B, dtype=I32) // HEAD_DIM
    return jnp.where(hid[:, None] == hid[None, :], value, 0.0).astype(BF16)


def _run_group(x_tm, b, t, cache_k, cache_v, shift0, wkv0, h0r, h0i, pr):
    n = b * t
    prompt = cache_k is None
    tm = 256 if prompt else 128
    q, kv, pb = _in_proj(x_tm, pr['l0_norm'], pr['l0_w_in'], tm)

    if prompt:
        attn = _attn_prompt(q, kv, pr['sinks'], b, t)
        kv3 = kv.reshape(t, b, 2, N_KV_A, HEAD_DIM)
        new_k = kv3[t - WINDOW:, :, 0].transpose(1, 0, 2, 3)
        new_v = kv3[t - WINDOW:, :, 1].transpose(1, 0, 2, 3)
        init = jnp.zeros((b, D_B_IN), F32)
    else:
        qs = q.astype(F32).reshape(t, b, Q_COLS).transpose(1, 0, 2)
        kvs = kv.reshape(t, b, 2 * KV_COLS).transpose(1, 0, 2)
        kn, vn = kvs[..., :KV_COLS], kvs[..., KV_COLS:]
        ck = cache_k.reshape(b, WINDOW, KV_COLS)
        cv = cache_v.reshape(b, WINDOW, KV_COLS)
        attn = _attn_sample(qs, kn, vn, ck, cv, pr['sinks'])
        attn = attn.transpose(1, 0, 2).reshape(n, Q_COLS).astype(BF16)
        new_k = jnp.concatenate([ck[:, t:], kn], axis=1).reshape(b, WINDOW, N_KV_A, HEAD_DIM)
        new_v = jnp.concatenate([cv[:, t:], vn], axis=1).reshape(b, WINDOW, N_KV_A, HEAD_DIM)
        init = shift0
    new_shift = pb[n - b:]

    r, w, k, v, kk, nkka, bonus, g = _rwkv_prep(pb, init, pr['rw'], b, tm)
    tc = 64 if prompt else t
    s0 = jnp.zeros((b // SEQ_PER_GROUP, HEAD_DIM, HEAD_DIM // 2, 128), F32) if prompt else _state_to_scan(wkv0, b)
    o, s_fin = _wkv_scan(r, w, k, kk, nkka, v, s0, b, t, tc)
    new_wkv = _state_from_scan(s_fin, b)

    x1, xn, idx, wts, cnt = _mix0_out(o, bonus, g, attn, x_tm, pr['rw'], pr['l0_route'], 256)
    out, pos1, pos2 = _moe(xn, idx, cnt, pr['l0_exp'])
    x1 = _moe_combine(x1, wts, out, pos1, pos2, pr['final_norm'], False)

    x2, xn, idx, wts, cnt, hr, hi = _mix1(x1, pr['s5'], pr['l1_route'], h0r, h0i, b, 256)
    out, pos1, pos2 = _moe(xn, idx, cnt, pr['l1_exp'])
    y = _moe_combine(x2, wts, out, pos1, pos2, pr['final_norm'], True)
    return (y, new_k, new_v, new_shift, new_wkv,
            hr.reshape(b, S5_GROUPS, S5_P), hi.reshape(b, S5_GROUPS, S5_P))


def kernel(x_prompt, x_sample, cache_win_k, cache_win_v, state_shift, state_wkv, state_s5_re, state_s5_im,
           l0_norm_mix, l0_w_in, l0_sinks, l0_mu, l0_w0, l0_w_lora_up, l0_a0, l0_a_lora_up, l0_g_lora_up,
           l0_k_k, l0_k_a, l0_r_k, l0_ln_w, l0_ln_b, l0_w_out,
           l0_norm_ffn, l0_router_coarse, l0_bias_coarse, l0_router_fine, l0_bias_fine,
           l0_exp_gate, l0_exp_up, l0_exp_down,
           l1_norm_mix, l1_s5_a_re, l1_s5_a_im, l1_s5_log_dt, l1_s5_b_re, l1_s5_b_im, l1_s5_c_re, l1_s5_c_im,
           l1_s5_d, l1_w_out,
           l1_norm_ffn, l1_router_coarse, l1_bias_coarse, l1_router_fine, l1_bias_fine,
           l1_exp_gate, l1_exp_up, l1_exp_down,
           final_norm):
    row = lambda z: z.reshape(1, -1)
    pr = {
        'l0_norm': row(l0_norm_mix), 'l0_w_in': l0_w_in.astype(BF16), 'sinks': l0_sinks,
        'rw': {'mu': row(l0_mu), 'w0': row(l0_w0), 'w_up': l0_w_lora_up.astype(BF16), 'a0': row(l0_a0),
               'a_up': l0_a_lora_up.astype(BF16), 'g_up': l0_g_lora_up.astype(BF16), 'k_k': row(l0_k_k),
               'k_a': row(l0_k_a), 'r_k': row(l0_r_k), 'ln_w': row(l0_ln_w), 'ln_b': row(l0_ln_b),
               'head_ones': _head_block(1.0), 'head_avg': _head_block(1.0 / HEAD_DIM),
               'w_out': l0_w_out.astype(BF16)},
        'l0_route': _router_params(l0_norm_ffn, l0_router_coarse, l0_bias_coarse, l0_router_fine, l0_bias_fine),
        'l0_exp': _expert_params(l0_exp_gate, l0_exp_up, l0_exp_down),
        's5': _s5_params(l1_norm_mix, l1_s5_a_re, l1_s5_a_im, l1_s5_log_dt, l1_s5_b_re, l1_s5_b_im,
                         l1_s5_c_re, l1_s5_c_im, l1_s5_d, l1_w_out),
        'l1_route': _router_params(l1_norm_ffn, l1_router_coarse, l1_bias_coarse, l1_router_fine, l1_bias_fine),
        'l1_exp': _expert_params(l1_exp_gate, l1_exp_up, l1_exp_down),
        'final_norm': row(final_norm),
    }
    bp, tp = x_prompt.shape[0], x_prompt.shape[1]
    bs, ts = x_sample.shape[0], x_sample.shape[1]
    xp = x_prompt.transpose(1, 0, 2).reshape(bp * tp, D_MODEL)
    xs = x_sample.transpose(1, 0, 2).reshape(bs * ts, D_MODEL)
    zero_state = jnp.zeros((bp, S5_STATE), F32)
    yp, pk, pv, psh, pwkv, pre, pim = _run_group(xp, bp, tp, None, None, None, None, zero_state, zero_state, pr)
    ys, sk, sv, ssh, swkv, sre, sim = _run_group(
        xs, bs, ts, cache_win_k, cache_win_v, state_shift, state_wkv,
        state_s5_re.reshape(bs, S5_STATE), state_s5_im.reshape(bs, S5_STATE), pr)
    y_prompt = yp.reshape(tp, bp, D_MODEL).transpose(1, 0, 2)
    y_sample = ys.reshape(ts, bs, D_MODEL).transpose(1, 0, 2)
    return (y_prompt, y_sample, pk, pv, psh, pwkv, pre, pim, sk, sv, ssh, swkv, sre, sim)
```

```python
import functools

import jax
import jax.numpy as jnp
from jax import lax
from jax.experimental import pallas as pl
from jax.experimental.pallas import tpu as pltpu

F32 = jnp.float32
BF16 = jnp.bfloat16
I32 = jnp.int32

D_MODEL = 1024
HEAD_DIM = 64
N_HEADS_A = 8
N_KV_A = 2
GQA_GROUP = 4
WINDOW = 128
Q_COLS = 512
KV_COLS = 128
D_A_IN = 768
N_HEADS_B = 8
D_B = 512
D_LORA_W = 64
D_LORA_A = 64
D_LORA_G = 128
D_B_IN = 1792
D_IN0 = 2560
RWKV_GN_EPS = 64e-5
S5_CH = 16
S5_GROUPS = 64
S5_P = 64
S5_STATE = S5_GROUPS * S5_P
N_EGROUPS = 4
EXP_PER_GROUP = 4
N_EXPERTS = 16
D_FF_E = 512
RMS_EPS = 1e-5
NEG_BIG = -1e30
PAIRS = 64
SEQ_PER_GROUP = PAIRS // N_HEADS_B
MOE_TILE = 256
ROW_CHUNKS = D_MODEL // 128
VMEM_LIMIT = 56 * 1024 * 1024


def _cparams(*sem):
    return pltpu.CompilerParams(dimension_semantics=sem, vmem_limit_bytes=VMEM_LIMIT)


def _dot(a, b):
    return jnp.dot(a, b, preferred_element_type=F32)


def _split_bf16(x):
    hi = x.astype(BF16)
    lo = (x - hi.astype(F32)).astype(BF16)
    return hi, lo


def _dot2(x, w):
    hi, lo = _split_bf16(x)
    return _dot(hi, w) + _dot(lo, w)


def _rms(x, g):
    return x * lax.rsqrt(jnp.mean(x * x, axis=-1, keepdims=True) + RMS_EPS) * g


def _sigmoid(x):
    return 1.0 / (1.0 + jnp.exp(-x))


def _store_row_tiles(ref, x):
    rows = x.shape[0]
    for c in range(ROW_CHUNKS):
        ref[pl.ds(c, rows, stride=ROW_CHUNKS), :] = x[:, c * 128:(c + 1) * 128]


def _load_row_tiles(ref):
    rows = ref.shape[0] // ROW_CHUNKS
    return jnp.concatenate([ref[pl.ds(c, rows, stride=ROW_CHUNKS), :] for c in range(ROW_CHUNKS)], axis=1)


def _in_proj_kernel(x_ref, g_ref, w_ref, q_ref, kv_ref, pb_ref):
    xn = _rms(x_ref[...], g_ref[...]).astype(BF16)
    q_ref[...] = _dot(xn, w_ref[:, :Q_COLS]).astype(BF16)
    kv_ref[...] = _dot(xn, w_ref[:, Q_COLS:D_A_IN])
    pb_ref[...] = _dot(xn, w_ref[:, D_A_IN:])


def _in_proj(x, g, w_bf16, tm):
    n = x.shape[0]
    return pl.pallas_call(
        _in_proj_kernel,
        grid=(n // tm,),
        in_specs=[pl.BlockSpec((tm, D_MODEL), lambda i: (i, 0)),
                  pl.BlockSpec((1, D_MODEL), lambda i: (0, 0)),
                  pl.BlockSpec((D_MODEL, D_IN0), lambda i: (0, 0))],
        out_specs=[pl.BlockSpec((tm, Q_COLS), lambda i: (i, 0)),
                   pl.BlockSpec((tm, 2 * KV_COLS), lambda i: (i, 0)),
                   pl.BlockSpec((tm, D_B_IN), lambda i: (i, 0))],
        out_shape=[jax.ShapeDtypeStruct((n, Q_COLS), BF16),
                   jax.ShapeDtypeStruct((n, 2 * KV_COLS), F32),
                   jax.ShapeDtypeStruct((n, D_B_IN), F32)],
        compiler_params=_cparams("parallel"),
        name="in_proj",
    )(x, g, w_bf16)


def _attn_prompt_kernel(sinks_ref, q_ref, kc_ref, kp_ref, vc_ref, vp_ref, o_ref):
    j = pl.program_id(1)
    qi = lax.broadcasted_iota(I32, (WINDOW, 2 * WINDOW), 0)
    kj = lax.broadcasted_iota(I32, (WINDOW, 2 * WINDOW), 1)
    valid = jnp.logical_and(kj > qi, kj <= qi + WINDOW)
    valid = jnp.logical_and(valid, jnp.logical_or(kj >= WINDOW, j > 0))
    dist = (WINDOW + qi - kj).astype(F32)
    for n in range(N_KV_A):
        cs = slice(n * HEAD_DIM, (n + 1) * HEAD_DIM)
        kb = jnp.concatenate([kp_ref[:, cs], kc_ref[:, cs]], axis=0).astype(BF16)
        vb = jnp.concatenate([vp_ref[:, cs], vc_ref[:, cs]], axis=0).astype(BF16)
        for g in range(GQA_GROUP):
            h = n * GQA_GROUP + g
            hs = slice(h * HEAD_DIM, (h + 1) * HEAD_DIM)
            s = lax.dot_general(q_ref[:, hs], kb, (((1,), (1,)), ((), ())), preferred_element_type=F32)
            s = s * (HEAD_DIM ** -0.5) - (2.0 ** -(h + 1)) * dist
            s = jnp.where(valid, s, NEG_BIG)
            sink = sinks_ref[h]
            m = jnp.maximum(jnp.max(s, axis=1, keepdims=True), sink)
            p = jnp.exp(s - m)
            l = jnp.sum(p, axis=1, keepdims=True) + jnp.exp(sink - m)
            o = _dot(p.astype(BF16), vb) / l
            o_ref[:, hs] = o.astype(BF16)


def _attn_prompt(q, kv, sinks, b, t):
    q2 = q.reshape(t, b * Q_COLS)
    kv2 = kv.reshape(t, b * 2 * KV_COLS)
    prev = lambda bi, j: jnp.maximum(j - 1, 0)
    out = pl.pallas_call(
        _attn_prompt_kernel,
        grid=(b, t // WINDOW),
        in_specs=[pl.BlockSpec(memory_space=pltpu.SMEM),
                  pl.BlockSpec((WINDOW, Q_COLS), lambda bi, j: (j, bi)),
                  pl.BlockSpec((WINDOW, KV_COLS), lambda bi, j: (j, 2 * bi)),
                  pl.BlockSpec((WINDOW, KV_COLS), lambda bi, j: (prev(bi, j), 2 * bi)),
                  pl.BlockSpec((WINDOW, KV_COLS), lambda bi, j: (j, 2 * bi + 1)),
                  pl.BlockSpec((WINDOW, KV_COLS), lambda bi, j: (prev(bi, j), 2 * bi + 1))],
        out_specs=pl.BlockSpec((WINDOW, Q_COLS), lambda bi, j: (j, bi)),
        out_shape=jax.ShapeDtypeStruct((t, b * Q_COLS), BF16),
        compiler_params=_cparams("parallel", "parallel"),
        name="attn_prompt",
    )(sinks, q2, kv2, kv2, kv2, kv2)
    return out.reshape(t * b, Q_COLS)


def _attn_sample_kernel(sinks_ref, q_ref, kn_ref, vn_ref, ck_ref, cv_ref, o_ref):
    bs, t = q_ref.shape[0], q_ref.shape[1]
    assert t & (t - 1) == 0
    nq, nk = GQA_GROUP * t, 2 * WINDOW
    r = lax.broadcasted_iota(I32, (nq, nk), 0)
    kj = lax.broadcasted_iota(I32, (nq, nk), 1)
    tq = jnp.bitwise_and(r, t - 1)
    valid = jnp.logical_and(kj > tq, kj <= tq + WINDOW)
    dist = (WINDOW + tq - kj).astype(F32)
    grp = jnp.right_shift(lax.broadcasted_iota(I32, (nq, 1), 0), t.bit_length() - 1)
    pad = jnp.zeros((bs, nk - WINDOW - t, HEAD_DIM), F32)
    for n in range(N_KV_A):
        cs = slice(n * HEAD_DIM, (n + 1) * HEAD_DIM)
        kb = jnp.concatenate([ck_ref[:, :, cs], kn_ref[:, :, cs], pad], axis=1).astype(BF16)
        vb = jnp.concatenate([cv_ref[:, :, cs], vn_ref[:, :, cs], pad], axis=1).astype(BF16)
        qn = jnp.concatenate([q_ref[:, :, (n * GQA_GROUP + g) * HEAD_DIM:(n * GQA_GROUP + g + 1) * HEAD_DIM]
                              for g in range(GQA_GROUP)], axis=1).astype(BF16)
        slope = jnp.zeros((nq, 1), F32)
        sink = jnp.zeros((nq, 1), F32)
        for g in range(GQA_GROUP):
            h = n * GQA_GROUP + g
            slope = jnp.where(grp == g, 2.0 ** -(h + 1), slope)
            sink = jnp.where(grp == g, sinks_ref[h], sink)
        s = jnp.einsum('bqd,bkd->bqk', qn, kb, preferred_element_type=F32)
        s = s * (HEAD_DIM ** -0.5) - (slope * dist)[None]
        s = jnp.where(valid[None], s, NEG_BIG)
        m = jnp.maximum(jnp.max(s, axis=2, keepdims=True), sink[None])
        p = jnp.exp(s - m)
        l = jnp.sum(p, axis=2, keepdims=True) + jnp.exp(sink[None] - m)
        o = jnp.einsum('bqk,bkd->bqd', p.astype(BF16), vb, preferred_element_type=F32) / l
        for g in range(GQA_GROUP):
            h = n * GQA_GROUP + g
            o_ref[:, :, h * HEAD_DIM:(h + 1) * HEAD_DIM] = o[:, g * t:(g + 1) * t, :]


def _attn_sample(q, kn, vn, ck, cv, sinks, bs=16):
    db, t = q.shape[0], q.shape[1]
    seq3 = lambda w: pl.BlockSpec((bs, t, w), lambda i: (i, 0, 0))
    cache = pl.BlockSpec((bs, WINDOW, KV_COLS), lambda i: (i, 0, 0))
    return pl.pallas_call(
        _attn_sample_kernel,
        grid=(db // bs,),
        in_specs=[pl.BlockSpec(memory_space=pltpu.SMEM), seq3(Q_COLS), seq3(KV_COLS), seq3(KV_COLS), cache, cache],
        out_specs=seq3(Q_COLS),
        out_shape=jax.ShapeDtypeStruct((db, t, Q_COLS), F32),
        compiler_params=_cparams("parallel"),
        name="attn_sample",
    )(sinks, q, kn, vn, ck, cv)


def _rwkv_prep_kernel(pb_ref, halo_ref, init_ref, mu_ref, w0_ref, wup_ref, a0_ref, aup_ref, gup_ref,
                      kk_ref, ka_ref, rk_ref, ones_ref,
                      r_o, w_o, k_o, v_o, kk_o, nkka_o, bonus_o, g_o, *, b):
    i = pl.program_id(0)
    pb = pb_ref[...]
    tm = pb.shape[0]
    halo = jnp.where(i == 0, init_ref[...], halo_ref[...])
    prev = halo if tm == b else jnp.concatenate([halo, pb[:tm - b]], axis=0)
    xs = pb + (prev - pb) * mu_ref[...]
    r = xs[:, :D_B]
    k = xs[:, D_B:2 * D_B]
    v = xs[:, 2 * D_B:3 * D_B]
    o1 = 3 * D_B
    wd = xs[:, o1:o1 + D_LORA_W]
    ad = xs[:, o1 + D_LORA_W:o1 + D_LORA_W + D_LORA_A]
    gd = xs[:, o1 + D_LORA_W + D_LORA_A:]
    z = -(w0_ref[...] + _dot(jnp.tanh(wd).astype(BF16), wup_ref[...]))
    softplus = jnp.maximum(z, 0.0) + jnp.log(1.0 + jnp.exp(-jnp.abs(z)))
    decay = jnp.exp(-jnp.exp(-softplus - 0.5))
    a = _sigmoid(a0_ref[...] + _dot(ad.astype(BF16), aup_ref[...]))
    g_o[...] = _dot(_sigmoid(gd).astype(BF16), gup_ref[...])
    ones = ones_ref[...]
    kk = k * kk_ref[...]
    kk = kk * lax.rsqrt(jnp.maximum(_dot2(kk * kk, ones), 1e-24))
    k2 = k * (1.0 + (a - 1.0) * ka_ref[...])
    bonus_o[...] = _dot2(r * k2 * rk_ref[...], ones) * v

    half = HEAD_DIM // 2
    lane8 = lax.broadcasted_iota(I32, (SEQ_PER_GROUP, 128), 1)
    low8 = lane8 < HEAD_DIM
    first_copy = jnp.bitwise_and(lax.broadcasted_iota(I32, (half, 128), 1), SEQ_PER_GROUP) == 0
    pairs = ((r, decay, r_o, w_o), (k2, kk, k_o, kk_o), (-(kk * a), v, nkka_o, None))
    for u in range(tm // SEQ_PER_GROUP):
        rows = slice(u * SEQ_PER_GROUP, (u + 1) * SEQ_PER_GROUP)
        for x, y, x_o, y_o in pairs:
            xu, yu = x[rows], y[rows]
            pieces = []
            for h in range(N_HEADS_B):
                cs = slice((h // 2) * 128, (h // 2 + 1) * 128)
                if h % 2 == 0:
                    p = jnp.where(low8, xu[:, cs], pltpu.roll(yu[:, cs], HEAD_DIM, 1))
                else:
                    p = jnp.where(low8, pltpu.roll(xu[:, cs], HEAD_DIM, 1), yu[:, cs])
                pieces += [p, p]
            tr = jnp.concatenate(pieces, axis=0).T
            x_o[u] = tr[:HEAD_DIM]
            if y_o is not None:
                y_o[u] = tr[HEAD_DIM:]
            else:
                v_o[u] = jnp.where(first_copy, tr[HEAD_DIM:HEAD_DIM + half], tr[HEAD_DIM + half:])


def _rwkv_prep(pb, init, p, b, tm):
    n = pb.shape[0]
    units = n // SEQ_PER_GROUP
    tu = tm // SEQ_PER_GROUP
    half = HEAD_DIM // 2
    row = lambda w: pl.BlockSpec((1, w), lambda i: (0, 0))
    full = lambda r, c: pl.BlockSpec((r, c), lambda i: (0, 0))
    tile = pl.BlockSpec((tm, D_B), lambda i: (i, 0))
    kspec = pl.BlockSpec((tu, HEAD_DIM, 128), lambda i: (i, 0, 0))
    vspec = pl.BlockSpec((tu, half, 128), lambda i: (i, 0, 0))
    kshape = jax.ShapeDtypeStruct((units, HEAD_DIM, 128), F32)
    halo_blocks = tm // b
    return pl.pallas_call(
        functools.partial(_rwkv_prep_kernel, b=b),
        grid=(n // tm,),
        in_specs=[pl.BlockSpec((tm, D_B_IN), lambda i: (i, 0)),
                  pl.BlockSpec((b, D_B_IN), lambda i: (jnp.maximum(i * halo_blocks - 1, 0), 0)),
                  full(b, D_B_IN), row(D_B_IN), row(D_B), full(D_LORA_W, D_B), row(D_B), full(D_LORA_A, D_B),
                  full(D_LORA_G, D_B), row(D_B), row(D_B), row(D_B), full(D_B, D_B)],
        out_specs=[kspec, kspec, kspec, vspec, kspec, kspec, tile, tile],
        out_shape=[kshape, kshape, kshape, jax.ShapeDtypeStruct((units, half, 128), F32), kshape, kshape,
                   jax.ShapeDtypeStruct((n, D_B), F32), jax.ShapeDtypeStruct((n, D_B), F32)],
        compiler_params=_cparams("arbitrary"),
        name="rwkv_prep",
    )(pb, pb, init, p['mu'], p['w0'], p['w_up'], p['a0'], p['a_up'], p['g_up'], p['k_k'], p['k_a'], p['r_k'],
      p['head_ones'])


def _wkv_scan_kernel(r_ref, w_ref, k_ref, kk_ref, nkka_ref, v_ref, s0_ref, o_ref, st_ref, s_scr):
    j = pl.program_id(1)

    @pl.when(j == 0)
    def _():
        s_scr[...] = s0_ref[0]

    tc = r_ref.shape[0]
    nsub = (HEAD_DIM // 2) // 8

    def bcast(ref, s, kx):
        return jnp.broadcast_to(ref[s, pl.ds(kx, 1), :], (8, 128))

    acc0 = [[jnp.zeros((8, 128), F32) for _ in range(2)] for _ in range(nsub)]
    for kx in range(HEAD_DIM):
        kkr = bcast(kk_ref, 0, kx)
        for i in range(nsub):
            acc0[i][kx % 2] = acc0[i][kx % 2] + s_scr[kx, 8 * i:8 * i + 8, :] * kkr

    def step(s, sa):
        nxt = jnp.minimum(s + 1, tc - 1)
        vv = [v_ref[s, 8 * i:8 * i + 8, :] for i in range(nsub)]
        oacc = [[jnp.zeros((8, 128), F32) for _ in range(2)] for _ in range(nsub)]
        nacc = [[jnp.zeros((8, 128), F32) for _ in range(2)] for _ in range(nsub)]
        for kx in range(HEAD_DIM):
            rr, wr, kr = bcast(r_ref, s, kx), bcast(w_ref, s, kx), bcast(k_ref, s, kx)
            nk, kkn = bcast(nkka_ref, s, kx), bcast(kk_ref, nxt, kx)
            for i in range(nsub):
                rows = slice(8 * i, 8 * i + 8)
                sk = s_scr[kx, rows, :] * wr + sa[i] * nk + vv[i] * kr
                s_scr[kx, rows, :] = sk
                oacc[i][kx % 2] = oacc[i][kx % 2] + sk * rr
                nacc[i][kx % 2] = nacc[i][kx % 2] + sk * kkn
        o_ref[s] = jnp.concatenate([a[0] + a[1] for a in oacc], axis=0)
        return [a[0] + a[1] for a in nacc]

    lax.fori_loop(0, tc, step, [a[0] + a[1] for a in acc0])

    @pl.when(j == pl.num_programs(1) - 1)
    def _():
        st_ref[0] = s_scr[...]


def _wkv_scan(r, w, k, kk, nkka, v, s0, b, t, tc):
    g = b // SEQ_PER_GROUP
    half = HEAD_DIM // 2
    kview = lambda z: z.reshape(t, g, HEAD_DIM, 128)
    kspec = pl.BlockSpec((tc, None, HEAD_DIM, 128), lambda gi, j: (j, gi, 0, 0))
    vspec = pl.BlockSpec((tc, None, half, 128), lambda gi, j: (j, gi, 0, 0))
    sspec = pl.BlockSpec((1, HEAD_DIM, half, 128), lambda gi, j: (gi, 0, 0, 0))
    o, st = pl.pallas_call(
        _wkv_scan_kernel,
        grid=(g, t // tc),
        in_specs=[kspec] * 5 + [vspec, sspec],
        out_specs=[vspec, sspec],
        out_shape=[jax.ShapeDtypeStruct((t, g, half, 128), F32),
                   jax.ShapeDtypeStruct((g, HEAD_DIM, half, 128), F32)],
        scratch_shapes=[pltpu.VMEM((HEAD_DIM, half, 128), F32)],
        compiler_params=_cparams("parallel", "arbitrary"),
        name="wkv_scan",
    )(kview(r), kview(w), kview(k), kview(kk), kview(nkka), v.reshape(t, g, half, 128), s0)
    return o.reshape(t * g, half, 128), st


def _state_to_scan(s, b):
    g = b // SEQ_PER_GROUP
    s = s.reshape(g, SEQ_PER_GROUP, N_HEADS_B, 2, HEAD_DIM // 2, HEAD_DIM).transpose(0, 5, 4, 2, 3, 1)
    return s.reshape(g, HEAD_DIM, HEAD_DIM // 2, 128)


def _state_from_scan(s, b):
    g = b // SEQ_PER_GROUP
    s = s.reshape(g, HEAD_DIM, HEAD_DIM // 2, N_HEADS_B, 2, SEQ_PER_GROUP).transpose(0, 5, 3, 4, 2, 1)
    return s.reshape(b, N_HEADS_B, HEAD_DIM, HEAD_DIM)


def _route_tile(x, nrm_ref, wr_ref, br_ref, cnt_scr, xn_o, idx_o, wts_o):
    tm = x.shape[0]
    xn = _rms(x, nrm_ref[...])
    hi, lo = _split_bf16(xn)
    _store_row_tiles(xn_o, xn)
    wr = wr_ref[...]
    pa = _dot(hi, wr)
    pb = _dot(lo, wr)
    lg = pa[:, 0:32] + pa[:, 32:64] + pb[:, 0:32] + pb[:, 32:64] + br_ref[...]
    col = lambda c: lg[:, c:c + 1]
    c = [col(gx) for gx in range(N_EGROUPS)]
    m = jnp.maximum(jnp.maximum(c[0], c[1]), jnp.maximum(c[2], c[3]))
    den = jnp.exp(c[0] - m) + jnp.exp(c[1] - m) + jnp.exp(c[2] - m) + jnp.exp(c[3] - m)
    pg = 1.0 / den
    gi = jnp.where(c[0] >= m, 0, jnp.where(c[1] >= m, 1, jnp.where(c[2] >= m, 2, 3))).astype(I32)
    sel = []
    for e in range(EXP_PER_GROUP):
        sel.append(jnp.where(gi == 0, col(4 + e), jnp.where(gi == 1, col(8 + e),
                                                            jnp.where(gi == 2, col(12 + e), col(16 + e)))))
    v1 = jnp.maximum(jnp.maximum(sel[0], sel[1]), jnp.maximum(sel[2], sel[3]))
    i1 = jnp.where(sel[0] >= v1, 0, jnp.where(sel[1] >= v1, 1, jnp.where(sel[2] >= v1, 2, 3))).astype(I32)
    rest = [jnp.where(i1 == e, -jnp.inf, sel[e]) for e in range(EXP_PER_GROUP)]
    v2 = jnp.maximum(jnp.maximum(rest[0], rest[1]), jnp.maximum(rest[2], rest[3]))
    i2 = jnp.where(rest[0] >= v2, 0, jnp.where(rest[1] >= v2, 1, jnp.where(rest[2] >= v2, 2, 3))).astype(I32)
    tt = jnp.exp(v2 - v1)
    w1 = pg / (1.0 + tt)
    w2 = pg * tt / (1.0 + tt)
    e1 = gi * EXP_PER_GROUP + i1
    e2 = gi * EXP_PER_GROUP + i2
    lane = lax.broadcasted_iota(I32, (tm, N_EXPERTS), 1)
    oh1 = lane == e1
    oh2 = lane == e2
    oh = jnp.where(jnp.logical_or(oh1, oh2), 1.0, 0.0)
    ri = lax.broadcasted_iota(I32, (tm, tm), 0)
    ci = lax.broadcasted_iota(I32, (tm, tm), 1)
    ltri = jnp.where(ri > ci, 1.0, 0.0).astype(BF16)
    cnt = cnt_scr[0:1, 0:N_EXPERTS]
    pre = _dot(ltri, oh.astype(BF16)) + cnt
    rank1 = jnp.sum(jnp.where(oh1, pre, 0.0), axis=1, keepdims=True)
    rank2 = jnp.sum(jnp.where(oh2, pre, 0.0), axis=1, keepdims=True)
    cnt_scr[0:1, 0:N_EXPERTS] = cnt + jnp.sum(oh, axis=0, keepdims=True)
    idx_o[:, 0:1] = e1
    idx_o[:, 1:2] = e2
    idx_o[:, 2:3] = rank1.astype(I32)
    idx_o[:, 3:4] = rank2.astype(I32)
    lw = lax.broadcasted_iota(I32, (tm, 128), 1)
    wts_o[...] = jnp.where(lw == 0, w1, jnp.where(lw == 1, w2, 0.0))


def _route_out_specs(tm):
    return [pl.BlockSpec((tm * ROW_CHUNKS, 128), lambda i: (i, 0)),
            pl.BlockSpec((tm, 4), lambda i: (i, 0)),
            pl.BlockSpec((tm, 128), lambda i: (i, 0)),
            pl.BlockSpec((8, 128), lambda i: (0, 0))]


def _route_out_shapes(n):
    return [jax.ShapeDtypeStruct((n * ROW_CHUNKS, 128), F32),
            jax.ShapeDtypeStruct((n, 4), I32),
            jax.ShapeDtypeStruct((n, 128), F32),
            jax.ShapeDtypeStruct((8, 128), F32)]


def _route_in_specs():
    return [pl.BlockSpec((1, D_MODEL), lambda i: (0, 0)),
            pl.BlockSpec((D_MODEL, 128), lambda i: (0, 0)),
            pl.BlockSpec((1, 32), lambda i: (0, 0))]


def _mix0_out_kernel(o_ref, bonus_ref, g_ref, attn_ref, x_ref, lnw_ref, lnb_ref, avg_ref, wo_ref,
                     nrm_ref, wr_ref, br_ref,
                     x1_o, xn_o, idx_o, wts_o, cnt_o, cnt_scr):
    @pl.when(pl.program_id(0) == 0)
    def _():
        cnt_scr[...] = jnp.zeros_like(cnt_scr)

    avg = avg_ref[...]
    half = HEAD_DIM // 2
    lane8 = lax.broadcasted_iota(I32, (SEQ_PER_GROUP, 128), 1)
    unit_rows = []
    for u in range(o_ref.shape[0]):
        ot = jnp.concatenate([o_ref[u], jnp.zeros((128 - half, 128), F32)], axis=0).T
        cols = []
        for jj in range(N_HEADS_B // 2):
            q = [ot[(4 * jj + i) * SEQ_PER_GROUP:(4 * jj + i + 1) * SEQ_PER_GROUP] for i in range(4)]
            c = jnp.where(lane8 < half, q[0], pltpu.roll(q[1], half, 1))
            c = jnp.where(lane8 < 2 * half, c, pltpu.roll(q[2], 2 * half, 1))
            c = jnp.where(lane8 < 3 * half, c, pltpu.roll(q[3], 3 * half, 1))
            cols.append(c)
        unit_rows.append(jnp.concatenate(cols, axis=1))
    o = jnp.concatenate(unit_rows, axis=0)
    d = o - _dot2(o, avg)
    var = _dot2(d * d, avg)
    on = d * lax.rsqrt(var + RWKV_GN_EPS) * lnw_ref[...] + lnb_ref[...]
    rout = ((on + bonus_ref[...]) * g_ref[...]).astype(BF16)
    y = _dot(attn_ref[...], wo_ref[:Q_COLS, :]) + _dot(rout, wo_ref[Q_COLS:, :]) + x_ref[...]
    x1_o[...] = y
    _route_tile(y, nrm_ref, wr_ref, br_ref, cnt_scr, xn_o, idx_o, wts_o)
    cnt_o[...] = cnt_scr[...]


def _mix0_out(o, bonus, g, attn, x, p, rp, tm):
    n = x.shape[0]
    tile = lambda w: pl.BlockSpec((tm, w), lambda i: (i, 0))
    row = lambda w: pl.BlockSpec((1, w), lambda i: (0, 0))
    full = lambda r, c: pl.BlockSpec((r, c), lambda i: (0, 0))
    return pl.pallas_call(
        _mix0_out_kernel,
        grid=(n // tm,),
        in_specs=[pl.BlockSpec((tm // SEQ_PER_GROUP, HEAD_DIM // 2, 128), lambda i: (i, 0, 0)),
                  tile(D_B), tile(D_B), tile(Q_COLS), tile(D_MODEL), row(D_B), row(D_B),
                  full(D_B, D_B), full(D_MODEL, D_MODEL)] + _route_in_specs(),
        out_specs=[tile(D_MODEL)] + _route_out_specs(tm),
        out_shape=[jax.ShapeDtypeStruct((n, D_MODEL), F32)] + _route_out_shapes(n),
        scratch_shapes=[pltpu.VMEM((8, 128), F32)],
        compiler_params=_cparams("arbitrary"),
        name="mix0_out",
    )(o, bonus, g, attn, x, p['ln_w'], p['ln_b'], p['head_avg'], p['w_out'], rp['norm'], rp['w'], rp['b'])


def _row_gather_start(idx_ref, base, n_rows, src_hbm, dst, sem):
    def body(r, carry):
        src_row = pl.multiple_of(idx_ref[base + r] * ROW_CHUNKS, ROW_CHUNKS)
        dst_row = pl.multiple_of(r * ROW_CHUNKS, ROW_CHUNKS)
        pltpu.make_async_copy(src_hbm.at[pl.ds(src_row, ROW_CHUNKS)], dst.at[pl.ds(dst_row, ROW_CHUNKS)], sem).start()
        return carry

    lax.fori_loop(0, n_rows, body, 0, unroll=8)


def _row_gather_wait(dst, sem):
    pltpu.make_async_copy(dst, dst, sem).wait()


def _expert_kernel(te_ref, nu_ref, src_ref, x_hbm, wg_ref, wu_ref, wd_ref, o_ref, xbuf, sem):
    i = pl.program_id(0)
    nu = nu_ref[0]

    @pl.when(i == 0)
    def _():
        _row_gather_start(src_ref, 0, MOE_TILE, x_hbm, xbuf.at[0], sem.at[0])

    @pl.when(i + 1 < nu)
    def _():
        nxt = (i + 1) % 2
        _row_gather_start(src_ref, (i + 1) * MOE_TILE, MOE_TILE, x_hbm, xbuf.at[nxt], sem.at[nxt])

    @pl.when(i < nu)
    def _():
        cur = i % 2
        _row_gather_wait(xbuf.at[cur], sem.at[cur])
        x = _load_row_tiles(xbuf.at[cur]).astype(BF16)
        hg = _dot(x, wg_ref[0])
        hu = _dot(x, wu_ref[0])
        h = (hg * _sigmoid(hg)) * hu
        _store_row_tiles(o_ref, _dot(h.astype(BF16), wd_ref[0]))

    @pl.when(i >= nu)
    def _():
        o_ref[...] = jnp.zeros_like(o_ref)


def _experts(xn, src, tile_expert, n_used, wg, wu, wd):
    r = src.shape[0]
    grid_spec = pltpu.PrefetchScalarGridSpec(
        num_scalar_prefetch=3,
        grid=(r // MOE_TILE,),
        in_specs=[pl.BlockSpec(memory_space=pl.ANY),
                  pl.BlockSpec((1, D_MODEL, D_FF_E), lambda i, te, nu, sr: (te[i], 0, 0)),
                  pl.BlockSpec((1, D_MODEL, D_FF_E), lambda i, te, nu, sr: (te[i], 0, 0)),
                  pl.BlockSpec((1, D_FF_E, D_MODEL), lambda i, te, nu, sr: (te[i], 0, 0))],
        out_specs=pl.BlockSpec((MOE_TILE * ROW_CHUNKS, 128), lambda i, te, nu, sr: (i, 0)),
        scratch_shapes=[pltpu.VMEM((2, MOE_TILE * ROW_CHUNKS, 128), F32), pltpu.SemaphoreType.DMA((2,))],
    )
    return pl.pallas_call(
        _expert_kernel,
        grid_spec=grid_spec,
        out_shape=jax.ShapeDtypeStruct((r * ROW_CHUNKS, 128), F32),
        compiler_params=_cparams("arbitrary"),
        name="moe_experts",
    )(tile_expert, n_used, src, xn, wg, wu, wd)


def _moe(xn, idx, cnt, ep):
    n = xn.shape[0] // ROW_CHUNKS
    rows = 2 * n + N_EXPERTS * MOE_TILE
    counts = cnt[0, :N_EXPERTS].astype(I32)
    padded = ((counts + MOE_TILE - 1) // MOE_TILE) * MOE_TILE
    ends = jnp.cumsum(padded)
    offs = ends - padded
    pos1 = offs[idx[:, 0]] + idx[:, 2]
    pos2 = offs[idx[:, 1]] + idx[:, 3]
    tok = jnp.arange(n, dtype=I32)
    src = jnp.zeros((rows,), I32).at[pos1].set(tok).at[pos2].set(tok)
    n_used = (ends[-1] // MOE_TILE).astype(I32)
    starts = jnp.arange(rows // MOE_TILE, dtype=I32) * MOE_TILE
    starts = jnp.minimum(starts, ends[-1] - 1)
    tile_expert = jnp.sum((starts[:, None] >= ends[None, :]).astype(I32), axis=1)
    tile_expert = jnp.minimum(tile_expert, N_EXPERTS - 1).astype(I32)
    out = _experts(xn, src, tile_expert, n_used.reshape(1), ep['wg'], ep['wu'], ep['wd'])
    return out, pos1, pos2


def _combine_kernel(p1_ref, p2_ref, x_ref, wts_ref, nrm_ref, out_hbm, y_o, gbuf, sem, *, final):
    i = pl.program_id(0)
    tm = x_ref.shape[0]

    def start(tile, slot):
        _row_gather_start(p1_ref, tile * tm, tm, out_hbm, gbuf.at[slot, 0], sem.at[slot])
        _row_gather_start(p2_ref, tile * tm, tm, out_hbm, gbuf.at[slot, 1], sem.at[slot])

    @pl.when(i == 0)
    def _():
        start(0, 0)

    @pl.when(i + 1 < pl.num_programs(0))
    def _():
        start(i + 1, (i + 1) % 2)

    cur = i % 2
    _row_gather_wait(gbuf.at[cur], sem.at[cur])
    wts = wts_ref[...]
    y = x_ref[...] + wts[:, 0:1] * _load_row_tiles(gbuf.at[cur, 0]) + wts[:, 1:2] * _load_row_tiles(gbuf.at[cur, 1])
    if final:
        y = _rms(y, nrm_ref[...])
    y_o[...] = y


def _moe_combine(x, wts, out, pos1, pos2, nrm, final, tm=256):
    n = x.shape[0]
    tile = lambda w: pl.BlockSpec((tm, w), lambda i, p1, p2: (i, 0))
    grid_spec = pltpu.PrefetchScalarGridSpec(
        num_scalar_prefetch=2,
        grid=(n // tm,),
        in_specs=[tile(D_MODEL), tile(128), pl.BlockSpec((1, D_MODEL), lambda i, p1, p2: (0, 0)),
                  pl.BlockSpec(memory_space=pl.ANY)],
        out_specs=tile(D_MODEL),
        scratch_shapes=[pltpu.VMEM((2, 2, tm * ROW_CHUNKS, 128), F32), pltpu.SemaphoreType.DMA((2,))],
    )
    return pl.pallas_call(
        functools.partial(_combine_kernel, final=final),
        grid_spec=grid_spec,
        out_shape=jax.ShapeDtypeStruct((n, D_MODEL), F32),
        compiler_params=_cparams("arbitrary"),
        name="moe_combine_final" if final else "moe_combine",
    )(pos1, pos2, x, wts, nrm, out)


def _gelu_tanh(x):
    return 0.5 * x * (1.0 + jnp.tanh(0.7978845608028654 * (x + 0.044715 * (x * x * x))))


def _mix1_kernel(x_ref, nmix_ref, bre_ref, bim_ref, are_ref, aim_ref, cre_ref, cim_ref,
                 dsk_ref, wo_ref, h0r_ref, h0i_ref, nrm_ref, wr_ref, br_ref,
                 x2_o, xn_o, idx_o, wts_o, cnt_o, hr_o, hi_o,
                 bur, bui, hr_scr, hi_scr, cnt_scr, *, b, cw):
    @pl.when(pl.program_id(0) == 0)
    def _():
        cnt_scr[...] = jnp.zeros_like(cnt_scr)
        hr_scr[...] = h0r_ref[...]
        hi_scr[...] = h0i_ref[...]

    x = x_ref[...]
    u = _rms(x, nmix_ref[...])
    ub = u.astype(BF16)
    nblk = bre_ref.shape[0]
    kin = D_MODEL // nblk
    kst = S5_STATE // nblk
    for cb in range(nblk):
        ucb = ub[:, cb * kin:(cb + 1) * kin]
        bur[:, cb * kst:(cb + 1) * kst] = _dot(ucb, bre_ref[cb])
        bui[:, cb * kst:(cb + 1) * kst] = _dot(ucb, bim_ref[cb])

    tc = x.shape[0] // b
    for c0 in range(0, S5_STATE, cw):
        cs = slice(c0, c0 + cw)
        ar = jnp.broadcast_to(are_ref[:, cs], (b, cw))
        ai = jnp.broadcast_to(aim_ref[:, cs], (b, cw))

        def step(s, carry, cs=cs, ar=ar, ai=ai):
            hr, hi = carry
            rows = pl.ds(pl.multiple_of(s * b, b), b)
            nr = ar * hr - ai * hi + bur[rows, cs]
            ni = ar * hi + ai * hr + bui[rows, cs]
            bur[rows, cs] = nr
            bui[rows, cs] = ni
            return nr, ni

        hr, hi = lax.fori_loop(0, tc, step, (hr_scr[:, cs], hi_scr[:, cs]))
        hr_scr[:, cs] = hr
        hi_scr[:, cs] = hi

    ych = []
    for cb in range(nblk):
        ss = slice(cb * kst, (cb + 1) * kst)
        ych.append(_dot(bur[:, ss].astype(BF16), cre_ref[cb]) - _dot(bui[:, ss].astype(BF16), cim_ref[cb]))
    y = jnp.concatenate(ych, axis=1) + dsk_ref[...] * u
    z = _dot(_gelu_tanh(y).astype(BF16), wo_ref[...])
    x2 = x + z[:, :D_MODEL] * _sigmoid(z[:, D_MODEL:])
    x2_o[...] = x2
    _route_tile(x2, nrm_ref, wr_ref, br_ref, cnt_scr, xn_o, idx_o, wts_o)
    cnt_o[...] = cnt_scr[...]
    hr_o[...] = hr_scr[...]
    hi_o[...] = hi_scr[...]


def _mix1(x, sp, rp, h0r, h0i, b, tr):
    n = x.shape[0]
    cw = 1024 if b == 8 else 128
    tile = lambda w: pl.BlockSpec((tr, w), lambda i: (i, 0))
    row = lambda w: pl.BlockSpec((1, w), lambda i: (0, 0))
    full = lambda *s: pl.BlockSpec(s, lambda i: (0,) * len(s))
    nblk = sp['b_re'].shape[0]
    return pl.pallas_call(
        functools.partial(_mix1_kernel, b=b, cw=cw),
        grid=(n // tr,),
        in_specs=[tile(D_MODEL), row(D_MODEL),
                  full(nblk, D_MODEL // nblk, S5_STATE // nblk), full(nblk, D_MODEL // nblk, S5_STATE // nblk),
                  row(S5_STATE), row(S5_STATE),
                  full(nblk, S5_STATE // nblk, D_MODEL // nblk), full(nblk, S5_STATE // nblk, D_MODEL // nblk),
                  row(D_MODEL), full(D_MODEL, 2 * D_MODEL), full(b, S5_STATE), full(b, S5_STATE)] + _route_in_specs(),
        out_specs=[tile(D_MODEL)] + _route_out_specs(tr) + [full(b, S5_STATE), full(b, S5_STATE)],
        out_shape=[jax.ShapeDtypeStruct((n, D_MODEL), F32)] + _route_out_shapes(n)
                  + [jax.ShapeDtypeStruct((b, S5_STATE), F32)] * 2,
        scratch_shapes=[pltpu.VMEM((tr, S5_STATE), F32), pltpu.VMEM((tr, S5_STATE), F32),
                        pltpu.VMEM((b, S5_STATE), F32), pltpu.VMEM((b, S5_STATE), F32),
                        pltpu.VMEM((8, 128), F32)],
        compiler_params=_cparams("arbitrary"),
        name="mix1",
    )(x, sp['norm'], sp['b_re'], sp['b_im'], sp['a_re'], sp['a_im'], sp['c_re'], sp['c_im'],
      sp['d'], sp['w_out'], h0r, h0i, rp['norm'], rp['w'], rp['b'])


def _router_params(norm, w_rc, b_rc, w_rf, b_rf):
    w = jnp.concatenate([w_rc, w_rf.reshape(D_MODEL, N_EXPERTS), jnp.zeros((D_MODEL, 12), F32)], axis=1)
    hi = w.astype(BF16)
    lo = (w - hi.astype(F32)).astype(BF16)
    wcat = jnp.concatenate([hi, lo, jnp.zeros((D_MODEL, 64), BF16)], axis=1)
    bias = jnp.concatenate([b_rc, b_rf.reshape(-1), jnp.zeros((12,), F32)]).reshape(1, 32)
    return {'norm': norm.reshape(1, D_MODEL), 'w': wcat, 'b': bias}


def _expert_params(wg, wu, wd):
    return {'wg': wg.astype(BF16), 'wu': wu.astype(BF16), 'wd': wd.astype(BF16)}


def _s5_params(norm, a_re, a_im, log_dt, b_re, b_im, c_re, c_im, d_skip, w_out, nblk=8):
    dt = jnp.exp(log_dt)
    mag = jnp.exp(dt * a_re)
    ab_re, ab_im = mag * jnp.cos(dt * a_im), mag * jnp.sin(dt * a_im)
    den = a_re * a_re + a_im * a_im
    f_re = ((ab_re - 1.0) * a_re + ab_im * a_im) / den
    f_im = (ab_im * a_re - (ab_re - 1.0) * a_im) / den
    bb_re = f_re[..., None] * b_re - f_im[..., None] * b_im
    bb_im = f_re[..., None] * b_im + f_im[..., None] * b_re
    gpb = S5_GROUPS // nblk
    eye = jnp.eye(gpb, dtype=F32)

    def in_blocks(bb):
        bb = bb.reshape(nblk, gpb, S5_P, S5_CH)
        w = jnp.einsum('ngpc,gh->ngchp', bb, eye)
        return w.reshape(nblk, gpb * S5_CH, gpb * S5_P).astype(BF16)

    def out_blocks(cc):
        cc = cc.reshape(nblk, gpb, S5_CH, S5_P)
        w = jnp.einsum('ngcp,gh->ngphc', cc, eye)
        return w.reshape(nblk, gpb * S5_P, gpb * S5_CH).astype(BF16)

    return {'norm': norm.reshape(1, D_MODEL), 'b_re': in_blocks(bb_re), 'b_im': in_blocks(bb_im),
            'a_re': ab_re.reshape(1, S5_STATE), 'a_im': ab_im.reshape(1, S5_STATE),
            'c_re': out_blocks(c_re), 'c_im': out_blocks(c_im), 'd': d_skip.reshape(1, D_MODEL),
            'w_out': w_out.astype(BF16)}


def _head_block(value):
    hid = jnp.arange(D_B, dtype=I32) // HEAD_DIM
    return jnp.where(hid[:, None] == hid[None, :], value, 0.0).astype(BF16)


def _run_group(x_tm, b, t, cache_k, cache_v, shift0, wkv0, h0r, h0i, pr):
    n = b * t
    prompt = cache_k is None
    tm = 256 if prompt else 128
    q, kv, pb = _in_proj(x_tm, pr['l0_norm'], pr['l0_w_in'], tm)

    if prompt:
        attn = _attn_prompt(q, kv, pr['sinks'], b, t)
        kv3 = kv.reshape(t, b, 2, N_KV_A, HEAD_DIM)
        new_k = kv3[t - WINDOW:, :, 0].transpose(1, 0, 2, 3)
        new_v = kv3[t - WINDOW:, :, 1].transpose(1, 0, 2, 3)
        init = jnp.zeros((b, D_B_IN), F32)
    else:
        qs = q.astype(F32).reshape(t, b, Q_COLS).transpose(1, 0, 2)
        kvs = kv.reshape(t, b, 2 * KV_COLS).transpose(1, 0, 2)
        kn, vn = kvs[..., :KV_COLS], kvs[..., KV_COLS:]
        ck = cache_k.reshape(b, WINDOW, KV_COLS)
        cv = cache_v.reshape(b, WINDOW, KV_COLS)
        attn = _attn_sample(qs, kn, vn, ck, cv, pr['sinks'])
        attn = attn.transpose(1, 0, 2).reshape(n, Q_COLS).astype(BF16)
        new_k = jnp.concatenate([ck[:, t:], kn], axis=1).reshape(b, WINDOW, N_KV_A, HEAD_DIM)
        new_v = jnp.concatenate([cv[:, t:], vn], axis=1).reshape(b, WINDOW, N_KV_A, HEAD_DIM)
        init = shift0
    new_shift = pb[n - b:]

    r, w, k, v, kk, nkka, bonus, g = _rwkv_prep(pb, init, pr['rw'], b, tm)
    tc = 64 if prompt else t
    s0 = jnp.zeros((b // SEQ_PER_GROUP, HEAD_DIM, HEAD_DIM // 2, 128), F32) if prompt else _state_to_scan(wkv0, b)
    o, s_fin = _wkv_scan(r, w, k, kk, nkka, v, s0, b, t, tc)
    new_wkv = _state_from_scan(s_fin, b)

    x1, xn, idx, wts, cnt = _mix0_out(o, bonus, g, attn, x_tm, pr['rw'], pr['l0_route'], 256)
    out, pos1, pos2 = _moe(xn, idx, cnt, pr['l0_exp'])
    x1 = _moe_combine(x1, wts, out, pos1, pos2, pr['final_norm'], False)

    x2, xn, idx, wts, cnt, hr, hi = _mix1(x1, pr['s5'], pr['l1_route'], h0r, h0i, b, 256)
    out, pos1, pos2 = _moe(xn, idx, cnt, pr['l1_exp'])
    y = _moe_combine(x2, wts, out, pos1, pos2, pr['final_norm'], True)
    return (y, new_k, new_v, new_shift, new_wkv,
            hr.reshape(b, S5_GROUPS, S5_P), hi.reshape(b, S5_GROUPS, S5_P))


def kernel(x_prompt, x_sample, cache_win_k, cache_win_v, state_shift, state_wkv, state_s5_re, state_s5_im,
           l0_norm_mix, l0_w_in, l0_sinks, l0_mu, l0_w0, l0_w_lora_up, l0_a0, l0_a_lora_up, l0_g_lora_up,
           l0_k_k, l0_k_a, l0_r_k, l0_ln_w, l0_ln_b, l0_w_out,
           l0_norm_ffn, l0_router_coarse, l0_bias_coarse, l0_router_fine, l0_bias_fine,
           l0_exp_gate, l0_exp_up, l0_exp_down,
           l1_norm_mix, l1_s5_a_re, l1_s5_a_im, l1_s5_log_dt, l1_s5_b_re, l1_s5_b_im, l1_s5_c_re, l1_s5_c_im,
           l1_s5_d, l1_w_out,
           l1_norm_ffn, l1_router_coarse, l1_bias_coarse, l1_router_fine, l1_bias_fine,
           l1_exp_gate, l1_exp_up, l1_exp_down,
           final_norm):
    row = lambda z: z.reshape(1, -1)
    pr = {
        'l0_norm': row(l0_norm_mix), 'l0_w_in': l0_w_in.astype(BF16), 'sinks': l0_sinks,
        'rw': {'mu': row(l0_mu), 'w0': row(l0_w0), 'w_up': l0_w_lora_up.astype(BF16), 'a0': row(l0_a0),
               'a_up': l0_a_lora_up.astype(BF16), 'g_up': l0_g_lora_up.astype(BF16), 'k_k': row(l0_k_k),
               'k_a': row(l0_k_a), 'r_k': row(l0_r_k), 'ln_w': row(l0_ln_w), 'ln_b': row(l0_ln_b),
               'head_ones': _head_block(1.0), 'head_avg': _head_block(1.0 / HEAD_DIM),
               'w_out': l0_w_out.astype(BF16)},
        'l0_route': _router_params(l0_norm_ffn, l0_router_coarse, l0_bias_coarse, l0_router_fine, l0_bias_fine),
        'l0_exp': _expert_params(l0_exp_gate, l0_exp_up, l0_exp_down),
        's5': _s5_params(l1_norm_mix, l1_s5_a_re, l1_s5_a_im, l1_s5_log_dt, l1_s5_b_re, l1_s5_b_im,
                         l1_s5_c_re, l1_s5_c_im, l1_s5_d, l1_w_out),
        'l1_route': _router_params(l1_norm_ffn, l1_router_coarse, l1_bias_coarse, l1_router_fine, l1_bias_fine),
        'l1_exp': _expert_params(l1_exp_gate, l1_exp_up, l1_exp_down),
        'final_norm': row(final_norm),
    }
    bp, tp = x_prompt.shape[0], x_prompt.shape[1]
    bs, ts = x_sample.shape[0], x_sample.shape[1]
    xp = x_prompt.transpose(1, 0, 2).reshape(bp * tp, D_MODEL)
    xs = x_sample.transpose(1, 0, 2).reshape(bs * ts, D_MODEL)
    zero_state = jnp.zeros((bp, S5_STATE), F32)
    yp, pk, pv, psh, pwkv, pre, pim = _run_group(xp, bp, tp, None, None, None, None, zero_state, zero_state, pr)
    ys, sk, sv, ssh, swkv, sre, sim = _run_group(
        xs, bs, ts, cache_win_k, cache_win_v, state_shift, state_wkv,
        state_s5_re.reshape(bs, S5_STATE), state_s5_im.reshape(bs, S5_STATE), pr)
    y_prompt = yp.reshape(tp, bp, D_MODEL).transpose(1, 0, 2)
    y_sample = ys.reshape(ts, bs, D_MODEL).transpose(1, 0, 2)
    return (y_prompt, y_sample, pk, pv, psh, pwkv, pre, pim, sk, sv, ssh, swkv, sre, sim)
```

```python
import functools

import jax
import jax.numpy as jnp
from jax import lax
from jax.experimental import pallas as pl
from jax.experimental.pallas import tpu as pltpu

F32 = jnp.float32
BF16 = jnp.bfloat16
I32 = jnp.int32

D_MODEL = 1024
HEAD_DIM = 64
N_HEADS_A = 8
N_KV_A = 2
GQA_GROUP = 4
WINDOW = 128
Q_COLS = 512
KV_COLS = 128
D_A_IN = 768
N_HEADS_B = 8
D_B = 512
D_LORA_W = 64
D_LORA_A = 64
D_LORA_G = 128
D_B_IN = 1792
D_IN0 = 2560
RWKV_GN_EPS = 64e-5
S5_CH = 16
S5_GROUPS = 64
S5_P = 64
S5_STATE = S5_GROUPS * S5_P
N_EGROUPS = 4
EXP_PER_GROUP = 4
N_EXPERTS = 16
D_FF_E = 512
RMS_EPS = 1e-5
NEG_BIG = -1e30
PAIRS = 64
SEQ_PER_GROUP = PAIRS // N_HEADS_B
MOE_TILE = 256
ROW_CHUNKS = D_MODEL // 128
VMEM_LIMIT = 56 * 1024 * 1024


def _cparams(*sem):
    return pltpu.CompilerParams(dimension_semantics=sem, vmem_limit_bytes=VMEM_LIMIT)


def _dot(a, b):
    return jnp.dot(a, b, preferred_element_type=F32)


def _split_bf16(x):
    hi = x.astype(BF16)
    lo = (x - hi.astype(F32)).astype(BF16)
    return hi, lo


def _dot2(x, w):
    hi, lo = _split_bf16(x)
    return _dot(hi, w) + _dot(lo, w)


def _rms(x, g):
    return x * lax.rsqrt(jnp.mean(x * x, axis=-1, keepdims=True) + RMS_EPS) * g


def _sigmoid(x):
    return 1.0 / (1.0 + jnp.exp(-x))


def _store_row_tiles(ref, x):
    rows = x.shape[0]
    for c in range(ROW_CHUNKS):
        ref[pl.ds(c, rows, stride=ROW_CHUNKS), :] = x[:, c * 128:(c + 1) * 128]


def _load_row_tiles(ref):
    rows = ref.shape[0] // ROW_CHUNKS
    return jnp.concatenate([ref[pl.ds(c, rows, stride=ROW_CHUNKS), :] for c in range(ROW_CHUNKS)], axis=1)


def _in_proj_kernel(x_ref, g_ref, w_ref, q_ref, kv_ref, pb_ref):
    xn = _rms(x_ref[...], g_ref[...]).astype(BF16)
    q_ref[...] = _dot(xn, w_ref[:, :Q_COLS]).astype(BF16)
    kv_ref[...] = _dot(xn, w_ref[:, Q_COLS:D_A_IN])
    pb_ref[...] = _dot(xn, w_ref[:, D_A_IN:])


def _in_proj(x, g, w_bf16, tm):
    n = x.shape[0]
    return pl.pallas_call(
        _in_proj_kernel,
        grid=(n // tm,),
        in_specs=[pl.BlockSpec((tm, D_MODEL), lambda i: (i, 0)),
                  pl.BlockSpec((1, D_MODEL), lambda i: (0, 0)),
                  pl.BlockSpec((D_MODEL, D_IN0), lambda i: (0, 0))],
        out_specs=[pl.BlockSpec((tm, Q_COLS), lambda i: (i, 0)),
                   pl.BlockSpec((tm, 2 * KV_COLS), lambda i: (i, 0)),
                   pl.BlockSpec((tm, D_B_IN), lambda i: (i, 0))],
        out_shape=[jax.ShapeDtypeStruct((n, Q_COLS), BF16),
                   jax.ShapeDtypeStruct((n, 2 * KV_COLS), F32),
                   jax.ShapeDtypeStruct((n, D_B_IN), F32)],
        compiler_params=_cparams("parallel"),
        name="in_proj",
    )(x, g, w_bf16)


def _attn_prompt_kernel(sinks_ref, q_ref, kc_ref, kp_ref, vc_ref, vp_ref, o_ref):
    j = pl.program_id(1)
    qi = lax.broadcasted_iota(I32, (WINDOW, 2 * WINDOW), 0)
    kj = lax.broadcasted_iota(I32, (WINDOW, 2 * WINDOW), 1)
    valid = jnp.logical_and(kj > qi, kj <= qi + WINDOW)
    valid = jnp.logical_and(valid, jnp.logical_or(kj >= WINDOW, j > 0))
    dist = (WINDOW + qi - kj).astype(F32)
    for n in range(N_KV_A):
        cs = slice(n * HEAD_DIM, (n + 1) * HEAD_DIM)
        kb = jnp.concatenate([kp_ref[:, cs], kc_ref[:, cs]], axis=0).astype(BF16)
        vb = jnp.concatenate([vp_ref[:, cs], vc_ref[:, cs]], axis=0).astype(BF16)
        for g in range(GQA_GROUP):
            h = n * GQA_GROUP + g
            hs = slice(h * HEAD_DIM, (h + 1) * HEAD_DIM)
            s = lax.dot_general(q_ref[:, hs], kb, (((1,), (1,)), ((), ())), preferred_element_type=F32)
            s = s * (HEAD_DIM ** -0.5) - (2.0 ** -(h + 1)) * dist
            s = jnp.where(valid, s, NEG_BIG)
            sink = sinks_ref[h]
            m = jnp.maximum(jnp.max(s, axis=1, keepdims=True), sink)
            p = jnp.exp(s - m)
            l = jnp.sum(p, axis=1, keepdims=True) + jnp.exp(sink - m)
            o = _dot(p.astype(BF16), vb) / l
            o_ref[:, hs] = o.astype(BF16)


def _attn_prompt(q, kv, sinks, b, t):
    q2 = q.reshape(t, b * Q_COLS)
    kv2 = kv.reshape(t, b * 2 * KV_COLS)
    prev = lambda bi, j: jnp.maximum(j - 1, 0)
    out = pl.pallas_call(
        _attn_prompt_kernel,
        grid=(b, t // WINDOW),
        in_specs=[pl.BlockSpec(memory_space=pltpu.SMEM),
                  pl.BlockSpec((WINDOW, Q_COLS), lambda bi, j: (j, bi)),
                  pl.BlockSpec((WINDOW, KV_COLS), lambda bi, j: (j, 2 * bi)),
                  pl.BlockSpec((WINDOW, KV_COLS), lambda bi, j: (prev(bi, j), 2 * bi)),
                  pl.BlockSpec((WINDOW, KV_COLS), lambda bi, j: (j, 2 * bi + 1)),
                  pl.BlockSpec((WINDOW, KV_COLS), lambda bi, j: (prev(bi, j), 2 * bi + 1))],
        out_specs=pl.BlockSpec((WINDOW, Q_COLS), lambda bi, j: (j, bi)),
        out_shape=jax.ShapeDtypeStruct((t, b * Q_COLS), BF16),
        compiler_params=_cparams("parallel", "parallel"),
        name="attn_prompt",
    )(sinks, q2, kv2, kv2, kv2, kv2)
    return out.reshape(t * b, Q_COLS)


def _attn_sample_kernel(sinks_ref, q_ref, kn_ref, vn_ref, ck_ref, cv_ref, o_ref):
    bs, t = q_ref.shape[0], q_ref.shape[1]
    assert t & (t - 1) == 0
    nq, nk = GQA_GROUP * t, 2 * WINDOW
    r = lax.broadcasted_iota(I32, (nq, nk), 0)
    kj = lax.broadcasted_iota(I32, (nq, nk), 1)
    tq = jnp.bitwise_and(r, t - 1)
    valid = jnp.logical_and(kj > tq, kj <= tq + WINDOW)
    dist = (WINDOW + tq - kj).astype(F32)
    grp = jnp.right_shift(lax.broadcasted_iota(I32, (nq, 1), 0), t.bit_length() - 1)
    pad = jnp.zeros((bs, nk - WINDOW - t, HEAD_DIM), F32)
    for n in range(N_KV_A):
        cs = slice(n * HEAD_DIM, (n + 1) * HEAD_DIM)
        kb = jnp.concatenate([ck_ref[:, :, cs], kn_ref[:, :, cs], pad], axis=1).astype(BF16)
        vb = jnp.concatenate([cv_ref[:, :, cs], vn_ref[:, :, cs], pad], axis=1).astype(BF16)
        qn = jnp.concatenate([q_ref[:, :, (n * GQA_GROUP + g) * HEAD_DIM:(n * GQA_GROUP + g + 1) * HEAD_DIM]
                              for g in range(GQA_GROUP)], axis=1).astype(BF16)
        slope = jnp.zeros((nq, 1), F32)
        sink = jnp.zeros((nq, 1), F32)
        for g in range(GQA_GROUP):
            h = n * GQA_GROUP + g
            slope = jnp.where(grp == g, 2.0 ** -(h + 1), slope)
            sink = jnp.where(grp == g, sinks_ref[h], sink)
        s = jnp.einsum('bqd,bkd->bqk', qn, kb, preferred_element_type=F32)
        s = s * (HEAD_DIM ** -0.5) - (slope * dist)[None]
        s = jnp.where(valid[None], s, NEG_BIG)
        m = jnp.maximum(jnp.max(s, axis=2, keepdims=True), sink[None])
        p = jnp.exp(s - m)
        l = jnp.sum(p, axis=2, keepdims=True) + jnp.exp(sink[None] - m)
        o = jnp.einsum('bqk,bkd->bqd', p.astype(BF16), vb, preferred_element_type=F32) / l
        for g in range(GQA_GROUP):
            h = n * GQA_GROUP + g
            o_ref[:, :, h * HEAD_DIM:(h + 1) * HEAD_DIM] = o[:, g * t:(g + 1) * t, :]


def _attn_sample(q, kn, vn, ck, cv, sinks, bs=16):
    db, t = q.shape[0], q.shape[1]
    seq3 = lambda w: pl.BlockSpec((bs, t, w), lambda i: (i, 0, 0))
    cache = pl.BlockSpec((bs, WINDOW, KV_COLS), lambda i: (i, 0, 0))
    return pl.pallas_call(
        _attn_sample_kernel,
        grid=(db // bs,),
        in_specs=[pl.BlockSpec(memory_space=pltpu.SMEM), seq3(Q_COLS), seq3(KV_COLS), seq3(KV_COLS), cache, cache],
        out_specs=seq3(Q_COLS),
        out_shape=jax.ShapeDtypeStruct((db, t, Q_COLS), F32),
        compiler_params=_cparams("parallel"),
        name="attn_sample",
    )(sinks, q, kn, vn, ck, cv)


def _rwkv_prep_kernel(pb_ref, halo_ref, init_ref, mu_ref, w0_ref, wup_ref, a0_ref, aup_ref, gup_ref,
                      kk_ref, ka_ref, rk_ref, ones_ref,
                      r_o, w_o, k_o, v_o, kk_o, nkka_o, bonus_o, g_o, *, b):
    i = pl.program_id(0)
    pb = pb_ref[...]
    tm = pb.shape[0]
    halo = jnp.where(i == 0, init_ref[...], halo_ref[...])
    prev = halo if tm == b else jnp.concatenate([halo, pb[:tm - b]], axis=0)
    xs = pb + (prev - pb) * mu_ref[...]
    r = xs[:, :D_B]
    k = xs[:, D_B:2 * D_B]
    v = xs[:, 2 * D_B:3 * D_B]
    o1 = 3 * D_B
    wd = xs[:, o1:o1 + D_LORA_W]
    ad = xs[:, o1 + D_LORA_W:o1 + D_LORA_W + D_LORA_A]
    gd = xs[:, o1 + D_LORA_W + D_LORA_A:]
    z = -(w0_ref[...] + _dot(jnp.tanh(wd).astype(BF16), wup_ref[...]))
    softplus = jnp.maximum(z, 0.0) + jnp.log(1.0 + jnp.exp(-jnp.abs(z)))
    decay = jnp.exp(-jnp.exp(-softplus - 0.5))
    a = _sigmoid(a0_ref[...] + _dot(ad.astype(BF16), aup_ref[...]))
    g_o[...] = _dot(_sigmoid(gd).astype(BF16), gup_ref[...])
    ones = ones_ref[...]
    kk = k * kk_ref[...]
    kk = kk * lax.rsqrt(jnp.maximum(_dot2(kk * kk, ones), 1e-24))
    k2 = k * (1.0 + (a - 1.0) * ka_ref[...])
    bonus_o[...] = _dot2(r * k2 * rk_ref[...], ones) * v

    half = HEAD_DIM // 2
    lane8 = lax.broadcasted_iota(I32, (SEQ_PER_GROUP, 128), 1)
    low8 = lane8 < HEAD_DIM
    first_copy = jnp.bitwise_and(lax.broadcasted_iota(I32, (half, 128), 1), SEQ_PER_GROUP) == 0
    pairs = ((r, decay, r_o, w_o), (k2, kk, k_o, kk_o), (-(kk * a), v, nkka_o, None))
    for u in range(tm // SEQ_PER_GROUP):
        rows = slice(u * SEQ_PER_GROUP, (u + 1) * SEQ_PER_GROUP)
        for x, y, x_o, y_o in pairs:
            xu, yu = x[rows], y[rows]
            pieces = []
            for h in range(N_HEADS_B):
                cs = slice((h // 2) * 128, (h // 2 + 1) * 128)
                if h % 2 == 0:
                    p = jnp.where(low8, xu[:, cs], pltpu.roll(yu[:, cs], HEAD_DIM, 1))
                else:
                    p = jnp.where(low8, pltpu.roll(xu[:, cs], HEAD_DIM, 1), yu[:, cs])
                pieces += [p, p]
            tr = jnp.concatenate(pieces, axis=0).T
            x_o[u] = tr[:HEAD_DIM]
            if y_o is not None:
                y_o[u] = tr[HEAD_DIM:]
            else:
                v_o[u] = jnp.where(first_copy, tr[HEAD_DIM:HEAD_DIM + half], tr[HEAD_DIM + half:])


def _rwkv_prep(pb, init, p, b, tm):
    n = pb.shape[0]
    units = n // SEQ_PER_GROUP
    tu = tm // SEQ_PER_GROUP
    half = HEAD_DIM // 2
    row = lambda w: pl.BlockSpec((1, w), lambda i: (0, 0))
    full = lambda r, c: pl.BlockSpec((r, c), lambda i: (0, 0))
    tile = pl.BlockSpec((tm, D_B), lambda i: (i, 0))
    kspec = pl.BlockSpec((tu, HEAD_DIM, 128), lambda i: (i, 0, 0))
    vspec = pl.BlockSpec((tu, half, 128), lambda i: (i, 0, 0))
    kshape = jax.ShapeDtypeStruct((units, HEAD_DIM, 128), F32)
    halo_blocks = tm // b
    return pl.pallas_call(
        functools.partial(_rwkv_prep_kernel, b=b),
        grid=(n // tm,),
        in_specs=[pl.BlockSpec((tm, D_B_IN), lambda i: (i, 0)),
                  pl.BlockSpec((b, D_B_IN), lambda i: (jnp.maximum(i * halo_blocks - 1, 0), 0)),
                  full(b, D_B_IN), row(D_B_IN), row(D_B), full(D_LORA_W, D_B), row(D_B), full(D_LORA_A, D_B),
                  full(D_LORA_G, D_B), row(D_B), row(D_B), row(D_B), full(D_B, D_B)],
        out_specs=[kspec, kspec, kspec, vspec, kspec, kspec, tile, tile],
        out_shape=[kshape, kshape, kshape, jax.ShapeDtypeStruct((units, half, 128), F32), kshape, kshape,
                   jax.ShapeDtypeStruct((n, D_B), F32), jax.ShapeDtypeStruct((n, D_B), F32)],
        compiler_params=_cparams("arbitrary"),
        name="rwkv_prep",
    )(pb, pb, init, p['mu'], p['w0'], p['w_up'], p['a0'], p['a_up'], p['g_up'], p['k_k'], p['k_a'], p['r_k'],
      p['head_ones'])


def _wkv_scan_kernel(r_ref, w_ref, k_ref, kk_ref, nkka_ref, v_ref, s0_ref, o_ref, st_ref, s_scr):
    j = pl.program_id(1)

    @pl.when(j == 0)
    def _():
        s_scr[...] = s0_ref[0]

    tc = r_ref.shape[0]
    nsub = (HEAD_DIM // 2) // 8

    def bcast(ref, s, kx):
        return jnp.broadcast_to(ref[s, pl.ds(kx, 1), :], (8, 128))

    acc0 = [[jnp.zeros((8, 128), F32) for _ in range(2)] for _ in range(nsub)]
    for kx in range(HEAD_DIM):
        kkr = bcast(kk_ref, 0, kx)
        for i in range(nsub):
            acc0[i][kx % 2] = acc0[i][kx % 2] + s_scr[kx, 8 * i:8 * i + 8, :] * kkr

    def step(s, sa):
        nxt = jnp.minimum(s + 1, tc - 1)
        vv = [v_ref[s, 8 * i:8 * i + 8, :] for i in range(nsub)]
        oacc = [[jnp.zeros((8, 128), F32) for _ in range(2)] for _ in range(nsub)]
        nacc = [[jnp.zeros((8, 128), F32) for _ in range(2)] for _ in range(nsub)]
        for kx in range(HEAD_DIM):
            rr, wr, kr = bcast(r_ref, s, kx), bcast(w_ref, s, kx), bcast(k_ref, s, kx)
            nk, kkn = bcast(nkka_ref, s, kx), bcast(kk_ref, nxt, kx)
            for i in range(nsub):
                rows = slice(8 * i, 8 * i + 8)
                sk = s_scr[kx, rows, :] * wr + sa[i] * nk + vv[i] * kr
                s_scr[kx, rows, :] = sk
                oacc[i][kx % 2] = oacc[i][kx % 2] + sk * rr
                nacc[i][kx % 2] = nacc[i][kx % 2] + sk * kkn
        o_ref[s] = jnp.concatenate([a[0] + a[1] for a in oacc], axis=0)
        return [a[0] + a[1] for a in nacc]

    lax.fori_loop(0, tc, step, [a[0] + a[1] for a in acc0])

    @pl.when(j == pl.num_programs(1) - 1)
    def _():
        st_ref[0] = s_scr[...]


def _wkv_scan(r, w, k, kk, nkka, v, s0, b, t, tc):
    g = b // SEQ_PER_GROUP
    half = HEAD_DIM // 2
    kview = lambda z: z.reshape(t, g, HEAD_DIM, 128)
    kspec = pl.BlockSpec((tc, None, HEAD_DIM, 128), lambda gi, j: (j, gi, 0, 0))
    vspec = pl.BlockSpec((tc, None, half, 128), lambda gi, j: (j, gi, 0, 0))
    sspec = pl.BlockSpec((1, HEAD_DIM, half, 128), lambda gi, j: (gi, 0, 0, 0))
    o, st = pl.pallas_call(
        _wkv_scan_kernel,
        grid=(g, t // tc),
        in_specs=[kspec] * 5 + [vspec, sspec],
        out_specs=[vspec, sspec],
        out_shape=[jax.ShapeDtypeStruct((t, g, half, 128), F32),
                   jax.ShapeDtypeStruct((g, HEAD_DIM, half, 128), F32)],
        scratch_shapes=[pltpu.VMEM((HEAD_DIM, half, 128), F32)],
        compiler_params=_cparams("parallel", "arbitrary"),
        name="wkv_scan",
    )(kview(r), kview(w), kview(k), kview(kk), kview(nkka), v.reshape(t, g, half, 128), s0)
    return o.reshape(t * g, half, 128), st


def _state_to_scan(s, b):
    g = b // SEQ_PER_GROUP
    s = s.reshape(g, SEQ_PER_GROUP, N_HEADS_B, 2, HEAD_DIM // 2, HEAD_DIM).transpose(0, 5, 4, 2, 3, 1)
    return s.reshape(g, HEAD_DIM, HEAD_DIM // 2, 128)


def _state_from_scan(s, b):
    g = b // SEQ_PER_GROUP
    s = s.reshape(g, HEAD_DIM, HEAD_DIM // 2, N_HEADS_B, 2, SEQ_PER_GROUP).transpose(0, 5, 3, 4, 2, 1)
    return s.reshape(b, N_HEADS_B, HEAD_DIM, HEAD_DIM)


def _route_tile(x, nrm_ref, wr_ref, br_ref, cnt_scr, xn_o, idx_o, wts_o):
    tm = x.shape[0]
    xn = _rms(x, nrm_ref[...])
    hi, lo = _split_bf16(xn)
    _store_row_tiles(xn_o, xn)
    wr = wr_ref[...]
    pa = _dot(hi, wr)
    pb = _dot(lo, wr)
    lg = pa[:, 0:32] + pa[:, 32:64] + pb[:, 0:32] + pb[:, 32:64] + br_ref[...]
    col = lambda c: lg[:, c:c + 1]
    c = [col(gx) for gx in range(N_EGROUPS)]
    m = jnp.maximum(jnp.maximum(c[0], c[1]), jnp.maximum(c[2], c[3]))
    den = jnp.exp(c[0] - m) + jnp.exp(c[1] - m) + jnp.exp(c[2] - m) + jnp.exp(c[3] - m)
    pg = 1.0 / den
    gi = jnp.where(c[0] >= m, 0, jnp.where(c[1] >= m, 1, jnp.where(c[2] >= m, 2, 3))).astype(I32)
    sel = []
    for e in range(EXP_PER_GROUP):
        sel.append(jnp.where(gi == 0, col(4 + e), jnp.where(gi == 1, col(8 + e),
                                                            jnp.where(gi == 2, col(12 + e), col(16 + e)))))
    v1 = jnp.maximum(jnp.maximum(sel[0], sel[1]), jnp.maximum(sel[2], sel[3]))
    i1 = jnp.where(sel[0] >= v1, 0, jnp.where(sel[1] >= v1, 1, jnp.where(sel[2] >= v1, 2, 3))).astype(I32)
    rest = [jnp.where(i1 == e, -jnp.inf, sel[e]) for e in range(EXP_PER_GROUP)]
    v2 = jnp.maximum(jnp.maximum(rest[0], rest[1]), jnp.maximum(rest[2], rest[3]))
    i2 = jnp.where(rest[0] >= v2, 0, jnp.where(rest[1] >= v2, 1, jnp.where(rest[2] >= v2, 2, 3))).astype(I32)
    tt = jnp.exp(v2 - v1)
    w1 = pg / (1.0 + tt)
    w2 = pg * tt / (1.0 + tt)
    e1 = gi * EXP_PER_GROUP + i1
    e2 = gi * EXP_PER_GROUP + i2
    lane = lax.broadcasted_iota(I32, (tm, N_EXPERTS), 1)
    oh1 = lane == e1
    oh2 = lane == e2
    oh = jnp.where(jnp.logical_or(oh1, oh2), 1.0, 0.0)
    ri = lax.broadcasted_iota(I32, (tm, tm), 0)
    ci = lax.broadcasted_iota(I32, (tm, tm), 1)
    ltri = jnp.where(ri > ci, 1.0, 0.0).astype(BF16)
    cnt = cnt_scr[0:1, 0:N_EXPERTS]
    pre = _dot(ltri, oh.astype(BF16)) + cnt
    rank1 = jnp.sum(jnp.where(oh1, pre, 0.0), axis=1, keepdims=True)
    rank2 = jnp.sum(jnp.where(oh2, pre, 0.0), axis=1, keepdims=True)
    cnt_scr[0:1, 0:N_EXPERTS] = cnt + jnp.sum(oh, axis=0, keepdims=True)
    idx_o[:, 0:1] = e1
    idx_o[:, 1:2] = e2
    idx_o[:, 2:3] = rank1.astype(I32)
    idx_o[:, 3:4] = rank2.astype(I32)
    lw = lax.broadcasted_iota(I32, (tm, 128), 1)
    wts_o[...] = jnp.where(lw == 0, w1, jnp.where(lw == 1, w2, 0.0))


def _route_out_specs(tm):
    return [pl.BlockSpec((tm * ROW_CHUNKS, 128), lambda i: (i, 0)),
            pl.BlockSpec((tm, 4), lambda i: (i, 0)),
            pl.BlockSpec((tm, 128), lambda i: (i, 0)),
            pl.BlockSpec((8, 128), lambda i: (0, 0))]


def _route_out_shapes(n):
    return [jax.ShapeDtypeStruct((n * ROW_CHUNKS, 128), F32),
            jax.ShapeDtypeStruct((n, 4), I32),
            jax.ShapeDtypeStruct((n, 128), F32),
            jax.ShapeDtypeStruct((8, 128), F32)]


def _route_in_specs():
    return [pl.BlockSpec((1, D_MODEL), lambda i: (0, 0)),
            pl.BlockSpec((D_MODEL, 128), lambda i: (0, 0)),
            pl.BlockSpec((1, 32), lambda i: (0, 0))]


def _mix0_out_kernel(o_ref, bonus_ref, g_ref, attn_ref, x_ref, lnw_ref, lnb_ref, avg_ref, wo_ref,
                     nrm_ref, wr_ref, br_ref,
                     x1_o, xn_o, idx_o, wts_o, cnt_o, cnt_scr):
    @pl.when(pl.program_id(0) == 0)
    def _():
        cnt_scr[...] = jnp.zeros_like(cnt_scr)

    avg = avg_ref[...]
    half = HEAD_DIM // 2
    lane8 = lax.broadcasted_iota(I32, (SEQ_PER_GROUP, 128), 1)
    unit_rows = []
    for u in range(o_ref.shape[0]):
        ot = jnp.concatenate([o_ref[u], jnp.zeros((128 - half, 128), F32)], axis=0).T
        cols = []
        for jj in range(N_HEADS_B // 2):
            q = [ot[(4 * jj + i) * SEQ_PER_GROUP:(4 * jj + i + 1) * SEQ_PER_GROUP] for i in range(4)]
            c = jnp.where(lane8 < half, q[0], pltpu.roll(q[1], half, 1))
            c = jnp.where(lane8 < 2 * half, c, pltpu.roll(q[2], 2 * half, 1))
            c = jnp.where(lane8 < 3 * half, c, pltpu.roll(q[3], 3 * half, 1))
            cols.append(c)
        unit_rows.append(jnp.concatenate(cols, axis=1))
    o = jnp.concatenate(unit_rows, axis=0)
    d = o - _dot2(o, avg)
    var = _dot2(d * d, avg)
    on = d * lax.rsqrt(var + RWKV_GN_EPS) * lnw_ref[...] + lnb_ref[...]
    rout = ((on + bonus_ref[...]) * g_ref[...]).astype(BF16)
    y = _dot(attn_ref[...], wo_ref[:Q_COLS, :]) + _dot(rout, wo_ref[Q_COLS:, :]) + x_ref[...]
    x1_o[...] = y
    _route_tile(y, nrm_ref, wr_ref, br_ref, cnt_scr, xn_o, idx_o, wts_o)
    cnt_o[...] = cnt_scr[...]


def _mix0_out(o, bonus, g, attn, x, p, rp, tm):
    n = x.shape[0]
    tile = lambda w: pl.BlockSpec((tm, w), lambda i: (i, 0))
    row = lambda w: pl.BlockSpec((1, w), lambda i: (0, 0))
    full = lambda r, c: pl.BlockSpec((r, c), lambda i: (0, 0))
    return pl.pallas_call(
        _mix0_out_kernel,
        grid=(n // tm,),
        in_specs=[pl.BlockSpec((tm // SEQ_PER_GROUP, HEAD_DIM // 2, 128), lambda i: (i, 0, 0)),
                  tile(D_B), tile(D_B), tile(Q_COLS), tile(D_MODEL), row(D_B), row(D_B),
                  full(D_B, D_B), full(D_MODEL, D_MODEL)] + _route_in_specs(),
        out_specs=[tile(D_MODEL)] + _route_out_specs(tm),
        out_shape=[jax.ShapeDtypeStruct((n, D_MODEL), F32)] + _route_out_shapes(n),
        scratch_shapes=[pltpu.VMEM((8, 128), F32)],
        compiler_params=_cparams("arbitrary"),
        name="mix0_out",
    )(o, bonus, g, attn, x, p['ln_w'], p['ln_b'], p['head_avg'], p['w_out'], rp['norm'], rp['w'], rp['b'])


def _row_gather_start(idx_ref, base, n_rows, src_hbm, dst, sem):
    def body(r, carry):
        src_row = pl.multiple_of(idx_ref[base + r] * ROW_CHUNKS, ROW_CHUNKS)
        dst_row = pl.multiple_of(r * ROW_CHUNKS, ROW_CHUNKS)
        pltpu.make_async_copy(src_hbm.at[pl.ds(src_row, ROW_CHUNKS)], dst.at[pl.ds(dst_row, ROW_CHUNKS)], sem).start()
        return carry

    lax.fori_loop(0, n_rows, body, 0, unroll=8)


def _row_gather_wait(dst, sem):
    pltpu.make_async_copy(dst, dst, sem).wait()


def _expert_kernel(te_ref, nu_ref, src_ref, x_hbm, wg_ref, wu_ref, wd_ref, o_ref, xbuf, sem):
    i = pl.program_id(0)
    nu = nu_ref[0]

    @pl.when(i == 0)
    def _():
        _row_gather_start(src_ref, 0, MOE_TILE, x_hbm, xbuf.at[0], sem.at[0])

    @pl.when(i + 1 < nu)
    def _():
        nxt = (i + 1) % 2
        _row_gather_start(src_ref, (i + 1) * MOE_TILE, MOE_TILE, x_hbm, xbuf.at[nxt], sem.at[nxt])

    @pl.when(i < nu)
    def _():
        cur = i % 2
        _row_gather_wait(xbuf.at[cur], sem.at[cur])
        x = _load_row_tiles(xbuf.at[cur]).astype(BF16)
        hg = _dot(x, wg_ref[0])
        hu = _dot(x, wu_ref[0])
        h = (hg * _sigmoid(hg)) * hu
        _store_row_tiles(o_ref, _dot(h.astype(BF16), wd_ref[0]))

    @pl.when(i >= nu)
    def _():
        o_ref[...] = jnp.zeros_like(o_ref)


def _experts(xn, src, tile_expert, n_used, wg, wu, wd):
    r = src.shape[0]
    grid_spec = pltpu.PrefetchScalarGridSpec(
        num_scalar_prefetch=3,
        grid=(r // MOE_TILE,),
        in_specs=[pl.BlockSpec(memory_space=pl.ANY),
                  pl.BlockSpec((1, D_MODEL, D_FF_E), lambda i, te, nu, sr: (te[i], 0, 0)),
                  pl.BlockSpec((1, D_MODEL, D_FF_E), lambda i, te, nu, sr: (te[i], 0, 0)),
                  pl.BlockSpec((1, D_FF_E, D_MODEL), lambda i, te, nu, sr: (te[i], 0, 0))],
        out_specs=pl.BlockSpec((MOE_TILE * ROW_CHUNKS, 128), lambda i, te, nu, sr: (i, 0)),
        scratch_shapes=[pltpu.VMEM((2, MOE_TILE * ROW_CHUNKS, 128), F32), pltpu.SemaphoreType.DMA((2,))],
    )
    return pl.pallas_call(
        _expert_kernel,
        grid_spec=grid_spec,
        out_shape=jax.ShapeDtypeStruct((r * ROW_CHUNKS, 128), F32),
        compiler_params=_cparams("arbitrary"),
        name="moe_experts",
    )(tile_expert, n_used, src, xn, wg, wu, wd)


def _moe(xn, idx, cnt, ep):
    n = xn.shape[0] // ROW_CHUNKS
    rows = 2 * n + N_EXPERTS * MOE_TILE
    counts = cnt[0, :N_EXPERTS].astype(I32)
    padded = ((counts + MOE_TILE - 1) // MOE_TILE) * MOE_TILE
    ends = jnp.cumsum(padded)
    offs = ends - padded
    pos1 = offs[idx[:, 0]] + idx[:, 2]
    pos2 = offs[idx[:, 1]] + idx[:, 3]
    tok = jnp.arange(n, dtype=I32)
    src = (jnp.arange(rows, dtype=I32) % n).at[jnp.concatenate([pos1, pos2])].set(
        jnp.concatenate([tok, tok]), unique_indices=True, indices_are_sorted=False)
    n_used = (ends[-1] // MOE_TILE).astype(I32)
    starts = jnp.arange(rows // MOE_TILE, dtype=I32) * MOE_TILE
    starts = jnp.minimum(starts, ends[-1] - 1)
    tile_expert = jnp.sum((starts[:, None] >= ends[None, :]).astype(I32), axis=1)
    tile_expert = jnp.minimum(tile_expert, N_EXPERTS - 1).astype(I32)
    out = _experts(xn, src, tile_expert, n_used.reshape(1), ep['wg'], ep['wu'], ep['wd'])
    return out, pos1, pos2


def _combine_kernel(p1_ref, p2_ref, x_ref, wts_ref, nrm_ref, out_hbm, y_o, gbuf, sem, *, final):
    i = pl.program_id(0)
    tm = x_ref.shape[0]

    def start(tile, slot):
        _row_gather_start(p1_ref, tile * tm, tm, out_hbm, gbuf.at[slot, 0], sem.at[slot])
        _row_gather_start(p2_ref, tile * tm, tm, out_hbm, gbuf.at[slot, 1], sem.at[slot])

    @pl.when(i == 0)
    def _():
        start(0, 0)

    @pl.when(i + 1 < pl.num_programs(0))
    def _():
        start(i + 1, (i + 1) % 2)

    cur = i % 2
    _row_gather_wait(gbuf.at[cur], sem.at[cur])
    wts = wts_ref[...]
    y = x_ref[...] + wts[:, 0:1] * _load_row_tiles(gbuf.at[cur, 0]) + wts[:, 1:2] * _load_row_tiles(gbuf.at[cur, 1])
    if final:
        y = _rms(y, nrm_ref[...])
    y_o[...] = y


def _moe_combine(x, wts, out, pos1, pos2, nrm, final, tm=256):
    n = x.shape[0]
    tile = lambda w: pl.BlockSpec((tm, w), lambda i, p1, p2: (i, 0))
    grid_spec = pltpu.PrefetchScalarGridSpec(
        num_scalar_prefetch=2,
        grid=(n // tm,),
        in_specs=[tile(D_MODEL), tile(128), pl.BlockSpec((1, D_MODEL), lambda i, p1, p2: (0, 0)),
                  pl.BlockSpec(memory_space=pl.ANY)],
        out_specs=tile(D_MODEL),
        scratch_shapes=[pltpu.VMEM((2, 2, tm * ROW_CHUNKS, 128), F32), pltpu.SemaphoreType.DMA((2,))],
    )
    return pl.pallas_call(
        functools.partial(_combine_kernel, final=final),
        grid_spec=grid_spec,
        out_shape=jax.ShapeDtypeStruct((n, D_MODEL), F32),
        compiler_params=_cparams("arbitrary"),
        name="moe_combine_final" if final else "moe_combine",
    )(pos1, pos2, x, wts, nrm, out)


def _gelu_tanh(x):
    return 0.5 * x * (1.0 + jnp.tanh(0.7978845608028654 * (x + 0.044715 * (x * x * x))))


def _mix1_kernel(x_ref, nmix_ref, bre_ref, bim_ref, are_ref, aim_ref, cre_ref, cim_ref,
                 dsk_ref, wo_ref, h0r_ref, h0i_ref, nrm_ref, wr_ref, br_ref,
                 x2_o, xn_o, idx_o, wts_o, cnt_o, hr_o, hi_o,
                 bur, bui, hr_scr, hi_scr, cnt_scr, *, b, cw):
    @pl.when(pl.program_id(0) == 0)
    def _():
        cnt_scr[...] = jnp.zeros_like(cnt_scr)
        hr_scr[...] = h0r_ref[...]
        hi_scr[...] = h0i_ref[...]

    x = x_ref[...]
    u = _rms(x, nmix_ref[...])
    ub = u.astype(BF16)
    nblk = bre_ref.shape[0]
    kin = D_MODEL // nblk
    kst = S5_STATE // nblk
    for cb in range(nblk):
        ucb = ub[:, cb * kin:(cb + 1) * kin]
        bur[:, cb * kst:(cb + 1) * kst] = _dot(ucb, bre_ref[cb])
        bui[:, cb * kst:(cb + 1) * kst] = _dot(ucb, bim_ref[cb])

    tc = x.shape[0] // b
    for c0 in range(0, S5_STATE, cw):
        cs = slice(c0, c0 + cw)
        ar = jnp.broadcast_to(are_ref[:, cs], (b, cw))
        ai = jnp.broadcast_to(aim_ref[:, cs], (b, cw))

        def step(s, carry, cs=cs, ar=ar, ai=ai):
            hr, hi = carry
            rows = pl.ds(pl.multiple_of(s * b, b), b)
            nr = ar * hr - ai * hi + bur[rows, cs]
            ni = ar * hi + ai * hr + bui[rows, cs]
            bur[rows, cs] = nr
            bui[rows, cs] = ni
            return nr, ni

        hr, hi = lax.fori_loop(0, tc, step, (hr_scr[:, cs], hi_scr[:, cs]), unroll=True)
        hr_scr[:, cs] = hr
        hi_scr[:, cs] = hi

    ych = []
    for cb in range(nblk):
        ss = slice(cb * kst, (cb + 1) * kst)
        ych.append(_dot(bur[:, ss].astype(BF16), cre_ref[cb]) - _dot(bui[:, ss].astype(BF16), cim_ref[cb]))
    y = jnp.concatenate(ych, axis=1) + dsk_ref[...] * u
    z = _dot(_gelu_tanh(y).astype(BF16), wo_ref[...])
    x2 = x + z[:, :D_MODEL] * _sigmoid(z[:, D_MODEL:])
    x2_o[...] = x2
    _route_tile(x2, nrm_ref, wr_ref, br_ref, cnt_scr, xn_o, idx_o, wts_o)
    cnt_o[...] = cnt_scr[...]
    hr_o[...] = hr_scr[...]
    hi_o[...] = hi_scr[...]


def _mix1(x, sp, rp, h0r, h0i, b, tr):
    n = x.shape[0]
    cw = 1024 if b == 8 else 128
    tile = lambda w: pl.BlockSpec((tr, w), lambda i: (i, 0))
    row = lambda w: pl.BlockSpec((1, w), lambda i: (0, 0))
    full = lambda *s: pl.BlockSpec(s, lambda i: (0,) * len(s))
    nblk = sp['b_re'].shape[0]
    return pl.pallas_call(
        functools.partial(_mix1_kernel, b=b, cw=cw),
        grid=(n // tr,),
        in_specs=[tile(D_MODEL), row(D_MODEL),
                  full(nblk, D_MODEL // nblk, S5_STATE // nblk), full(nblk, D_MODEL // nblk, S5_STATE // nblk),
                  row(S5_STATE), row(S5_STATE),
                  full(nblk, S5_STATE // nblk, D_MODEL // nblk), full(nblk, S5_STATE // nblk, D_MODEL // nblk),
                  row(D_MODEL), full(D_MODEL, 2 * D_MODEL), full(b, S5_STATE), full(b, S5_STATE)] + _route_in_specs(),
        out_specs=[tile(D_MODEL)] + _route_out_specs(tr) + [full(b, S5_STATE), full(b, S5_STATE)],
        out_shape=[jax.ShapeDtypeStruct((n, D_MODEL), F32)] + _route_out_shapes(n)
                  + [jax.ShapeDtypeStruct((b, S5_STATE), F32)] * 2,
        scratch_shapes=[pltpu.VMEM((tr, S5_STATE), F32), pltpu.VMEM((tr, S5_STATE), F32),
                        pltpu.VMEM((b, S5_STATE), F32), pltpu.VMEM((b, S5_STATE), F32),
                        pltpu.VMEM((8, 128), F32)],
        compiler_params=_cparams("arbitrary"),
        name="mix1",
    )(x, sp['norm'], sp['b_re'], sp['b_im'], sp['a_re'], sp['a_im'], sp['c_re'], sp['c_im'],
      sp['d'], sp['w_out'], h0r, h0i, rp['norm'], rp['w'], rp['b'])


def _router_params(norm, w_rc, b_rc, w_rf, b_rf):
    w = jnp.concatenate([w_rc, w_rf.reshape(D_MODEL, N_EXPERTS), jnp.zeros((D_MODEL, 12), F32)], axis=1)
    hi = w.astype(BF16)
    lo = (w - hi.astype(F32)).astype(BF16)
    wcat = jnp.concatenate([hi, lo, jnp.zeros((D_MODEL, 64), BF16)], axis=1)
    bias = jnp.concatenate([b_rc, b_rf.reshape(-1), jnp.zeros((12,), F32)]).reshape(1, 32)
    return {'norm': norm.reshape(1, D_MODEL), 'w': wcat, 'b': bias}


def _expert_params(wg, wu, wd):
    return {'wg': wg.astype(BF16), 'wu': wu.astype(BF16), 'wd': wd.astype(BF16)}


def _s5_params(norm, a_re, a_im, log_dt, b_re, b_im, c_re, c_im, d_skip, w_out, nblk=8):
    dt = jnp.exp(log_dt)
    mag = jnp.exp(dt * a_re)
    ab_re, ab_im = mag * jnp.cos(dt * a_im), mag * jnp.sin(dt * a_im)
    den = a_re * a_re + a_im * a_im
    f_re = ((ab_re - 1.0) * a_re + ab_im * a_im) / den
    f_im = (ab_im * a_re - (ab_re - 1.0) * a_im) / den
    bb_re = f_re[..., None] * b_re - f_im[..., None] * b_im
    bb_im = f_re[..., None] * b_im + f_im[..., None] * b_re
    gpb = S5_GROUPS // nblk
    eye = jnp.eye(gpb, dtype=F32)

    def in_blocks(bb):
        bb = bb.reshape(nblk, gpb, S5_P, S5_CH)
        w = jnp.einsum('ngpc,gh->ngchp', bb, eye)
        return w.reshape(nblk, gpb * S5_CH, gpb * S5_P).astype(BF16)

    def out_blocks(cc):
        cc = cc.reshape(nblk, gpb, S5_CH, S5_P)
        w = jnp.einsum('ngcp,gh->ngphc', cc, eye)
        return w.reshape(nblk, gpb * S5_P, gpb * S5_CH).astype(BF16)

    return {'norm': norm.reshape(1, D_MODEL), 'b_re': in_blocks(bb_re), 'b_im': in_blocks(bb_im),
            'a_re': ab_re.reshape(1, S5_STATE), 'a_im': ab_im.reshape(1, S5_STATE),
            'c_re': out_blocks(c_re), 'c_im': out_blocks(c_im), 'd': d_skip.reshape(1, D_MODEL),
            'w_out': w_out.astype(BF16)}


def _head_block(value):
    hid = jnp.arange(D_B, dtype=I32) // HEAD_DIM
    return jnp.where(hid[:, None] == hid[None, :], value, 0.0).astype(BF16)


def _run_group(x_tm, b, t, cache_k, cache_v, shift0, wkv0, h0r, h0i, pr):
    n = b * t
    prompt = cache_k is None
    tm = 256 if prompt else 128
    q, kv, pb = _in_proj(x_tm, pr['l0_norm'], pr['l0_w_in'], tm)

    if prompt:
        attn = _attn_prompt(q, kv, pr['sinks'], b, t)
        kv3 = kv.reshape(t, b, 2, N_KV_A, HEAD_DIM)
        new_k = kv3[t - WINDOW:, :, 0].transpose(1, 0, 2, 3)
        new_v = kv3[t - WINDOW:, :, 1].transpose(1, 0, 2, 3)
        init = jnp.zeros((b, D_B_IN), F32)
    else:
        qs = q.astype(F32).reshape(t, b, Q_COLS).transpose(1, 0, 2)
        kvs = kv.reshape(t, b, 2 * KV_COLS).transpose(1, 0, 2)
        kn, vn = kvs[..., :KV_COLS], kvs[..., KV_COLS:]
        ck = cache_k.reshape(b, WINDOW, KV_COLS)
        cv = cache_v.reshape(b, WINDOW, KV_COLS)
        attn = _attn_sample(qs, kn, vn, ck, cv, pr['sinks'])
        attn = attn.transpose(1, 0, 2).reshape(n, Q_COLS).astype(BF16)
        new_k = jnp.concatenate([ck[:, t:], kn], axis=1).reshape(b, WINDOW, N_KV_A, HEAD_DIM)
        new_v = jnp.concatenate([cv[:, t:], vn], axis=1).reshape(b, WINDOW, N_KV_A, HEAD_DIM)
        init = shift0
    new_shift = pb[n - b:]

    r, w, k, v, kk, nkka, bonus, g = _rwkv_prep(pb, init, pr['rw'], b, tm)
    tc = 64 if prompt else t
    s0 = jnp.zeros((b // SEQ_PER_GROUP, HEAD_DIM, HEAD_DIM // 2, 128), F32) if prompt else _state_to_scan(wkv0, b)
    o, s_fin = _wkv_scan(r, w, k, kk, nkka, v, s0, b, t, tc)
    new_wkv = _state_from_scan(s_fin, b)

    x1, xn, idx, wts, cnt = _mix0_out(o, bonus, g, attn, x_tm, pr['rw'], pr['l0_route'], 256)
    out, pos1, pos2 = _moe(xn, idx, cnt, pr['l0_exp'])
    x1 = _moe_combine(x1, wts, out, pos1, pos2, pr['final_norm'], False)

    x2, xn, idx, wts, cnt, hr, hi = _mix1(x1, pr['s5'], pr['l1_route'], h0r, h0i, b, 256)
    out, pos1, pos2 = _moe(xn, idx, cnt, pr['l1_exp'])
    y = _moe_combine(x2, wts, out, pos1, pos2, pr['final_norm'], True)
    return (y, new_k, new_v, new_shift, new_wkv,
            hr.reshape(b, S5_GROUPS, S5_P), hi.reshape(b, S5_GROUPS, S5_P))


def kernel(x_prompt, x_sample, cache_win_k, cache_win_v, state_shift, state_wkv, state_s5_re, state_s5_im,
           l0_norm_mix, l0_w_in, l0_sinks, l0_mu, l0_w0, l0_w_lora_up, l0_a0, l0_a_lora_up, l0_g_lora_up,
           l0_k_k, l0_k_a, l0_r_k, l0_ln_w, l0_ln_b, l0_w_out,
           l0_norm_ffn, l0_router_coarse, l0_bias_coarse, l0_router_fine, l0_bias_fine,
           l0_exp_gate, l0_exp_up, l0_exp_down,
           l1_norm_mix, l1_s5_a_re, l1_s5_a_im, l1_s5_log_dt, l1_s5_b_re, l1_s5_b_im, l1_s5_c_re, l1_s5_c_im,
           l1_s5_d, l1_w_out,
           l1_norm_ffn, l1_router_coarse, l1_bias_coarse, l1_router_fine, l1_bias_fine,
           l1_exp_gate, l1_exp_up, l1_exp_down,
           final_norm):
    row = lambda z: z.reshape(1, -1)
    pr = {
        'l0_norm': row(l0_norm_mix), 'l0_w_in': l0_w_in.astype(BF16), 'sinks': l0_sinks,
        'rw': {'mu': row(l0_mu), 'w0': row(l0_w0), 'w_up': l0_w_lora_up.astype(BF16), 'a0': row(l0_a0),
               'a_up': l0_a_lora_up.astype(BF16), 'g_up': l0_g_lora_up.astype(BF16), 'k_k': row(l0_k_k),
               'k_a': row(l0_k_a), 'r_k': row(l0_r_k), 'ln_w': row(l0_ln_w), 'ln_b': row(l0_ln_b),
               'head_ones': _head_block(1.0), 'head_avg': _head_block(1.0 / HEAD_DIM),
               'w_out': l0_w_out.astype(BF16)},
        'l0_route': _router_params(l0_norm_ffn, l0_router_coarse, l0_bias_coarse, l0_router_fine, l0_bias_fine),
        'l0_exp': _expert_params(l0_exp_gate, l0_exp_up, l0_exp_down),
        's5': _s5_params(l1_norm_mix, l1_s5_a_re, l1_s5_a_im, l1_s5_log_dt, l1_s5_b_re, l1_s5_b_im,
                         l1_s5_c_re, l1_s5_c_im, l1_s5_d, l1_w_out),
        'l1_route': _router_params(l1_norm_ffn, l1_router_coarse, l1_bias_coarse, l1_router_fine, l1_bias_fine),
        'l1_exp': _expert_params(l1_exp_gate, l1_exp_up, l1_exp_down),
        'final_norm': row(final_norm),
    }
    bp, tp = x_prompt.shape[0], x_prompt.shape[1]
    bs, ts = x_sample.shape[0], x_sample.shape[1]
    xp = x_prompt.transpose(1, 0, 2).reshape(bp * tp, D_MODEL)
    xs = x_sample.transpose(1, 0, 2).reshape(bs * ts, D_MODEL)
    zero_state = jnp.zeros((bp, S5_STATE), F32)
    yp, pk, pv, psh, pwkv, pre, pim = _run_group(xp, bp, tp, None, None, None, None, zero_state, zero_state, pr)
    ys, sk, sv, ssh, swkv, sre, sim = _run_group(
        xs, bs, ts, cache_win_k, cache_win_v, state_shift, state_wkv,
        state_s5_re.reshape(bs, S5_STATE), state_s5_im.reshape(bs, S5_STATE), pr)
    y_prompt = yp.reshape(tp, bp, D_MODEL).transpose(1, 0, 2)
    y_sample = ys.reshape(ts, bs, D_MODEL).transpose(1, 0, 2)
    return (y_prompt, y_sample, pk, pv, psh, pwkv, pre, pim, sk, sv, ssh, swkv, sre, sim)
```

```python
import functools

import jax
import jax.numpy as jnp
from jax import lax
from jax.experimental import pallas as pl
from jax.experimental.pallas import tpu as pltpu

F32 = jnp.float32
BF16 = jnp.bfloat16
I32 = jnp.int32

D_MODEL = 1024
HEAD_DIM = 64
N_HEADS_A = 8
N_KV_A = 2
GQA_GROUP = 4
WINDOW = 128
Q_COLS = 512
KV_COLS = 128
D_A_IN = 768
N_HEADS_B = 8
D_B = 512
D_LORA_W = 64
D_LORA_A = 64
D_LORA_G = 128
D_B_IN = 1792
D_IN0 = 2560
RWKV_GN_EPS = 64e-5
S5_CH = 16
S5_GROUPS = 64
S5_P = 64
S5_STATE = S5_GROUPS * S5_P
N_EGROUPS = 4
EXP_PER_GROUP = 4
N_EXPERTS = 16
D_FF_E = 512
RMS_EPS = 1e-5
NEG_BIG = -1e30
PAIRS = 64
SEQ_PER_GROUP = PAIRS // N_HEADS_B
MOE_TILE = 256
ROW_CHUNKS = D_MODEL // 128
VMEM_LIMIT = 56 * 1024 * 1024


def _cparams(*sem):
    return pltpu.CompilerParams(dimension_semantics=sem, vmem_limit_bytes=VMEM_LIMIT)


def _dot(a, b):
    return jnp.dot(a, b, preferred_element_type=F32)


def _split_bf16(x):
    hi = x.astype(BF16)
    lo = (x - hi.astype(F32)).astype(BF16)
    return hi, lo


def _dot2(x, w):
    hi, lo = _split_bf16(x)
    return _dot(hi, w) + _dot(lo, w)


def _rms(x, g):
    return x * lax.rsqrt(jnp.mean(x * x, axis=-1, keepdims=True) + RMS_EPS) * g


def _sigmoid(x):
    return 1.0 / (1.0 + jnp.exp(-x))


def _store_row_tiles(ref, x):
    rows = x.shape[0]
    for c in range(ROW_CHUNKS):
        ref[pl.ds(c, rows, stride=ROW_CHUNKS), :] = x[:, c * 128:(c + 1) * 128]


def _time_major_spec(b, t, d, tm):
    if b == SEQ_PER_GROUP:
        return (b, t, d), pl.BlockSpec((b, tm // b, d), lambda i, *_: (0, i, 0))
    return (b * t, d), pl.BlockSpec((tm, d), lambda i, *_: (i, 0))


def _interleave_rows(pieces, scr):
    nb, steps = len(pieces), pieces[0].shape[0]
    chunks = pieces[0].shape[1] // 128
    for s, p in enumerate(pieces):
        for c in range(chunks):
            scr[c, pl.ds(s, steps, stride=nb), :] = p[:, c * 128:(c + 1) * 128].astype(scr.dtype)
    return jnp.concatenate([scr[c] for c in range(chunks)], axis=1)


def _deinterleave_rows(x, nb, scr):
    steps = x.shape[0] // nb
    chunks = x.shape[1] // 128
    for c in range(chunks):
        scr[c] = x[:, c * 128:(c + 1) * 128].astype(scr.dtype)
    return [jnp.concatenate([scr[c, pl.ds(s, steps, stride=nb), :] for c in range(chunks)], axis=1)
            for s in range(nb)]


def _load_time_major(x_ref, scr, b):
    if len(x_ref.shape) == 3:
        return _interleave_rows([x_ref[s] for s in range(b)], scr)
    return x_ref[...]


def _store_batch_major(y_ref, scr, y, b):
    if len(y_ref.shape) == 3:
        for s, p in enumerate(_deinterleave_rows(y, b, scr)):
            y_ref[s] = p
    else:
        y_ref[...] = y


def _load_row_tiles(ref):
    rows = ref.shape[0] // ROW_CHUNKS
    return jnp.concatenate([ref[pl.ds(c, rows, stride=ROW_CHUNKS), :] for c in range(ROW_CHUNKS)], axis=1)


def _in_proj_kernel(x_ref, g_ref, w_ref, q_ref, kv_ref, pb_ref, xs_scr, q_scr, kv_scr, *, b):
    x = _load_time_major(x_ref, xs_scr, b)
    xn = _rms(x, g_ref[...]).astype(BF16)
    q = _dot(xn, w_ref[:, :Q_COLS])
    kv = _dot(xn, w_ref[:, Q_COLS:D_A_IN])
    pb_ref[...] = _dot(xn, w_ref[:, D_A_IN:])
    if len(x_ref.shape) == 3:
        q_ref[...] = jnp.concatenate(_deinterleave_rows(q, b, q_scr), axis=1).astype(BF16)
        kv_ref[...] = jnp.concatenate(_deinterleave_rows(kv, b, kv_scr), axis=1)
    else:
        q_ref[...] = q
        kv_ref[...] = kv


def _in_proj(x, g, w_bf16, b, t, tm):
    n = b * t
    xshape, xspec = _time_major_spec(b, t, D_MODEL, tm)
    slab = len(xshape) == 3
    steps = tm // b
    if slab:
        qkv_specs = [pl.BlockSpec((steps, b * Q_COLS), lambda i: (i, 0)),
                     pl.BlockSpec((steps, b * 2 * KV_COLS), lambda i: (i, 0))]
        qkv_shapes = [jax.ShapeDtypeStruct((t, b * Q_COLS), BF16), jax.ShapeDtypeStruct((t, b * 2 * KV_COLS), F32)]
    else:
        qkv_specs = [pl.BlockSpec((tm, Q_COLS), lambda i: (i, 0)), pl.BlockSpec((tm, 2 * KV_COLS), lambda i: (i, 0))]
        qkv_shapes = [jax.ShapeDtypeStruct((n, Q_COLS), F32), jax.ShapeDtypeStruct((n, 2 * KV_COLS), F32)]
    return pl.pallas_call(
        functools.partial(_in_proj_kernel, b=b),
        grid=(n // tm,),
        in_specs=[xspec,
                  pl.BlockSpec((1, D_MODEL), lambda i: (0, 0)),
                  pl.BlockSpec((D_MODEL, D_IN0), lambda i: (0, 0))],
        out_specs=qkv_specs + [pl.BlockSpec((tm, D_B_IN), lambda i: (i, 0))],
        out_shape=qkv_shapes + [jax.ShapeDtypeStruct((n, D_B_IN), F32)],
        scratch_shapes=[pltpu.VMEM((D_MODEL // 128, tm, 128), F32), pltpu.VMEM((Q_COLS // 128, tm, 128), F32),
                        pltpu.VMEM((2 * KV_COLS // 128, tm, 128), F32)],
        compiler_params=_cparams("parallel"),
        name="in_proj",
    )(x.reshape(xshape), g, w_bf16)


def _attn_prompt_kernel(sinks_ref, q_ref, kc_ref, kp_ref, vc_ref, vp_ref, o_ref):
    j = pl.program_id(1)
    qi = lax.broadcasted_iota(I32, (WINDOW, 2 * WINDOW), 0)
    kj = lax.broadcasted_iota(I32, (WINDOW, 2 * WINDOW), 1)
    valid = jnp.logical_and(kj > qi, kj <= qi + WINDOW)
    valid = jnp.logical_and(valid, jnp.logical_or(kj >= WINDOW, j > 0))
    dist = (WINDOW + qi - kj).astype(F32)
    for n in range(N_KV_A):
        cs = slice(n * HEAD_DIM, (n + 1) * HEAD_DIM)
        kb = jnp.concatenate([kp_ref[:, cs], kc_ref[:, cs]], axis=0).astype(BF16)
        vb = jnp.concatenate([vp_ref[:, cs], vc_ref[:, cs]], axis=0).astype(BF16)
        for g in range(GQA_GROUP):
            h = n * GQA_GROUP + g
            hs = slice(h * HEAD_DIM, (h + 1) * HEAD_DIM)
            s = lax.dot_general(q_ref[:, hs], kb, (((1,), (1,)), ((), ())), preferred_element_type=F32)
            s = s * (HEAD_DIM ** -0.5) - (2.0 ** -(h + 1)) * dist
            s = jnp.where(valid, s, NEG_BIG)
            sink = sinks_ref[h]
            m = jnp.maximum(jnp.max(s, axis=1, keepdims=True), sink)
            p = jnp.exp(s - m)
            l = jnp.sum(p, axis=1, keepdims=True) + jnp.exp(sink - m)
            o = _dot(p.astype(BF16), vb) / l
            o_ref[:, hs] = o.astype(BF16)


def _attn_prompt(q2, kv2, sinks, b, t):
    prev = lambda bi, j: jnp.maximum(j - 1, 0)
    return pl.pallas_call(
        _attn_prompt_kernel,
        grid=(b, t // WINDOW),
        in_specs=[pl.BlockSpec(memory_space=pltpu.SMEM),
                  pl.BlockSpec((WINDOW, Q_COLS), lambda bi, j: (j, bi)),
                  pl.BlockSpec((WINDOW, KV_COLS), lambda bi, j: (j, 2 * bi)),
                  pl.BlockSpec((WINDOW, KV_COLS), lambda bi, j: (prev(bi, j), 2 * bi)),
                  pl.BlockSpec((WINDOW, KV_COLS), lambda bi, j: (j, 2 * bi + 1)),
                  pl.BlockSpec((WINDOW, KV_COLS), lambda bi, j: (prev(bi, j), 2 * bi + 1))],
        out_specs=pl.BlockSpec((WINDOW, Q_COLS), lambda bi, j: (j, bi)),
        out_shape=jax.ShapeDtypeStruct((t, b * Q_COLS), BF16),
        compiler_params=_cparams("parallel", "parallel"),
        name="attn_prompt",
    )(sinks, q2, kv2, kv2, kv2, kv2)


def _attn_sample_kernel(sinks_ref, q_ref, kn_ref, vn_ref, ck_ref, cv_ref, o_ref):
    bs, t = q_ref.shape[0], q_ref.shape[1]
    assert t & (t - 1) == 0
    nq, nk = GQA_GROUP * t, 2 * WINDOW
    r = lax.broadcasted_iota(I32, (nq, nk), 0)
    kj = lax.broadcasted_iota(I32, (nq, nk), 1)
    tq = jnp.bitwise_and(r, t - 1)
    valid = jnp.logical_and(kj > tq, kj <= tq + WINDOW)
    dist = (WINDOW + tq - kj).astype(F32)
    grp = jnp.right_shift(lax.broadcasted_iota(I32, (nq, 1), 0), t.bit_length() - 1)
    pad = jnp.zeros((bs, nk - WINDOW - t, HEAD_DIM), F32)
    for n in range(N_KV_A):
        cs = slice(n * HEAD_DIM, (n + 1) * HEAD_DIM)
        kb = jnp.concatenate([ck_ref[:, :, cs], kn_ref[:, :, cs], pad], axis=1).astype(BF16)
        vb = jnp.concatenate([cv_ref[:, :, cs], vn_ref[:, :, cs], pad], axis=1).astype(BF16)
        qn = jnp.concatenate([q_ref[:, :, (n * GQA_GROUP + g) * HEAD_DIM:(n * GQA_GROUP + g + 1) * HEAD_DIM]
                              for g in range(GQA_GROUP)], axis=1).astype(BF16)
        slope = jnp.zeros((nq, 1), F32)
        sink = jnp.zeros((nq, 1), F32)
        for g in range(GQA_GROUP):
            h = n * GQA_GROUP + g
            slope = jnp.where(grp == g, 2.0 ** -(h + 1), slope)
            sink = jnp.where(grp == g, sinks_ref[h], sink)
        s = jnp.einsum('bqd,bkd->bqk', qn, kb, preferred_element_type=F32)
        s = s * (HEAD_DIM ** -0.5) - (slope * dist)[None]
        s = jnp.where(valid[None], s, NEG_BIG)
        m = jnp.maximum(jnp.max(s, axis=2, keepdims=True), sink[None])
        p = jnp.exp(s - m)
        l = jnp.sum(p, axis=2, keepdims=True) + jnp.exp(sink[None] - m)
        o = jnp.einsum('bqk,bkd->bqd', p.astype(BF16), vb, preferred_element_type=F32) / l
        for g in range(GQA_GROUP):
            h = n * GQA_GROUP + g
            o_ref[:, :, h * HEAD_DIM:(h + 1) * HEAD_DIM] = o[:, g * t:(g + 1) * t, :]


def _attn_sample(q, kn, vn, ck, cv, sinks, bs=16):
    db, t = q.shape[0], q.shape[1]
    seq3 = lambda w: pl.BlockSpec((bs, t, w), lambda i: (i, 0, 0))
    cache = pl.BlockSpec((bs, WINDOW, KV_COLS), lambda i: (i, 0, 0))
    return pl.pallas_call(
        _attn_sample_kernel,
        grid=(db // bs,),
        in_specs=[pl.BlockSpec(memory_space=pltpu.SMEM), seq3(Q_COLS), seq3(KV_COLS), seq3(KV_COLS), cache, cache],
        out_specs=seq3(Q_COLS),
        out_shape=jax.ShapeDtypeStruct((db, t, Q_COLS), F32),
        compiler_params=_cparams("parallel"),
        name="attn_sample",
    )(sinks, q, kn, vn, ck, cv)


def _rwkv_prep_kernel(pb_ref, halo_ref, init_ref, mu_ref, w0_ref, wup_ref, a0_ref, aup_ref, gup_ref,
                      kk_ref, ka_ref, rk_ref, ones_ref,
                      r_o, w_o, k_o, v_o, kk_o, nkka_o, bonus_o, g_o, *, b):
    i = pl.program_id(0)
    pb = pb_ref[...]
    tm = pb.shape[0]
    halo = jnp.where(i == 0, init_ref[...], halo_ref[...])
    prev = halo if tm == b else jnp.concatenate([halo, pb[:tm - b]], axis=0)
    xs = pb + (prev - pb) * mu_ref[...]
    r = xs[:, :D_B]
    k = xs[:, D_B:2 * D_B]
    v = xs[:, 2 * D_B:3 * D_B]
    o1 = 3 * D_B
    wd = xs[:, o1:o1 + D_LORA_W]
    ad = xs[:, o1 + D_LORA_W:o1 + D_LORA_W + D_LORA_A]
    gd = xs[:, o1 + D_LORA_W + D_LORA_A:]
    z = -(w0_ref[...] + _dot(jnp.tanh(wd).astype(BF16), wup_ref[...]))
    softplus = jnp.maximum(z, 0.0) + jnp.log(1.0 + jnp.exp(-jnp.abs(z)))
    decay = jnp.exp(-jnp.exp(-softplus - 0.5))
    a = _sigmoid(a0_ref[...] + _dot(ad.astype(BF16), aup_ref[...]))
    g_o[...] = _dot(_sigmoid(gd).astype(BF16), gup_ref[...])
    ones = ones_ref[...]
    kk = k * kk_ref[...]
    kk = kk * lax.rsqrt(jnp.maximum(_dot2(kk * kk, ones), 1e-24))
    k2 = k * (1.0 + (a - 1.0) * ka_ref[...])
    bonus_o[...] = _dot2(r * k2 * rk_ref[...], ones) * v

    half = HEAD_DIM // 2
    lane8 = lax.broadcasted_iota(I32, (SEQ_PER_GROUP, 128), 1)
    low8 = lane8 < HEAD_DIM
    first_copy = jnp.bitwise_and(lax.broadcasted_iota(I32, (half, 128), 1), SEQ_PER_GROUP) == 0
    pairs = ((r, decay, r_o, w_o), (k2, kk, k_o, kk_o), (-(kk * a), v, nkka_o, None))
    for u in range(tm // SEQ_PER_GROUP):
        rows = slice(u * SEQ_PER_GROUP, (u + 1) * SEQ_PER_GROUP)
        for x, y, x_o, y_o in pairs:
            xu, yu = x[rows], y[rows]
            pieces = []
            for h in range(N_HEADS_B):
                cs = slice((h // 2) * 128, (h // 2 + 1) * 128)
                if h % 2 == 0:
                    p = jnp.where(low8, xu[:, cs], pltpu.roll(yu[:, cs], HEAD_DIM, 1))
                else:
                    p = jnp.where(low8, pltpu.roll(xu[:, cs], HEAD_DIM, 1), yu[:, cs])
                pieces += [p, p]
            tr = jnp.concatenate(pieces, axis=0).T
            x_o[u] = tr[:HEAD_DIM]
            if y_o is not None:
                y_o[u] = tr[HEAD_DIM:]
            else:
                v_o[u] = jnp.where(first_copy, tr[HEAD_DIM:HEAD_DIM + half], tr[HEAD_DIM + half:])


def _rwkv_prep(pb, init, p, b, tm):
    n = pb.shape[0]
    units = n // SEQ_PER_GROUP
    tu = tm // SEQ_PER_GROUP
    half = HEAD_DIM // 2
    row = lambda w: pl.BlockSpec((1, w), lambda i: (0, 0))
    full = lambda r, c: pl.BlockSpec((r, c), lambda i: (0, 0))
    tile = pl.BlockSpec((tm, D_B), lambda i: (i, 0))
    kspec = pl.BlockSpec((tu, HEAD_DIM, 128), lambda i: (i, 0, 0))
    vspec = pl.BlockSpec((tu, half, 128), lambda i: (i, 0, 0))
    kshape = jax.ShapeDtypeStruct((units, HEAD_DIM, 128), F32)
    halo_blocks = tm // b
    return pl.pallas_call(
        functools.partial(_rwkv_prep_kernel, b=b),
        grid=(n // tm,),
        in_specs=[pl.BlockSpec((tm, D_B_IN), lambda i: (i, 0)),
                  pl.BlockSpec((b, D_B_IN), lambda i: (jnp.maximum(i * halo_blocks - 1, 0), 0)),
                  full(b, D_B_IN), row(D_B_IN), row(D_B), full(D_LORA_W, D_B), row(D_B), full(D_LORA_A, D_B),
                  full(D_LORA_G, D_B), row(D_B), row(D_B), row(D_B), full(D_B, D_B)],
        out_specs=[kspec, kspec, kspec, vspec, kspec, kspec, tile, tile],
        out_shape=[kshape, kshape, kshape, jax.ShapeDtypeStruct((units, half, 128), F32), kshape, kshape,
                   jax.ShapeDtypeStruct((n, D_B), F32), jax.ShapeDtypeStruct((n, D_B), F32)],
        compiler_params=_cparams("arbitrary"),
        name="rwkv_prep",
    )(pb, pb, init, p['mu'], p['w0'], p['w_up'], p['a0'], p['a_up'], p['g_up'], p['k_k'], p['k_a'], p['r_k'],
      p['head_ones'])


def _wkv_scan_kernel(r_ref, w_ref, k_ref, kk_ref, nkka_ref, v_ref, s0_ref, o_ref, st_ref, s_scr):
    j = pl.program_id(1)

    @pl.when(j == 0)
    def _():
        s_scr[...] = s0_ref[0]

    tc = r_ref.shape[0]
    nsub = (HEAD_DIM // 2) // 8

    def bcast(ref, s, kx):
        return jnp.broadcast_to(ref[s, pl.ds(kx, 1), :], (8, 128))

    acc0 = [[jnp.zeros((8, 128), F32) for _ in range(2)] for _ in range(nsub)]
    for kx in range(HEAD_DIM):
        kkr = bcast(kk_ref, 0, kx)
        for i in range(nsub):
            acc0[i][kx % 2] = acc0[i][kx % 2] + s_scr[kx, 8 * i:8 * i + 8, :] * kkr

    def step(s, sa):
        nxt = jnp.minimum(s + 1, tc - 1)
        vv = [v_ref[s, 8 * i:8 * i + 8, :] for i in range(nsub)]
        oacc = [[jnp.zeros((8, 128), F32) for _ in range(2)] for _ in range(nsub)]
        nacc = [[jnp.zeros((8, 128), F32) for _ in range(2)] for _ in range(nsub)]
        for kx in range(HEAD_DIM):
            rr, wr, kr = bcast(r_ref, s, kx), bcast(w_ref, s, kx), bcast(k_ref, s, kx)
            nk, kkn = bcast(nkka_ref, s, kx), bcast(kk_ref, nxt, kx)
            for i in range(nsub):
                rows = slice(8 * i, 8 * i + 8)
                sk = s_scr[kx, rows, :] * wr + sa[i] * nk + vv[i] * kr
                s_scr[kx, rows, :] = sk
                oacc[i][kx % 2] = oacc[i][kx % 2] + sk * rr
                nacc[i][kx % 2] = nacc[i][kx % 2] + sk * kkn
        o_ref[s] = jnp.concatenate([a[0] + a[1] for a in oacc], axis=0)
        return [a[0] + a[1] for a in nacc]

    lax.fori_loop(0, tc, step, [a[0] + a[1] for a in acc0])

    @pl.when(j == pl.num_programs(1) - 1)
    def _():
        st_ref[0] = s_scr[...]


def _wkv_scan(r, w, k, kk, nkka, v, s0, b, t, tc):
    g = b // SEQ_PER_GROUP
    half = HEAD_DIM // 2
    kview = lambda z: z.reshape(t, g, HEAD_DIM, 128)
    kspec = pl.BlockSpec((tc, None, HEAD_DIM, 128), lambda gi, j: (j, gi, 0, 0))
    vspec = pl.BlockSpec((tc, None, half, 128), lambda gi, j: (j, gi, 0, 0))
    sspec = pl.BlockSpec((1, HEAD_DIM, half, 128), lambda gi, j: (gi, 0, 0, 0))
    o, st = pl.pallas_call(
        _wkv_scan_kernel,
        grid=(g, t // tc),
        in_specs=[kspec] * 5 + [vspec, sspec],
        out_specs=[vspec, sspec],
        out_shape=[jax.ShapeDtypeStruct((t, g, half, 128), F32),
                   jax.ShapeDtypeStruct((g, HEAD_DIM, half, 128), F32)],
        scratch_shapes=[pltpu.VMEM((HEAD_DIM, half, 128), F32)],
        compiler_params=_cparams("parallel", "arbitrary"),
        name="wkv_scan",
    )(kview(r), kview(w), kview(k), kview(kk), kview(nkka), v.reshape(t, g, half, 128), s0)
    return o.reshape(t * g, half, 128), st


def _state_to_scan(s, b):
    g = b // SEQ_PER_GROUP
    s = s.reshape(g, SEQ_PER_GROUP, N_HEADS_B, 2, HEAD_DIM // 2, HEAD_DIM).transpose(0, 5, 4, 2, 3, 1)
    return s.reshape(g, HEAD_DIM, HEAD_DIM // 2, 128)


def _state_from_scan(s, b):
    g = b // SEQ_PER_GROUP
    s = s.reshape(g, HEAD_DIM, HEAD_DIM // 2, N_HEADS_B, 2, SEQ_PER_GROUP).transpose(0, 5, 3, 4, 2, 1)
    return s.reshape(b, N_HEADS_B, HEAD_DIM, HEAD_DIM)


def _route_tile(x, nrm_ref, wr_ref, br_ref, cnt_scr, xn_o, idx_o, wts_o):
    tm = x.shape[0]
    xn = _rms(x, nrm_ref[...])
    hi, lo = _split_bf16(xn)
    _store_row_tiles(xn_o, xn)
    wr = wr_ref[...]
    pa = _dot(hi, wr)
    pb = _dot(lo, wr)
    lg = pa[:, 0:32] + pa[:, 32:64] + pb[:, 0:32] + pb[:, 32:64] + br_ref[...]
    col = lambda c: lg[:, c:c + 1]
    c = [col(gx) for gx in range(N_EGROUPS)]
    m = jnp.maximum(jnp.maximum(c[0], c[1]), jnp.maximum(c[2], c[3]))
    den = jnp.exp(c[0] - m) + jnp.exp(c[1] - m) + jnp.exp(c[2] - m) + jnp.exp(c[3] - m)
    pg = 1.0 / den
    gi = jnp.where(c[0] >= m, 0, jnp.where(c[1] >= m, 1, jnp.where(c[2] >= m, 2, 3))).astype(I32)
    sel = []
    for e in range(EXP_PER_GROUP):
        sel.append(jnp.where(gi == 0, col(4 + e), jnp.where(gi == 1, col(8 + e),
                                                            jnp.where(gi == 2, col(12 + e), col(16 + e)))))
    v1 = jnp.maximum(jnp.maximum(sel[0], sel[1]), jnp.maximum(sel[2], sel[3]))
    i1 = jnp.where(sel[0] >= v1, 0, jnp.where(sel[1] >= v1, 1, jnp.where(sel[2] >= v1, 2, 3))).astype(I32)
    rest = [jnp.where(i1 == e, -jnp.inf, sel[e]) for e in range(EXP_PER_GROUP)]
    v2 = jnp.maximum(jnp.maximum(rest[0], rest[1]), jnp.maximum(rest[2], rest[3]))
    i2 = jnp.where(rest[0] >= v2, 0, jnp.where(rest[1] >= v2, 1, jnp.where(rest[2] >= v2, 2, 3))).astype(I32)
    tt = jnp.exp(v2 - v1)
    w1 = pg / (1.0 + tt)
    w2 = pg * tt / (1.0 + tt)
    e1 = gi * EXP_PER_GROUP + i1
    e2 = gi * EXP_PER_GROUP + i2
    lane = lax.broadcasted_iota(I32, (tm, N_EXPERTS), 1)
    oh1 = lane == e1
    oh2 = lane == e2
    oh = jnp.where(jnp.logical_or(oh1, oh2), 1.0, 0.0)
    ri = lax.broadcasted_iota(I32, (tm, tm), 0)
    ci = lax.broadcasted_iota(I32, (tm, tm), 1)
    ltri = jnp.where(ri > ci, 1.0, 0.0).astype(BF16)
    cnt = cnt_scr[0:1, 0:N_EXPERTS]
    pre = _dot(ltri, oh.astype(BF16)) + cnt
    rank1 = jnp.sum(jnp.where(oh1, pre, 0.0), axis=1, keepdims=True)
    rank2 = jnp.sum(jnp.where(oh2, pre, 0.0), axis=1, keepdims=True)
    cnt_scr[0:1, 0:N_EXPERTS] = cnt + jnp.sum(oh, axis=0, keepdims=True)
    idx_o[:, 0:1] = e1
    idx_o[:, 1:2] = e2
    idx_o[:, 2:3] = rank1.astype(I32)
    idx_o[:, 3:4] = rank2.astype(I32)
    lw = lax.broadcasted_iota(I32, (tm, 128), 1)
    wts_o[...] = jnp.where(lw == 0, w1, jnp.where(lw == 1, w2, 0.0))


def _route_out_specs(tm):
    return [pl.BlockSpec((tm * ROW_CHUNKS, 128), lambda i: (i, 0)),
            pl.BlockSpec((tm, 4), lambda i: (i, 0)),
            pl.BlockSpec((tm, 128), lambda i: (i, 0)),
            pl.BlockSpec((8, 128), lambda i: (0, 0))]


def _route_out_shapes(n):
    return [jax.ShapeDtypeStruct((n * ROW_CHUNKS, 128), F32),
            jax.ShapeDtypeStruct((n, 4), I32),
            jax.ShapeDtypeStruct((n, 128), F32),
            jax.ShapeDtypeStruct((8, 128), F32)]


def _route_in_specs():
    return [pl.BlockSpec((1, D_MODEL), lambda i: (0, 0)),
            pl.BlockSpec((D_MODEL, 128), lambda i: (0, 0)),
            pl.BlockSpec((1, 32), lambda i: (0, 0))]


def _mix0_out_kernel(o_ref, bonus_ref, g_ref, attn_ref, x_ref, lnw_ref, lnb_ref, avg_ref, wo_ref,
                     nrm_ref, wr_ref, br_ref,
                     x1_o, xn_o, idx_o, wts_o, cnt_o, cnt_scr, xs_scr, at_scr, *, b):
    @pl.when(pl.program_id(0) == 0)
    def _():
        cnt_scr[...] = jnp.zeros_like(cnt_scr)

    avg = avg_ref[...]
    half = HEAD_DIM // 2
    lane8 = lax.broadcasted_iota(I32, (SEQ_PER_GROUP, 128), 1)
    unit_rows = []
    for u in range(o_ref.shape[0]):
        ot = jnp.concatenate([o_ref[u], jnp.zeros((128 - half, 128), F32)], axis=0).T
        cols = []
        for jj in range(N_HEADS_B // 2):
            q = [ot[(4 * jj + i) * SEQ_PER_GROUP:(4 * jj + i + 1) * SEQ_PER_GROUP] for i in range(4)]
            c = jnp.where(lane8 < half, q[0], pltpu.roll(q[1], half, 1))
            c = jnp.where(lane8 < 2 * half, c, pltpu.roll(q[2], 2 * half, 1))
            c = jnp.where(lane8 < 3 * half, c, pltpu.roll(q[3], 3 * half, 1))
            cols.append(c)
        unit_rows.append(jnp.concatenate(cols, axis=1))
    o = jnp.concatenate(unit_rows, axis=0)
    d = o - _dot2(o, avg)
    var = _dot2(d * d, avg)
    on = d * lax.rsqrt(var + RWKV_GN_EPS) * lnw_ref[...] + lnb_ref[...]
    rout = ((on + bonus_ref[...]) * g_ref[...]).astype(BF16)
    x = _load_time_major(x_ref, xs_scr, b)
    if len(x_ref.shape) == 3:
        attn = _interleave_rows([attn_ref[:, s * Q_COLS:(s + 1) * Q_COLS] for s in range(b)], at_scr)
    else:
        attn = attn_ref[...]
    y = _dot(attn.astype(BF16), wo_ref[:Q_COLS, :]) + _dot(rout, wo_ref[Q_COLS:, :]) + x
    x1_o[...] = y
    _route_tile(y, nrm_ref, wr_ref, br_ref, cnt_scr, xn_o, idx_o, wts_o)
    cnt_o[...] = cnt_scr[...]


def _mix0_out(o, bonus, g, attn, x, p, rp, b, t, tm):
    n = b * t
    xshape, xspec = _time_major_spec(b, t, D_MODEL, tm)
    aspec = (pl.BlockSpec((tm // b, b * Q_COLS), lambda i: (i, 0)) if len(xshape) == 3
             else pl.BlockSpec((tm, Q_COLS), lambda i: (i, 0)))
    tile = lambda w: pl.BlockSpec((tm, w), lambda i: (i, 0))
    row = lambda w: pl.BlockSpec((1, w), lambda i: (0, 0))
    full = lambda r, c: pl.BlockSpec((r, c), lambda i: (0, 0))
    return pl.pallas_call(
        functools.partial(_mix0_out_kernel, b=b),
        grid=(n // tm,),
        in_specs=[pl.BlockSpec((tm // SEQ_PER_GROUP, HEAD_DIM // 2, 128), lambda i: (i, 0, 0)),
                  tile(D_B), tile(D_B), aspec, xspec, row(D_B), row(D_B),
                  full(D_B, D_B), full(D_MODEL, D_MODEL)] + _route_in_specs(),
        out_specs=[tile(D_MODEL)] + _route_out_specs(tm),
        out_shape=[jax.ShapeDtypeStruct((n, D_MODEL), F32)] + _route_out_shapes(n),
        scratch_shapes=[pltpu.VMEM((8, 128), F32), pltpu.VMEM((D_MODEL // 128, tm, 128), F32),
                        pltpu.VMEM((Q_COLS // 128, tm, 128), F32)],
        compiler_params=_cparams("arbitrary"),
        name="mix0_out",
    )(o, bonus, g, attn, x.reshape(xshape), p['ln_w'], p['ln_b'], p['head_avg'], p['w_out'], rp['norm'], rp['w'], rp['b'])


def _row_gather_start(idx_ref, base, n_rows, src_hbm, dst, sem):
    def body(r, carry):
        src_row = pl.multiple_of(idx_ref[base + r] * ROW_CHUNKS, ROW_CHUNKS)
        dst_row = pl.multiple_of(r * ROW_CHUNKS, ROW_CHUNKS)
        pltpu.make_async_copy(src_hbm.at[pl.ds(src_row, ROW_CHUNKS)], dst.at[pl.ds(dst_row, ROW_CHUNKS)], sem).start()
        return carry

    lax.fori_loop(0, n_rows, body, 0, unroll=8)


def _row_gather_wait(dst, sem):
    pltpu.make_async_copy(dst, dst, sem).wait()


def _expert_kernel(te_ref, nu_ref, pad_ref, p1_ref, p2_ref, x_hbm, wg_ref, wu_ref, wd_ref, o_ref, src_ref, xbuf, sem):
    i = pl.program_id(0)
    nu = nu_ref[0]

    @pl.when(i == 0)
    def _():
        def fill_pad(e, carry):
            start = pad_ref[2 * e]

            def body(r, c):
                src_ref[r] = r - start
                return c

            lax.fori_loop(start, pad_ref[2 * e + 1], body, 0)
            return carry

        lax.fori_loop(0, N_EXPERTS, fill_pad, 0)

        def invert(n, carry):
            src_ref[p1_ref[n]] = n
            src_ref[p2_ref[n]] = n
            return carry

        lax.fori_loop(0, p1_ref.shape[0], invert, 0, unroll=8)
        _row_gather_start(src_ref, 0, MOE_TILE, x_hbm, xbuf.at[0], sem.at[0])

    @pl.when(i + 1 < nu)
    def _():
        nxt = (i + 1) % 2
        _row_gather_start(src_ref, (i + 1) * MOE_TILE, MOE_TILE, x_hbm, xbuf.at[nxt], sem.at[nxt])

    @pl.when(i < nu)
    def _():
        cur = i % 2
        _row_gather_wait(xbuf.at[cur], sem.at[cur])
        x = _load_row_tiles(xbuf.at[cur]).astype(BF16)
        hg = _dot(x, wg_ref[0])
        hu = _dot(x, wu_ref[0])
        h = (hg * _sigmoid(hg)) * hu
        _store_row_tiles(o_ref, _dot(h.astype(BF16), wd_ref[0]))

    @pl.when(i >= nu)
    def _():
        o_ref[...] = jnp.zeros_like(o_ref)


def _experts(xn, r, pos1, pos2, tile_expert, n_used, pad_ranges, wg, wu, wd):
    wspec = lambda a, b: pl.BlockSpec((1, a, b), lambda i, te, nu, pad, p1, p2: (te[i], 0, 0))
    grid_spec = pltpu.PrefetchScalarGridSpec(
        num_scalar_prefetch=5,
        grid=(r // MOE_TILE,),
        in_specs=[pl.BlockSpec(memory_space=pl.ANY),
                  wspec(D_MODEL, D_FF_E), wspec(D_MODEL, D_FF_E), wspec(D_FF_E, D_MODEL)],
        out_specs=pl.BlockSpec((MOE_TILE * ROW_CHUNKS, 128), lambda i, te, nu, pad, p1, p2: (i, 0)),
        scratch_shapes=[pltpu.SMEM((r,), I32), pltpu.VMEM((2, MOE_TILE * ROW_CHUNKS, 128), F32),
                        pltpu.SemaphoreType.DMA((2,))],
    )
    return pl.pallas_call(
        _expert_kernel,
        grid_spec=grid_spec,
        out_shape=jax.ShapeDtypeStruct((r * ROW_CHUNKS, 128), F32),
        compiler_params=_cparams("arbitrary"),
        name="moe_experts",
    )(tile_expert, n_used, pad_ranges, pos1, pos2, xn, wg, wu, wd)


def _moe(xn, idx, cnt, ep):
    n = xn.shape[0] // ROW_CHUNKS
    rows = 2 * n + N_EXPERTS * MOE_TILE
    counts = cnt[0, :N_EXPERTS].astype(I32)
    padded = ((counts + MOE_TILE - 1) // MOE_TILE) * MOE_TILE
    ends = jnp.cumsum(padded)
    offs = ends - padded
    pos1 = offs[idx[:, 0]] + idx[:, 2]
    pos2 = offs[idx[:, 1]] + idx[:, 3]
    pad_ranges = jnp.stack([offs + counts, ends], axis=1).reshape(-1).astype(I32)
    n_used = (ends[-1] // MOE_TILE).astype(I32)
    starts = jnp.arange(rows // MOE_TILE, dtype=I32) * MOE_TILE
    starts = jnp.minimum(starts, ends[-1] - 1)
    tile_expert = jnp.sum((starts[:, None] >= ends[None, :]).astype(I32), axis=1)
    tile_expert = jnp.minimum(tile_expert, N_EXPERTS - 1).astype(I32)
    out = _experts(xn, rows, pos1, pos2, tile_expert, n_used.reshape(1), pad_ranges, ep['wg'], ep['wu'], ep['wd'])
    return out, pos1, pos2


def _combine_kernel(p1_ref, p2_ref, x_ref, wts_ref, nrm_ref, out_hbm, y_o, gbuf, sem, ys_scr, *, final, b):
    i = pl.program_id(0)
    tm = x_ref.shape[0]

    def start(tile, slot):
        _row_gather_start(p1_ref, tile * tm, tm, out_hbm, gbuf.at[slot, 0], sem.at[slot])
        _row_gather_start(p2_ref, tile * tm, tm, out_hbm, gbuf.at[slot, 1], sem.at[slot])

    @pl.when(i == 0)
    def _():
        start(0, 0)

    @pl.when(i + 1 < pl.num_programs(0))
    def _():
        start(i + 1, (i + 1) % 2)

    cur = i % 2
    _row_gather_wait(gbuf.at[cur], sem.at[cur])
    wts = wts_ref[...]
    y = x_ref[...] + wts[:, 0:1] * _load_row_tiles(gbuf.at[cur, 0]) + wts[:, 1:2] * _load_row_tiles(gbuf.at[cur, 1])
    if final:
        _store_batch_major(y_o, ys_scr, _rms(y, nrm_ref[...]), b)
    else:
        y_o[...] = y


def _moe_combine(x, wts, out, pos1, pos2, nrm, final, b, t, tm=256):
    n = x.shape[0]
    tile = lambda w: pl.BlockSpec((tm, w), lambda i, p1, p2: (i, 0))
    if final:
        yshape, yspec = _time_major_spec(b, t, D_MODEL, tm)
    else:
        yshape, yspec = (n, D_MODEL), tile(D_MODEL)
    grid_spec = pltpu.PrefetchScalarGridSpec(
        num_scalar_prefetch=2,
        grid=(n // tm,),
        in_specs=[tile(D_MODEL), tile(128), pl.BlockSpec((1, D_MODEL), lambda i, p1, p2: (0, 0)),
                  pl.BlockSpec(memory_space=pl.ANY)],
        out_specs=yspec,
        scratch_shapes=[pltpu.VMEM((2, 2, tm * ROW_CHUNKS, 128), F32), pltpu.SemaphoreType.DMA((2,)),
                        pltpu.VMEM((D_MODEL // 128, tm, 128), F32)],
    )
    return pl.pallas_call(
        functools.partial(_combine_kernel, final=final, b=b),
        grid_spec=grid_spec,
        out_shape=jax.ShapeDtypeStruct(yshape, F32),
        compiler_params=_cparams("arbitrary"),
        name="moe_combine_final" if final else "moe_combine",
    )(pos1, pos2, x, wts, nrm, out)


def _gelu_tanh(x):
    return 0.5 * x * (1.0 + jnp.tanh(0.7978845608028654 * (x + 0.044715 * (x * x * x))))


def _mix1_kernel(x_ref, nmix_ref, bre_ref, bim_ref, are_ref, aim_ref, cre_ref, cim_ref,
                 dsk_ref, wo_ref, h0r_ref, h0i_ref, nrm_ref, wr_ref, br_ref,
                 x2_o, xn_o, idx_o, wts_o, cnt_o, hr_o, hi_o,
                 bur, bui, hr_scr, hi_scr, cnt_scr, *, b, cw):
    @pl.when(pl.program_id(0) == 0)
    def _():
        cnt_scr[...] = jnp.zeros_like(cnt_scr)
        hr_scr[...] = h0r_ref[...]
        hi_scr[...] = h0i_ref[...]

    x = x_ref[...]
    u = _rms(x, nmix_ref[...])
    ub = u.astype(BF16)
    nblk = bre_ref.shape[0]
    kin = D_MODEL // nblk
    kst = S5_STATE // nblk
    for cb in range(nblk):
        ucb = ub[:, cb * kin:(cb + 1) * kin]
        bur[:, cb * kst:(cb + 1) * kst] = _dot(ucb, bre_ref[cb])
        bui[:, cb * kst:(cb + 1) * kst] = _dot(ucb, bim_ref[cb])

    tc = x.shape[0] // b
    for c0 in range(0, S5_STATE, cw):
        cs = slice(c0, c0 + cw)
        ar = jnp.broadcast_to(are_ref[:, cs], (b, cw))
        ai = jnp.broadcast_to(aim_ref[:, cs], (b, cw))

        def step(s, carry, cs=cs, ar=ar, ai=ai):
            hr, hi = carry
            rows = pl.ds(pl.multiple_of(s * b, b), b)
            nr = ar * hr - ai * hi + bur[rows, cs]
            ni = ar * hi + ai * hr + bui[rows, cs]
            bur[rows, cs] = nr
            bui[rows, cs] = ni
            return nr, ni

        hr, hi = lax.fori_loop(0, tc, step, (hr_scr[:, cs], hi_scr[:, cs]), unroll=True)
        hr_scr[:, cs] = hr
        hi_scr[:, cs] = hi

    ych = []
    for cb in range(nblk):
        ss = slice(cb * kst, (cb + 1) * kst)
        ych.append(_dot(bur[:, ss].astype(BF16), cre_ref[cb]) - _dot(bui[:, ss].astype(BF16), cim_ref[cb]))
    y = jnp.concatenate(ych, axis=1) + dsk_ref[...] * u
    z = _dot(_gelu_tanh(y).astype(BF16), wo_ref[...])
    x2 = x + z[:, :D_MODEL] * _sigmoid(z[:, D_MODEL:])
    x2_o[...] = x2
    _route_tile(x2, nrm_ref, wr_ref, br_ref, cnt_scr, xn_o, idx_o, wts_o)
    cnt_o[...] = cnt_scr[...]
    hr_o[...] = hr_scr[...]
    hi_o[...] = hi_scr[...]


def _mix1(x, sp, rp, h0r, h0i, b, tr):
    n = x.shape[0]
    cw = 1024 if b == 8 else 128
    tile = lambda w: pl.BlockSpec((tr, w), lambda i: (i, 0))
    row = lambda w: pl.BlockSpec((1, w), lambda i: (0, 0))
    full = lambda *s: pl.BlockSpec(s, lambda i: (0,) * len(s))
    nblk = sp['b_re'].shape[0]
    return pl.pallas_call(
        functools.partial(_mix1_kernel, b=b, cw=cw),
        grid=(n // tr,),
        in_specs=[tile(D_MODEL), row(D_MODEL),
                  full(nblk, D_MODEL // nblk, S5_STATE // nblk), full(nblk, D_MODEL // nblk, S5_STATE // nblk),
                  row(S5_STATE), row(S5_STATE),
                  full(nblk, S5_STATE // nblk, D_MODEL // nblk), full(nblk, S5_STATE // nblk, D_MODEL // nblk),
                  row(D_MODEL), full(D_MODEL, 2 * D_MODEL), full(b, S5_STATE), full(b, S5_STATE)] + _route_in_specs(),
        out_specs=[tile(D_MODEL)] + _route_out_specs(tr) + [full(b, S5_STATE), full(b, S5_STATE)],
        out_shape=[jax.ShapeDtypeStruct((n, D_MODEL), F32)] + _route_out_shapes(n)
                  + [jax.ShapeDtypeStruct((b, S5_STATE), F32)] * 2,
        scratch_shapes=[pltpu.VMEM((tr, S5_STATE), F32), pltpu.VMEM((tr, S5_STATE), F32),
                        pltpu.VMEM((b, S5_STATE), F32), pltpu.VMEM((b, S5_STATE), F32),
                        pltpu.VMEM((8, 128), F32)],
        compiler_params=_cparams("arbitrary"),
        name="mix1",
    )(x, sp['norm'], sp['b_re'], sp['b_im'], sp['a_re'], sp['a_im'], sp['c_re'], sp['c_im'],
      sp['d'], sp['w_out'], h0r, h0i, rp['norm'], rp['w'], rp['b'])


def _router_params(norm, w_rc, b_rc, w_rf, b_rf):
    w = jnp.concatenate([w_rc, w_rf.reshape(D_MODEL, N_EXPERTS), jnp.zeros((D_MODEL, 12), F32)], axis=1)
    hi = w.astype(BF16)
    lo = (w - hi.astype(F32)).astype(BF16)
    wcat = jnp.concatenate([hi, lo, jnp.zeros((D_MODEL, 64), BF16)], axis=1)
    bias = jnp.concatenate([b_rc, b_rf.reshape(-1), jnp.zeros((12,), F32)]).reshape(1, 32)
    return {'norm': norm.reshape(1, D_MODEL), 'w': wcat, 'b': bias}


def _expert_params(wg, wu, wd):
    return {'wg': wg.astype(BF16), 'wu': wu.astype(BF16), 'wd': wd.astype(BF16)}


def _s5_params(norm, a_re, a_im, log_dt, b_re, b_im, c_re, c_im, d_skip, w_out, nblk=8):
    dt = jnp.exp(log_dt)
    mag = jnp.exp(dt * a_re)
    ab_re, ab_im = mag * jnp.cos(dt * a_im), mag * jnp.sin(dt * a_im)
    den = a_re * a_re + a_im * a_im
    f_re = ((ab_re - 1.0) * a_re + ab_im * a_im) / den
    f_im = (ab_im * a_re - (ab_re - 1.0) * a_im) / den
    bb_re = f_re[..., None] * b_re - f_im[..., None] * b_im
    bb_im = f_re[..., None] * b_im + f_im[..., None] * b_re
    gpb = S5_GROUPS // nblk
    eye = jnp.eye(gpb, dtype=F32)

    def in_blocks(bb):
        bb = bb.reshape(nblk, gpb, S5_P, S5_CH)
        w = jnp.einsum('ngpc,gh->ngchp', bb, eye)
        return w.reshape(nblk, gpb * S5_CH, gpb * S5_P).astype(BF16)

    def out_blocks(cc):
        cc = cc.reshape(nblk, gpb, S5_CH, S5_P)
        w = jnp.einsum('ngcp,gh->ngphc', cc, eye)
        return w.reshape(nblk, gpb * S5_P, gpb * S5_CH).astype(BF16)

    return {'norm': norm.reshape(1, D_MODEL), 'b_re': in_blocks(bb_re), 'b_im': in_blocks(bb_im),
            'a_re': ab_re.reshape(1, S5_STATE), 'a_im': ab_im.reshape(1, S5_STATE),
            'c_re': out_blocks(c_re), 'c_im': out_blocks(c_im), 'd': d_skip.reshape(1, D_MODEL),
            'w_out': w_out.astype(BF16)}


def _head_block(value):
    hid = jnp.arange(D_B, dtype=I32) // HEAD_DIM
    return jnp.where(hid[:, None] == hid[None, :], value, 0.0).astype(BF16)


def _run_group(x, cache_k, cache_v, shift0, wkv0, h0r, h0i, pr):
    b, t = x.shape[0], x.shape[1]
    n = b * t
    prompt = cache_k is None
    tm = 256 if prompt else 128
    if not prompt:
        x = x.transpose(1, 0, 2)
    q, kv, pb = _in_proj(x, pr['l0_norm'], pr['l0_w_in'], b, t, tm)

    if prompt:
        attn = _attn_prompt(q, kv, pr['sinks'], b, t)
        kv3 = kv[t - WINDOW:].reshape(WINDOW, b, 2, N_KV_A, HEAD_DIM)
        new_k = kv3[:, :, 0].transpose(1, 0, 2, 3)
        new_v = kv3[:, :, 1].transpose(1, 0, 2, 3)
        init = jnp.zeros((b, D_B_IN), F32)
    else:
        qs = q.reshape(t, b, Q_COLS).transpose(1, 0, 2)
        kvs = kv.reshape(t, b, 2 * KV_COLS).transpose(1, 0, 2)
        kn, vn = kvs[..., :KV_COLS], kvs[..., KV_COLS:]
        ck = cache_k.reshape(b, WINDOW, KV_COLS)
        cv = cache_v.reshape(b, WINDOW, KV_COLS)
        attn = _attn_sample(qs, kn, vn, ck, cv, pr['sinks'])
        attn = attn.transpose(1, 0, 2).reshape(n, Q_COLS)
        new_k = jnp.concatenate([ck[:, t:], kn], axis=1).reshape(b, WINDOW, N_KV_A, HEAD_DIM)
        new_v = jnp.concatenate([cv[:, t:], vn], axis=1).reshape(b, WINDOW, N_KV_A, HEAD_DIM)
        init = shift0
    new_shift = pb[n - b:]

    r, w, k, v, kk, nkka, bonus, g = _rwkv_prep(pb, init, pr['rw'], b, tm)
    tc = 64 if prompt else t
    s0 = jnp.zeros((b // SEQ_PER_GROUP, HEAD_DIM, HEAD_DIM // 2, 128), F32) if prompt else _state_to_scan(wkv0, b)
    o, s_fin = _wkv_scan(r, w, k, kk, nkka, v, s0, b, t, tc)
    new_wkv = _state_from_scan(s_fin, b)

    x1, xn, idx, wts, cnt = _mix0_out(o, bonus, g, attn, x, pr['rw'], pr['l0_route'], b, t, 256)
    out, pos1, pos2 = _moe(xn, idx, cnt, pr['l0_exp'])
    x1 = _moe_combine(x1, wts, out, pos1, pos2, pr['final_norm'], False, b, t)

    x2, xn, idx, wts, cnt, hr, hi = _mix1(x1, pr['s5'], pr['l1_route'], h0r, h0i, b, 256)
    out, pos1, pos2 = _moe(xn, idx, cnt, pr['l1_exp'])
    y = _moe_combine(x2, wts, out, pos1, pos2, pr['final_norm'], True, b, t)
    y = y if prompt else y.reshape(t, b, D_MODEL).transpose(1, 0, 2)
    return (y, new_k, new_v, new_shift, new_wkv,
            hr.reshape(b, S5_GROUPS, S5_P), hi.reshape(b, S5_GROUPS, S5_P))


def kernel(x_prompt, x_sample, cache_win_k, cache_win_v, state_shift, state_wkv, state_s5_re, state_s5_im,
           l0_norm_mix, l0_w_in, l0_sinks, l0_mu, l0_w0, l0_w_lora_up, l0_a0, l0_a_lora_up, l0_g_lora_up,
           l0_k_k, l0_k_a, l0_r_k, l0_ln_w, l0_ln_b, l0_w_out,
           l0_norm_ffn, l0_router_coarse, l0_bias_coarse, l0_router_fine, l0_bias_fine,
           l0_exp_gate, l0_exp_up, l0_exp_down,
           l1_norm_mix, l1_s5_a_re, l1_s5_a_im, l1_s5_log_dt, l1_s5_b_re, l1_s5_b_im, l1_s5_c_re, l1_s5_c_im,
           l1_s5_d, l1_w_out,
           l1_norm_ffn, l1_router_coarse, l1_bias_coarse, l1_router_fine, l1_bias_fine,
           l1_exp_gate, l1_exp_up, l1_exp_down,
           final_norm):
    row = lambda z: z.reshape(1, -1)
    pr = {
        'l0_norm': row(l0_norm_mix), 'l0_w_in': l0_w_in.astype(BF16), 'sinks': l0_sinks,
        'rw': {'mu': row(l0_mu), 'w0': row(l0_w0), 'w_up': l0_w_lora_up.astype(BF16), 'a0': row(l0_a0),
               'a_up': l0_a_lora_up.astype(BF16), 'g_up': l0_g_lora_up.astype(BF16), 'k_k': row(l0_k_k),
               'k_a': row(l0_k_a), 'r_k': row(l0_r_k), 'ln_w': row(l0_ln_w), 'ln_b': row(l0_ln_b),
               'head_ones': _head_block(1.0), 'head_avg': _head_block(1.0 / HEAD_DIM),
               'w_out': l0_w_out.astype(BF16)},
        'l0_route': _router_params(l0_norm_ffn, l0_router_coarse, l0_bias_coarse, l0_router_fine, l0_bias_fine),
        'l0_exp': _expert_params(l0_exp_gate, l0_exp_up, l0_exp_down),
        's5': _s5_params(l1_norm_mix, l1_s5_a_re, l1_s5_a_im, l1_s5_log_dt, l1_s5_b_re, l1_s5_b_im,
                         l1_s5_c_re, l1_s5_c_im, l1_s5_d, l1_w_out),
        'l1_route': _router_params(l1_norm_ffn, l1_router_coarse, l1_bias_coarse, l1_router_fine, l1_bias_fine),
        'l1_exp': _expert_params(l1_exp_gate, l1_exp_up, l1_exp_down),
        'final_norm': row(final_norm),
    }
    bp, bs = x_prompt.shape[0], x_sample.shape[0]
    zero_state = jnp.zeros((bp, S5_STATE), F32)
    yp, pk, pv, psh, pwkv, pre, pim = _run_group(x_prompt, None, None, None, None, zero_state, zero_state, pr)
    ys, sk, sv, ssh, swkv, sre, sim = _run_group(
        x_sample, cache_win_k, cache_win_v, state_shift, state_wkv,
        state_s5_re.reshape(bs, S5_STATE), state_s5_im.reshape(bs, S5_STATE), pr)
    return (yp, ys, pk, pv, psh, pwkv, pre, pim, sk, sv, ssh, swkv, sre, sim)
```

```python
import functools

import jax
import jax.numpy as jnp
from jax import lax
from jax.experimental import pallas as pl
from jax.experimental.pallas import tpu as pltpu

F32 = jnp.float32
BF16 = jnp.bfloat16
I32 = jnp.int32

D_MODEL = 1024
HEAD_DIM = 64
N_HEADS_A = 8
N_KV_A = 2
GQA_GROUP = 4
WINDOW = 128
Q_COLS = 512
KV_COLS = 128
D_A_IN = 768
N_HEADS_B = 8
D_B = 512
D_LORA_W = 64
D_LORA_A = 64
D_LORA_G = 128
D_B_IN = 1792
D_IN0 = 2560
RWKV_GN_EPS = 64e-5
S5_CH = 16
S5_GROUPS = 64
S5_P = 64
S5_STATE = S5_GROUPS * S5_P
N_EGROUPS = 4
EXP_PER_GROUP = 4
N_EXPERTS = 16
D_FF_E = 512
RMS_EPS = 1e-5
NEG_BIG = -1e30
PAIRS = 64
SEQ_PER_GROUP = PAIRS // N_HEADS_B
MOE_TILE = 256
ROW_CHUNKS = D_MODEL // 128
VMEM_LIMIT = 56 * 1024 * 1024


def _cparams(*sem):
    return pltpu.CompilerParams(dimension_semantics=sem, vmem_limit_bytes=VMEM_LIMIT)


def _dot(a, b):
    return jnp.dot(a, b, preferred_element_type=F32)


def _split_bf16(x):
    hi = x.astype(BF16)
    lo = (x - hi.astype(F32)).astype(BF16)
    return hi, lo


def _dot2(x, w):
    hi, lo = _split_bf16(x)
    return _dot(hi, w) + _dot(lo, w)


def _rms(x, g):
    return x * lax.rsqrt(jnp.mean(x * x, axis=-1, keepdims=True) + RMS_EPS) * g


def _sigmoid(x):
    return 1.0 / (1.0 + jnp.exp(-x))


def _store_row_tiles(ref, x):
    rows = x.shape[0]
    for c in range(ROW_CHUNKS):
        ref[pl.ds(c, rows, stride=ROW_CHUNKS), :] = x[:, c * 128:(c + 1) * 128]


def _time_major_spec(b, t, d, tm):
    if b == SEQ_PER_GROUP:
        return (b, t, d), pl.BlockSpec((b, tm // b, d), lambda i, *_: (0, i, 0))
    return (b * t, d), pl.BlockSpec((tm, d), lambda i, *_: (i, 0))


def _interleave_rows(pieces, scr):
    nb, steps = len(pieces), pieces[0].shape[0]
    chunks = pieces[0].shape[1] // 128
    for s, p in enumerate(pieces):
        for c in range(chunks):
            scr[c, pl.ds(s, steps, stride=nb), :] = p[:, c * 128:(c + 1) * 128].astype(scr.dtype)
    return jnp.concatenate([scr[c] for c in range(chunks)], axis=1)


def _deinterleave_rows(x, nb, scr):
    steps = x.shape[0] // nb
    chunks = x.shape[1] // 128
    for c in range(chunks):
        scr[c] = x[:, c * 128:(c + 1) * 128].astype(scr.dtype)
    return [jnp.concatenate([scr[c, pl.ds(s, steps, stride=nb), :] for c in range(chunks)], axis=1)
            for s in range(nb)]


def _load_time_major(x_ref, scr, b):
    if len(x_ref.shape) == 3:
        return _interleave_rows([x_ref[s] for s in range(b)], scr)
    return x_ref[...]


def _store_batch_major(y_ref, scr, y, b):
    if len(y_ref.shape) == 3:
        for s, p in enumerate(_deinterleave_rows(y, b, scr)):
            y_ref[s] = p
    else:
        y_ref[...] = y


def _load_row_tiles(ref):
    rows = ref.shape[0] // ROW_CHUNKS
    return jnp.concatenate([ref[pl.ds(c, rows, stride=ROW_CHUNKS), :] for c in range(ROW_CHUNKS)], axis=1)


def _in_proj_kernel(x_ref, g_ref, w_ref, q_ref, kv_ref, pb_ref, xs_scr, q_scr, kv_scr, *, b):
    x = _load_time_major(x_ref, xs_scr, b)
    xn = _rms(x, g_ref[...]).astype(BF16)
    q = _dot(xn, w_ref[:, :Q_COLS])
    kv = _dot(xn, w_ref[:, Q_COLS:D_A_IN])
    pb_ref[...] = _dot(xn, w_ref[:, D_A_IN:])
    if len(x_ref.shape) == 3:
        q_ref[...] = jnp.concatenate(_deinterleave_rows(q, b, q_scr), axis=1).astype(BF16)
        kv_ref[...] = jnp.concatenate(_deinterleave_rows(kv, b, kv_scr), axis=1)
    else:
        q_ref[...] = q
        kv_ref[...] = kv


def _in_proj(x, g, w_bf16, b, t, tm):
    n = b * t
    xshape, xspec = _time_major_spec(b, t, D_MODEL, tm)
    slab = len(xshape) == 3
    steps = tm // b
    if slab:
        qkv_specs = [pl.BlockSpec((steps, b * Q_COLS), lambda i: (i, 0)),
                     pl.BlockSpec((steps, b * 2 * KV_COLS), lambda i: (i, 0))]
        qkv_shapes = [jax.ShapeDtypeStruct((t, b * Q_COLS), BF16), jax.ShapeDtypeStruct((t, b * 2 * KV_COLS), F32)]
    else:
        qkv_specs = [pl.BlockSpec((tm, Q_COLS), lambda i: (i, 0)), pl.BlockSpec((tm, 2 * KV_COLS), lambda i: (i, 0))]
        qkv_shapes = [jax.ShapeDtypeStruct((n, Q_COLS), F32), jax.ShapeDtypeStruct((n, 2 * KV_COLS), F32)]
    return pl.pallas_call(
        functools.partial(_in_proj_kernel, b=b),
        grid=(n // tm,),
        in_specs=[xspec,
                  pl.BlockSpec((1, D_MODEL), lambda i: (0, 0)),
                  pl.BlockSpec((D_MODEL, D_IN0), lambda i: (0, 0))],
        out_specs=qkv_specs + [pl.BlockSpec((tm, D_B_IN), lambda i: (i, 0))],
        out_shape=qkv_shapes + [jax.ShapeDtypeStruct((n, D_B_IN), F32)],
        scratch_shapes=[pltpu.VMEM((D_MODEL // 128, tm, 128), F32), pltpu.VMEM((Q_COLS // 128, tm, 128), F32),
                        pltpu.VMEM((2 * KV_COLS // 128, tm, 128), F32)],
        compiler_params=_cparams("parallel"),
        name="in_proj",
    )(x.reshape(xshape), g, w_bf16)


def _attn_prompt_kernel(sinks_ref, q_ref, kc_ref, kp_ref, vc_ref, vp_ref, o_ref):
    j = pl.program_id(1)
    qi = lax.broadcasted_iota(I32, (WINDOW, 2 * WINDOW), 0)
    kj = lax.broadcasted_iota(I32, (WINDOW, 2 * WINDOW), 1)
    valid = jnp.logical_and(kj > qi, kj <= qi + WINDOW)
    valid = jnp.logical_and(valid, jnp.logical_or(kj >= WINDOW, j > 0))
    dist = (WINDOW + qi - kj).astype(F32)
    for n in range(N_KV_A):
        cs = slice(n * HEAD_DIM, (n + 1) * HEAD_DIM)
        kb = jnp.concatenate([kp_ref[:, cs], kc_ref[:, cs]], axis=0).astype(BF16)
        vb = jnp.concatenate([vp_ref[:, cs], vc_ref[:, cs]], axis=0).astype(BF16)
        for g in range(GQA_GROUP):
            h = n * GQA_GROUP + g
            hs = slice(h * HEAD_DIM, (h + 1) * HEAD_DIM)
            s = lax.dot_general(q_ref[:, hs], kb, (((1,), (1,)), ((), ())), preferred_element_type=F32)
            s = s * (HEAD_DIM ** -0.5) - (2.0 ** -(h + 1)) * dist
            s = jnp.where(valid, s, NEG_BIG)
            sink = sinks_ref[h]
            m = jnp.maximum(jnp.max(s, axis=1, keepdims=True), sink)
            p = jnp.exp(s - m)
            l = jnp.sum(p, axis=1, keepdims=True) + jnp.exp(sink - m)
            o = _dot(p.astype(BF16), vb) / l
            o_ref[:, hs] = o.astype(BF16)


def _attn_prompt(q2, kv2, sinks, b, t):
    prev = lambda bi, j: jnp.maximum(j - 1, 0)
    return pl.pallas_call(
        _attn_prompt_kernel,
        grid=(b, t // WINDOW),
        in_specs=[pl.BlockSpec(memory_space=pltpu.SMEM),
                  pl.BlockSpec((WINDOW, Q_COLS), lambda bi, j: (j, bi)),
                  pl.BlockSpec((WINDOW, KV_COLS), lambda bi, j: (j, 2 * bi)),
                  pl.BlockSpec((WINDOW, KV_COLS), lambda bi, j: (prev(bi, j), 2 * bi)),
                  pl.BlockSpec((WINDOW, KV_COLS), lambda bi, j: (j, 2 * bi + 1)),
                  pl.BlockSpec((WINDOW, KV_COLS), lambda bi, j: (prev(bi, j), 2 * bi + 1))],
        out_specs=pl.BlockSpec((WINDOW, Q_COLS), lambda bi, j: (j, bi)),
        out_shape=jax.ShapeDtypeStruct((t, b * Q_COLS), BF16),
        compiler_params=_cparams("parallel", "parallel"),
        name="attn_prompt",
    )(sinks, q2, kv2, kv2, kv2, kv2)


def _attn_sample_kernel(sinks_ref, q_ref, kn_ref, vn_ref, ck_ref, cv_ref, o_ref):
    bs, t = q_ref.shape[0], q_ref.shape[1]
    assert t & (t - 1) == 0
    nq, nk = GQA_GROUP * t, 2 * WINDOW
    r = lax.broadcasted_iota(I32, (nq, nk), 0)
    kj = lax.broadcasted_iota(I32, (nq, nk), 1)
    tq = jnp.bitwise_and(r, t - 1)
    valid = jnp.logical_and(kj > tq, kj <= tq + WINDOW)
    dist = (WINDOW + tq - kj).astype(F32)
    grp = jnp.right_shift(lax.broadcasted_iota(I32, (nq, 1), 0), t.bit_length() - 1)
    pad = jnp.zeros((bs, nk - WINDOW - t, HEAD_DIM), F32)
    for n in range(N_KV_A):
        cs = slice(n * HEAD_DIM, (n + 1) * HEAD_DIM)
        kb = jnp.concatenate([ck_ref[:, :, cs], kn_ref[:, :, cs], pad], axis=1).astype(BF16)
        vb = jnp.concatenate([cv_ref[:, :, cs], vn_ref[:, :, cs], pad], axis=1).astype(BF16)
        qn = jnp.concatenate([q_ref[:, :, (n * GQA_GROUP + g) * HEAD_DIM:(n * GQA_GROUP + g + 1) * HEAD_DIM]
                              for g in range(GQA_GROUP)], axis=1).astype(BF16)
        slope = jnp.zeros((nq, 1), F32)
        sink = jnp.zeros((nq, 1), F32)
        for g in range(GQA_GROUP):
            h = n * GQA_GROUP + g
            slope = jnp.where(grp == g, 2.0 ** -(h + 1), slope)
            sink = jnp.where(grp == g, sinks_ref[h], sink)
        s = jnp.einsum('bqd,bkd->bqk', qn, kb, preferred_element_type=F32)
        s = s * (HEAD_DIM ** -0.5) - (slope * dist)[None]
        s = jnp.where(valid[None], s, NEG_BIG)
        m = jnp.maximum(jnp.max(s, axis=2, keepdims=True), sink[None])
        p = jnp.exp(s - m)
        l = jnp.sum(p, axis=2, keepdims=True) + jnp.exp(sink[None] - m)
        o = jnp.einsum('bqk,bkd->bqd', p.astype(BF16), vb, preferred_element_type=F32) / l
        for g in range(GQA_GROUP):
            h = n * GQA_GROUP + g
            o_ref[:, :, h * HEAD_DIM:(h + 1) * HEAD_DIM] = o[:, g * t:(g + 1) * t, :]


def _attn_sample(q, kn, vn, ck, cv, sinks, bs=16):
    db, t = q.shape[0], q.shape[1]
    seq3 = lambda w: pl.BlockSpec((bs, t, w), lambda i: (i, 0, 0))
    cache = pl.BlockSpec((bs, WINDOW, KV_COLS), lambda i: (i, 0, 0))
    return pl.pallas_call(
        _attn_sample_kernel,
        grid=(db // bs,),
        in_specs=[pl.BlockSpec(memory_space=pltpu.SMEM), seq3(Q_COLS), seq3(KV_COLS), seq3(KV_COLS), cache, cache],
        out_specs=seq3(Q_COLS),
        out_shape=jax.ShapeDtypeStruct((db, t, Q_COLS), F32),
        compiler_params=_cparams("parallel"),
        name="attn_sample",
    )(sinks, q, kn, vn, ck, cv)


def _rwkv_prep_kernel(pb_ref, halo_ref, init_ref, mu_ref, w0_ref, wup_ref, a0_ref, aup_ref, gup_ref,
                      kk_ref, ka_ref, rk_ref, ones_ref,
                      r_o, w_o, k_o, v_o, kk_o, nkka_o, bonus_o, g_o, *, b):
    i = pl.program_id(0)
    pb = pb_ref[...]
    tm = pb.shape[0]
    halo = jnp.where(i == 0, init_ref[...], halo_ref[...])
    prev = halo if tm == b else jnp.concatenate([halo, pb[:tm - b]], axis=0)
    xs = pb + (prev - pb) * mu_ref[...]
    r = xs[:, :D_B]
    k = xs[:, D_B:2 * D_B]
    v = xs[:, 2 * D_B:3 * D_B]
    o1 = 3 * D_B
    wd = xs[:, o1:o1 + D_LORA_W]
    ad = xs[:, o1 + D_LORA_W:o1 + D_LORA_W + D_LORA_A]
    gd = xs[:, o1 + D_LORA_W + D_LORA_A:]
    z = -(w0_ref[...] + _dot(jnp.tanh(wd).astype(BF16), wup_ref[...]))
    softplus = jnp.maximum(z, 0.0) + jnp.log(1.0 + jnp.exp(-jnp.abs(z)))
    decay = jnp.exp(-jnp.exp(-softplus - 0.5))
    a = _sigmoid(a0_ref[...] + _dot(ad.astype(BF16), aup_ref[...]))
    g_o[...] = _dot(_sigmoid(gd).astype(BF16), gup_ref[...])
    ones = ones_ref[...]
    kk = k * kk_ref[...]
    kk = kk * lax.rsqrt(jnp.maximum(_dot2(kk * kk, ones), 1e-24))
    k2 = k * (1.0 + (a - 1.0) * ka_ref[...])
    bonus_o[...] = _dot2(r * k2 * rk_ref[...], ones) * v

    half = HEAD_DIM // 2
    lane8 = lax.broadcasted_iota(I32, (SEQ_PER_GROUP, 128), 1)
    low8 = lane8 < HEAD_DIM
    first_copy = jnp.bitwise_and(lax.broadcasted_iota(I32, (half, 128), 1), SEQ_PER_GROUP) == 0
    pairs = ((r, decay, r_o, w_o), (k2, kk, k_o, kk_o), (-(kk * a), v, nkka_o, None))
    for u in range(tm // SEQ_PER_GROUP):
        rows = slice(u * SEQ_PER_GROUP, (u + 1) * SEQ_PER_GROUP)
        for x, y, x_o, y_o in pairs:
            xu, yu = x[rows], y[rows]
            pieces = []
            for h in range(N_HEADS_B):
                cs = slice((h // 2) * 128, (h // 2 + 1) * 128)
                if h % 2 == 0:
                    p = jnp.where(low8, xu[:, cs], pltpu.roll(yu[:, cs], HEAD_DIM, 1))
                else:
                    p = jnp.where(low8, pltpu.roll(xu[:, cs], HEAD_DIM, 1), yu[:, cs])
                pieces += [p, p]
            tr = jnp.concatenate(pieces, axis=0).T
            x_o[u] = tr[:HEAD_DIM]
            if y_o is not None:
                y_o[u] = tr[HEAD_DIM:]
            else:
                v_o[u] = jnp.where(first_copy, tr[HEAD_DIM:HEAD_DIM + half], tr[HEAD_DIM + half:])


def _rwkv_prep(pb, init, p, b, tm):
    n = pb.shape[0]
    units = n // SEQ_PER_GROUP
    tu = tm // SEQ_PER_GROUP
    half = HEAD_DIM // 2
    row = lambda w: pl.BlockSpec((1, w), lambda i: (0, 0))
    full = lambda r, c: pl.BlockSpec((r, c), lambda i: (0, 0))
    tile = pl.BlockSpec((tm, D_B), lambda i: (i, 0))
    kspec = pl.BlockSpec((tu, HEAD_DIM, 128), lambda i: (i, 0, 0))
    vspec = pl.BlockSpec((tu, half, 128), lambda i: (i, 0, 0))
    kshape = jax.ShapeDtypeStruct((units, HEAD_DIM, 128), F32)
    halo_blocks = tm // b
    return pl.pallas_call(
        functools.partial(_rwkv_prep_kernel, b=b),
        grid=(n // tm,),
        in_specs=[pl.BlockSpec((tm, D_B_IN), lambda i: (i, 0)),
                  pl.BlockSpec((b, D_B_IN), lambda i: (jnp.maximum(i * halo_blocks - 1, 0), 0)),
                  full(b, D_B_IN), row(D_B_IN), row(D_B), full(D_LORA_W, D_B), row(D_B), full(D_LORA_A, D_B),
                  full(D_LORA_G, D_B), row(D_B), row(D_B), row(D_B), full(D_B, D_B)],
        out_specs=[kspec, kspec, kspec, vspec, kspec, kspec, tile, tile],
        out_shape=[kshape, kshape, kshape, jax.ShapeDtypeStruct((units, half, 128), F32), kshape, kshape,
                   jax.ShapeDtypeStruct((n, D_B), F32), jax.ShapeDtypeStruct((n, D_B), F32)],
        compiler_params=_cparams("arbitrary"),
        name="rwkv_prep",
    )(pb, pb, init, p['mu'], p['w0'], p['w_up'], p['a0'], p['a_up'], p['g_up'], p['k_k'], p['k_a'], p['r_k'],
      p['head_ones'])


def _wkv_scan_kernel(r_ref, w_ref, k_ref, kk_ref, nkka_ref, v_ref, s0_ref, o_ref, st_ref, s_scr):
    j = pl.program_id(1)

    @pl.when(j == 0)
    def _():
        s_scr[...] = s0_ref[0]

    tc = r_ref.shape[0]
    nsub = (HEAD_DIM // 2) // 8

    def bcast(ref, s, kx):
        return jnp.broadcast_to(ref[s, pl.ds(kx, 1), :], (8, 128))

    acc0 = [[jnp.zeros((8, 128), F32) for _ in range(2)] for _ in range(nsub)]
    for kx in range(HEAD_DIM):
        kkr = bcast(kk_ref, 0, kx)
        for i in range(nsub):
            acc0[i][kx % 2] = acc0[i][kx % 2] + s_scr[kx, 8 * i:8 * i + 8, :] * kkr

    def step(s, sa):
        nxt = jnp.minimum(s + 1, tc - 1)
        vv = [v_ref[s, 8 * i:8 * i + 8, :] for i in range(nsub)]
        oacc = [[jnp.zeros((8, 128), F32) for _ in range(2)] for _ in range(nsub)]
        nacc = [[jnp.zeros((8, 128), F32) for _ in range(2)] for _ in range(nsub)]
        for kx in range(HEAD_DIM):
            rr, wr, kr = bcast(r_ref, s, kx), bcast(w_ref, s, kx), bcast(k_ref, s, kx)
            nk, kkn = bcast(nkka_ref, s, kx), bcast(kk_ref, nxt, kx)
            for i in range(nsub):
                rows = slice(8 * i, 8 * i + 8)
                sk = s_scr[kx, rows, :] * wr + sa[i] * nk + vv[i] * kr
                s_scr[kx, rows, :] = sk
                oacc[i][kx % 2] = oacc[i][kx % 2] + sk * rr
                nacc[i][kx % 2] = nacc[i][kx % 2] + sk * kkn
        o_ref[s] = jnp.concatenate([a[0] + a[1] for a in oacc], axis=0)
        return [a[0] + a[1] for a in nacc]

    lax.fori_loop(0, tc, step, [a[0] + a[1] for a in acc0])

    @pl.when(j == pl.num_programs(1) - 1)
    def _():
        st_ref[0] = s_scr[...]


def _wkv_scan(r, w, k, kk, nkka, v, s0, b, t, tc):
    g = b // SEQ_PER_GROUP
    half = HEAD_DIM // 2
    kview = lambda z: z.reshape(t, g, HEAD_DIM, 128)
    kspec = pl.BlockSpec((tc, None, HEAD_DIM, 128), lambda gi, j: (j, gi, 0, 0))
    vspec = pl.BlockSpec((tc, None, half, 128), lambda gi, j: (j, gi, 0, 0))
    sspec = pl.BlockSpec((1, HEAD_DIM, half, 128), lambda gi, j: (gi, 0, 0, 0))
    o, st = pl.pallas_call(
        _wkv_scan_kernel,
        grid=(g, t // tc),
        in_specs=[kspec] * 5 + [vspec, sspec],
        out_specs=[vspec, sspec],
        out_shape=[jax.ShapeDtypeStruct((t, g, half, 128), F32),
                   jax.ShapeDtypeStruct((g, HEAD_DIM, half, 128), F32)],
        scratch_shapes=[pltpu.VMEM((HEAD_DIM, half, 128), F32)],
        compiler_params=_cparams("parallel", "arbitrary"),
        name="wkv_scan",
    )(kview(r), kview(w), kview(k), kview(kk), kview(nkka), v.reshape(t, g, half, 128), s0)
    return o.reshape(t * g, half, 128), st


def _state_to_scan(s, b):
    g = b // SEQ_PER_GROUP
    s = s.reshape(g, SEQ_PER_GROUP, N_HEADS_B, 2, HEAD_DIM // 2, HEAD_DIM).transpose(0, 5, 4, 2, 3, 1)
    return s.reshape(g, HEAD_DIM, HEAD_DIM // 2, 128)


def _state_from_scan(s, b):
    g = b // SEQ_PER_GROUP
    s = s.reshape(g, HEAD_DIM, HEAD_DIM // 2, N_HEADS_B, 2, SEQ_PER_GROUP).transpose(0, 5, 3, 4, 2, 1)
    return s.reshape(b, N_HEADS_B, HEAD_DIM, HEAD_DIM)


def _route_tile(x, nrm_ref, wr_ref, br_ref, cnt_scr, xn_o, idx_o, wts_o):
    tm = x.shape[0]
    xn = _rms(x, nrm_ref[...])
    hi, lo = _split_bf16(xn)
    _store_row_tiles(xn_o, xn)
    wr = wr_ref[...]
    pa = _dot(hi, wr)
    pb = _dot(lo, wr)
    lg = pa[:, 0:32] + pa[:, 32:64] + pb[:, 0:32] + pb[:, 32:64] + br_ref[...]
    col = lambda c: lg[:, c:c + 1]
    c = [col(gx) for gx in range(N_EGROUPS)]
    m = jnp.maximum(jnp.maximum(c[0], c[1]), jnp.maximum(c[2], c[3]))
    den = jnp.exp(c[0] - m) + jnp.exp(c[1] - m) + jnp.exp(c[2] - m) + jnp.exp(c[3] - m)
    pg = 1.0 / den
    gi = jnp.where(c[0] >= m, 0, jnp.where(c[1] >= m, 1, jnp.where(c[2] >= m, 2, 3))).astype(I32)
    sel = []
    for e in range(EXP_PER_GROUP):
        sel.append(jnp.where(gi == 0, col(4 + e), jnp.where(gi == 1, col(8 + e),
                                                            jnp.where(gi == 2, col(12 + e), col(16 + e)))))
    v1 = jnp.maximum(jnp.maximum(sel[0], sel[1]), jnp.maximum(sel[2], sel[3]))
    i1 = jnp.where(sel[0] >= v1, 0, jnp.where(sel[1] >= v1, 1, jnp.where(sel[2] >= v1, 2, 3))).astype(I32)
    rest = [jnp.where(i1 == e, -jnp.inf, sel[e]) for e in range(EXP_PER_GROUP)]
    v2 = jnp.maximum(jnp.maximum(rest[0], rest[1]), jnp.maximum(rest[2], rest[3]))
    i2 = jnp.where(rest[0] >= v2, 0, jnp.where(rest[1] >= v2, 1, jnp.where(rest[2] >= v2, 2, 3))).astype(I32)
    tt = jnp.exp(v2 - v1)
    w1 = pg / (1.0 + tt)
    w2 = pg * tt / (1.0 + tt)
    e1 = gi * EXP_PER_GROUP + i1
    e2 = gi * EXP_PER_GROUP + i2
    lane = lax.broadcasted_iota(I32, (tm, N_EXPERTS), 1)
    oh1 = lane == e1
    oh2 = lane == e2
    oh = jnp.where(jnp.logical_or(oh1, oh2), 1.0, 0.0)
    ri = lax.broadcasted_iota(I32, (tm, tm), 0)
    ci = lax.broadcasted_iota(I32, (tm, tm), 1)
    ltri = jnp.where(ri > ci, 1.0, 0.0).astype(BF16)
    cnt = cnt_scr[0:1, 0:N_EXPERTS]
    pre = _dot(ltri, oh.astype(BF16)) + cnt
    rank1 = jnp.sum(jnp.where(oh1, pre, 0.0), axis=1, keepdims=True)
    rank2 = jnp.sum(jnp.where(oh2, pre, 0.0), axis=1, keepdims=True)
    cnt_scr[0:1, 0:N_EXPERTS] = cnt + jnp.sum(oh, axis=0, keepdims=True)
    lw = lax.broadcasted_iota(I32, (tm, 128), 1)
    wts_o[...] = jnp.where(lw == 0, w1, jnp.where(lw == 1, w2, 0.0))
    cols = jnp.where(lw == 0, e1.astype(F32), jnp.where(lw == 1, e2.astype(F32),
                                                         jnp.where(lw == 2, rank1, jnp.where(lw == 3, rank2, 0.0))))
    idx_o[...] = cols.T[0:8, :].astype(I32)


def _route_out_specs(tm):
    return [pl.BlockSpec((tm * ROW_CHUNKS, 128), lambda i: (i, 0)),
            pl.BlockSpec((8, tm), lambda i: (0, i)),
            pl.BlockSpec((tm, 128), lambda i: (i, 0)),
            pl.BlockSpec((8, 128), lambda i: (0, 0))]


def _route_out_shapes(n):
    return [jax.ShapeDtypeStruct((n * ROW_CHUNKS, 128), F32),
            jax.ShapeDtypeStruct((8, n), I32),
            jax.ShapeDtypeStruct((n, 128), F32),
            jax.ShapeDtypeStruct((8, 128), F32)]


def _route_in_specs():
    return [pl.BlockSpec((1, D_MODEL), lambda i: (0, 0)),
            pl.BlockSpec((D_MODEL, 128), lambda i: (0, 0)),
            pl.BlockSpec((1, 32), lambda i: (0, 0))]


def _mix0_out_kernel(o_ref, bonus_ref, g_ref, attn_ref, x_ref, lnw_ref, lnb_ref, avg_ref, wo_ref,
                     nrm_ref, wr_ref, br_ref,
                     x1_o, xn_o, idx_o, wts_o, cnt_o, cnt_scr, xs_scr, at_scr, *, b):
    @pl.when(pl.program_id(0) == 0)
    def _():
        cnt_scr[...] = jnp.zeros_like(cnt_scr)

    avg = avg_ref[...]
    half = HEAD_DIM // 2
    lane8 = lax.broadcasted_iota(I32, (SEQ_PER_GROUP, 128), 1)
    unit_rows = []
    for u in range(o_ref.shape[0]):
        ot = jnp.concatenate([o_ref[u], jnp.zeros((128 - half, 128), F32)], axis=0).T
        cols = []
        for jj in range(N_HEADS_B // 2):
            q = [ot[(4 * jj + i) * SEQ_PER_GROUP:(4 * jj + i + 1) * SEQ_PER_GROUP] for i in range(4)]
            c = jnp.where(lane8 < half, q[0], pltpu.roll(q[1], half, 1))
            c = jnp.where(lane8 < 2 * half, c, pltpu.roll(q[2], 2 * half, 1))
            c = jnp.where(lane8 < 3 * half, c, pltpu.roll(q[3], 3 * half, 1))
            cols.append(c)
        unit_rows.append(jnp.concatenate(cols, axis=1))
    o = jnp.concatenate(unit_rows, axis=0)
    d = o - _dot2(o, avg)
    var = _dot2(d * d, avg)
    on = d * lax.rsqrt(var + RWKV_GN_EPS) * lnw_ref[...] + lnb_ref[...]
    rout = ((on + bonus_ref[...]) * g_ref[...]).astype(BF16)
    x = _load_time_major(x_ref, xs_scr, b)
    if len(x_ref.shape) == 3:
        attn = _interleave_rows([attn_ref[:, s * Q_COLS:(s + 1) * Q_COLS] for s in range(b)], at_scr)
    else:
        attn = attn_ref[...]
    y = _dot(attn.astype(BF16), wo_ref[:Q_COLS, :]) + _dot(rout, wo_ref[Q_COLS:, :]) + x
    x1_o[...] = y
    _route_tile(y, nrm_ref, wr_ref, br_ref, cnt_scr, xn_o, idx_o, wts_o)
    cnt_o[...] = cnt_scr[...]


def _mix0_out(o, bonus, g, attn, x, p, rp, b, t, tm):
    n = b * t
    xshape, xspec = _time_major_spec(b, t, D_MODEL, tm)
    aspec = (pl.BlockSpec((tm // b, b * Q_COLS), lambda i: (i, 0)) if len(xshape) == 3
             else pl.BlockSpec((tm, Q_COLS), lambda i: (i, 0)))
    tile = lambda w: pl.BlockSpec((tm, w), lambda i: (i, 0))
    row = lambda w: pl.BlockSpec((1, w), lambda i: (0, 0))
    full = lambda r, c: pl.BlockSpec((r, c), lambda i: (0, 0))
    return pl.pallas_call(
        functools.partial(_mix0_out_kernel, b=b),
        grid=(n // tm,),
        in_specs=[pl.BlockSpec((tm // SEQ_PER_GROUP, HEAD_DIM // 2, 128), lambda i: (i, 0, 0)),
                  tile(D_B), tile(D_B), aspec, xspec, row(D_B), row(D_B),
                  full(D_B, D_B), full(D_MODEL, D_MODEL)] + _route_in_specs(),
        out_specs=[tile(D_MODEL)] + _route_out_specs(tm),
        out_shape=[jax.ShapeDtypeStruct((n, D_MODEL), F32)] + _route_out_shapes(n),
        scratch_shapes=[pltpu.VMEM((8, 128), F32), pltpu.VMEM((D_MODEL // 128, tm, 128), F32),
                        pltpu.VMEM((Q_COLS // 128, tm, 128), F32)],
        compiler_params=_cparams("arbitrary"),
        name="mix0_out",
    )(o, bonus, g, attn, x.reshape(xshape), p['ln_w'], p['ln_b'], p['head_avg'], p['w_out'], rp['norm'], rp['w'], rp['b'])


def _row_gather_start(idx_ref, base, n_rows, src_hbm, dst, sem):
    def body(r, carry):
        src_row = pl.multiple_of(idx_ref[base + r] * ROW_CHUNKS, ROW_CHUNKS)
        dst_row = pl.multiple_of(r * ROW_CHUNKS, ROW_CHUNKS)
        pltpu.make_async_copy(src_hbm.at[pl.ds(src_row, ROW_CHUNKS)], dst.at[pl.ds(dst_row, ROW_CHUNKS)], sem).start()
        return carry

    lax.fori_loop(0, n_rows, body, 0, unroll=8)


def _row_gather_wait(dst, sem):
    pltpu.make_async_copy(dst, dst, sem).wait()


def _expert_kernel(te_ref, nu_ref, pad_ref, p1_ref, p2_ref, x_hbm, wg_ref, wu_ref, wd_ref, o_ref,
                   src_ref, xbuf, sem, wgb, wub, wdb):
    i = pl.program_id(0)
    nu = nu_ref[0]

    @pl.when(i == 0)
    def _():
        def fill_pad(e, carry):
            start = pad_ref[2 * e]

            def body(r, c):
                src_ref[r] = r - start
                return c

            lax.fori_loop(start, pad_ref[2 * e + 1], body, 0)
            return carry

        lax.fori_loop(0, N_EXPERTS + 1, fill_pad, 0)

        def invert(n, carry):
            src_ref[p1_ref[n]] = n
            src_ref[p2_ref[n]] = n
            return carry

        lax.fori_loop(0, p1_ref.shape[0], invert, 0, unroll=8)
        _row_gather_start(src_ref, 0, MOE_TILE, x_hbm, xbuf.at[0], sem.at[0])

    new_expert = jnp.logical_or(i == 0, te_ref[i] != te_ref[jnp.maximum(i - 1, 0)])

    @pl.when(jnp.logical_and(i < nu, new_expert))
    def _():
        wgb[...] = wg_ref[0].astype(BF16)
        wub[...] = wu_ref[0].astype(BF16)
        wdb[...] = wd_ref[0].astype(BF16)

    @pl.when(i < nu)
    def _():
        cur = i % 2
        nxt = 1 - cur
        base = (i + 1) * MOE_TILE
        cuts = (0, MOE_TILE // 3, 2 * MOE_TILE // 3, MOE_TILE)

        def start_rows(lo, hi):
            for r in range(lo, hi):
                src_row = pl.multiple_of(src_ref[base + r] * ROW_CHUNKS, ROW_CHUNKS)
                pltpu.make_async_copy(x_hbm.at[pl.ds(src_row, ROW_CHUNKS)],
                                      xbuf.at[nxt, pl.ds(r * ROW_CHUNKS, ROW_CHUNKS)], sem.at[nxt]).start()

        _row_gather_wait(xbuf.at[cur], sem.at[cur])
        x = _load_row_tiles(xbuf.at[cur]).astype(BF16)
        hg = _dot(x, wgb[...])
        start_rows(cuts[0], cuts[1])
        hu = _dot(x, wub[...])
        start_rows(cuts[1], cuts[2])
        h = (hg * _sigmoid(hg)) * hu
        res = _dot(h.astype(BF16), wdb[...])
        start_rows(cuts[2], cuts[3])
        _store_row_tiles(o_ref, res)

    @pl.when(i == nu)
    def _():
        _row_gather_wait(xbuf.at[i % 2], sem.at[i % 2])

    @pl.when(i >= nu)
    def _():
        o_ref[...] = jnp.zeros_like(o_ref)


def _experts(xn, r, pos1, pos2, tile_expert, n_used, pad_ranges, wg, wu, wd):
    wspec = lambda a, b: pl.BlockSpec((1, a, b), lambda i, te, nu, pad, p1, p2: (te[i], 0, 0))
    grid_spec = pltpu.PrefetchScalarGridSpec(
        num_scalar_prefetch=5,
        grid=(r // MOE_TILE,),
        in_specs=[pl.BlockSpec(memory_space=pl.ANY),
                  wspec(D_MODEL, D_FF_E), wspec(D_MODEL, D_FF_E), wspec(D_FF_E, D_MODEL)],
        out_specs=pl.BlockSpec((MOE_TILE * ROW_CHUNKS, 128), lambda i, te, nu, pad, p1, p2: (i, 0)),
        scratch_shapes=[pltpu.SMEM((r,), I32), pltpu.VMEM((2, MOE_TILE * ROW_CHUNKS, 128), F32),
                        pltpu.SemaphoreType.DMA((2,)),
                        pltpu.VMEM((D_MODEL, D_FF_E), BF16), pltpu.VMEM((D_MODEL, D_FF_E), BF16),
                        pltpu.VMEM((D_FF_E, D_MODEL), BF16)],
    )
    return pl.pallas_call(
        _expert_kernel,
        grid_spec=grid_spec,
        out_shape=jax.ShapeDtypeStruct((r * ROW_CHUNKS, 128), F32),
        compiler_params=_cparams("arbitrary"),
        name="moe_experts",
    )(tile_expert, n_used, pad_ranges, pos1, pos2, xn, wg, wu, wd)


def _moe(xn, idx, cnt, ep):
    n = xn.shape[0] // ROW_CHUNKS
    rows = 2 * n + N_EXPERTS * MOE_TILE
    counts = cnt[0, :N_EXPERTS].astype(I32)
    padded = ((counts + MOE_TILE - 1) // MOE_TILE) * MOE_TILE
    ends = jnp.cumsum(padded)
    offs = ends - padded
    pos1 = offs[idx[0]] + idx[2]
    pos2 = offs[idx[1]] + idx[3]
    pad_ranges = jnp.stack([jnp.append(offs + counts, ends[-1]), jnp.append(ends, ends[-1] + MOE_TILE)], axis=1)
    pad_ranges = pad_ranges.reshape(-1).astype(I32)
    n_used = (ends[-1] // MOE_TILE).astype(I32)
    starts = jnp.arange(rows // MOE_TILE, dtype=I32) * MOE_TILE
    starts = jnp.minimum(starts, ends[-1] - 1)
    tile_expert = jnp.sum((starts[:, None] >= ends[None, :]).astype(I32), axis=1)
    tile_expert = jnp.minimum(tile_expert, N_EXPERTS - 1).astype(I32)
    out = _experts(xn, rows, pos1, pos2, tile_expert, n_used.reshape(1), pad_ranges, ep['wg'], ep['wu'], ep['wd'])
    return out, pos1, pos2


def _combine_kernel(p1_ref, p2_ref, x_ref, wts_ref, nrm_ref, out_hbm, y_o, gbuf, sem, ys_scr, *, final, b):
    i = pl.program_id(0)
    tm = x_ref.shape[0]

    def start(tile, slot):
        _row_gather_start(p1_ref, tile * tm, tm, out_hbm, gbuf.at[slot, 0], sem.at[slot])
        _row_gather_start(p2_ref, tile * tm, tm, out_hbm, gbuf.at[slot, 1], sem.at[slot])

    @pl.when(i == 0)
    def _():
        start(0, 0)

    @pl.when(i + 1 < pl.num_programs(0))
    def _():
        start(i + 1, (i + 1) % 2)

    cur = i % 2
    _row_gather_wait(gbuf.at[cur], sem.at[cur])
    wts = wts_ref[...]
    y = x_ref[...] + wts[:, 0:1] * _load_row_tiles(gbuf.at[cur, 0]) + wts[:, 1:2] * _load_row_tiles(gbuf.at[cur, 1])
    if final:
        _store_batch_major(y_o, ys_scr, _rms(y, nrm_ref[...]), b)
    else:
        y_o[...] = y


def _moe_combine(x, wts, out, pos1, pos2, nrm, final, b, t, tm=256):
    n = x.shape[0]
    tile = lambda w: pl.BlockSpec((tm, w), lambda i, p1, p2: (i, 0))
    if final:
        yshape, yspec = _time_major_spec(b, t, D_MODEL, tm)
    else:
        yshape, yspec = (n, D_MODEL), tile(D_MODEL)
    grid_spec = pltpu.PrefetchScalarGridSpec(
        num_scalar_prefetch=2,
        grid=(n // tm,),
        in_specs=[tile(D_MODEL), tile(128), pl.BlockSpec((1, D_MODEL), lambda i, p1, p2: (0, 0)),
                  pl.BlockSpec(memory_space=pl.ANY)],
        out_specs=yspec,
        scratch_shapes=[pltpu.VMEM((2, 2, tm * ROW_CHUNKS, 128), F32), pltpu.SemaphoreType.DMA((2,)),
                        pltpu.VMEM((D_MODEL // 128, tm, 128), F32)],
    )
    return pl.pallas_call(
        functools.partial(_combine_kernel, final=final, b=b),
        grid_spec=grid_spec,
        out_shape=jax.ShapeDtypeStruct(yshape, F32),
        compiler_params=_cparams("arbitrary"),
        name="moe_combine_final" if final else "moe_combine",
    )(pos1, pos2, x, wts, nrm, out)


def _gelu_tanh(x):
    return 0.5 * x * (1.0 + jnp.tanh(0.7978845608028654 * (x + 0.044715 * (x * x * x))))


def _mix1_kernel(x_ref, nmix_ref, bre_ref, bim_ref, are_ref, aim_ref, cre_ref, cim_ref,
                 dsk_ref, wo_ref, h0r_ref, h0i_ref, nrm_ref, wr_ref, br_ref,
                 x2_o, xn_o, idx_o, wts_o, cnt_o, hr_o, hi_o,
                 bur, bui, hr_scr, hi_scr, cnt_scr, *, b, cw):
    @pl.when(pl.program_id(0) == 0)
    def _():
        cnt_scr[...] = jnp.zeros_like(cnt_scr)
        hr_scr[...] = h0r_ref[...]
        hi_scr[...] = h0i_ref[...]

    x = x_ref[...]
    u = _rms(x, nmix_ref[...])
    ub = u.astype(BF16)
    nblk = bre_ref.shape[0]
    kin = D_MODEL // nblk
    kst = S5_STATE // nblk
    for cb in range(nblk):
        ucb = ub[:, cb * kin:(cb + 1) * kin]
        bur[:, cb * kst:(cb + 1) * kst] = _dot(ucb, bre_ref[cb])
        bui[:, cb * kst:(cb + 1) * kst] = _dot(ucb, bim_ref[cb])

    tc = x.shape[0] // b
    for c0 in range(0, S5_STATE, cw):
        cs = slice(c0, c0 + cw)
        ar = jnp.broadcast_to(are_ref[:, cs], (b, cw))
        ai = jnp.broadcast_to(aim_ref[:, cs], (b, cw))

        def step(s, carry, cs=cs, ar=ar, ai=ai):
            hr, hi = carry
            rows = pl.ds(pl.multiple_of(s * b, b), b)
            nr = ar * hr - ai * hi + bur[rows, cs]
            ni = ar * hi + ai * hr + bui[rows, cs]
            bur[rows, cs] = nr
            bui[rows, cs] = ni
            return nr, ni

        hr, hi = lax.fori_loop(0, tc, step, (hr_scr[:, cs], hi_scr[:, cs]), unroll=True)
        hr_scr[:, cs] = hr
        hi_scr[:, cs] = hi

    ych = []
    for cb in range(nblk):
        ss = slice(cb * kst, (cb + 1) * kst)
        ych.append(_dot(bur[:, ss].astype(BF16), cre_ref[cb]) - _dot(bui[:, ss].astype(BF16), cim_ref[cb]))
    y = jnp.concatenate(ych, axis=1) + dsk_ref[...] * u
    z = _dot(_gelu_tanh(y).astype(BF16), wo_ref[...])
    x2 = x + z[:, :D_MODEL] * _sigmoid(z[:, D_MODEL:])
    x2_o[...] = x2
    _route_tile(x2, nrm_ref, wr_ref, br_ref, cnt_scr, xn_o, idx_o, wts_o)
    cnt_o[...] = cnt_scr[...]
    hr_o[...] = hr_scr[...]
    hi_o[...] = hi_scr[...]


def _mix1(x, sp, rp, h0r, h0i, b, tr):
    n = x.shape[0]
    cw = 1024 if b == 8 else 128
    tile = lambda w: pl.BlockSpec((tr, w), lambda i: (i, 0))
    row = lambda w: pl.BlockSpec((1, w), lambda i: (0, 0))
    full = lambda *s: pl.BlockSpec(s, lambda i: (0,) * len(s))
    nblk = sp['b_re'].shape[0]
    return pl.pallas_call(
        functools.partial(_mix1_kernel, b=b, cw=cw),
        grid=(n // tr,),
        in_specs=[tile(D_MODEL), row(D_MODEL),
                  full(nblk, D_MODEL // nblk, S5_STATE // nblk), full(nblk, D_MODEL // nblk, S5_STATE // nblk),
                  row(S5_STATE), row(S5_STATE),
                  full(nblk, S5_STATE // nblk, D_MODEL // nblk), full(nblk, S5_STATE // nblk, D_MODEL // nblk),
                  row(D_MODEL), full(D_MODEL, 2 * D_MODEL), full(b, S5_STATE), full(b, S5_STATE)] + _route_in_specs(),
        out_specs=[tile(D_MODEL)] + _route_out_specs(tr) + [full(b, S5_STATE), full(b, S5_STATE)],
        out_shape=[jax.ShapeDtypeStruct((n, D_MODEL), F32)] + _route_out_shapes(n)
                  + [jax.ShapeDtypeStruct((b, S5_STATE), F32)] * 2,
        scratch_shapes=[pltpu.VMEM((tr, S5_STATE), F32), pltpu.VMEM((tr, S5_STATE), F32),
                        pltpu.VMEM((b, S5_STATE), F32), pltpu.VMEM((b, S5_STATE), F32),
                        pltpu.VMEM((8, 128), F32)],
        compiler_params=_cparams("arbitrary"),
        name="mix1",
    )(x, sp['norm'], sp['b_re'], sp['b_im'], sp['a_re'], sp['a_im'], sp['c_re'], sp['c_im'],
      sp['d'], sp['w_out'], h0r, h0i, rp['norm'], rp['w'], rp['b'])


def _router_params(norm, w_rc, b_rc, w_rf, b_rf):
    w = jnp.concatenate([w_rc, w_rf.reshape(D_MODEL, N_EXPERTS), jnp.zeros((D_MODEL, 12), F32)], axis=1)
    hi = w.astype(BF16)
    lo = (w - hi.astype(F32)).astype(BF16)
    wcat = jnp.concatenate([hi, lo, jnp.zeros((D_MODEL, 64), BF16)], axis=1)
    bias = jnp.concatenate([b_rc, b_rf.reshape(-1), jnp.zeros((12,), F32)]).reshape(1, 32)
    return {'norm': norm.reshape(1, D_MODEL), 'w': wcat, 'b': bias}


def _expert_params(wg, wu, wd):
    return {'wg': wg, 'wu': wu, 'wd': wd}


def _s5_params(norm, a_re, a_im, log_dt, b_re, b_im, c_re, c_im, d_skip, w_out, nblk=8):
    dt = jnp.exp(log_dt)
    mag = jnp.exp(dt * a_re)
    ab_re, ab_im = mag * jnp.cos(dt * a_im), mag * jnp.sin(dt * a_im)
    den = a_re * a_re + a_im * a_im
    f_re = ((ab_re - 1.0) * a_re + ab_im * a_im) / den
    f_im = (ab_im * a_re - (ab_re - 1.0) * a_im) / den
    bb_re = f_re[..., None] * b_re - f_im[..., None] * b_im
    bb_im = f_re[..., None] * b_im + f_im[..., None] * b_re
    gpb = S5_GROUPS // nblk
    eye = jnp.eye(gpb, dtype=F32)

    def in_blocks(bb):
        bb = bb.reshape(nblk, gpb, S5_P, S5_CH)
        w = jnp.einsum('ngpc,gh->ngchp', bb, eye)
        return w.reshape(nblk, gpb * S5_CH, gpb * S5_P).astype(BF16)

    def out_blocks(cc):
        cc = cc.reshape(nblk, gpb, S5_CH, S5_P)
        w = jnp.einsum('ngcp,gh->ngphc', cc, eye)
        return w.reshape(nblk, gpb * S5_P, gpb * S5_CH).astype(BF16)

    return {'norm': norm.reshape(1, D_MODEL), 'b_re': in_blocks(bb_re), 'b_im': in_blocks(bb_im),
            'a_re': ab_re.reshape(1, S5_STATE), 'a_im': ab_im.reshape(1, S5_STATE),
            'c_re': out_blocks(c_re), 'c_im': out_blocks(c_im), 'd': d_skip.reshape(1, D_MODEL),
            'w_out': w_out.astype(BF16)}


def _head_block(value):
    hid = jnp.arange(D_B, dtype=I32) // HEAD_DIM
    return jnp.where(hid[:, None] == hid[None, :], value, 0.0).astype(BF16)


def _run_group(x, cache_k, cache_v, shift0, wkv0, h0r, h0i, pr):
    b, t = x.shape[0], x.shape[1]
    n = b * t
    prompt = cache_k is None
    tm = 256 if prompt else 128
    if not prompt:
        x = x.transpose(1, 0, 2)
    q, kv, pb = _in_proj(x, pr['l0_norm'], pr['l0_w_in'], b, t, tm)

    if prompt:
        attn = _attn_prompt(q, kv, pr['sinks'], b, t)
        kv3 = kv[t - WINDOW:].reshape(WINDOW, b, 2, N_KV_A, HEAD_DIM)
        new_k = kv3[:, :, 0].transpose(1, 0, 2, 3)
        new_v = kv3[:, :, 1].transpose(1, 0, 2, 3)
        init = jnp.zeros((b, D_B_IN), F32)
    else:
        qs = q.reshape(t, b, Q_COLS).transpose(1, 0, 2)
        kvs = kv.reshape(t, b, 2 * KV_COLS).transpose(1, 0, 2)
        kn, vn = kvs[..., :KV_COLS], kvs[..., KV_COLS:]
        ck = cache_k.reshape(b, WINDOW, KV_COLS)
        cv = cache_v.reshape(b, WINDOW, KV_COLS)
        attn = _attn_sample(qs, kn, vn, ck, cv, pr['sinks'])
        attn = attn.transpose(1, 0, 2).reshape(n, Q_COLS)
        new_k = jnp.concatenate([ck[:, t:], kn], axis=1).reshape(b, WINDOW, N_KV_A, HEAD_DIM)
        new_v = jnp.concatenate([cv[:, t:], vn], axis=1).reshape(b, WINDOW, N_KV_A, HEAD_DIM)
        init = shift0
    new_shift = pb[n - b:]

    r, w, k, v, kk, nkka, bonus, g = _rwkv_prep(pb, init, pr['rw'], b, tm)
    tc = 64 if prompt else t
    s0 = jnp.zeros((b // SEQ_PER_GROUP, HEAD_DIM, HEAD_DIM // 2, 128), F32) if prompt else _state_to_scan(wkv0, b)
    o, s_fin = _wkv_scan(r, w, k, kk, nkka, v, s0, b, t, tc)
    new_wkv = _state_from_scan(s_fin, b)

    x1, xn, idx, wts, cnt = _mix0_out(o, bonus, g, attn, x, pr['rw'], pr['l0_route'], b, t, 256)
    out, pos1, pos2 = _moe(xn, idx, cnt, pr['l0_exp'])
    x1 = _moe_combine(x1, wts, out, pos1, pos2, pr['final_norm'], False, b, t)

    x2, xn, idx, wts, cnt, hr, hi = _mix1(x1, pr['s5'], pr['l1_route'], h0r, h0i, b, 256)
    out, pos1, pos2 = _moe(xn, idx, cnt, pr['l1_exp'])
    y = _moe_combine(x2, wts, out, pos1, pos2, pr['final_norm'], True, b, t)
    y = y if prompt else y.reshape(t, b, D_MODEL).transpose(1, 0, 2)
    return (y, new_k, new_v, new_shift, new_wkv,
            hr.reshape(b, S5_GROUPS, S5_P), hi.reshape(b, S5_GROUPS, S5_P))


def kernel(x_prompt, x_sample, cache_win_k, cache_win_v, state_shift, state_wkv, state_s5_re, state_s5_im,
           l0_norm_mix, l0_w_in, l0_sinks, l0_mu, l0_w0, l0_w_lora_up, l0_a0, l0_a_lora_up, l0_g_lora_up,
           l0_k_k, l0_k_a, l0_r_k, l0_ln_w, l0_ln_b, l0_w_out,
           l0_norm_ffn, l0_router_coarse, l0_bias_coarse, l0_router_fine, l0_bias_fine,
           l0_exp_gate, l0_exp_up, l0_exp_down,
           l1_norm_mix, l1_s5_a_re, l1_s5_a_im, l1_s5_log_dt, l1_s5_b_re, l1_s5_b_im, l1_s5_c_re, l1_s5_c_im,
           l1_s5_d, l1_w_out,
           l1_norm_ffn, l1_router_coarse, l1_bias_coarse, l1_router_fine, l1_bias_fine,
           l1_exp_gate, l1_exp_up, l1_exp_down,
           final_norm):
    row = lambda z: z.reshape(1, -1)
    pr = {
        'l0_norm': row(l0_norm_mix), 'l0_w_in': l0_w_in.astype(BF16), 'sinks': l0_sinks,
        'rw': {'mu': row(l0_mu), 'w0': row(l0_w0), 'w_up': l0_w_lora_up.astype(BF16), 'a0': row(l0_a0),
               'a_up': l0_a_lora_up.astype(BF16), 'g_up': l0_g_lora_up.astype(BF16), 'k_k': row(l0_k_k),
               'k_a': row(l0_k_a), 'r_k': row(l0_r_k), 'ln_w': row(l0_ln_w), 'ln_b': row(l0_ln_b),
               'head_ones': _head_block(1.0), 'head_avg': _head_block(1.0 / HEAD_DIM),
               'w_out': l0_w_out.astype(BF16)},
        'l0_route': _router_params(l0_norm_ffn, l0_router_coarse, l0_bias_coarse, l0_router_fine, l0_bias_fine),
        'l0_exp': _expert_params(l0_exp_gate, l0_exp_up, l0_exp_down),
        's5': _s5_params(l1_norm_mix, l1_s5_a_re, l1_s5_a_im, l1_s5_log_dt, l1_s5_b_re, l1_s5_b_im,
                         l1_s5_c_re, l1_s5_c_im, l1_s5_d, l1_w_out),
        'l1_route': _router_params(l1_norm_ffn, l1_router_coarse, l1_bias_coarse, l1_router_fine, l1_bias_fine),
        'l1_exp': _expert_params(l1_exp_gate, l1_exp_up, l1_exp_down),
        'final_norm': row(final_norm),
    }
    bp, bs = x_prompt.shape[0], x_sample.shape[0]
    zero_state = jnp.zeros((bp, S5_STATE), F32)
    yp, pk, pv, psh, pwkv, pre, pim = _run_group(x_prompt, None, None, None, None, zero_state, zero_state, pr)
    ys, sk, sv, ssh, swkv, sre, sim = _run_group(
        x_sample, cache_win_k, cache_win_v, state_shift, state_wkv,
        state_s5_re.reshape(bs, S5_STATE), state_s5_im.reshape(bs, S5_STATE), pr)
    return (yp, ys, pk, pv, psh, pwkv, pre, pim, sk, sv, ssh, swkv, sre, sim)
```

```python
import functools

import jax
import jax.numpy as jnp
from jax import lax
from jax.experimental import pallas as pl
from jax.experimental.pallas import tpu as pltpu

F32 = jnp.float32
BF16 = jnp.bfloat16
I32 = jnp.int32

D_MODEL = 1024
HEAD_DIM = 64
N_HEADS_A = 8
N_KV_A = 2
GQA_GROUP = 4
WINDOW = 128
Q_COLS = 512
KV_COLS = 128
D_A_IN = 768
N_HEADS_B = 8
D_B = 512
D_LORA_W = 64
D_LORA_A = 64
D_LORA_G = 128
D_B_IN = 1792
D_IN0 = 2560
RWKV_GN_EPS = 64e-5
S5_CH = 16
S5_GROUPS = 64
S5_P = 64
S5_STATE = S5_GROUPS * S5_P
N_EGROUPS = 4
EXP_PER_GROUP = 4
N_EXPERTS = 16
D_FF_E = 512
RMS_EPS = 1e-5
NEG_BIG = -1e30
PAIRS = 64
SEQ_PER_GROUP = PAIRS // N_HEADS_B
MOE_TILE = 256
ROW_CHUNKS = D_MODEL // 128
VMEM_LIMIT = 56 * 1024 * 1024


def _cparams(*sem):
    return pltpu.CompilerParams(dimension_semantics=sem, vmem_limit_bytes=VMEM_LIMIT)


def _dot(a, b):
    return jnp.dot(a, b, preferred_element_type=F32)


def _split_bf16(x):
    hi = x.astype(BF16)
    lo = (x - hi.astype(F32)).astype(BF16)
    return hi, lo


def _dot2(x, w):
    hi, lo = _split_bf16(x)
    return _dot(hi, w) + _dot(lo, w)


def _rms(x, g):
    return x * lax.rsqrt(jnp.mean(x * x, axis=-1, keepdims=True) + RMS_EPS) * g


def _sigmoid(x):
    return 1.0 / (1.0 + jnp.exp(-x))


def _store_row_tiles(ref, x):
    rows = x.shape[0]
    for c in range(ROW_CHUNKS):
        ref[pl.ds(c, rows, stride=ROW_CHUNKS), :] = x[:, c * 128:(c + 1) * 128]


def _time_major_spec(b, t, d, tm):
    if b == SEQ_PER_GROUP:
        return (b, t, d), pl.BlockSpec((b, tm // b, d), lambda i, *_: (0, i, 0))
    return (b * t, d), pl.BlockSpec((tm, d), lambda i, *_: (i, 0))


def _interleave_rows(pieces, scr):
    nb, steps = len(pieces), pieces[0].shape[0]
    chunks = pieces[0].shape[1] // 128
    for s, p in enumerate(pieces):
        for c in range(chunks):
            scr[c, pl.ds(s, steps, stride=nb), :] = p[:, c * 128:(c + 1) * 128].astype(scr.dtype)
    return jnp.concatenate([scr[c] for c in range(chunks)], axis=1)


def _deinterleave_rows(x, nb, scr):
    steps = x.shape[0] // nb
    chunks = x.shape[1] // 128
    for c in range(chunks):
        scr[c] = x[:, c * 128:(c + 1) * 128].astype(scr.dtype)
    return [jnp.concatenate([scr[c, pl.ds(s, steps, stride=nb), :] for c in range(chunks)], axis=1)
            for s in range(nb)]


def _load_time_major(x_ref, scr, b):
    if len(x_ref.shape) == 3:
        return _interleave_rows([x_ref[s] for s in range(b)], scr)
    return x_ref[...]


def _store_batch_major(y_ref, scr, y, b):
    if len(y_ref.shape) == 3:
        for s, p in enumerate(_deinterleave_rows(y, b, scr)):
            y_ref[s] = p
    else:
        y_ref[...] = y


def _load_row_tiles(ref):
    rows = ref.shape[0] // ROW_CHUNKS
    return jnp.concatenate([ref[pl.ds(c, rows, stride=ROW_CHUNKS), :] for c in range(ROW_CHUNKS)], axis=1)


def _in_proj_kernel(x_ref, g_ref, w_ref, q_ref, kv_ref, pb_ref, xs_scr, q_scr, kv_scr, *, b):
    x = _load_time_major(x_ref, xs_scr, b)
    xn = _rms(x, g_ref[...]).astype(BF16)
    q = _dot(xn, w_ref[:, :Q_COLS])
    kv = _dot(xn, w_ref[:, Q_COLS:D_A_IN])
    pb_ref[...] = _dot(xn, w_ref[:, D_A_IN:])
    if len(x_ref.shape) == 3:
        q_ref[...] = jnp.concatenate(_deinterleave_rows(q, b, q_scr), axis=1).astype(BF16)
        kv_ref[...] = jnp.concatenate(_deinterleave_rows(kv, b, kv_scr), axis=1)
    else:
        q_ref[...] = q
        kv_ref[...] = kv


def _in_proj(x, g, w_bf16, b, t, tm):
    n = b * t
    xshape, xspec = _time_major_spec(b, t, D_MODEL, tm)
    slab = len(xshape) == 3
    steps = tm // b
    if slab:
        qkv_specs = [pl.BlockSpec((steps, b * Q_COLS), lambda i: (i, 0)),
                     pl.BlockSpec((steps, b * 2 * KV_COLS), lambda i: (i, 0))]
        qkv_shapes = [jax.ShapeDtypeStruct((t, b * Q_COLS), BF16), jax.ShapeDtypeStruct((t, b * 2 * KV_COLS), F32)]
    else:
        qkv_specs = [pl.BlockSpec((tm, Q_COLS), lambda i: (i, 0)), pl.BlockSpec((tm, 2 * KV_COLS), lambda i: (i, 0))]
        qkv_shapes = [jax.ShapeDtypeStruct((n, Q_COLS), F32), jax.ShapeDtypeStruct((n, 2 * KV_COLS), F32)]
    return pl.pallas_call(
        functools.partial(_in_proj_kernel, b=b),
        grid=(n // tm,),
        in_specs=[xspec,
                  pl.BlockSpec((1, D_MODEL), lambda i: (0, 0)),
                  pl.BlockSpec((D_MODEL, D_IN0), lambda i: (0, 0))],
        out_specs=qkv_specs + [pl.BlockSpec((tm, D_B_IN), lambda i: (i, 0))],
        out_shape=qkv_shapes + [jax.ShapeDtypeStruct((n, D_B_IN), F32)],
        scratch_shapes=[pltpu.VMEM((D_MODEL // 128, tm, 128), F32), pltpu.VMEM((Q_COLS // 128, tm, 128), F32),
                        pltpu.VMEM((2 * KV_COLS // 128, tm, 128), F32)],
        compiler_params=_cparams("parallel"),
        name="in_proj",
    )(x.reshape(xshape), g, w_bf16)


def _attn_prompt_kernel(sinks_ref, q_ref, kc_ref, kp_ref, vc_ref, vp_ref, o_ref):
    j = pl.program_id(1)
    qi = lax.broadcasted_iota(I32, (WINDOW, 2 * WINDOW), 0)
    kj = lax.broadcasted_iota(I32, (WINDOW, 2 * WINDOW), 1)
    valid = jnp.logical_and(kj > qi, kj <= qi + WINDOW)
    valid = jnp.logical_and(valid, jnp.logical_or(kj >= WINDOW, j > 0))
    dist = (WINDOW + qi - kj).astype(F32)
    for n in range(N_KV_A):
        cs = slice(n * HEAD_DIM, (n + 1) * HEAD_DIM)
        kb = jnp.concatenate([kp_ref[:, cs], kc_ref[:, cs]], axis=0).astype(BF16)
        vb = jnp.concatenate([vp_ref[:, cs], vc_ref[:, cs]], axis=0).astype(BF16)
        for g in range(GQA_GROUP):
            h = n * GQA_GROUP + g
            hs = slice(h * HEAD_DIM, (h + 1) * HEAD_DIM)
            s = lax.dot_general(q_ref[:, hs], kb, (((1,), (1,)), ((), ())), preferred_element_type=F32)
            s = s * (HEAD_DIM ** -0.5) - (2.0 ** -(h + 1)) * dist
            s = jnp.where(valid, s, NEG_BIG)
            sink = sinks_ref[h]
            m = jnp.maximum(jnp.max(s, axis=1, keepdims=True), sink)
            p = jnp.exp(s - m)
            l = jnp.sum(p, axis=1, keepdims=True) + jnp.exp(sink - m)
            o = _dot(p.astype(BF16), vb) / l
            o_ref[:, hs] = o.astype(BF16)


def _attn_prompt(q2, kv2, sinks, b, t):
    prev = lambda bi, j: jnp.maximum(j - 1, 0)
    return pl.pallas_call(
        _attn_prompt_kernel,
        grid=(b, t // WINDOW),
        in_specs=[pl.BlockSpec(memory_space=pltpu.SMEM),
                  pl.BlockSpec((WINDOW, Q_COLS), lambda bi, j: (j, bi)),
                  pl.BlockSpec((WINDOW, KV_COLS), lambda bi, j: (j, 2 * bi)),
                  pl.BlockSpec((WINDOW, KV_COLS), lambda bi, j: (prev(bi, j), 2 * bi)),
                  pl.BlockSpec((WINDOW, KV_COLS), lambda bi, j: (j, 2 * bi + 1)),
                  pl.BlockSpec((WINDOW, KV_COLS), lambda bi, j: (prev(bi, j), 2 * bi + 1))],
        out_specs=pl.BlockSpec((WINDOW, Q_COLS), lambda bi, j: (j, bi)),
        out_shape=jax.ShapeDtypeStruct((t, b * Q_COLS), BF16),
        compiler_params=_cparams("parallel", "parallel"),
        name="attn_prompt",
    )(sinks, q2, kv2, kv2, kv2, kv2)


def _attn_sample_kernel(sinks_ref, q_ref, kn_ref, vn_ref, ck_ref, cv_ref, o_ref):
    bs, t = q_ref.shape[0], q_ref.shape[1]
    assert t & (t - 1) == 0
    nq, nk = GQA_GROUP * t, 2 * WINDOW
    r = lax.broadcasted_iota(I32, (nq, nk), 0)
    kj = lax.broadcasted_iota(I32, (nq, nk), 1)
    tq = jnp.bitwise_and(r, t - 1)
    valid = jnp.logical_and(kj > tq, kj <= tq + WINDOW)
    dist = (WINDOW + tq - kj).astype(F32)
    grp = jnp.right_shift(lax.broadcasted_iota(I32, (nq, 1), 0), t.bit_length() - 1)
    pad = jnp.zeros((bs, nk - WINDOW - t, HEAD_DIM), F32)
    for n in range(N_KV_A):
        cs = slice(n * HEAD_DIM, (n + 1) * HEAD_DIM)
        kb = jnp.concatenate([ck_ref[:, :, cs], kn_ref[:, :, cs], pad], axis=1).astype(BF16)
        vb = jnp.concatenate([cv_ref[:, :, cs], vn_ref[:, :, cs], pad], axis=1).astype(BF16)
        qn = jnp.concatenate([q_ref[:, :, (n * GQA_GROUP + g) * HEAD_DIM:(n * GQA_GROUP + g + 1) * HEAD_DIM]
                              for g in range(GQA_GROUP)], axis=1).astype(BF16)
        slope = jnp.zeros((nq, 1), F32)
        sink = jnp.zeros((nq, 1), F32)
        for g in range(GQA_GROUP):
            h = n * GQA_GROUP + g
            slope = jnp.where(grp == g, 2.0 ** -(h + 1), slope)
            sink = jnp.where(grp == g, sinks_ref[h], sink)
        s = jnp.einsum('bqd,bkd->bqk', qn, kb, preferred_element_type=F32)
        s = s * (HEAD_DIM ** -0.5) - (slope * dist)[None]
        s = jnp.where(valid[None], s, NEG_BIG)
        m = jnp.maximum(jnp.max(s, axis=2, keepdims=True), sink[None])
        p = jnp.exp(s - m)
        l = jnp.sum(p, axis=2, keepdims=True) + jnp.exp(sink[None] - m)
        o = jnp.einsum('bqk,bkd->bqd', p.astype(BF16), vb, preferred_element_type=F32) / l
        for g in range(GQA_GROUP):
            h = n * GQA_GROUP + g
            o_ref[:, :, h * HEAD_DIM:(h + 1) * HEAD_DIM] = o[:, g * t:(g + 1) * t, :]


def _attn_sample(q, kn, vn, ck, cv, sinks, bs=16):
    db, t = q.shape[0], q.shape[1]
    seq3 = lambda w: pl.BlockSpec((bs, t, w), lambda i: (i, 0, 0))
    cache = pl.BlockSpec((bs, WINDOW, KV_COLS), lambda i: (i, 0, 0))
    return pl.pallas_call(
        _attn_sample_kernel,
        grid=(db // bs,),
        in_specs=[pl.BlockSpec(memory_space=pltpu.SMEM), seq3(Q_COLS), seq3(KV_COLS), seq3(KV_COLS), cache, cache],
        out_specs=seq3(Q_COLS),
        out_shape=jax.ShapeDtypeStruct((db, t, Q_COLS), F32),
        compiler_params=_cparams("parallel"),
        name="attn_sample",
    )(sinks, q, kn, vn, ck, cv)


def _rwkv_prep_kernel(pb_ref, halo_ref, init_ref, mu_ref, w0_ref, wup_ref, a0_ref, aup_ref, gup_ref,
                      kk_ref, ka_ref, rk_ref, ones_ref,
                      r_o, w_o, k_o, v_o, kk_o, nkka_o, bonus_o, g_o, *, b):
    i = pl.program_id(0)
    pb = pb_ref[...]
    tm = pb.shape[0]
    halo = jnp.where(i == 0, init_ref[...], halo_ref[...])
    prev = halo if tm == b else jnp.concatenate([halo, pb[:tm - b]], axis=0)
    xs = pb + (prev - pb) * mu_ref[...]
    r = xs[:, :D_B]
    k = xs[:, D_B:2 * D_B]
    v = xs[:, 2 * D_B:3 * D_B]
    o1 = 3 * D_B
    wd = xs[:, o1:o1 + D_LORA_W]
    ad = xs[:, o1 + D_LORA_W:o1 + D_LORA_W + D_LORA_A]
    gd = xs[:, o1 + D_LORA_W + D_LORA_A:]
    z = -(w0_ref[...] + _dot(jnp.tanh(wd).astype(BF16), wup_ref[...]))
    softplus = jnp.maximum(z, 0.0) + jnp.log(1.0 + jnp.exp(-jnp.abs(z)))
    decay = jnp.exp(-jnp.exp(-softplus - 0.5))
    a = _sigmoid(a0_ref[...] + _dot(ad.astype(BF16), aup_ref[...]))
    g_o[...] = _dot(_sigmoid(gd).astype(BF16), gup_ref[...])
    ones = ones_ref[...]
    kk = k * kk_ref[...]
    kk = kk * lax.rsqrt(jnp.maximum(_dot2(kk * kk, ones), 1e-24))
    k2 = k * (1.0 + (a - 1.0) * ka_ref[...])
    bonus_o[...] = _dot2(r * k2 * rk_ref[...], ones) * v

    half = HEAD_DIM // 2
    lane8 = lax.broadcasted_iota(I32, (SEQ_PER_GROUP, 128), 1)
    low8 = lane8 < HEAD_DIM
    first_copy = jnp.bitwise_and(lax.broadcasted_iota(I32, (half, 128), 1), SEQ_PER_GROUP) == 0
    pairs = ((r, decay, r_o, w_o), (k2, kk, k_o, kk_o), (-(kk * a), v, nkka_o, None))
    for u in range(tm // SEQ_PER_GROUP):
        rows = slice(u * SEQ_PER_GROUP, (u + 1) * SEQ_PER_GROUP)
        for x, y, x_o, y_o in pairs:
            xu, yu = x[rows], y[rows]
            pieces = []
            for h in range(N_HEADS_B):
                cs = slice((h // 2) * 128, (h // 2 + 1) * 128)
                if h % 2 == 0:
                    p = jnp.where(low8, xu[:, cs], pltpu.roll(yu[:, cs], HEAD_DIM, 1))
                else:
                    p = jnp.where(low8, pltpu.roll(xu[:, cs], HEAD_DIM, 1), yu[:, cs])
                pieces += [p, p]
            tr = jnp.concatenate(pieces, axis=0).T
            x_o[u] = tr[:HEAD_DIM]
            if y_o is not None:
                y_o[u] = tr[HEAD_DIM:]
            else:
                v_o[u] = jnp.where(first_copy, tr[HEAD_DIM:HEAD_DIM + half], tr[HEAD_DIM + half:])


def _rwkv_prep(pb, init, p, b, tm):
    n = pb.shape[0]
    units = n // SEQ_PER_GROUP
    tu = tm // SEQ_PER_GROUP
    half = HEAD_DIM // 2
    row = lambda w: pl.BlockSpec((1, w), lambda i: (0, 0))
    full = lambda r, c: pl.BlockSpec((r, c), lambda i: (0, 0))
    tile = pl.BlockSpec((tm, D_B), lambda i: (i, 0))
    kspec = pl.BlockSpec((tu, HEAD_DIM, 128), lambda i: (i, 0, 0))
    vspec = pl.BlockSpec((tu, half, 128), lambda i: (i, 0, 0))
    kshape = jax.ShapeDtypeStruct((units, HEAD_DIM, 128), F32)
    halo_blocks = tm // b
    return pl.pallas_call(
        functools.partial(_rwkv_prep_kernel, b=b),
        grid=(n // tm,),
        in_specs=[pl.BlockSpec((tm, D_B_IN), lambda i: (i, 0)),
                  pl.BlockSpec((b, D_B_IN), lambda i: (jnp.maximum(i * halo_blocks - 1, 0), 0)),
                  full(b, D_B_IN), row(D_B_IN), row(D_B), full(D_LORA_W, D_B), row(D_B), full(D_LORA_A, D_B),
                  full(D_LORA_G, D_B), row(D_B), row(D_B), row(D_B), full(D_B, D_B)],
        out_specs=[kspec, kspec, kspec, vspec, kspec, kspec, tile, tile],
        out_shape=[kshape, kshape, kshape, jax.ShapeDtypeStruct((units, half, 128), F32), kshape, kshape,
                   jax.ShapeDtypeStruct((n, D_B), F32), jax.ShapeDtypeStruct((n, D_B), F32)],
        compiler_params=_cparams("arbitrary"),
        name="rwkv_prep",
    )(pb, pb, init, p['mu'], p['w0'], p['w_up'], p['a0'], p['a_up'], p['g_up'], p['k_k'], p['k_a'], p['r_k'],
      p['head_ones'])


def _wkv_scan_kernel(r_ref, w_ref, k_ref, kk_ref, nkka_ref, v_ref, s0_ref, o_ref, st_ref, s_scr):
    j = pl.program_id(1)

    @pl.when(j == 0)
    def _():
        s_scr[...] = s0_ref[0]

    tc = r_ref.shape[0]
    nsub = (HEAD_DIM // 2) // 8

    def bcast(ref, s, kx):
        return jnp.broadcast_to(ref[s, pl.ds(kx, 1), :], (8, 128))

    acc0 = [[jnp.zeros((8, 128), F32) for _ in range(2)] for _ in range(nsub)]
    for kx in range(HEAD_DIM):
        kkr = bcast(kk_ref, 0, kx)
        for i in range(nsub):
            acc0[i][kx % 2] = acc0[i][kx % 2] + s_scr[kx, 8 * i:8 * i + 8, :] * kkr

    def step(s, sa):
        nxt = jnp.minimum(s + 1, tc - 1)
        vv = [v_ref[s, 8 * i:8 * i + 8, :] for i in range(nsub)]
        oacc = [[jnp.zeros((8, 128), F32) for _ in range(2)] for _ in range(nsub)]
        nacc = [[jnp.zeros((8, 128), F32) for _ in range(2)] for _ in range(nsub)]
        for kx in range(HEAD_DIM):
            rr, wr, kr = bcast(r_ref, s, kx), bcast(w_ref, s, kx), bcast(k_ref, s, kx)
            nk, kkn = bcast(nkka_ref, s, kx), bcast(kk_ref, nxt, kx)
            for i in range(nsub):
                rows = slice(8 * i, 8 * i + 8)
                sk = s_scr[kx, rows, :] * wr + sa[i] * nk + vv[i] * kr
                s_scr[kx, rows, :] = sk
                oacc[i][kx % 2] = oacc[i][kx % 2] + sk * rr
                nacc[i][kx % 2] = nacc[i][kx % 2] + sk * kkn
        o_ref[s] = jnp.concatenate([a[0] + a[1] for a in oacc], axis=0)
        return [a[0] + a[1] for a in nacc]

    lax.fori_loop(0, tc, step, [a[0] + a[1] for a in acc0])

    @pl.when(j == pl.num_programs(1) - 1)
    def _():
        st_ref[0] = s_scr[...]


def _wkv_scan(r, w, k, kk, nkka, v, s0, b, t, tc):
    g = b // SEQ_PER_GROUP
    half = HEAD_DIM // 2
    kview = lambda z: z.reshape(t, g, HEAD_DIM, 128)
    kspec = pl.BlockSpec((tc, None, HEAD_DIM, 128), lambda gi, j: (j, gi, 0, 0))
    vspec = pl.BlockSpec((tc, None, half, 128), lambda gi, j: (j, gi, 0, 0))
    sspec = pl.BlockSpec((1, HEAD_DIM, half, 128), lambda gi, j: (gi, 0, 0, 0))
    o, st = pl.pallas_call(
        _wkv_scan_kernel,
        grid=(g, t // tc),
        in_specs=[kspec] * 5 + [vspec, sspec],
        out_specs=[vspec, sspec],
        out_shape=[jax.ShapeDtypeStruct((t, g, half, 128), F32),
                   jax.ShapeDtypeStruct((g, HEAD_DIM, half, 128), F32)],
        scratch_shapes=[pltpu.VMEM((HEAD_DIM, half, 128), F32)],
        compiler_params=_cparams("parallel", "arbitrary"),
        name="wkv_scan",
    )(kview(r), kview(w), kview(k), kview(kk), kview(nkka), v.reshape(t, g, half, 128), s0)
    return o.reshape(t * g, half, 128), st


def _state_to_scan(s, b):
    g = b // SEQ_PER_GROUP
    s = s.reshape(g, SEQ_PER_GROUP, N_HEADS_B, 2, HEAD_DIM // 2, HEAD_DIM).transpose(0, 5, 4, 2, 3, 1)
    return s.reshape(g, HEAD_DIM, HEAD_DIM // 2, 128)


def _state_from_scan(s, b):
    g = b // SEQ_PER_GROUP
    s = s.reshape(g, HEAD_DIM, HEAD_DIM // 2, N_HEADS_B, 2, SEQ_PER_GROUP).transpose(0, 5, 3, 4, 2, 1)
    return s.reshape(b, N_HEADS_B, HEAD_DIM, HEAD_DIM)


def _route_tile(x, nrm_ref, wr_ref, br_ref, cnt_scr, xn_o, idx_o, wts_o):
    tm = x.shape[0]
    xn = _rms(x, nrm_ref[...])
    hi, lo = _split_bf16(xn)
    _store_row_tiles(xn_o, xn)
    wr = wr_ref[...]
    pa = _dot(hi, wr)
    pb = _dot(lo, wr)
    lg = pa[:, 0:32] + pa[:, 32:64] + pb[:, 0:32] + pb[:, 32:64] + br_ref[...]
    col = lambda c: lg[:, c:c + 1]
    c = [col(gx) for gx in range(N_EGROUPS)]
    m = jnp.maximum(jnp.maximum(c[0], c[1]), jnp.maximum(c[2], c[3]))
    den = jnp.exp(c[0] - m) + jnp.exp(c[1] - m) + jnp.exp(c[2] - m) + jnp.exp(c[3] - m)
    pg = 1.0 / den
    gi = jnp.where(c[0] >= m, 0, jnp.where(c[1] >= m, 1, jnp.where(c[2] >= m, 2, 3))).astype(I32)
    sel = []
    for e in range(EXP_PER_GROUP):
        sel.append(jnp.where(gi == 0, col(4 + e), jnp.where(gi == 1, col(8 + e),
                                                            jnp.where(gi == 2, col(12 + e), col(16 + e)))))
    v1 = jnp.maximum(jnp.maximum(sel[0], sel[1]), jnp.maximum(sel[2], sel[3]))
    i1 = jnp.where(sel[0] >= v1, 0, jnp.where(sel[1] >= v1, 1, jnp.where(sel[2] >= v1, 2, 3))).astype(I32)
    rest = [jnp.where(i1 == e, -jnp.inf, sel[e]) for e in range(EXP_PER_GROUP)]
    v2 = jnp.maximum(jnp.maximum(rest[0], rest[1]), jnp.maximum(rest[2], rest[3]))
    i2 = jnp.where(rest[0] >= v2, 0, jnp.where(rest[1] >= v2, 1, jnp.where(rest[2] >= v2, 2, 3))).astype(I32)
    tt = jnp.exp(v2 - v1)
    w1 = pg / (1.0 + tt)
    w2 = pg * tt / (1.0 + tt)
    e1 = gi * EXP_PER_GROUP + i1
    e2 = gi * EXP_PER_GROUP + i2
    lane = lax.broadcasted_iota(I32, (tm, N_EXPERTS), 1)
    oh1 = lane == e1
    oh2 = lane == e2
    oh = jnp.where(jnp.logical_or(oh1, oh2), 1.0, 0.0)
    ri = lax.broadcasted_iota(I32, (tm, tm), 0)
    ci = lax.broadcasted_iota(I32, (tm, tm), 1)
    ltri = jnp.where(ri > ci, 1.0, 0.0).astype(BF16)
    cnt = cnt_scr[0:1, 0:N_EXPERTS]
    pre = _dot(ltri, oh.astype(BF16)) + cnt
    rank1 = jnp.sum(jnp.where(oh1, pre, 0.0), axis=1, keepdims=True)
    rank2 = jnp.sum(jnp.where(oh2, pre, 0.0), axis=1, keepdims=True)
    cnt_scr[0:1, 0:N_EXPERTS] = cnt + jnp.sum(oh, axis=0, keepdims=True)
    lw = lax.broadcasted_iota(I32, (tm, 128), 1)
    wts_o[...] = jnp.where(lw == 0, w1, jnp.where(lw == 1, w2, 0.0))
    cols = jnp.where(lw == 0, e1.astype(F32), jnp.where(lw == 1, e2.astype(F32),
                                                         jnp.where(lw == 2, rank1, jnp.where(lw == 3, rank2, 0.0))))
    idx_o[...] = cols.T[0:8, :].astype(I32)


def _route_out_specs(tm):
    return [pl.BlockSpec((tm * ROW_CHUNKS, 128), lambda i: (i, 0)),
            pl.BlockSpec((8, tm), lambda i: (0, i)),
            pl.BlockSpec((tm, 128), lambda i: (i, 0)),
            pl.BlockSpec((8, 128), lambda i: (0, 0))]


def _route_out_shapes(n):
    return [jax.ShapeDtypeStruct((n * ROW_CHUNKS, 128), F32),
            jax.ShapeDtypeStruct((8, n), I32),
            jax.ShapeDtypeStruct((n, 128), F32),
            jax.ShapeDtypeStruct((8, 128), F32)]


def _route_in_specs():
    return [pl.BlockSpec((1, D_MODEL), lambda i: (0, 0)),
            pl.BlockSpec((D_MODEL, 128), lambda i: (0, 0)),
            pl.BlockSpec((1, 32), lambda i: (0, 0))]


def _mix0_out_kernel(o_ref, bonus_ref, g_ref, attn_ref, x_ref, lnw_ref, lnb_ref, avg_ref, wo_ref,
                     nrm_ref, wr_ref, br_ref,
                     x1_o, xn_o, idx_o, wts_o, cnt_o, cnt_scr, xs_scr, at_scr, *, b):
    @pl.when(pl.program_id(0) == 0)
    def _():
        cnt_scr[...] = jnp.zeros_like(cnt_scr)

    avg = avg_ref[...]
    half = HEAD_DIM // 2
    lane8 = lax.broadcasted_iota(I32, (SEQ_PER_GROUP, 128), 1)
    unit_rows = []
    for u in range(o_ref.shape[0]):
        ot = jnp.concatenate([o_ref[u], jnp.zeros((128 - half, 128), F32)], axis=0).T
        cols = []
        for jj in range(N_HEADS_B // 2):
            q = [ot[(4 * jj + i) * SEQ_PER_GROUP:(4 * jj + i + 1) * SEQ_PER_GROUP] for i in range(4)]
            c = jnp.where(lane8 < half, q[0], pltpu.roll(q[1], half, 1))
            c = jnp.where(lane8 < 2 * half, c, pltpu.roll(q[2], 2 * half, 1))
            c = jnp.where(lane8 < 3 * half, c, pltpu.roll(q[3], 3 * half, 1))
            cols.append(c)
        unit_rows.append(jnp.concatenate(cols, axis=1))
    o = jnp.concatenate(unit_rows, axis=0)
    d = o - _dot2(o, avg)
    var = _dot2(d * d, avg)
    on = d * lax.rsqrt(var + RWKV_GN_EPS) * lnw_ref[...] + lnb_ref[...]
    rout = ((on + bonus_ref[...]) * g_ref[...]).astype(BF16)
    x = _load_time_major(x_ref, xs_scr, b)
    if len(x_ref.shape) == 3:
        attn = _interleave_rows([attn_ref[:, s * Q_COLS:(s + 1) * Q_COLS] for s in range(b)], at_scr)
    else:
        attn = attn_ref[...]
    y = _dot(attn.astype(BF16), wo_ref[:Q_COLS, :]) + _dot(rout, wo_ref[Q_COLS:, :]) + x
    x1_o[...] = y
    _route_tile(y, nrm_ref, wr_ref, br_ref, cnt_scr, xn_o, idx_o, wts_o)
    cnt_o[...] = cnt_scr[...]


def _mix0_out(o, bonus, g, attn, x, p, rp, b, t, tm):
    n = b * t
    xshape, xspec = _time_major_spec(b, t, D_MODEL, tm)
    aspec = (pl.BlockSpec((tm // b, b * Q_COLS), lambda i: (i, 0)) if len(xshape) == 3
             else pl.BlockSpec((tm, Q_COLS), lambda i: (i, 0)))
    tile = lambda w: pl.BlockSpec((tm, w), lambda i: (i, 0))
    row = lambda w: pl.BlockSpec((1, w), lambda i: (0, 0))
    full = lambda r, c: pl.BlockSpec((r, c), lambda i: (0, 0))
    return pl.pallas_call(
        functools.partial(_mix0_out_kernel, b=b),
        grid=(n // tm,),
        in_specs=[pl.BlockSpec((tm // SEQ_PER_GROUP, HEAD_DIM // 2, 128), lambda i: (i, 0, 0)),
                  tile(D_B), tile(D_B), aspec, xspec, row(D_B), row(D_B),
                  full(D_B, D_B), full(D_MODEL, D_MODEL)] + _route_in_specs(),
        out_specs=[tile(D_MODEL)] + _route_out_specs(tm),
        out_shape=[jax.ShapeDtypeStruct((n, D_MODEL), F32)] + _route_out_shapes(n),
        scratch_shapes=[pltpu.VMEM((8, 128), F32), pltpu.VMEM((D_MODEL // 128, tm, 128), F32),
                        pltpu.VMEM((Q_COLS // 128, tm, 128), F32)],
        compiler_params=_cparams("arbitrary"),
        name="mix0_out",
    )(o, bonus, g, attn, x.reshape(xshape), p['ln_w'], p['ln_b'], p['head_avg'], p['w_out'], rp['norm'], rp['w'], rp['b'])


def _row_dma_start(idx_ref, pos, r, src_hbm, dst, sem, priority):
    src_row = pl.multiple_of(idx_ref[pos] * ROW_CHUNKS, ROW_CHUNKS)
    dst_row = pl.multiple_of(r * ROW_CHUNKS, ROW_CHUNKS)
    pltpu.make_async_copy(src_hbm.at[pl.ds(src_row, ROW_CHUNKS)], dst.at[pl.ds(dst_row, ROW_CHUNKS)],
                          sem).start(priority=priority)


def _row_gather_start(idx_ref, base, n_rows, src_hbm, dst, sem):
    def body(r2, carry):
        for p in range(2):
            _row_dma_start(idx_ref, base + 2 * r2 + p, 2 * r2 + p, src_hbm, dst, sem, p)
        return carry

    lax.fori_loop(0, n_rows // 2, body, 0, unroll=4)


def _row_gather_wait(dst, sem):
    pltpu.make_async_copy(dst, dst, sem).wait()


def _expert_kernel(te_ref, nu_ref, pad_ref, p1_ref, p2_ref, x_hbm, wg_ref, wu_ref, wd_ref, o_ref,
                   src_ref, xbuf, sem, wgb, wub, wdb):
    i = pl.program_id(0)
    nu = nu_ref[0]

    @pl.when(i == 0)
    def _():
        def fill_pad(e, carry):
            start = pad_ref[2 * e]

            def body(r, c):
                src_ref[r] = r - start
                return c

            lax.fori_loop(start, pad_ref[2 * e + 1], body, 0)
            return carry

        lax.fori_loop(0, N_EXPERTS + 1, fill_pad, 0)

        def invert(n, carry):
            src_ref[p1_ref[n]] = n
            src_ref[p2_ref[n]] = n
            return carry

        lax.fori_loop(0, p1_ref.shape[0], invert, 0, unroll=8)
        _row_gather_start(src_ref, 0, MOE_TILE, x_hbm, xbuf.at[0], sem.at[0])

    new_expert = jnp.logical_or(i == 0, te_ref[i] != te_ref[jnp.maximum(i - 1, 0)])

    @pl.when(jnp.logical_and(i < nu, new_expert))
    def _():
        wgb[...] = wg_ref[0].astype(BF16)
        wub[...] = wu_ref[0].astype(BF16)
        wdb[...] = wd_ref[0].astype(BF16)

    @pl.when(i < nu)
    def _():
        cur = i % 2
        nxt = 1 - cur
        base = (i + 1) * MOE_TILE
        nb = 2 * D_FF_E // 256 + D_MODEL // 256
        per = MOE_TILE // nb
        batch = [0]

        def start_batch():
            for r in range(batch[0] * per, (batch[0] + 1) * per):
                _row_dma_start(src_ref, base + r, r, x_hbm, xbuf.at[nxt], sem.at[nxt], r % 2)
            batch[0] += 1

        _row_gather_wait(xbuf.at[cur], sem.at[cur])
        x = _load_row_tiles(xbuf.at[cur]).astype(BF16)
        hg, hu = [], []
        for c in range(D_FF_E // 256):
            hg.append(_dot(x, wgb[:, c * 256:(c + 1) * 256]))
            start_batch()
        for c in range(D_FF_E // 256):
            hu.append(_dot(x, wub[:, c * 256:(c + 1) * 256]))
            start_batch()
        hg = jnp.concatenate(hg, axis=1)
        h = ((hg * _sigmoid(hg)) * jnp.concatenate(hu, axis=1)).astype(BF16)
        for c in range(D_MODEL // 256):
            res = _dot(h, wdb[:, c * 256:(c + 1) * 256])
            for cc in range(2):
                o_ref[pl.ds(2 * c + cc, MOE_TILE, stride=ROW_CHUNKS), :] = res[:, cc * 128:(cc + 1) * 128]
            start_batch()

    @pl.when(i == nu)
    def _():
        _row_gather_wait(xbuf.at[i % 2], sem.at[i % 2])

    @pl.when(i >= nu)
    def _():
        o_ref[...] = jnp.zeros_like(o_ref)


def _experts(xn, r, pos1, pos2, tile_expert, n_used, pad_ranges, wg, wu, wd):
    wspec = lambda a, b: pl.BlockSpec((1, a, b), lambda i, te, nu, pad, p1, p2: (te[i], 0, 0))
    grid_spec = pltpu.PrefetchScalarGridSpec(
        num_scalar_prefetch=5,
        grid=(r // MOE_TILE,),
        in_specs=[pl.BlockSpec(memory_space=pl.ANY),
                  wspec(D_MODEL, D_FF_E), wspec(D_MODEL, D_FF_E), wspec(D_FF_E, D_MODEL)],
        out_specs=pl.BlockSpec((MOE_TILE * ROW_CHUNKS, 128), lambda i, te, nu, pad, p1, p2: (i, 0)),
        scratch_shapes=[pltpu.SMEM((r,), I32), pltpu.VMEM((2, MOE_TILE * ROW_CHUNKS, 128), F32),
                        pltpu.SemaphoreType.DMA((2,)),
                        pltpu.VMEM((D_MODEL, D_FF_E), BF16), pltpu.VMEM((D_MODEL, D_FF_E), BF16),
                        pltpu.VMEM((D_FF_E, D_MODEL), BF16)],
    )
    return pl.pallas_call(
        _expert_kernel,
        grid_spec=grid_spec,
        out_shape=jax.ShapeDtypeStruct((r * ROW_CHUNKS, 128), F32),
        compiler_params=_cparams("arbitrary"),
        name="moe_experts",
    )(tile_expert, n_used, pad_ranges, pos1, pos2, xn, wg, wu, wd)


def _moe(xn, idx, cnt, ep):
    n = xn.shape[0] // ROW_CHUNKS
    rows = 2 * n + N_EXPERTS * MOE_TILE
    counts = cnt[0, :N_EXPERTS].astype(I32)
    padded = ((counts + MOE_TILE - 1) // MOE_TILE) * MOE_TILE
    ends = jnp.cumsum(padded)
    offs = ends - padded
    pos1 = offs[idx[0]] + idx[2]
    pos2 = offs[idx[1]] + idx[3]
    pad_ranges = jnp.stack([jnp.append(offs + counts, ends[-1]), jnp.append(ends, ends[-1] + MOE_TILE)], axis=1)
    pad_ranges = pad_ranges.reshape(-1).astype(I32)
    n_used = (ends[-1] // MOE_TILE).astype(I32)
    starts = jnp.arange(rows // MOE_TILE, dtype=I32) * MOE_TILE
    starts = jnp.minimum(starts, ends[-1] - 1)
    tile_expert = jnp.sum((starts[:, None] >= ends[None, :]).astype(I32), axis=1)
    tile_expert = jnp.minimum(tile_expert, N_EXPERTS - 1).astype(I32)
    out = _experts(xn, rows, pos1, pos2, tile_expert, n_used.reshape(1), pad_ranges, ep['wg'], ep['wu'], ep['wd'])
    return out, pos1, pos2


def _combine_kernel(p1_ref, p2_ref, x_ref, wts_ref, nrm_ref, out_hbm, y_o, gbuf, sem, ys_scr, *, final, b):
    i = pl.program_id(0)
    tm = x_ref.shape[0]

    def start(tile, slot):
        _row_gather_start(p1_ref, tile * tm, tm, out_hbm, gbuf.at[slot, 0], sem.at[slot])
        _row_gather_start(p2_ref, tile * tm, tm, out_hbm, gbuf.at[slot, 1], sem.at[slot])

    @pl.when(i == 0)
    def _():
        start(0, 0)

    @pl.when(i + 1 < pl.num_programs(0))
    def _():
        start(i + 1, (i + 1) % 2)

    cur = i % 2
    _row_gather_wait(gbuf.at[cur], sem.at[cur])
    wts = wts_ref[...]
    y = x_ref[...] + wts[:, 0:1] * _load_row_tiles(gbuf.at[cur, 0]) + wts[:, 1:2] * _load_row_tiles(gbuf.at[cur, 1])
    if final:
        _store_batch_major(y_o, ys_scr, _rms(y, nrm_ref[...]), b)
    else:
        y_o[...] = y


def _moe_combine(x, wts, out, pos1, pos2, nrm, final, b, t, tm=256):
    n = x.shape[0]
    tile = lambda w: pl.BlockSpec((tm, w), lambda i, p1, p2: (i, 0))
    if final:
        yshape, yspec = _time_major_spec(b, t, D_MODEL, tm)
    else:
        yshape, yspec = (n, D_MODEL), tile(D_MODEL)
    grid_spec = pltpu.PrefetchScalarGridSpec(
        num_scalar_prefetch=2,
        grid=(n // tm,),
        in_specs=[tile(D_MODEL), tile(128), pl.BlockSpec((1, D_MODEL), lambda i, p1, p2: (0, 0)),
                  pl.BlockSpec(memory_space=pl.ANY)],
        out_specs=yspec,
        scratch_shapes=[pltpu.VMEM((2, 2, tm * ROW_CHUNKS, 128), F32), pltpu.SemaphoreType.DMA((2,)),
                        pltpu.VMEM((D_MODEL // 128, tm, 128), F32)],
    )
    return pl.pallas_call(
        functools.partial(_combine_kernel, final=final, b=b),
        grid_spec=grid_spec,
        out_shape=jax.ShapeDtypeStruct(yshape, F32),
        compiler_params=_cparams("arbitrary"),
        name="moe_combine_final" if final else "moe_combine",
    )(pos1, pos2, x, wts, nrm, out)


def _gelu_tanh(x):
    return 0.5 * x * (1.0 + jnp.tanh(0.7978845608028654 * (x + 0.044715 * (x * x * x))))


def _mix1_kernel(x_ref, nmix_ref, bre_ref, bim_ref, are_ref, aim_ref, cre_ref, cim_ref,
                 dsk_ref, wo_ref, h0r_ref, h0i_ref, nrm_ref, wr_ref, br_ref,
                 x2_o, xn_o, idx_o, wts_o, cnt_o, hr_o, hi_o,
                 bur, bui, hr_scr, hi_scr, cnt_scr, *, b, cw):
    @pl.when(pl.program_id(0) == 0)
    def _():
        cnt_scr[...] = jnp.zeros_like(cnt_scr)
        hr_scr[...] = h0r_ref[...]
        hi_scr[...] = h0i_ref[...]

    x = x_ref[...]
    u = _rms(x, nmix_ref[...])
    ub = u.astype(BF16)
    nblk = bre_ref.shape[0]
    kin = D_MODEL // nblk
    kst = S5_STATE // nblk
    for cb in range(nblk):
        ucb = ub[:, cb * kin:(cb + 1) * kin]
        bur[:, cb * kst:(cb + 1) * kst] = _dot(ucb, bre_ref[cb])
        bui[:, cb * kst:(cb + 1) * kst] = _dot(ucb, bim_ref[cb])

    tc = x.shape[0] // b
    for c0 in range(0, S5_STATE, cw):
        cs = slice(c0, c0 + cw)
        ar = jnp.broadcast_to(are_ref[:, cs], (b, cw))
        ai = jnp.broadcast_to(aim_ref[:, cs], (b, cw))

        def step(s, carry, cs=cs, ar=ar, ai=ai):
            hr, hi = carry
            rows = pl.ds(pl.multiple_of(s * b, b), b)
            nr = ar * hr - ai * hi + bur[rows, cs]
            ni = ar * hi + ai * hr + bui[rows, cs]
            bur[rows, cs] = nr
            bui[rows, cs] = ni
            return nr, ni

        hr, hi = lax.fori_loop(0, tc, step, (hr_scr[:, cs], hi_scr[:, cs]), unroll=True)
        hr_scr[:, cs] = hr
        hi_scr[:, cs] = hi

    ych = []
    for cb in range(nblk):
        ss = slice(cb * kst, (cb + 1) * kst)
        ych.append(_dot(bur[:, ss].astype(BF16), cre_ref[cb]) - _dot(bui[:, ss].astype(BF16), cim_ref[cb]))
    y = jnp.concatenate(ych, axis=1) + dsk_ref[...] * u
    z = _dot(_gelu_tanh(y).astype(BF16), wo_ref[...])
    x2 = x + z[:, :D_MODEL] * _sigmoid(z[:, D_MODEL:])
    x2_o[...] = x2
    _route_tile(x2, nrm_ref, wr_ref, br_ref, cnt_scr, xn_o, idx_o, wts_o)
    cnt_o[...] = cnt_scr[...]
    hr_o[...] = hr_scr[...]
    hi_o[...] = hi_scr[...]


def _mix1(x, sp, rp, h0r, h0i, b, tr):
    n = x.shape[0]
    cw = 1024 if b == 8 else 128
    tile = lambda w: pl.BlockSpec((tr, w), lambda i: (i, 0))
    row = lambda w: pl.BlockSpec((1, w), lambda i: (0, 0))
    full = lambda *s: pl.BlockSpec(s, lambda i: (0,) * len(s))
    nblk = sp['b_re'].shape[0]
    return pl.pallas_call(
        functools.partial(_mix1_kernel, b=b, cw=cw),
        grid=(n // tr,),
        in_specs=[tile(D_MODEL), row(D_MODEL),
                  full(nblk, D_MODEL // nblk, S5_STATE // nblk), full(nblk, D_MODEL // nblk, S5_STATE // nblk),
                  row(S5_STATE), row(S5_STATE),
                  full(nblk, S5_STATE // nblk, D_MODEL // nblk), full(nblk, S5_STATE // nblk, D_MODEL // nblk),
                  row(D_MODEL), full(D_MODEL, 2 * D_MODEL), full(b, S5_STATE), full(b, S5_STATE)] + _route_in_specs(),
        out_specs=[tile(D_MODEL)] + _route_out_specs(tr) + [full(b, S5_STATE), full(b, S5_STATE)],
        out_shape=[jax.ShapeDtypeStruct((n, D_MODEL), F32)] + _route_out_shapes(n)
                  + [jax.ShapeDtypeStruct((b, S5_STATE), F32)] * 2,
        scratch_shapes=[pltpu.VMEM((tr, S5_STATE), F32), pltpu.VMEM((tr, S5_STATE), F32),
                        pltpu.VMEM((b, S5_STATE), F32), pltpu.VMEM((b, S5_STATE), F32),
                        pltpu.VMEM((8, 128), F32)],
        compiler_params=_cparams("arbitrary"),
        name="mix1",
    )(x, sp['norm'], sp['b_re'], sp['b_im'], sp['a_re'], sp['a_im'], sp['c_re'], sp['c_im'],
      sp['d'], sp['w_out'], h0r, h0i, rp['norm'], rp['w'], rp['b'])


def _router_params(norm, w_rc, b_rc, w_rf, b_rf):
    w = jnp.concatenate([w_rc, w_rf.reshape(D_MODEL, N_EXPERTS), jnp.zeros((D_MODEL, 12), F32)], axis=1)
    hi = w.astype(BF16)
    lo = (w - hi.astype(F32)).astype(BF16)
    wcat = jnp.concatenate([hi, lo, jnp.zeros((D_MODEL, 64), BF16)], axis=1)
    bias = jnp.concatenate([b_rc, b_rf.reshape(-1), jnp.zeros((12,), F32)]).reshape(1, 32)
    return {'norm': norm.reshape(1, D_MODEL), 'w': wcat, 'b': bias}


def _expert_params(wg, wu, wd):
    return {'wg': wg, 'wu': wu, 'wd': wd}


def _s5_params(norm, a_re, a_im, log_dt, b_re, b_im, c_re, c_im, d_skip, w_out, nblk=8):
    dt = jnp.exp(log_dt)
    mag = jnp.exp(dt * a_re)
    ab_re, ab_im = mag * jnp.cos(dt * a_im), mag * jnp.sin(dt * a_im)
    den = a_re * a_re + a_im * a_im
    f_re = ((ab_re - 1.0) * a_re + ab_im * a_im) / den
    f_im = (ab_im * a_re - (ab_re - 1.0) * a_im) / den
    bb_re = f_re[..., None] * b_re - f_im[..., None] * b_im
    bb_im = f_re[..., None] * b_im + f_im[..., None] * b_re
    gpb = S5_GROUPS // nblk
    eye = jnp.eye(gpb, dtype=F32)

    def in_blocks(bb):
        bb = bb.reshape(nblk, gpb, S5_P, S5_CH)
        w = jnp.einsum('ngpc,gh->ngchp', bb, eye)
        return w.reshape(nblk, gpb * S5_CH, gpb * S5_P).astype(BF16)

    def out_blocks(cc):
        cc = cc.reshape(nblk, gpb, S5_CH, S5_P)
        w = jnp.einsum('ngcp,gh->ngphc', cc, eye)
        return w.reshape(nblk, gpb * S5_P, gpb * S5_CH).astype(BF16)

    return {'norm': norm.reshape(1, D_MODEL), 'b_re': in_blocks(bb_re), 'b_im': in_blocks(bb_im),
            'a_re': ab_re.reshape(1, S5_STATE), 'a_im': ab_im.reshape(1, S5_STATE),
            'c_re': out_blocks(c_re), 'c_im': out_blocks(c_im), 'd': d_skip.reshape(1, D_MODEL),
            'w_out': w_out.astype(BF16)}


def _head_block(value):
    hid = jnp.arange(D_B, dtype=I32) // HEAD_DIM
    return jnp.where(hid[:, None] == hid[None, :], value, 0.0).astype(BF16)


def _run_group(x, cache_k, cache_v, shift0, wkv0, h0r, h0i, pr):
    b, t = x.shape[0], x.shape[1]
    n = b * t
    prompt = cache_k is None
    tm = 256 if prompt else 128
    if not prompt:
        x = x.transpose(1, 0, 2)
    q, kv, pb = _in_proj(x, pr['l0_norm'], pr['l0_w_in'], b, t, tm)

    if prompt:
        attn = _attn_prompt(q, kv, pr['sinks'], b, t)
        kv3 = kv[t - WINDOW:].reshape(WINDOW, b, 2, N_KV_A, HEAD_DIM)
        new_k = kv3[:, :, 0].transpose(1, 0, 2, 3)
        new_v = kv3[:, :, 1].transpose(1, 0, 2, 3)
        init = jnp.zeros((b, D_B_IN), F32)
    else:
        qs = q.reshape(t, b, Q_COLS).transpose(1, 0, 2)
        kvs = kv.reshape(t, b, 2 * KV_COLS).transpose(1, 0, 2)
        kn, vn = kvs[..., :KV_COLS], kvs[..., KV_COLS:]
        ck = cache_k.reshape(b, WINDOW, KV_COLS)
        cv = cache_v.reshape(b, WINDOW, KV_COLS)
        attn = _attn_sample(qs, kn, vn, ck, cv, pr['sinks'])
        attn = attn.transpose(1, 0, 2).reshape(n, Q_COLS)
        new_k = jnp.concatenate([ck[:, t:], kn], axis=1).reshape(b, WINDOW, N_KV_A, HEAD_DIM)
        new_v = jnp.concatenate([cv[:, t:], vn], axis=1).reshape(b, WINDOW, N_KV_A, HEAD_DIM)
        init = shift0
    new_shift = pb[n - b:]

    r, w, k, v, kk, nkka, bonus, g = _rwkv_prep(pb, init, pr['rw'], b, tm)
    tc = 64 if prompt else t
    s0 = jnp.zeros((b // SEQ_PER_GROUP, HEAD_DIM, HEAD_DIM // 2, 128), F32) if prompt else _state_to_scan(wkv0, b)
    o, s_fin = _wkv_scan(r, w, k, kk, nkka, v, s0, b, t, tc)
    new_wkv = _state_from_scan(s_fin, b)

    x1, xn, idx, wts, cnt = _mix0_out(o, bonus, g, attn, x, pr['rw'], pr['l0_route'], b, t, 256)
    out, pos1, pos2 = _moe(xn, idx, cnt, pr['l0_exp'])
    x1 = _moe_combine(x1, wts, out, pos1, pos2, pr['final_norm'], False, b, t)

    x2, xn, idx, wts, cnt, hr, hi = _mix1(x1, pr['s5'], pr['l1_route'], h0r, h0i, b, 256)
    out, pos1, pos2 = _moe(xn, idx, cnt, pr['l1_exp'])
    y = _moe_combine(x2, wts, out, pos1, pos2, pr['final_norm'], True, b, t)
    y = y if prompt else y.reshape(t, b, D_MODEL).transpose(1, 0, 2)
    return (y, new_k, new_v, new_shift, new_wkv,
            hr.reshape(b, S5_GROUPS, S5_P), hi.reshape(b, S5_GROUPS, S5_P))


def kernel(x_prompt, x_sample, cache_win_k, cache_win_v, state_shift, state_wkv, state_s5_re, state_s5_im,
           l0_norm_mix, l0_w_in, l0_sinks, l0_mu, l0_w0, l0_w_lora_up, l0_a0, l0_a_lora_up, l0_g_lora_up,
           l0_k_k, l0_k_a, l0_r_k, l0_ln_w, l0_ln_b, l0_w_out,
           l0_norm_ffn, l0_router_coarse, l0_bias_coarse, l0_router_fine, l0_bias_fine,
           l0_exp_gate, l0_exp_up, l0_exp_down,
           l1_norm_mix, l1_s5_a_re, l1_s5_a_im, l1_s5_log_dt, l1_s5_b_re, l1_s5_b_im, l1_s5_c_re, l1_s5_c_im,
           l1_s5_d, l1_w_out,
           l1_norm_ffn, l1_router_coarse, l1_bias_coarse, l1_router_fine, l1_bias_fine,
           l1_exp_gate, l1_exp_up, l1_exp_down,
           final_norm):
    row = lambda z: z.reshape(1, -1)
    pr = {
        'l0_norm': row(l0_norm_mix), 'l0_w_in': l0_w_in.astype(BF16), 'sinks': l0_sinks,
        'rw': {'mu': row(l0_mu), 'w0': row(l0_w0), 'w_up': l0_w_lora_up.astype(BF16), 'a0': row(l0_a0),
               'a_up': l0_a_lora_up.astype(BF16), 'g_up': l0_g_lora_up.astype(BF16), 'k_k': row(l0_k_k),
               'k_a': row(l0_k_a), 'r_k': row(l0_r_k), 'ln_w': row(l0_ln_w), 'ln_b': row(l0_ln_b),
               'head_ones': _head_block(1.0), 'head_avg': _head_block(1.0 / HEAD_DIM),
               'w_out': l0_w_out.astype(BF16)},
        'l0_route': _router_params(l0_norm_ffn, l0_router_coarse, l0_bias_coarse, l0_router_fine, l0_bias_fine),
        'l0_exp': _expert_params(l0_exp_gate, l0_exp_up, l0_exp_down),
        's5': _s5_params(l1_norm_mix, l1_s5_a_re, l1_s5_a_im, l1_s5_log_dt, l1_s5_b_re, l1_s5_b_im,
                         l1_s5_c_re, l1_s5_c_im, l1_s5_d, l1_w_out),
        'l1_route': _router_params(l1_norm_ffn, l1_router_coarse, l1_bias_coarse, l1_router_fine, l1_bias_fine),
        'l1_exp': _expert_params(l1_exp_gate, l1_exp_up, l1_exp_down),
        'final_norm': row(final_norm),
    }
    bp, bs = x_prompt.shape[0], x_sample.shape[0]
    zero_state = jnp.zeros((bp, S5_STATE), F32)
    yp, pk, pv, psh, pwkv, pre, pim = _run_group(x_prompt, None, None, None, None, zero_state, zero_state, pr)
    ys, sk, sv, ssh, swkv, sre, sim = _run_group(
        x_sample, cache_win_k, cache_win_v, state_shift, state_wkv,
        state_s5_re.reshape(bs, S5_STATE), state_s5_im.reshape(bs, S5_STATE), pr)
    return (yp, ys, pk, pv, psh, pwkv, pre, pim, sk, sv, ssh, swkv, sre, sim)
```

```python
import functools

import jax
import jax.numpy as jnp
from jax import lax
from jax.experimental import pallas as pl
from jax.experimental.pallas import tpu as pltpu

F32 = jnp.float32
BF16 = jnp.bfloat16
I32 = jnp.int32

D_MODEL = 1024
HEAD_DIM = 64
N_HEADS_A = 8
N_KV_A = 2
GQA_GROUP = 4
WINDOW = 128
Q_COLS = 512
KV_COLS = 128
D_A_IN = 768
N_HEADS_B = 8
D_B = 512
D_LORA_W = 64
D_LORA_A = 64
D_LORA_G = 128
D_B_IN = 1792
D_IN0 = 2560
RWKV_GN_EPS = 64e-5
S5_CH = 16
S5_GROUPS = 64
S5_P = 64
S5_STATE = S5_GROUPS * S5_P
N_EGROUPS = 4
EXP_PER_GROUP = 4
N_EXPERTS = 16
D_FF_E = 512
RMS_EPS = 1e-5
NEG_BIG = -1e30
PAIRS = 64
SEQ_PER_GROUP = PAIRS // N_HEADS_B
MOE_TILE = 256
ROW_CHUNKS = D_MODEL // 128
VMEM_LIMIT = 56 * 1024 * 1024


def _cparams(*sem):
    return pltpu.CompilerParams(dimension_semantics=sem, vmem_limit_bytes=VMEM_LIMIT)


def _dot(a, b):
    return jnp.dot(a, b, preferred_element_type=F32)


def _split_bf16(x):
    hi = x.astype(BF16)
    lo = (x - hi.astype(F32)).astype(BF16)
    return hi, lo


def _dot2(x, w):
    hi, lo = _split_bf16(x)
    return _dot(hi, w) + _dot(lo, w)


def _rms(x, g):
    return x * lax.rsqrt(jnp.mean(x * x, axis=-1, keepdims=True) + RMS_EPS) * g


def _sigmoid(x):
    return 1.0 / (1.0 + jnp.exp(-x))


def _store_row_tiles(ref, x):
    rows = x.shape[0]
    for c in range(ROW_CHUNKS):
        ref[pl.ds(c, rows, stride=ROW_CHUNKS), :] = x[:, c * 128:(c + 1) * 128]


def _time_major_spec(b, t, d, tm):
    if b == SEQ_PER_GROUP:
        return (b, t, d), pl.BlockSpec((b, tm // b, d), lambda i, *_: (0, i, 0))
    return (b * t, d), pl.BlockSpec((tm, d), lambda i, *_: (i, 0))


def _interleave_rows(pieces, scr):
    nb, steps = len(pieces), pieces[0].shape[0]
    chunks = pieces[0].shape[1] // 128
    for s, p in enumerate(pieces):
        for c in range(chunks):
            scr[c, pl.ds(s, steps, stride=nb), :] = p[:, c * 128:(c + 1) * 128].astype(scr.dtype)
    return jnp.concatenate([scr[c] for c in range(chunks)], axis=1)


def _deinterleave_rows(x, nb, scr):
    steps = x.shape[0] // nb
    chunks = x.shape[1] // 128
    for c in range(chunks):
        scr[c] = x[:, c * 128:(c + 1) * 128].astype(scr.dtype)
    return [jnp.concatenate([scr[c, pl.ds(s, steps, stride=nb), :] for c in range(chunks)], axis=1)
            for s in range(nb)]


def _load_time_major(x_ref, scr, b):
    if len(x_ref.shape) == 3:
        return _interleave_rows([x_ref[s] for s in range(b)], scr)
    return x_ref[...]


def _store_batch_major(y_ref, scr, y, b):
    if len(y_ref.shape) == 3:
        for s, p in enumerate(_deinterleave_rows(y, b, scr)):
            y_ref[s] = p
    else:
        y_ref[...] = y


def _load_row_tiles(ref):
    rows = ref.shape[0] // ROW_CHUNKS
    return jnp.concatenate([ref[pl.ds(c, rows, stride=ROW_CHUNKS), :] for c in range(ROW_CHUNKS)], axis=1)


def _in_proj_kernel(x_ref, g_ref, w_ref, q_ref, kv_ref, pb_ref, xs_scr, q_scr, kv_scr, *, b):
    x = _load_time_major(x_ref, xs_scr, b)
    xn = _rms(x, g_ref[...]).astype(BF16)
    q = _dot(xn, w_ref[:, :Q_COLS])
    kv = _dot(xn, w_ref[:, Q_COLS:D_A_IN])
    pb_ref[...] = _dot(xn, w_ref[:, D_A_IN:])
    if len(x_ref.shape) == 3:
        q_ref[...] = jnp.concatenate(_deinterleave_rows(q, b, q_scr), axis=1).astype(BF16)
        kv_ref[...] = jnp.concatenate(_deinterleave_rows(kv, b, kv_scr), axis=1)
    else:
        q_ref[...] = q
        kv_ref[...] = kv


def _in_proj(x, g, w_bf16, b, t, tm):
    n = b * t
    xshape, xspec = _time_major_spec(b, t, D_MODEL, tm)
    slab = len(xshape) == 3
    steps = tm // b
    if slab:
        qkv_specs = [pl.BlockSpec((steps, b * Q_COLS), lambda i: (i, 0)),
                     pl.BlockSpec((steps, b * 2 * KV_COLS), lambda i: (i, 0))]
        qkv_shapes = [jax.ShapeDtypeStruct((t, b * Q_COLS), BF16), jax.ShapeDtypeStruct((t, b * 2 * KV_COLS), F32)]
    else:
        qkv_specs = [pl.BlockSpec((tm, Q_COLS), lambda i: (i, 0)), pl.BlockSpec((tm, 2 * KV_COLS), lambda i: (i, 0))]
        qkv_shapes = [jax.ShapeDtypeStruct((n, Q_COLS), F32), jax.ShapeDtypeStruct((n, 2 * KV_COLS), F32)]
    return pl.pallas_call(
        functools.partial(_in_proj_kernel, b=b),
        grid=(n // tm,),
        in_specs=[xspec,
                  pl.BlockSpec((1, D_MODEL), lambda i: (0, 0)),
                  pl.BlockSpec((D_MODEL, D_IN0), lambda i: (0, 0))],
        out_specs=qkv_specs + [pl.BlockSpec((tm, D_B_IN), lambda i: (i, 0))],
        out_shape=qkv_shapes + [jax.ShapeDtypeStruct((n, D_B_IN), F32)],
        scratch_shapes=[pltpu.VMEM((D_MODEL // 128, tm, 128), F32), pltpu.VMEM((Q_COLS // 128, tm, 128), F32),
                        pltpu.VMEM((2 * KV_COLS // 128, tm, 128), F32)],
        compiler_params=_cparams("parallel"),
        name="in_proj",
    )(x.reshape(xshape), g, w_bf16)


def _attn_prompt_kernel(sinks_ref, q_ref, kc_ref, kp_ref, vc_ref, vp_ref, o_ref):
    j = pl.program_id(1)
    qi = lax.broadcasted_iota(I32, (WINDOW, 2 * WINDOW), 0)
    kj = lax.broadcasted_iota(I32, (WINDOW, 2 * WINDOW), 1)
    valid = jnp.logical_and(kj > qi, kj <= qi + WINDOW)
    valid = jnp.logical_and(valid, jnp.logical_or(kj >= WINDOW, j > 0))
    dist = (WINDOW + qi - kj).astype(F32)
    for n in range(N_KV_A):
        cs = slice(n * HEAD_DIM, (n + 1) * HEAD_DIM)
        kb = jnp.concatenate([kp_ref[:, cs], kc_ref[:, cs]], axis=0).astype(BF16)
        vb = jnp.concatenate([vp_ref[:, cs], vc_ref[:, cs]], axis=0).astype(BF16)
        for g in range(GQA_GROUP):
            h = n * GQA_GROUP + g
            hs = slice(h * HEAD_DIM, (h + 1) * HEAD_DIM)
            s = lax.dot_general(q_ref[:, hs], kb, (((1,), (1,)), ((), ())), preferred_element_type=F32)
            s = s * (HEAD_DIM ** -0.5) - (2.0 ** -(h + 1)) * dist
            s = jnp.where(valid, s, NEG_BIG)
            sink = sinks_ref[h]
            m = jnp.maximum(jnp.max(s, axis=1, keepdims=True), sink)
            p = jnp.exp(s - m)
            l = jnp.sum(p, axis=1, keepdims=True) + jnp.exp(sink - m)
            o = _dot(p.astype(BF16), vb) / l
            o_ref[:, hs] = o.astype(BF16)


def _attn_prompt(q2, kv2, sinks, b, t):
    prev = lambda bi, j: jnp.maximum(j - 1, 0)
    return pl.pallas_call(
        _attn_prompt_kernel,
        grid=(b, t // WINDOW),
        in_specs=[pl.BlockSpec(memory_space=pltpu.SMEM),
                  pl.BlockSpec((WINDOW, Q_COLS), lambda bi, j: (j, bi)),
                  pl.BlockSpec((WINDOW, KV_COLS), lambda bi, j: (j, 2 * bi)),
                  pl.BlockSpec((WINDOW, KV_COLS), lambda bi, j: (prev(bi, j), 2 * bi)),
                  pl.BlockSpec((WINDOW, KV_COLS), lambda bi, j: (j, 2 * bi + 1)),
                  pl.BlockSpec((WINDOW, KV_COLS), lambda bi, j: (prev(bi, j), 2 * bi + 1))],
        out_specs=pl.BlockSpec((WINDOW, Q_COLS), lambda bi, j: (j, bi)),
        out_shape=jax.ShapeDtypeStruct((t, b * Q_COLS), BF16),
        compiler_params=_cparams("parallel", "parallel"),
        name="attn_prompt",
    )(sinks, q2, kv2, kv2, kv2, kv2)


def _attn_sample_kernel(sinks_ref, q_ref, kn_ref, vn_ref, ck_ref, cv_ref, o_ref):
    bs, t = q_ref.shape[0], q_ref.shape[1]
    assert t & (t - 1) == 0
    nq, nk = GQA_GROUP * t, 2 * WINDOW
    r = lax.broadcasted_iota(I32, (nq, nk), 0)
    kj = lax.broadcasted_iota(I32, (nq, nk), 1)
    tq = jnp.bitwise_and(r, t - 1)
    valid = jnp.logical_and(kj > tq, kj <= tq + WINDOW)
    dist = (WINDOW + tq - kj).astype(F32)
    grp = jnp.right_shift(lax.broadcasted_iota(I32, (nq, 1), 0), t.bit_length() - 1)
    pad = jnp.zeros((bs, nk - WINDOW - t, HEAD_DIM), F32)
    for n in range(N_KV_A):
        cs = slice(n * HEAD_DIM, (n + 1) * HEAD_DIM)
        kb = jnp.concatenate([ck_ref[:, :, cs], kn_ref[:, :, cs], pad], axis=1).astype(BF16)
        vb = jnp.concatenate([cv_ref[:, :, cs], vn_ref[:, :, cs], pad], axis=1).astype(BF16)
        qn = jnp.concatenate([q_ref[:, :, (n * GQA_GROUP + g) * HEAD_DIM:(n * GQA_GROUP + g + 1) * HEAD_DIM]
                              for g in range(GQA_GROUP)], axis=1).astype(BF16)
        slope = jnp.zeros((nq, 1), F32)
        sink = jnp.zeros((nq, 1), F32)
        for g in range(GQA_GROUP):
            h = n * GQA_GROUP + g
            slope = jnp.where(grp == g, 2.0 ** -(h + 1), slope)
            sink = jnp.where(grp == g, sinks_ref[h], sink)
        s = jnp.einsum('bqd,bkd->bqk', qn, kb, preferred_element_type=F32)
        s = s * (HEAD_DIM ** -0.5) - (slope * dist)[None]
        s = jnp.where(valid[None], s, NEG_BIG)
        m = jnp.maximum(jnp.max(s, axis=2, keepdims=True), sink[None])
        p = jnp.exp(s - m)
        l = jnp.sum(p, axis=2, keepdims=True) + jnp.exp(sink[None] - m)
        o = jnp.einsum('bqk,bkd->bqd', p.astype(BF16), vb, preferred_element_type=F32) / l
        for g in range(GQA_GROUP):
            h = n * GQA_GROUP + g
            o_ref[:, :, h * HEAD_DIM:(h + 1) * HEAD_DIM] = o[:, g * t:(g + 1) * t, :]


def _attn_sample(q, kn, vn, ck, cv, sinks, bs=16):
    db, t = q.shape[0], q.shape[1]
    seq3 = lambda w: pl.BlockSpec((bs, t, w), lambda i: (i, 0, 0))
    cache = pl.BlockSpec((bs, WINDOW, KV_COLS), lambda i: (i, 0, 0))
    return pl.pallas_call(
        _attn_sample_kernel,
        grid=(db // bs,),
        in_specs=[pl.BlockSpec(memory_space=pltpu.SMEM), seq3(Q_COLS), seq3(KV_COLS), seq3(KV_COLS), cache, cache],
        out_specs=seq3(Q_COLS),
        out_shape=jax.ShapeDtypeStruct((db, t, Q_COLS), F32),
        compiler_params=_cparams("parallel"),
        name="attn_sample",
    )(sinks, q, kn, vn, ck, cv)


def _rwkv_prep_kernel(pb_ref, halo_ref, init_ref, mu_ref, w0_ref, wup_ref, a0_ref, aup_ref, gup_ref,
                      kk_ref, ka_ref, rk_ref, ones_ref,
                      r_o, w_o, k_o, v_o, kk_o, nkka_o, bonus_o, g_o, *, b):
    i = pl.program_id(0)
    pb = pb_ref[...]
    tm = pb.shape[0]
    halo = jnp.where(i == 0, init_ref[...], halo_ref[...])
    prev = halo if tm == b else jnp.concatenate([halo, pb[:tm - b]], axis=0)
    xs = pb + (prev - pb) * mu_ref[...]
    r = xs[:, :D_B]
    k = xs[:, D_B:2 * D_B]
    v = xs[:, 2 * D_B:3 * D_B]
    o1 = 3 * D_B
    wd = xs[:, o1:o1 + D_LORA_W]
    ad = xs[:, o1 + D_LORA_W:o1 + D_LORA_W + D_LORA_A]
    gd = xs[:, o1 + D_LORA_W + D_LORA_A:]
    z = -(w0_ref[...] + _dot(jnp.tanh(wd).astype(BF16), wup_ref[...]))
    softplus = jnp.maximum(z, 0.0) + jnp.log(1.0 + jnp.exp(-jnp.abs(z)))
    decay = jnp.exp(-jnp.exp(-softplus - 0.5))
    a = _sigmoid(a0_ref[...] + _dot(ad.astype(BF16), aup_ref[...]))
    g_o[...] = _dot(_sigmoid(gd).astype(BF16), gup_ref[...])
    ones = ones_ref[...]
    kk = k * kk_ref[...]
    kk = kk * lax.rsqrt(jnp.maximum(_dot2(kk * kk, ones), 1e-24))
    k2 = k * (1.0 + (a - 1.0) * ka_ref[...])
    bonus_o[...] = _dot2(r * k2 * rk_ref[...], ones) * v

    half = HEAD_DIM // 2
    lane8 = lax.broadcasted_iota(I32, (SEQ_PER_GROUP, 128), 1)
    low8 = lane8 < HEAD_DIM
    first_copy = jnp.bitwise_and(lax.broadcasted_iota(I32, (half, 128), 1), SEQ_PER_GROUP) == 0
    pairs = ((r, decay, r_o, w_o), (k2, kk, k_o, kk_o), (-(kk * a), v, nkka_o, None))
    for u in range(tm // SEQ_PER_GROUP):
        rows = slice(u * SEQ_PER_GROUP, (u + 1) * SEQ_PER_GROUP)
        for x, y, x_o, y_o in pairs:
            xu, yu = x[rows], y[rows]
            pieces = []
            for h in range(N_HEADS_B):
                cs = slice((h // 2) * 128, (h // 2 + 1) * 128)
                if h % 2 == 0:
                    p = jnp.where(low8, xu[:, cs], pltpu.roll(yu[:, cs], HEAD_DIM, 1))
                else:
                    p = jnp.where(low8, pltpu.roll(xu[:, cs], HEAD_DIM, 1), yu[:, cs])
                pieces += [p, p]
            tr = jnp.concatenate(pieces, axis=0).T
            x_o[u] = tr[:HEAD_DIM]
            if y_o is not None:
                y_o[u] = tr[HEAD_DIM:]
            else:
                v_o[u] = jnp.where(first_copy, tr[HEAD_DIM:HEAD_DIM + half], tr[HEAD_DIM + half:])


def _rwkv_prep(pb, init, p, b, tm):
    n = pb.shape[0]
    units = n // SEQ_PER_GROUP
    tu = tm // SEQ_PER_GROUP
    half = HEAD_DIM // 2
    row = lambda w: pl.BlockSpec((1, w), lambda i: (0, 0))
    full = lambda r, c: pl.BlockSpec((r, c), lambda i: (0, 0))
    tile = pl.BlockSpec((tm, D_B), lambda i: (i, 0))
    kspec = pl.BlockSpec((tu, HEAD_DIM, 128), lambda i: (i, 0, 0))
    vspec = pl.BlockSpec((tu, half, 128), lambda i: (i, 0, 0))
    kshape = jax.ShapeDtypeStruct((units, HEAD_DIM, 128), F32)
    halo_blocks = tm // b
    return pl.pallas_call(
        functools.partial(_rwkv_prep_kernel, b=b),
        grid=(n // tm,),
        in_specs=[pl.BlockSpec((tm, D_B_IN), lambda i: (i, 0)),
                  pl.BlockSpec((b, D_B_IN), lambda i: (jnp.maximum(i * halo_blocks - 1, 0), 0)),
                  full(b, D_B_IN), row(D_B_IN), row(D_B), full(D_LORA_W, D_B), row(D_B), full(D_LORA_A, D_B),
                  full(D_LORA_G, D_B), row(D_B), row(D_B), row(D_B), full(D_B, D_B)],
        out_specs=[kspec, kspec, kspec, vspec, kspec, kspec, tile, tile],
        out_shape=[kshape, kshape, kshape, jax.ShapeDtypeStruct((units, half, 128), F32), kshape, kshape,
                   jax.ShapeDtypeStruct((n, D_B), F32), jax.ShapeDtypeStruct((n, D_B), F32)],
        compiler_params=_cparams("arbitrary"),
        name="rwkv_prep",
    )(pb, pb, init, p['mu'], p['w0'], p['w_up'], p['a0'], p['a_up'], p['g_up'], p['k_k'], p['k_a'], p['r_k'],
      p['head_ones'])


def _wkv_scan_kernel(r_ref, w_ref, k_ref, kk_ref, nkka_ref, v_ref, s0_ref, o_ref, st_ref, s_scr):
    j = pl.program_id(1)

    @pl.when(j == 0)
    def _():
        s_scr[...] = s0_ref[0]

    tc = r_ref.shape[0]
    nsub = (HEAD_DIM // 2) // 8

    def bcast(ref, s, kx):
        return jnp.broadcast_to(ref[s, pl.ds(kx, 1), :], (8, 128))

    acc0 = [[jnp.zeros((8, 128), F32) for _ in range(2)] for _ in range(nsub)]
    for kx in range(HEAD_DIM):
        kkr = bcast(kk_ref, 0, kx)
        for i in range(nsub):
            acc0[i][kx % 2] = acc0[i][kx % 2] + s_scr[kx, 8 * i:8 * i + 8, :] * kkr

    def step(s, sa):
        nxt = jnp.minimum(s + 1, tc - 1)
        vv = [v_ref[s, 8 * i:8 * i + 8, :] for i in range(nsub)]
        oacc = [[jnp.zeros((8, 128), F32) for _ in range(2)] for _ in range(nsub)]
        nacc = [[jnp.zeros((8, 128), F32) for _ in range(2)] for _ in range(nsub)]
        for kx in range(HEAD_DIM):
            rr, wr, kr = bcast(r_ref, s, kx), bcast(w_ref, s, kx), bcast(k_ref, s, kx)
            nk, kkn = bcast(nkka_ref, s, kx), bcast(kk_ref, nxt, kx)
            for i in range(nsub):
                rows = slice(8 * i, 8 * i + 8)
                sk = s_scr[kx, rows, :] * wr + sa[i] * nk + vv[i] * kr
                s_scr[kx, rows, :] = sk
                oacc[i][kx % 2] = oacc[i][kx % 2] + sk * rr
                nacc[i][kx % 2] = nacc[i][kx % 2] + sk * kkn
        o_ref[s] = jnp.concatenate([a[0] + a[1] for a in oacc], axis=0)
        return [a[0] + a[1] for a in nacc]

    lax.fori_loop(0, tc, step, [a[0] + a[1] for a in acc0])

    @pl.when(j == pl.num_programs(1) - 1)
    def _():
        st_ref[0] = s_scr[...]


def _wkv_scan(r, w, k, kk, nkka, v, s0, b, t, tc):
    g = b // SEQ_PER_GROUP
    half = HEAD_DIM // 2
    kview = lambda z: z.reshape(t, g, HEAD_DIM, 128)
    kspec = pl.BlockSpec((tc, None, HEAD_DIM, 128), lambda gi, j: (j, gi, 0, 0))
    vspec = pl.BlockSpec((tc, None, half, 128), lambda gi, j: (j, gi, 0, 0))
    sspec = pl.BlockSpec((1, HEAD_DIM, half, 128), lambda gi, j: (gi, 0, 0, 0))
    o, st = pl.pallas_call(
        _wkv_scan_kernel,
        grid=(g, t // tc),
        in_specs=[kspec] * 5 + [vspec, sspec],
        out_specs=[vspec, sspec],
        out_shape=[jax.ShapeDtypeStruct((t, g, half, 128), F32),
                   jax.ShapeDtypeStruct((g, HEAD_DIM, half, 128), F32)],
        scratch_shapes=[pltpu.VMEM((HEAD_DIM, half, 128), F32)],
        compiler_params=_cparams("parallel", "arbitrary"),
        name="wkv_scan",
    )(kview(r), kview(w), kview(k), kview(kk), kview(nkka), v.reshape(t, g, half, 128), s0)
    return o.reshape(t * g, half, 128), st


def _state_to_scan(s, b):
    g = b // SEQ_PER_GROUP
    s = s.reshape(g, SEQ_PER_GROUP, N_HEADS_B, 2, HEAD_DIM // 2, HEAD_DIM).transpose(0, 5, 4, 2, 3, 1)
    return s.reshape(g, HEAD_DIM, HEAD_DIM // 2, 128)


def _state_from_scan(s, b):
    g = b // SEQ_PER_GROUP
    s = s.reshape(g, HEAD_DIM, HEAD_DIM // 2, N_HEADS_B, 2, SEQ_PER_GROUP).transpose(0, 5, 3, 4, 2, 1)
    return s.reshape(b, N_HEADS_B, HEAD_DIM, HEAD_DIM)


def _route_tile(x, nrm_ref, wr_ref, br_ref, cnt_scr, xn_o, idx_o, wts_o):
    tm = x.shape[0]
    xn = _rms(x, nrm_ref[...])
    hi, lo = _split_bf16(xn)
    _store_row_tiles(xn_o, xn)
    wr = wr_ref[...]
    pa = _dot(hi, wr)
    pb = _dot(lo, wr)
    lg = pa[:, 0:32] + pa[:, 32:64] + pb[:, 0:32] + pb[:, 32:64] + br_ref[...]
    col = lambda c: lg[:, c:c + 1]
    c = [col(gx) for gx in range(N_EGROUPS)]
    m = jnp.maximum(jnp.maximum(c[0], c[1]), jnp.maximum(c[2], c[3]))
    den = jnp.exp(c[0] - m) + jnp.exp(c[1] - m) + jnp.exp(c[2] - m) + jnp.exp(c[3] - m)
    pg = 1.0 / den
    gi = jnp.where(c[0] >= m, 0, jnp.where(c[1] >= m, 1, jnp.where(c[2] >= m, 2, 3))).astype(I32)
    sel = []
    for e in range(EXP_PER_GROUP):
        sel.append(jnp.where(gi == 0, col(4 + e), jnp.where(gi == 1, col(8 + e),
                                                            jnp.where(gi == 2, col(12 + e), col(16 + e)))))
    v1 = jnp.maximum(jnp.maximum(sel[0], sel[1]), jnp.maximum(sel[2], sel[3]))
    i1 = jnp.where(sel[0] >= v1, 0, jnp.where(sel[1] >= v1, 1, jnp.where(sel[2] >= v1, 2, 3))).astype(I32)
    rest = [jnp.where(i1 == e, -jnp.inf, sel[e]) for e in range(EXP_PER_GROUP)]
    v2 = jnp.maximum(jnp.maximum(rest[0], rest[1]), jnp.maximum(rest[2], rest[3]))
    i2 = jnp.where(rest[0] >= v2, 0, jnp.where(rest[1] >= v2, 1, jnp.where(rest[2] >= v2, 2, 3))).astype(I32)
    tt = jnp.exp(v2 - v1)
    w1 = pg / (1.0 + tt)
    w2 = pg * tt / (1.0 + tt)
    e1 = gi * EXP_PER_GROUP + i1
    e2 = gi * EXP_PER_GROUP + i2
    lane = lax.broadcasted_iota(I32, (tm, N_EXPERTS), 1)
    oh1 = lane == e1
    oh2 = lane == e2
    oh = jnp.where(jnp.logical_or(oh1, oh2), 1.0, 0.0)
    ri = lax.broadcasted_iota(I32, (tm, tm), 0)
    ci = lax.broadcasted_iota(I32, (tm, tm), 1)
    ltri = jnp.where(ri > ci, 1.0, 0.0).astype(BF16)
    cnt = cnt_scr[0:1, 0:N_EXPERTS]
    pre = _dot(ltri, oh.astype(BF16)) + cnt
    rank1 = jnp.sum(jnp.where(oh1, pre, 0.0), axis=1, keepdims=True)
    rank2 = jnp.sum(jnp.where(oh2, pre, 0.0), axis=1, keepdims=True)
    cnt_scr[0:1, 0:N_EXPERTS] = cnt + jnp.sum(oh, axis=0, keepdims=True)
    lw = lax.broadcasted_iota(I32, (tm, 128), 1)
    wts_o[...] = jnp.where(lw == 0, w1, jnp.where(lw == 1, w2, 0.0))
    cols = jnp.where(lw == 0, e1.astype(F32), jnp.where(lw == 1, e2.astype(F32),
                                                         jnp.where(lw == 2, rank1, jnp.where(lw == 3, rank2, 0.0))))
    idx_o[...] = cols.T[0:8, :].astype(I32)


def _route_out_specs(tm):
    return [pl.BlockSpec((tm * ROW_CHUNKS, 128), lambda i: (i, 0)),
            pl.BlockSpec((8, tm), lambda i: (0, i)),
            pl.BlockSpec((tm, 128), lambda i: (i, 0)),
            pl.BlockSpec((8, 128), lambda i: (0, 0))]


def _route_out_shapes(n):
    return [jax.ShapeDtypeStruct((n * ROW_CHUNKS, 128), F32),
            jax.ShapeDtypeStruct((8, n), I32),
            jax.ShapeDtypeStruct((n, 128), F32),
            jax.ShapeDtypeStruct((8, 128), F32)]


def _route_in_specs():
    return [pl.BlockSpec((1, D_MODEL), lambda i: (0, 0)),
            pl.BlockSpec((D_MODEL, 128), lambda i: (0, 0)),
            pl.BlockSpec((1, 32), lambda i: (0, 0))]


def _mix0_out_kernel(o_ref, bonus_ref, g_ref, attn_ref, x_ref, lnw_ref, lnb_ref, avg_ref, wo_ref,
                     nrm_ref, wr_ref, br_ref,
                     x1_o, xn_o, idx_o, wts_o, cnt_o, cnt_scr, xs_scr, at_scr, *, b):
    @pl.when(pl.program_id(0) == 0)
    def _():
        cnt_scr[...] = jnp.zeros_like(cnt_scr)

    avg = avg_ref[...]
    half = HEAD_DIM // 2
    lane8 = lax.broadcasted_iota(I32, (SEQ_PER_GROUP, 128), 1)
    unit_rows = []
    for u in range(o_ref.shape[0]):
        ot = jnp.concatenate([o_ref[u], jnp.zeros((128 - half, 128), F32)], axis=0).T
        cols = []
        for jj in range(N_HEADS_B // 2):
            q = [ot[(4 * jj + i) * SEQ_PER_GROUP:(4 * jj + i + 1) * SEQ_PER_GROUP] for i in range(4)]
            c = jnp.where(lane8 < half, q[0], pltpu.roll(q[1], half, 1))
            c = jnp.where(lane8 < 2 * half, c, pltpu.roll(q[2], 2 * half, 1))
            c = jnp.where(lane8 < 3 * half, c, pltpu.roll(q[3], 3 * half, 1))
            cols.append(c)
        unit_rows.append(jnp.concatenate(cols, axis=1))
    o = jnp.concatenate(unit_rows, axis=0)
    d = o - _dot2(o, avg)
    var = _dot2(d * d, avg)
    on = d * lax.rsqrt(var + RWKV_GN_EPS) * lnw_ref[...] + lnb_ref[...]
    rout = ((on + bonus_ref[...]) * g_ref[...]).astype(BF16)
    x = _load_time_major(x_ref, xs_scr, b)
    if len(x_ref.shape) == 3:
        attn = _interleave_rows([attn_ref[:, s * Q_COLS:(s + 1) * Q_COLS] for s in range(b)], at_scr)
    else:
        attn = attn_ref[...]
    y = _dot(attn.astype(BF16), wo_ref[:Q_COLS, :]) + _dot(rout, wo_ref[Q_COLS:, :]) + x
    x1_o[...] = y
    _route_tile(y, nrm_ref, wr_ref, br_ref, cnt_scr, xn_o, idx_o, wts_o)
    cnt_o[...] = cnt_scr[...]


def _mix0_out(o, bonus, g, attn, x, p, rp, b, t, tm):
    n = b * t
    xshape, xspec = _time_major_spec(b, t, D_MODEL, tm)
    aspec = (pl.BlockSpec((tm // b, b * Q_COLS), lambda i: (i, 0)) if len(xshape) == 3
             else pl.BlockSpec((tm, Q_COLS), lambda i: (i, 0)))
    tile = lambda w: pl.BlockSpec((tm, w), lambda i: (i, 0))
    row = lambda w: pl.BlockSpec((1, w), lambda i: (0, 0))
    full = lambda r, c: pl.BlockSpec((r, c), lambda i: (0, 0))
    return pl.pallas_call(
        functools.partial(_mix0_out_kernel, b=b),
        grid=(n // tm,),
        in_specs=[pl.BlockSpec((tm // SEQ_PER_GROUP, HEAD_DIM // 2, 128), lambda i: (i, 0, 0)),
                  tile(D_B), tile(D_B), aspec, xspec, row(D_B), row(D_B),
                  full(D_B, D_B), full(D_MODEL, D_MODEL)] + _route_in_specs(),
        out_specs=[tile(D_MODEL)] + _route_out_specs(tm),
        out_shape=[jax.ShapeDtypeStruct((n, D_MODEL), F32)] + _route_out_shapes(n),
        scratch_shapes=[pltpu.VMEM((8, 128), F32), pltpu.VMEM((D_MODEL // 128, tm, 128), F32),
                        pltpu.VMEM((Q_COLS // 128, tm, 128), F32)],
        compiler_params=_cparams("arbitrary"),
        name="mix0_out",
    )(o, bonus, g, attn, x.reshape(xshape), p['ln_w'], p['ln_b'], p['head_avg'], p['w_out'], rp['norm'], rp['w'], rp['b'])


def _row_dma_start(idx_ref, pos, r, src_hbm, dst, sem, priority):
    src_row = pl.multiple_of(idx_ref[pos] * ROW_CHUNKS, ROW_CHUNKS)
    dst_row = pl.multiple_of(r * ROW_CHUNKS, ROW_CHUNKS)
    pltpu.make_async_copy(src_hbm.at[pl.ds(src_row, ROW_CHUNKS)], dst.at[pl.ds(dst_row, ROW_CHUNKS)],
                          sem).start(priority=priority)


def _row_gather_start(idx_ref, base, n_rows, src_hbm, dst, sem):
    def body(r2, carry):
        for p in range(2):
            _row_dma_start(idx_ref, base + 2 * r2 + p, 2 * r2 + p, src_hbm, dst, sem, p)
        return carry

    lax.fori_loop(0, n_rows // 2, body, 0, unroll=4)


def _row_gather_wait(dst, sem):
    pltpu.make_async_copy(dst, dst, sem).wait()


def _expert_kernel(te_ref, nu_ref, pad_ref, p1_ref, p2_ref, x_hbm, wg_ref, wu_ref, wd_ref, o_ref,
                   src_ref, xbuf, sem, wgb, wub, wdb):
    i = pl.program_id(0)
    nu = nu_ref[0]

    @pl.when(i == 0)
    def _():
        def fill_pad(e, carry):
            start = pad_ref[2 * e]

            def body(r, c):
                src_ref[r] = r - start
                return c

            lax.fori_loop(start, pad_ref[2 * e + 1], body, 0)
            return carry

        lax.fori_loop(0, N_EXPERTS + 1, fill_pad, 0)

        def invert(n, carry):
            src_ref[p1_ref[n]] = n
            src_ref[p2_ref[n]] = n
            return carry

        lax.fori_loop(0, p1_ref.shape[0], invert, 0, unroll=8)
        _row_gather_start(src_ref, 0, MOE_TILE, x_hbm, xbuf.at[0], sem.at[0])

    new_expert = jnp.logical_or(i == 0, te_ref[i] != te_ref[jnp.maximum(i - 1, 0)])

    @pl.when(jnp.logical_and(i < nu, new_expert))
    def _():
        wgb[...] = wg_ref[0].astype(BF16)
        wub[...] = wu_ref[0].astype(BF16)
        wdb[...] = wd_ref[0].astype(BF16)

    @pl.when(i < nu)
    def _():
        cur = i % 2
        nxt = 1 - cur
        base = (i + 1) * MOE_TILE
        nb = 2 * D_FF_E // 256
        per = MOE_TILE // nb
        batch = [0]

        def start_batch():
            for r in range(batch[0] * per, (batch[0] + 1) * per):
                _row_dma_start(src_ref, base + r, r, x_hbm, xbuf.at[nxt], sem.at[nxt], r % 2)
            batch[0] += 1

        _row_gather_wait(xbuf.at[cur], sem.at[cur])
        x = _load_row_tiles(xbuf.at[cur]).astype(BF16)
        hg, hu = [], []
        for c in range(D_FF_E // 256):
            hg.append(_dot(x, wgb[:, c * 256:(c + 1) * 256]))
            start_batch()
        for c in range(D_FF_E // 256):
            hu.append(_dot(x, wub[:, c * 256:(c + 1) * 256]))
            start_batch()
        hg = jnp.concatenate(hg, axis=1)
        h = ((hg * _sigmoid(hg)) * jnp.concatenate(hu, axis=1)).astype(BF16)
        for c in range(D_MODEL // 256):
            res = _dot(h, wdb[:, c * 256:(c + 1) * 256])
            for cc in range(2):
                o_ref[pl.ds(2 * c + cc, MOE_TILE, stride=ROW_CHUNKS), :] = res[:, cc * 128:(cc + 1) * 128]

    @pl.when(i == nu)
    def _():
        _row_gather_wait(xbuf.at[i % 2], sem.at[i % 2])

    @pl.when(i >= nu)
    def _():
        o_ref[...] = jnp.zeros_like(o_ref)


def _experts(xn, r, pos1, pos2, tile_expert, n_used, pad_ranges, wg, wu, wd):
    wspec = lambda a, b: pl.BlockSpec((1, a, b), lambda i, te, nu, pad, p1, p2: (te[i], 0, 0))
    grid_spec = pltpu.PrefetchScalarGridSpec(
        num_scalar_prefetch=5,
        grid=(r // MOE_TILE,),
        in_specs=[pl.BlockSpec(memory_space=pl.ANY),
                  wspec(D_MODEL, D_FF_E), wspec(D_MODEL, D_FF_E), wspec(D_FF_E, D_MODEL)],
        out_specs=pl.BlockSpec((MOE_TILE * ROW_CHUNKS, 128), lambda i, te, nu, pad, p1, p2: (i, 0)),
        scratch_shapes=[pltpu.SMEM((r,), I32), pltpu.VMEM((2, MOE_TILE * ROW_CHUNKS, 128), F32),
                        pltpu.SemaphoreType.DMA((2,)),
                        pltpu.VMEM((D_MODEL, D_FF_E), BF16), pltpu.VMEM((D_MODEL, D_FF_E), BF16),
                        pltpu.VMEM((D_FF_E, D_MODEL), BF16)],
    )
    return pl.pallas_call(
        _expert_kernel,
        grid_spec=grid_spec,
        out_shape=jax.ShapeDtypeStruct((r * ROW_CHUNKS, 128), F32),
        compiler_params=_cparams("arbitrary"),
        name="moe_experts",
    )(tile_expert, n_used, pad_ranges, pos1, pos2, xn, wg, wu, wd)


def _moe(xn, idx, cnt, ep):
    n = xn.shape[0] // ROW_CHUNKS
    rows = 2 * n + N_EXPERTS * MOE_TILE
    counts = cnt[0, :N_EXPERTS].astype(I32)
    padded = ((counts + MOE_TILE - 1) // MOE_TILE) * MOE_TILE
    ends = jnp.cumsum(padded)
    offs = ends - padded
    pos1 = offs[idx[0]] + idx[2]
    pos2 = offs[idx[1]] + idx[3]
    pad_ranges = jnp.stack([jnp.append(offs + counts, ends[-1]), jnp.append(ends, ends[-1] + MOE_TILE)], axis=1)
    pad_ranges = pad_ranges.reshape(-1).astype(I32)
    n_used = (ends[-1] // MOE_TILE).astype(I32)
    starts = jnp.arange(rows // MOE_TILE, dtype=I32) * MOE_TILE
    starts = jnp.minimum(starts, ends[-1] - 1)
    tile_expert = jnp.sum((starts[:, None] >= ends[None, :]).astype(I32), axis=1)
    tile_expert = jnp.minimum(tile_expert, N_EXPERTS - 1).astype(I32)
    out = _experts(xn, rows, pos1, pos2, tile_expert, n_used.reshape(1), pad_ranges, ep['wg'], ep['wu'], ep['wd'])
    return out, pos1, pos2


def _combine_kernel(p1_ref, p2_ref, x_ref, wts_ref, nrm_ref, out_hbm, y_o, gbuf, sem, ys_scr, *, final, b):
    i = pl.program_id(0)
    tm = x_ref.shape[0]

    def start(tile, slot):
        _row_gather_start(p1_ref, tile * tm, tm, out_hbm, gbuf.at[slot, 0], sem.at[slot])
        _row_gather_start(p2_ref, tile * tm, tm, out_hbm, gbuf.at[slot, 1], sem.at[slot])

    @pl.when(i == 0)
    def _():
        start(0, 0)

    @pl.when(i + 1 < pl.num_programs(0))
    def _():
        start(i + 1, (i + 1) % 2)

    cur = i % 2
    _row_gather_wait(gbuf.at[cur], sem.at[cur])
    wts = wts_ref[...]
    y = x_ref[...] + wts[:, 0:1] * _load_row_tiles(gbuf.at[cur, 0]) + wts[:, 1:2] * _load_row_tiles(gbuf.at[cur, 1])
    if final:
        _store_batch_major(y_o, ys_scr, _rms(y, nrm_ref[...]), b)
    else:
        y_o[...] = y


def _moe_combine(x, wts, out, pos1, pos2, nrm, final, b, t, tm=256):
    n = x.shape[0]
    tile = lambda w: pl.BlockSpec((tm, w), lambda i, p1, p2: (i, 0))
    if final:
        yshape, yspec = _time_major_spec(b, t, D_MODEL, tm)
    else:
        yshape, yspec = (n, D_MODEL), tile(D_MODEL)
    grid_spec = pltpu.PrefetchScalarGridSpec(
        num_scalar_prefetch=2,
        grid=(n // tm,),
        in_specs=[tile(D_MODEL), tile(128), pl.BlockSpec((1, D_MODEL), lambda i, p1, p2: (0, 0)),
                  pl.BlockSpec(memory_space=pl.ANY)],
        out_specs=yspec,
        scratch_shapes=[pltpu.VMEM((2, 2, tm * ROW_CHUNKS, 128), F32), pltpu.SemaphoreType.DMA((2,)),
                        pltpu.VMEM((D_MODEL // 128, tm, 128), F32)],
    )
    return pl.pallas_call(
        functools.partial(_combine_kernel, final=final, b=b),
        grid_spec=grid_spec,
        out_shape=jax.ShapeDtypeStruct(yshape, F32),
        compiler_params=_cparams("arbitrary"),
        name="moe_combine_final" if final else "moe_combine",
    )(pos1, pos2, x, wts, nrm, out)


def _gelu_tanh(x):
    return 0.5 * x * (1.0 + jnp.tanh(0.7978845608028654 * (x + 0.044715 * (x * x * x))))


def _mix1_kernel(x_ref, nmix_ref, bre_ref, bim_ref, are_ref, aim_ref, cre_ref, cim_ref,
                 dsk_ref, wo_ref, h0r_ref, h0i_ref, nrm_ref, wr_ref, br_ref,
                 x2_o, xn_o, idx_o, wts_o, cnt_o, hr_o, hi_o,
                 bur, bui, hr_scr, hi_scr, cnt_scr, *, b, cw):
    @pl.when(pl.program_id(0) == 0)
    def _():
        cnt_scr[...] = jnp.zeros_like(cnt_scr)
        hr_scr[...] = h0r_ref[...]
        hi_scr[...] = h0i_ref[...]

    x = x_ref[...]
    u = _rms(x, nmix_ref[...])
    ub = u.astype(BF16)
    nblk = bre_ref.shape[0]
    kin = D_MODEL // nblk
    kst = S5_STATE // nblk
    for cb in range(nblk):
        ucb = ub[:, cb * kin:(cb + 1) * kin]
        bur[:, cb * kst:(cb + 1) * kst] = _dot(ucb, bre_ref[cb])
        bui[:, cb * kst:(cb + 1) * kst] = _dot(ucb, bim_ref[cb])

    tc = x.shape[0] // b
    for c0 in range(0, S5_STATE, cw):
        cs = slice(c0, c0 + cw)
        ar = jnp.broadcast_to(are_ref[:, cs], (b, cw))
        ai = jnp.broadcast_to(aim_ref[:, cs], (b, cw))

        def step(s, carry, cs=cs, ar=ar, ai=ai):
            hr, hi = carry
            rows = pl.ds(pl.multiple_of(s * b, b), b)
            nr = ar * hr - ai * hi + bur[rows, cs]
            ni = ar * hi + ai * hr + bui[rows, cs]
            bur[rows, cs] = nr
            bui[rows, cs] = ni
            return nr, ni

        hr, hi = lax.fori_loop(0, tc, step, (hr_scr[:, cs], hi_scr[:, cs]), unroll=True)
        hr_scr[:, cs] = hr
        hi_scr[:, cs] = hi

    ych = []
    for cb in range(nblk):
        ss = slice(cb * kst, (cb + 1) * kst)
        ych.append(_dot(bur[:, ss].astype(BF16), cre_ref[cb]) - _dot(bui[:, ss].astype(BF16), cim_ref[cb]))
    y = jnp.concatenate(ych, axis=1) + dsk_ref[...] * u
    z = _dot(_gelu_tanh(y).astype(BF16), wo_ref[...])
    x2 = x + z[:, :D_MODEL] * _sigmoid(z[:, D_MODEL:])
    x2_o[...] = x2
    _route_tile(x2, nrm_ref, wr_ref, br_ref, cnt_scr, xn_o, idx_o, wts_o)
    cnt_o[...] = cnt_scr[...]
    hr_o[...] = hr_scr[...]
    hi_o[...] = hi_scr[...]


def _mix1(x, sp, rp, h0r, h0i, b, tr):
    n = x.shape[0]
    cw = 1024 if b == 8 else 128
    tile = lambda w: pl.BlockSpec((tr, w), lambda i: (i, 0))
    row = lambda w: pl.BlockSpec((1, w), lambda i: (0, 0))
    full = lambda *s: pl.BlockSpec(s, lambda i: (0,) * len(s))
    nblk = sp['b_re'].shape[0]
    return pl.pallas_call(
        functools.partial(_mix1_kernel, b=b, cw=cw),
        grid=(n // tr,),
        in_specs=[tile(D_MODEL), row(D_MODEL),
                  full(nblk, D_MODEL // nblk, S5_STATE // nblk), full(nblk, D_MODEL // nblk, S5_STATE // nblk),
                  row(S5_STATE), row(S5_STATE),
                  full(nblk, S5_STATE // nblk, D_MODEL // nblk), full(nblk, S5_STATE // nblk, D_MODEL // nblk),
                  row(D_MODEL), full(D_MODEL, 2 * D_MODEL), full(b, S5_STATE), full(b, S5_STATE)] + _route_in_specs(),
        out_specs=[tile(D_MODEL)] + _route_out_specs(tr) + [full(b, S5_STATE), full(b, S5_STATE)],
        out_shape=[jax.ShapeDtypeStruct((n, D_MODEL), F32)] + _route_out_shapes(n)
                  + [jax.ShapeDtypeStruct((b, S5_STATE), F32)] * 2,
        scratch_shapes=[pltpu.VMEM((tr, S5_STATE), F32), pltpu.VMEM((tr, S5_STATE), F32),
                        pltpu.VMEM((b, S5_STATE), F32), pltpu.VMEM((b, S5_STATE), F32),
                        pltpu.VMEM((8, 128), F32)],
        compiler_params=_cparams("arbitrary"),
        name="mix1",
    )(x, sp['norm'], sp['b_re'], sp['b_im'], sp['a_re'], sp['a_im'], sp['c_re'], sp['c_im'],
      sp['d'], sp['w_out'], h0r, h0i, rp['norm'], rp['w'], rp['b'])


def _router_params(norm, w_rc, b_rc, w_rf, b_rf):
    w = jnp.concatenate([w_rc, w_rf.reshape(D_MODEL, N_EXPERTS), jnp.zeros((D_MODEL, 12), F32)], axis=1)
    hi = w.astype(BF16)
    lo = (w - hi.astype(F32)).astype(BF16)
    wcat = jnp.concatenate([hi, lo, jnp.zeros((D_MODEL, 64), BF16)], axis=1)
    bias = jnp.concatenate([b_rc, b_rf.reshape(-1), jnp.zeros((12,), F32)]).reshape(1, 32)
    return {'norm': norm.reshape(1, D_MODEL), 'w': wcat, 'b': bias}


def _expert_params(wg, wu, wd):
    return {'wg': wg, 'wu': wu, 'wd': wd}


def _s5_params(norm, a_re, a_im, log_dt, b_re, b_im, c_re, c_im, d_skip, w_out, nblk=8):
    dt = jnp.exp(log_dt)
    mag = jnp.exp(dt * a_re)
    ab_re, ab_im = mag * jnp.cos(dt * a_im), mag * jnp.sin(dt * a_im)
    den = a_re * a_re + a_im * a_im
    f_re = ((ab_re - 1.0) * a_re + ab_im * a_im) / den
    f_im = (ab_im * a_re - (ab_re - 1.0) * a_im) / den
    bb_re = f_re[..., None] * b_re - f_im[..., None] * b_im
    bb_im = f_re[..., None] * b_im + f_im[..., None] * b_re
    gpb = S5_GROUPS // nblk
    eye = jnp.eye(gpb, dtype=F32)

    def in_blocks(bb):
        bb = bb.reshape(nblk, gpb, S5_P, S5_CH)
        w = jnp.einsum('ngpc,gh->ngchp', bb, eye)
        return w.reshape(nblk, gpb * S5_CH, gpb * S5_P).astype(BF16)

    def out_blocks(cc):
        cc = cc.reshape(nblk, gpb, S5_CH, S5_P)
        w = jnp.einsum('ngcp,gh->ngphc', cc, eye)
        return w.reshape(nblk, gpb * S5_P, gpb * S5_CH).astype(BF16)

    return {'norm': norm.reshape(1, D_MODEL), 'b_re': in_blocks(bb_re), 'b_im': in_blocks(bb_im),
            'a_re': ab_re.reshape(1, S5_STATE), 'a_im': ab_im.reshape(1, S5_STATE),
            'c_re': out_blocks(c_re), 'c_im': out_blocks(c_im), 'd': d_skip.reshape(1, D_MODEL),
            'w_out': w_out.astype(BF16)}


def _head_block(value):
    hid = jnp.arange(D_B, dtype=I32) // HEAD_DIM
    return jnp.where(hid[:, None] == hid[None, :], value, 0.0).astype(BF16)


def _run_group(x, cache_k, cache_v, shift0, wkv0, h0r, h0i, pr):
    b, t = x.shape[0], x.shape[1]
    n = b * t
    prompt = cache_k is None
    tm = 256 if prompt else 128
    if not prompt:
        x = x.transpose(1, 0, 2)
    q, kv, pb = _in_proj(x, pr['l0_norm'], pr['l0_w_in'], b, t, tm)

    if prompt:
        attn = _attn_prompt(q, kv, pr['sinks'], b, t)
        kv3 = kv[t - WINDOW:].reshape(WINDOW, b, 2, N_KV_A, HEAD_DIM)
        new_k = kv3[:, :, 0].transpose(1, 0, 2, 3)
        new_v = kv3[:, :, 1].transpose(1, 0, 2, 3)
        init = jnp.zeros((b, D_B_IN), F32)
    else:
        qs = q.reshape(t, b, Q_COLS).transpose(1, 0, 2)
        kvs = kv.reshape(t, b, 2 * KV_COLS).transpose(1, 0, 2)
        kn, vn = kvs[..., :KV_COLS], kvs[..., KV_COLS:]
        ck = cache_k.reshape(b, WINDOW, KV_COLS)
        cv = cache_v.reshape(b, WINDOW, KV_COLS)
        attn = _attn_sample(qs, kn, vn, ck, cv, pr['sinks'])
        attn = attn.transpose(1, 0, 2).reshape(n, Q_COLS)
        new_k = jnp.concatenate([ck[:, t:], kn], axis=1).reshape(b, WINDOW, N_KV_A, HEAD_DIM)
        new_v = jnp.concatenate([cv[:, t:], vn], axis=1).reshape(b, WINDOW, N_KV_A, HEAD_DIM)
        init = shift0
    new_shift = pb[n - b:]

    r, w, k, v, kk, nkka, bonus, g = _rwkv_prep(pb, init, pr['rw'], b, tm)
    tc = 64 if prompt else t
    s0 = jnp.zeros((b // SEQ_PER_GROUP, HEAD_DIM, HEAD_DIM // 2, 128), F32) if prompt else _state_to_scan(wkv0, b)
    o, s_fin = _wkv_scan(r, w, k, kk, nkka, v, s0, b, t, tc)
    new_wkv = _state_from_scan(s_fin, b)

    x1, xn, idx, wts, cnt = _mix0_out(o, bonus, g, attn, x, pr['rw'], pr['l0_route'], b, t, 256)
    out, pos1, pos2 = _moe(xn, idx, cnt, pr['l0_exp'])
    x1 = _moe_combine(x1, wts, out, pos1, pos2, pr['final_norm'], False, b, t)

    x2, xn, idx, wts, cnt, hr, hi = _mix1(x1, pr['s5'], pr['l1_route'], h0r, h0i, b, 256)
    out, pos1, pos2 = _moe(xn, idx, cnt, pr['l1_exp'])
    y = _moe_combine(x2, wts, out, pos1, pos2, pr['final_norm'], True, b, t)
    y = y if prompt else y.reshape(t, b, D_MODEL).transpose(1, 0, 2)
    return (y, new_k, new_v, new_shift, new_wkv,
            hr.reshape(b, S5_GROUPS, S5_P), hi.reshape(b, S5_GROUPS, S5_P))


def kernel(x_prompt, x_sample, cache_win_k, cache_win_v, state_shift, state_wkv, state_s5_re, state_s5_im,
           l0_norm_mix, l0_w_in, l0_sinks, l0_mu, l0_w0, l0_w_lora_up, l0_a0, l0_a_lora_up, l0_g_lora_up,
           l0_k_k, l0_k_a, l0_r_k, l0_ln_w, l0_ln_b, l0_w_out,
           l0_norm_ffn, l0_router_coarse, l0_bias_coarse, l0_router_fine, l0_bias_fine,
           l0_exp_gate, l0_exp_up, l0_exp_down,
           l1_norm_mix, l1_s5_a_re, l1_s5_a_im, l1_s5_log_dt, l1_s5_b_re, l1_s5_b_im, l1_s5_c_re, l1_s5_c_im,
           l1_s5_d, l1_w_out,
           l1_norm_ffn, l1_router_coarse, l1_bias_coarse, l1_router_fine, l1_bias_fine,
           l1_exp_gate, l1_exp_up, l1_exp_down,
           final_norm):
    row = lambda z: z.reshape(1, -1)
    pr = {
        'l0_norm': row(l0_norm_mix), 'l0_w_in': l0_w_in.astype(BF16), 'sinks': l0_sinks,
        'rw': {'mu': row(l0_mu), 'w0': row(l0_w0), 'w_up': l0_w_lora_up.astype(BF16), 'a0': row(l0_a0),
               'a_up': l0_a_lora_up.astype(BF16), 'g_up': l0_g_lora_up.astype(BF16), 'k_k': row(l0_k_k),
               'k_a': row(l0_k_a), 'r_k': row(l0_r_k), 'ln_w': row(l0_ln_w), 'ln_b': row(l0_ln_b),
               'head_ones': _head_block(1.0), 'head_avg': _head_block(1.0 / HEAD_DIM),
               'w_out': l0_w_out.astype(BF16)},
        'l0_route': _router_params(l0_norm_ffn, l0_router_coarse, l0_bias_coarse, l0_router_fine, l0_bias_fine),
        'l0_exp': _expert_params(l0_exp_gate, l0_exp_up, l0_exp_down),
        's5': _s5_params(l1_norm_mix, l1_s5_a_re, l1_s5_a_im, l1_s5_log_dt, l1_s5_b_re, l1_s5_b_im,
                         l1_s5_c_re, l1_s5_c_im, l1_s5_d, l1_w_out),
        'l1_route': _router_params(l1_norm_ffn, l1_router_coarse, l1_bias_coarse, l1_router_fine, l1_bias_fine),
        'l1_exp': _expert_params(l1_exp_gate, l1_exp_up, l1_exp_down),
        'final_norm': row(final_norm),
    }
    bp, bs = x_prompt.shape[0], x_sample.shape[0]
    zero_state = jnp.zeros((bp, S5_STATE), F32)
    yp, pk, pv, psh, pwkv, pre, pim = _run_group(x_prompt, None, None, None, None, zero_state, zero_state, pr)
    ys, sk, sv, ssh, swkv, sre, sim = _run_group(
        x_sample, cache_win_k, cache_win_v, state_shift, state_wkv,
        state_s5_re.reshape(bs, S5_STATE), state_s5_im.reshape(bs, S5_STATE), pr)
    return (yp, ys, pk, pv, psh, pwkv, pre, pim, sk, sv, ssh, swkv, sre, sim)
```

```python
import functools

import jax
import jax.numpy as jnp
from jax import lax
from jax.experimental import pallas as pl
from jax.experimental.pallas import tpu as pltpu

F32 = jnp.float32
BF16 = jnp.bfloat16
I32 = jnp.int32

D_MODEL = 1024
HEAD_DIM = 64
N_HEADS_A = 8
N_KV_A = 2
GQA_GROUP = 4
WINDOW = 128
Q_COLS = 512
KV_COLS = 128
D_A_IN = 768
N_HEADS_B = 8
D_B = 512
D_LORA_W = 64
D_LORA_A = 64
D_LORA_G = 128
D_B_IN = 1792
D_IN0 = 2560
RWKV_GN_EPS = 64e-5
S5_CH = 16
S5_GROUPS = 64
S5_P = 64
S5_STATE = S5_GROUPS * S5_P
N_EGROUPS = 4
EXP_PER_GROUP = 4
N_EXPERTS = 16
D_FF_E = 512
RMS_EPS = 1e-5
NEG_BIG = -1e30
PAIRS = 64
SEQ_PER_GROUP = PAIRS // N_HEADS_B
MOE_TILE = 256
ROW_CHUNKS = D_MODEL // 128
VMEM_LIMIT = 56 * 1024 * 1024


def _cparams(*sem):
    return pltpu.CompilerParams(dimension_semantics=sem, vmem_limit_bytes=VMEM_LIMIT)


def _dot(a, b):
    return jnp.dot(a, b, preferred_element_type=F32)


def _split_bf16(x):
    hi = x.astype(BF16)
    lo = (x - hi.astype(F32)).astype(BF16)
    return hi, lo


def _dot2(x, w):
    hi, lo = _split_bf16(x)
    return _dot(hi, w) + _dot(lo, w)


def _rms(x, g):
    return x * lax.rsqrt(jnp.mean(x * x, axis=-1, keepdims=True) + RMS_EPS) * g


def _sigmoid(x):
    return 1.0 / (1.0 + jnp.exp(-x))


def _store_row_tiles(ref, x):
    rows = x.shape[0]
    for c in range(ROW_CHUNKS):
        ref[pl.ds(c, rows, stride=ROW_CHUNKS), :] = x[:, c * 128:(c + 1) * 128]


def _time_major_spec(b, t, d, tm):
    if b == SEQ_PER_GROUP:
        return (b, t, d), pl.BlockSpec((b, tm // b, d), lambda i, *_: (0, i, 0))
    return (b * t, d), pl.BlockSpec((tm, d), lambda i, *_: (i, 0))


def _interleave_rows(pieces, scr):
    nb, steps = len(pieces), pieces[0].shape[0]
    chunks = pieces[0].shape[1] // 128
    for s, p in enumerate(pieces):
        for c in range(chunks):
            scr[c, pl.ds(s, steps, stride=nb), :] = p[:, c * 128:(c + 1) * 128].astype(scr.dtype)
    return jnp.concatenate([scr[c] for c in range(chunks)], axis=1)


def _deinterleave_rows(x, nb, scr):
    steps = x.shape[0] // nb
    chunks = x.shape[1] // 128
    for c in range(chunks):
        scr[c] = x[:, c * 128:(c + 1) * 128].astype(scr.dtype)
    return [jnp.concatenate([scr[c, pl.ds(s, steps, stride=nb), :] for c in range(chunks)], axis=1)
            for s in range(nb)]


def _load_time_major(x_ref, scr, b):
    if len(x_ref.shape) == 3:
        return _interleave_rows([x_ref[s] for s in range(b)], scr)
    return x_ref[...]


def _store_batch_major(y_ref, scr, y, b):
    if len(y_ref.shape) == 3:
        for s, p in enumerate(_deinterleave_rows(y, b, scr)):
            y_ref[s] = p
    else:
        y_ref[...] = y


def _load_row_tiles(ref):
    rows = ref.shape[0] // ROW_CHUNKS
    return jnp.concatenate([ref[pl.ds(c, rows, stride=ROW_CHUNKS), :] for c in range(ROW_CHUNKS)], axis=1)


def _in_proj_kernel(x_ref, g_ref, w_ref, q_ref, kv_ref, pb_ref, xs_scr, q_scr, kv_scr, *, b):
    x = _load_time_major(x_ref, xs_scr, b)
    xn = _rms(x, g_ref[...]).astype(BF16)
    q = _dot(xn, w_ref[:, :Q_COLS])
    kv = _dot(xn, w_ref[:, Q_COLS:D_A_IN])
    pb_ref[...] = _dot(xn, w_ref[:, D_A_IN:])
    if len(x_ref.shape) == 3:
        q_ref[...] = jnp.concatenate(_deinterleave_rows(q, b, q_scr), axis=1).astype(BF16)
        kv_ref[...] = jnp.concatenate(_deinterleave_rows(kv, b, kv_scr), axis=1)
    else:
        q_ref[...] = q
        kv_ref[...] = kv


def _in_proj(x, g, w_bf16, b, t, tm):
    n = b * t
    xshape, xspec = _time_major_spec(b, t, D_MODEL, tm)
    slab = len(xshape) == 3
    steps = tm // b
    if slab:
        qkv_specs = [pl.BlockSpec((steps, b * Q_COLS), lambda i: (i, 0)),
                     pl.BlockSpec((steps, b * 2 * KV_COLS), lambda i: (i, 0))]
        qkv_shapes = [jax.ShapeDtypeStruct((t, b * Q_COLS), BF16), jax.ShapeDtypeStruct((t, b * 2 * KV_COLS), F32)]
    else:
        qkv_specs = [pl.BlockSpec((tm, Q_COLS), lambda i: (i, 0)), pl.BlockSpec((tm, 2 * KV_COLS), lambda i: (i, 0))]
        qkv_shapes = [jax.ShapeDtypeStruct((n, Q_COLS), F32), jax.ShapeDtypeStruct((n, 2 * KV_COLS), F32)]
    return pl.pallas_call(
        functools.partial(_in_proj_kernel, b=b),
        grid=(n // tm,),
        in_specs=[xspec,
                  pl.BlockSpec((1, D_MODEL), lambda i: (0, 0)),
                  pl.BlockSpec((D_MODEL, D_IN0), lambda i: (0, 0))],
        out_specs=qkv_specs + [pl.BlockSpec((tm, D_B_IN), lambda i: (i, 0))],
        out_shape=qkv_shapes + [jax.ShapeDtypeStruct((n, D_B_IN), F32)],
        scratch_shapes=[pltpu.VMEM((D_MODEL // 128, tm, 128), F32), pltpu.VMEM((Q_COLS // 128, tm, 128), F32),
                        pltpu.VMEM((2 * KV_COLS // 128, tm, 128), F32)],
        compiler_params=_cparams("parallel"),
        name="in_proj",
    )(x.reshape(xshape), g, w_bf16)


def _attn_prompt_kernel(sinks_ref, q_ref, kc_ref, kp_ref, vc_ref, vp_ref, o_ref):
    j = pl.program_id(1)
    qi = lax.broadcasted_iota(I32, (WINDOW, 2 * WINDOW), 0)
    kj = lax.broadcasted_iota(I32, (WINDOW, 2 * WINDOW), 1)
    valid = jnp.logical_and(kj > qi, kj <= qi + WINDOW)
    valid = jnp.logical_and(valid, jnp.logical_or(kj >= WINDOW, j > 0))
    dist = (WINDOW + qi - kj).astype(F32)
    for n in range(N_KV_A):
        cs = slice(n * HEAD_DIM, (n + 1) * HEAD_DIM)
        kb = jnp.concatenate([kp_ref[:, cs], kc_ref[:, cs]], axis=0).astype(BF16)
        vb = jnp.concatenate([vp_ref[:, cs], vc_ref[:, cs]], axis=0).astype(BF16)
        for g in range(GQA_GROUP):
            h = n * GQA_GROUP + g
            hs = slice(h * HEAD_DIM, (h + 1) * HEAD_DIM)
            s = lax.dot_general(q_ref[:, hs], kb, (((1,), (1,)), ((), ())), preferred_element_type=F32)
            s = s * (HEAD_DIM ** -0.5) - (2.0 ** -(h + 1)) * dist
            s = jnp.where(valid, s, NEG_BIG)
            sink = sinks_ref[h]
            m = jnp.maximum(jnp.max(s, axis=1, keepdims=True), sink)
            p = jnp.exp(s - m)
            l = jnp.sum(p, axis=1, keepdims=True) + jnp.exp(sink - m)
            o = _dot(p.astype(BF16), vb) / l
            o_ref[:, hs] = o.astype(BF16)


def _attn_prompt(q2, kv2, sinks, b, t):
    prev = lambda bi, j: jnp.maximum(j - 1, 0)
    return pl.pallas_call(
        _attn_prompt_kernel,
        grid=(b, t // WINDOW),
        in_specs=[pl.BlockSpec(memory_space=pltpu.SMEM),
                  pl.BlockSpec((WINDOW, Q_COLS), lambda bi, j: (j, bi)),
                  pl.BlockSpec((WINDOW, KV_COLS), lambda bi, j: (j, 2 * bi)),
                  pl.BlockSpec((WINDOW, KV_COLS), lambda bi, j: (prev(bi, j), 2 * bi)),
                  pl.BlockSpec((WINDOW, KV_COLS), lambda bi, j: (j, 2 * bi + 1)),
                  pl.BlockSpec((WINDOW, KV_COLS), lambda bi, j: (prev(bi, j), 2 * bi + 1))],
        out_specs=pl.BlockSpec((WINDOW, Q_COLS), lambda bi, j: (j, bi)),
        out_shape=jax.ShapeDtypeStruct((t, b * Q_COLS), BF16),
        compiler_params=_cparams("parallel", "parallel"),
        name="attn_prompt",
    )(sinks, q2, kv2, kv2, kv2, kv2)


def _attn_sample_kernel(sinks_ref, q_ref, kn_ref, vn_ref, ck_ref, cv_ref, o_ref):
    bs, t = q_ref.shape[0], q_ref.shape[1]
    assert t & (t - 1) == 0
    nq, nk = GQA_GROUP * t, 2 * WINDOW
    r = lax.broadcasted_iota(I32, (nq, nk), 0)
    kj = lax.broadcasted_iota(I32, (nq, nk), 1)
    tq = jnp.bitwise_and(r, t - 1)
    valid = jnp.logical_and(kj > tq, kj <= tq + WINDOW)
    dist = (WINDOW + tq - kj).astype(F32)
    grp = jnp.right_shift(lax.broadcasted_iota(I32, (nq, 1), 0), t.bit_length() - 1)
    pad = jnp.zeros((bs, nk - WINDOW - t, HEAD_DIM), F32)
    for n in range(N_KV_A):
        cs = slice(n * HEAD_DIM, (n + 1) * HEAD_DIM)
        kb = jnp.concatenate([ck_ref[:, :, cs], kn_ref[:, :, cs], pad], axis=1).astype(BF16)
        vb = jnp.concatenate([cv_ref[:, :, cs], vn_ref[:, :, cs], pad], axis=1).astype(BF16)
        qn = jnp.concatenate([q_ref[:, :, (n * GQA_GROUP + g) * HEAD_DIM:(n * GQA_GROUP + g + 1) * HEAD_DIM]
                              for g in range(GQA_GROUP)], axis=1).astype(BF16)
        slope = jnp.zeros((nq, 1), F32)
        sink = jnp.zeros((nq, 1), F32)
        for g in range(GQA_GROUP):
            h = n * GQA_GROUP + g
            slope = jnp.where(grp == g, 2.0 ** -(h + 1), slope)
            sink = jnp.where(grp == g, sinks_ref[h], sink)
        s = jnp.einsum('bqd,bkd->bqk', qn, kb, preferred_element_type=F32)
        s = s * (HEAD_DIM ** -0.5) - (slope * dist)[None]
        s = jnp.where(valid[None], s, NEG_BIG)
        m = jnp.maximum(jnp.max(s, axis=2, keepdims=True), sink[None])
        p = jnp.exp(s - m)
        l = jnp.sum(p, axis=2, keepdims=True) + jnp.exp(sink[None] - m)
        o = jnp.einsum('bqk,bkd->bqd', p.astype(BF16), vb, preferred_element_type=F32) / l
        for g in range(GQA_GROUP):
            h = n * GQA_GROUP + g
            o_ref[:, :, h * HEAD_DIM:(h + 1) * HEAD_DIM] = o[:, g * t:(g + 1) * t, :]


def _attn_sample(q, kn, vn, ck, cv, sinks, bs=16):
    db, t = q.shape[0], q.shape[1]
    seq3 = lambda w: pl.BlockSpec((bs, t, w), lambda i: (i, 0, 0))
    cache = pl.BlockSpec((bs, WINDOW, KV_COLS), lambda i: (i, 0, 0))
    return pl.pallas_call(
        _attn_sample_kernel,
        grid=(db // bs,),
        in_specs=[pl.BlockSpec(memory_space=pltpu.SMEM), seq3(Q_COLS), seq3(KV_COLS), seq3(KV_COLS), cache, cache],
        out_specs=seq3(Q_COLS),
        out_shape=jax.ShapeDtypeStruct((db, t, Q_COLS), F32),
        compiler_params=_cparams("parallel"),
        name="attn_sample",
    )(sinks, q, kn, vn, ck, cv)


def _rwkv_prep_kernel(pb_ref, halo_ref, init_ref, mu_ref, w0_ref, wup_ref, a0_ref, aup_ref, gup_ref,
                      kk_ref, ka_ref, rk_ref, ones_ref,
                      r_o, w_o, k_o, v_o, kk_o, nkka_o, bonus_o, g_o, *, b):
    i = pl.program_id(0)
    pb = pb_ref[...]
    tm = pb.shape[0]
    halo = jnp.where(i == 0, init_ref[...], halo_ref[...])
    prev = halo if tm == b else jnp.concatenate([halo, pb[:tm - b]], axis=0)
    xs = pb + (prev - pb) * mu_ref[...]
    r = xs[:, :D_B]
    k = xs[:, D_B:2 * D_B]
    v = xs[:, 2 * D_B:3 * D_B]
    o1 = 3 * D_B
    wd = xs[:, o1:o1 + D_LORA_W]
    ad = xs[:, o1 + D_LORA_W:o1 + D_LORA_W + D_LORA_A]
    gd = xs[:, o1 + D_LORA_W + D_LORA_A:]
    z = -(w0_ref[...] + _dot(jnp.tanh(wd).astype(BF16), wup_ref[...]))
    softplus = jnp.maximum(z, 0.0) + jnp.log(1.0 + jnp.exp(-jnp.abs(z)))
    decay = jnp.exp(-jnp.exp(-softplus - 0.5))
    a = _sigmoid(a0_ref[...] + _dot(ad.astype(BF16), aup_ref[...]))
    g_o[...] = _dot(_sigmoid(gd).astype(BF16), gup_ref[...])
    ones = ones_ref[...]
    kk = k * kk_ref[...]
    kk = kk * lax.rsqrt(jnp.maximum(_dot2(kk * kk, ones), 1e-24))
    k2 = k * (1.0 + (a - 1.0) * ka_ref[...])
    bonus_o[...] = _dot2(r * k2 * rk_ref[...], ones) * v

    half = HEAD_DIM // 2
    lane8 = lax.broadcasted_iota(I32, (SEQ_PER_GROUP, 128), 1)
    low8 = lane8 < HEAD_DIM
    first_copy = jnp.bitwise_and(lax.broadcasted_iota(I32, (half, 128), 1), SEQ_PER_GROUP) == 0
    pairs = ((r, decay, r_o, w_o), (k2, kk, k_o, kk_o), (-(kk * a), v, nkka_o, None))
    for u in range(tm // SEQ_PER_GROUP):
        rows = slice(u * SEQ_PER_GROUP, (u + 1) * SEQ_PER_GROUP)
        for x, y, x_o, y_o in pairs:
            xu, yu = x[rows], y[rows]
            pieces = []
            for h in range(N_HEADS_B):
                cs = slice((h // 2) * 128, (h // 2 + 1) * 128)
                if h % 2 == 0:
                    p = jnp.where(low8, xu[:, cs], pltpu.roll(yu[:, cs], HEAD_DIM, 1))
                else:
                    p = jnp.where(low8, pltpu.roll(xu[:, cs], HEAD_DIM, 1), yu[:, cs])
                pieces += [p, p]
            tr = jnp.concatenate(pieces, axis=0).T
            x_o[u] = tr[:HEAD_DIM]
            if y_o is not None:
                y_o[u] = tr[HEAD_DIM:]
            else:
                v_o[u] = jnp.where(first_copy, tr[HEAD_DIM:HEAD_DIM + half], tr[HEAD_DIM + half:])


def _rwkv_prep(pb, init, p, b, tm):
    n = pb.shape[0]
    units = n // SEQ_PER_GROUP
    tu = tm // SEQ_PER_GROUP
    half = HEAD_DIM // 2
    row = lambda w: pl.BlockSpec((1, w), lambda i: (0, 0))
    full = lambda r, c: pl.BlockSpec((r, c), lambda i: (0, 0))
    tile = pl.BlockSpec((tm, D_B), lambda i: (i, 0))
    kspec = pl.BlockSpec((tu, HEAD_DIM, 128), lambda i: (i, 0, 0))
    vspec = pl.BlockSpec((tu, half, 128), lambda i: (i, 0, 0))
    kshape = jax.ShapeDtypeStruct((units, HEAD_DIM, 128), F32)
    halo_blocks = tm // b
    return pl.pallas_call(
        functools.partial(_rwkv_prep_kernel, b=b),
        grid=(n // tm,),
        in_specs=[pl.BlockSpec((tm, D_B_IN), lambda i: (i, 0)),
                  pl.BlockSpec((b, D_B_IN), lambda i: (jnp.maximum(i * halo_blocks - 1, 0), 0)),
                  full(b, D_B_IN), row(D_B_IN), row(D_B), full(D_LORA_W, D_B), row(D_B), full(D_LORA_A, D_B),
                  full(D_LORA_G, D_B), row(D_B), row(D_B), row(D_B), full(D_B, D_B)],
        out_specs=[kspec, kspec, kspec, vspec, kspec, kspec, tile, tile],
        out_shape=[kshape, kshape, kshape, jax.ShapeDtypeStruct((units, half, 128), F32), kshape, kshape,
                   jax.ShapeDtypeStruct((n, D_B), F32), jax.ShapeDtypeStruct((n, D_B), F32)],
        compiler_params=_cparams("arbitrary"),
        name="rwkv_prep",
    )(pb, pb, init, p['mu'], p['w0'], p['w_up'], p['a0'], p['a_up'], p['g_up'], p['k_k'], p['k_a'], p['r_k'],
      p['head_ones'])


def _wkv_scan_kernel(r_ref, w_ref, k_ref, kk_ref, nkka_ref, v_ref, s0_ref, o_ref, st_ref, s_scr):
    j = pl.program_id(1)

    @pl.when(j == 0)
    def _():
        s_scr[...] = s0_ref[0]

    tc = r_ref.shape[0]
    nsub = (HEAD_DIM // 2) // 8

    def bcast(ref, s, kx):
        return jnp.broadcast_to(ref[s, pl.ds(kx, 1), :], (8, 128))

    acc0 = [[jnp.zeros((8, 128), F32) for _ in range(2)] for _ in range(nsub)]
    for kx in range(HEAD_DIM):
        kkr = bcast(kk_ref, 0, kx)
        for i in range(nsub):
            acc0[i][kx % 2] = acc0[i][kx % 2] + s_scr[kx, 8 * i:8 * i + 8, :] * kkr

    def step(s, sa):
        nxt = jnp.minimum(s + 1, tc - 1)
        vv = [v_ref[s, 8 * i:8 * i + 8, :] for i in range(nsub)]
        oacc = [[jnp.zeros((8, 128), F32) for _ in range(2)] for _ in range(nsub)]
        nacc = [[jnp.zeros((8, 128), F32) for _ in range(2)] for _ in range(nsub)]
        for kx in range(HEAD_DIM):
            rr, wr, kr = bcast(r_ref, s, kx), bcast(w_ref, s, kx), bcast(k_ref, s, kx)
            nk, kkn = bcast(nkka_ref, s, kx), bcast(kk_ref, nxt, kx)
            for i in range(nsub):
                rows = slice(8 * i, 8 * i + 8)
                sk = s_scr[kx, rows, :] * wr + sa[i] * nk + vv[i] * kr
                s_scr[kx, rows, :] = sk
                oacc[i][kx % 2] = oacc[i][kx % 2] + sk * rr
                nacc[i][kx % 2] = nacc[i][kx % 2] + sk * kkn
        o_ref[s] = jnp.concatenate([a[0] + a[1] for a in oacc], axis=0)
        return [a[0] + a[1] for a in nacc]

    lax.fori_loop(0, tc, step, [a[0] + a[1] for a in acc0])

    @pl.when(j == pl.num_programs(1) - 1)
    def _():
        st_ref[0] = s_scr[...]


def _wkv_scan(r, w, k, kk, nkka, v, s0, b, t, tc):
    g = b // SEQ_PER_GROUP
    half = HEAD_DIM // 2
    kview = lambda z: z.reshape(t, g, HEAD_DIM, 128)
    kspec = pl.BlockSpec((tc, None, HEAD_DIM, 128), lambda gi, j: (j, gi, 0, 0))
    vspec = pl.BlockSpec((tc, None, half, 128), lambda gi, j: (j, gi, 0, 0))
    sspec = pl.BlockSpec((1, HEAD_DIM, half, 128), lambda gi, j: (gi, 0, 0, 0))
    o, st = pl.pallas_call(
        _wkv_scan_kernel,
        grid=(g, t // tc),
        in_specs=[kspec] * 5 + [vspec, sspec],
        out_specs=[vspec, sspec],
        out_shape=[jax.ShapeDtypeStruct((t, g, half, 128), F32),
                   jax.ShapeDtypeStruct((g, HEAD_DIM, half, 128), F32)],
        scratch_shapes=[pltpu.VMEM((HEAD_DIM, half, 128), F32)],
        compiler_params=_cparams("parallel", "arbitrary"),
        name="wkv_scan",
    )(kview(r), kview(w), kview(k), kview(kk), kview(nkka), v.reshape(t, g, half, 128), s0)
    return o.reshape(t * g, half, 128), st


def _state_to_scan(s, b):
    g = b // SEQ_PER_GROUP
    s = s.reshape(g, SEQ_PER_GROUP, N_HEADS_B, 2, HEAD_DIM // 2, HEAD_DIM).transpose(0, 5, 4, 2, 3, 1)
    return s.reshape(g, HEAD_DIM, HEAD_DIM // 2, 128)


def _state_from_scan(s, b):
    g = b // SEQ_PER_GROUP
    s = s.reshape(g, HEAD_DIM, HEAD_DIM // 2, N_HEADS_B, 2, SEQ_PER_GROUP).transpose(0, 5, 3, 4, 2, 1)
    return s.reshape(b, N_HEADS_B, HEAD_DIM, HEAD_DIM)


def _route_tile(x, nrm_ref, wr_ref, br_ref, cnt_scr, xn_o, idx_o, wts_o):
    tm = x.shape[0]
    xn = _rms(x, nrm_ref[...])
    hi, lo = _split_bf16(xn)
    _store_row_tiles(xn_o, xn)
    wr = wr_ref[...]
    pa = _dot(hi, wr)
    pb = _dot(lo, wr)
    lg = pa[:, 0:32] + pa[:, 32:64] + pb[:, 0:32] + pb[:, 32:64] + br_ref[...]
    col = lambda c: lg[:, c:c + 1]
    c = [col(gx) for gx in range(N_EGROUPS)]
    m = jnp.maximum(jnp.maximum(c[0], c[1]), jnp.maximum(c[2], c[3]))
    den = jnp.exp(c[0] - m) + jnp.exp(c[1] - m) + jnp.exp(c[2] - m) + jnp.exp(c[3] - m)
    pg = 1.0 / den
    gi = jnp.where(c[0] >= m, 0, jnp.where(c[1] >= m, 1, jnp.where(c[2] >= m, 2, 3))).astype(I32)
    sel = []
    for e in range(EXP_PER_GROUP):
        sel.append(jnp.where(gi == 0, col(4 + e), jnp.where(gi == 1, col(8 + e),
                                                            jnp.where(gi == 2, col(12 + e), col(16 + e)))))
    v1 = jnp.maximum(jnp.maximum(sel[0], sel[1]), jnp.maximum(sel[2], sel[3]))
    i1 = jnp.where(sel[0] >= v1, 0, jnp.where(sel[1] >= v1, 1, jnp.where(sel[2] >= v1, 2, 3))).astype(I32)
    rest = [jnp.where(i1 == e, -jnp.inf, sel[e]) for e in range(EXP_PER_GROUP)]
    v2 = jnp.maximum(jnp.maximum(rest[0], rest[1]), jnp.maximum(rest[2], rest[3]))
    i2 = jnp.where(rest[0] >= v2, 0, jnp.where(rest[1] >= v2, 1, jnp.where(rest[2] >= v2, 2, 3))).astype(I32)
    tt = jnp.exp(v2 - v1)
    w1 = pg / (1.0 + tt)
    w2 = pg * tt / (1.0 + tt)
    e1 = gi * EXP_PER_GROUP + i1
    e2 = gi * EXP_PER_GROUP + i2
    lane = lax.broadcasted_iota(I32, (tm, N_EXPERTS), 1)
    oh1 = lane == e1
    oh2 = lane == e2
    oh = jnp.where(jnp.logical_or(oh1, oh2), 1.0, 0.0)
    ri = lax.broadcasted_iota(I32, (tm, tm), 0)
    ci = lax.broadcasted_iota(I32, (tm, tm), 1)
    ltri = jnp.where(ri > ci, 1.0, 0.0).astype(BF16)
    cnt = cnt_scr[0:1, 0:N_EXPERTS]
    pre = _dot(ltri, oh.astype(BF16)) + cnt
    rank1 = jnp.sum(jnp.where(oh1, pre, 0.0), axis=1, keepdims=True)
    rank2 = jnp.sum(jnp.where(oh2, pre, 0.0), axis=1, keepdims=True)
    cnt_scr[0:1, 0:N_EXPERTS] = cnt + jnp.sum(oh, axis=0, keepdims=True)
    lw = lax.broadcasted_iota(I32, (tm, 128), 1)
    wts_o[...] = jnp.where(lw == 0, w1, jnp.where(lw == 1, w2, 0.0))
    cols = jnp.where(lw == 0, e1.astype(F32), jnp.where(lw == 1, e2.astype(F32),
                                                         jnp.where(lw == 2, rank1, jnp.where(lw == 3, rank2, 0.0))))
    idx_o[...] = cols.T[0:8, :].astype(I32)


def _route_out_specs(tm):
    return [pl.BlockSpec((tm * ROW_CHUNKS, 128), lambda i: (i, 0)),
            pl.BlockSpec((8, tm), lambda i: (0, i)),
            pl.BlockSpec((tm, 128), lambda i: (i, 0)),
            pl.BlockSpec((8, 128), lambda i: (0, 0))]


def _route_out_shapes(n):
    return [jax.ShapeDtypeStruct((n * ROW_CHUNKS, 128), F32),
            jax.ShapeDtypeStruct((8, n), I32),
            jax.ShapeDtypeStruct((n, 128), F32),
            jax.ShapeDtypeStruct((8, 128), F32)]


def _route_in_specs():
    return [pl.BlockSpec((1, D_MODEL), lambda i: (0, 0)),
            pl.BlockSpec((D_MODEL, 128), lambda i: (0, 0)),
            pl.BlockSpec((1, 32), lambda i: (0, 0))]


def _mix0_out_kernel(o_ref, bonus_ref, g_ref, attn_ref, x_ref, lnw_ref, lnb_ref, avg_ref, wo_ref,
                     nrm_ref, wr_ref, br_ref,
                     x1_o, xn_o, idx_o, wts_o, cnt_o, cnt_scr, xs_scr, at_scr, *, b):
    @pl.when(pl.program_id(0) == 0)
    def _():
        cnt_scr[...] = jnp.zeros_like(cnt_scr)

    avg = avg_ref[...]
    half = HEAD_DIM // 2
    lane8 = lax.broadcasted_iota(I32, (SEQ_PER_GROUP, 128), 1)
    unit_rows = []
    for u in range(o_ref.shape[0]):
        ot = jnp.concatenate([o_ref[u], jnp.zeros((128 - half, 128), F32)], axis=0).T
        cols = []
        for jj in range(N_HEADS_B // 2):
            q = [ot[(4 * jj + i) * SEQ_PER_GROUP:(4 * jj + i + 1) * SEQ_PER_GROUP] for i in range(4)]
            c = jnp.where(lane8 < half, q[0], pltpu.roll(q[1], half, 1))
            c = jnp.where(lane8 < 2 * half, c, pltpu.roll(q[2], 2 * half, 1))
            c = jnp.where(lane8 < 3 * half, c, pltpu.roll(q[3], 3 * half, 1))
            cols.append(c)
        unit_rows.append(jnp.concatenate(cols, axis=1))
    o = jnp.concatenate(unit_rows, axis=0)
    d = o - _dot2(o, avg)
    var = _dot2(d * d, avg)
    on = d * lax.rsqrt(var + RWKV_GN_EPS) * lnw_ref[...] + lnb_ref[...]
    rout = ((on + bonus_ref[...]) * g_ref[...]).astype(BF16)
    x = _load_time_major(x_ref, xs_scr, b)
    if len(x_ref.shape) == 3:
        attn = _interleave_rows([attn_ref[:, s * Q_COLS:(s + 1) * Q_COLS] for s in range(b)], at_scr)
    else:
        attn = attn_ref[...]
    y = _dot(attn.astype(BF16), wo_ref[:Q_COLS, :]) + _dot(rout, wo_ref[Q_COLS:, :]) + x
    x1_o[...] = y
    _route_tile(y, nrm_ref, wr_ref, br_ref, cnt_scr, xn_o, idx_o, wts_o)
    cnt_o[...] = cnt_scr[...]


def _mix0_out(o, bonus, g, attn, x, p, rp, b, t, tm):
    n = b * t
    xshape, xspec = _time_major_spec(b, t, D_MODEL, tm)
    aspec = (pl.BlockSpec((tm // b, b * Q_COLS), lambda i: (i, 0)) if len(xshape) == 3
             else pl.BlockSpec((tm, Q_COLS), lambda i: (i, 0)))
    tile = lambda w: pl.BlockSpec((tm, w), lambda i: (i, 0))
    row = lambda w: pl.BlockSpec((1, w), lambda i: (0, 0))
    full = lambda r, c: pl.BlockSpec((r, c), lambda i: (0, 0))
    return pl.pallas_call(
        functools.partial(_mix0_out_kernel, b=b),
        grid=(n // tm,),
        in_specs=[pl.BlockSpec((tm // SEQ_PER_GROUP, HEAD_DIM // 2, 128), lambda i: (i, 0, 0)),
                  tile(D_B), tile(D_B), aspec, xspec, row(D_B), row(D_B),
                  full(D_B, D_B), full(D_MODEL, D_MODEL)] + _route_in_specs(),
        out_specs=[tile(D_MODEL)] + _route_out_specs(tm),
        out_shape=[jax.ShapeDtypeStruct((n, D_MODEL), F32)] + _route_out_shapes(n),
        scratch_shapes=[pltpu.VMEM((8, 128), F32), pltpu.VMEM((D_MODEL // 128, tm, 128), F32),
                        pltpu.VMEM((Q_COLS // 128, tm, 128), F32)],
        compiler_params=_cparams("arbitrary"),
        name="mix0_out",
    )(o, bonus, g, attn, x.reshape(xshape), p['ln_w'], p['ln_b'], p['head_avg'], p['w_out'], rp['norm'], rp['w'], rp['b'])


def _row_dma_start(idx_ref, pos, r, src_hbm, dst, sem, priority):
    src_row = pl.multiple_of(idx_ref[pos] * ROW_CHUNKS, ROW_CHUNKS)
    dst_row = pl.multiple_of(r * ROW_CHUNKS, ROW_CHUNKS)
    pltpu.make_async_copy(src_hbm.at[pl.ds(src_row, ROW_CHUNKS)], dst.at[pl.ds(dst_row, ROW_CHUNKS)],
                          sem).start(priority=priority)


def _row_gather_start(idx_ref, base, n_rows, src_hbm, dst, sem):
    def body(r2, carry):
        for p in range(2):
            _row_dma_start(idx_ref, base + 2 * r2 + p, 2 * r2 + p, src_hbm, dst, sem, p)
        return carry

    lax.fori_loop(0, n_rows // 2, body, 0, unroll=4)


def _row_gather_wait(dst, sem):
    pltpu.make_async_copy(dst, dst, sem).wait()


def _expert_kernel(te_ref, nu_ref, pad_ref, p1_ref, p2_ref, x_hbm, wg_ref, wu_ref, wd_ref, o_ref,
                   src_ref, xbuf, sem, wgb, wub, wdb):
    i = pl.program_id(0)
    nu = nu_ref[0]

    @pl.when(i == 0)
    def _():
        def fill_pad(e, carry):
            start = pad_ref[2 * e]

            def body(r, c):
                src_ref[r] = r - start
                return c

            lax.fori_loop(start, pad_ref[2 * e + 1], body, 0)
            return carry

        lax.fori_loop(0, N_EXPERTS + 1, fill_pad, 0)

        def invert(n, carry):
            src_ref[p1_ref[n]] = n
            src_ref[p2_ref[n]] = n
            return carry

        lax.fori_loop(0, p1_ref.shape[0], invert, 0, unroll=8)
        _row_gather_start(src_ref, 0, MOE_TILE, x_hbm, xbuf.at[0], sem.at[0])

    new_expert = jnp.logical_or(i == 0, te_ref[i] != te_ref[jnp.maximum(i - 1, 0)])

    @pl.when(jnp.logical_and(i < nu, new_expert))
    def _():
        wgb[...] = wg_ref[0].astype(BF16)
        wub[...] = wu_ref[0].astype(BF16)
        wdb[...] = wd_ref[0].astype(BF16)

    @pl.when(i < nu)
    def _():
        cur = i % 2
        nxt = 1 - cur
        base = (i + 1) * MOE_TILE
        pieces = 2 * (D_FF_E // 256) * (D_MODEL // 256) + (D_MODEL // 256) * (D_FF_E // 256)
        per = -(-MOE_TILE // pieces)
        issued = [0]

        def start_some():
            for r in range(issued[0], min(issued[0] + per, MOE_TILE)):
                _row_dma_start(src_ref, base + r, r, x_hbm, xbuf.at[nxt], sem.at[nxt], r % 2)
            issued[0] = min(issued[0] + per, MOE_TILE)

        def block_dot(a, w_ref, c):
            acc = None
            for kt in range(a.shape[1] // 256):
                part = _dot(a[:, kt * 256:(kt + 1) * 256], w_ref[kt * 256:(kt + 1) * 256, c * 256:(c + 1) * 256])
                acc = part if acc is None else acc + part
                start_some()
            return acc

        _row_gather_wait(xbuf.at[cur], sem.at[cur])
        x = _load_row_tiles(xbuf.at[cur]).astype(BF16)
        hg = jnp.concatenate([block_dot(x, wgb, c) for c in range(D_FF_E // 256)], axis=1)
        hu = jnp.concatenate([block_dot(x, wub, c) for c in range(D_FF_E // 256)], axis=1)
        h = ((hg * _sigmoid(hg)) * hu).astype(BF16)
        for c in range(D_MODEL // 256):
            res = block_dot(h, wdb, c)
            for cc in range(2):
                o_ref[pl.ds(2 * c + cc, MOE_TILE, stride=ROW_CHUNKS), :] = res[:, cc * 128:(cc + 1) * 128]
        assert issued[0] == MOE_TILE

    @pl.when(i == nu)
    def _():
        _row_gather_wait(xbuf.at[i % 2], sem.at[i % 2])

    @pl.when(i >= nu)
    def _():
        o_ref[...] = jnp.zeros_like(o_ref)


def _experts(xn, r, pos1, pos2, tile_expert, n_used, pad_ranges, wg, wu, wd):
    wspec = lambda a, b: pl.BlockSpec((1, a, b), lambda i, te, nu, pad, p1, p2: (te[i], 0, 0))
    grid_spec = pltpu.PrefetchScalarGridSpec(
        num_scalar_prefetch=5,
        grid=(r // MOE_TILE,),
        in_specs=[pl.BlockSpec(memory_space=pl.ANY),
                  wspec(D_MODEL, D_FF_E), wspec(D_MODEL, D_FF_E), wspec(D_FF_E, D_MODEL)],
        out_specs=pl.BlockSpec((MOE_TILE * ROW_CHUNKS, 128), lambda i, te, nu, pad, p1, p2: (i, 0)),
        scratch_shapes=[pltpu.SMEM((r,), I32), pltpu.VMEM((2, MOE_TILE * ROW_CHUNKS, 128), F32),
                        pltpu.SemaphoreType.DMA((2,)),
                        pltpu.VMEM((D_MODEL, D_FF_E), BF16), pltpu.VMEM((D_MODEL, D_FF_E), BF16),
                        pltpu.VMEM((D_FF_E, D_MODEL), BF16)],
    )
    return pl.pallas_call(
        _expert_kernel,
        grid_spec=grid_spec,
        out_shape=jax.ShapeDtypeStruct((r * ROW_CHUNKS, 128), F32),
        compiler_params=_cparams("arbitrary"),
        name="moe_experts",
    )(tile_expert, n_used, pad_ranges, pos1, pos2, xn, wg, wu, wd)


def _moe(xn, idx, cnt, ep):
    n = xn.shape[0] // ROW_CHUNKS
    rows = 2 * n + N_EXPERTS * MOE_TILE
    counts = cnt[0, :N_EXPERTS].astype(I32)
    padded = ((counts + MOE_TILE - 1) // MOE_TILE) * MOE_TILE
    ends = jnp.cumsum(padded)
    offs = ends - padded
    pos1 = offs[idx[0]] + idx[2]
    pos2 = offs[idx[1]] + idx[3]
    pad_ranges = jnp.stack([jnp.append(offs + counts, ends[-1]), jnp.append(ends, ends[-1] + MOE_TILE)], axis=1)
    pad_ranges = pad_ranges.reshape(-1).astype(I32)
    n_used = (ends[-1] // MOE_TILE).astype(I32)
    starts = jnp.arange(rows // MOE_TILE, dtype=I32) * MOE_TILE
    starts = jnp.minimum(starts, ends[-1] - 1)
    tile_expert = jnp.sum((starts[:, None] >= ends[None, :]).astype(I32), axis=1)
    tile_expert = jnp.minimum(tile_expert, N_EXPERTS - 1).astype(I32)
    out = _experts(xn, rows, pos1, pos2, tile_expert, n_used.reshape(1), pad_ranges, ep['wg'], ep['wu'], ep['wd'])
    return out, pos1, pos2


def _combine_kernel(p1_ref, p2_ref, x_ref, wts_ref, nrm_ref, out_hbm, y_o, gbuf, sem, ys_scr, *, final, b):
    i = pl.program_id(0)
    tm = x_ref.shape[0]

    def start(tile, slot):
        _row_gather_start(p1_ref, tile * tm, tm, out_hbm, gbuf.at[slot, 0], sem.at[slot])
        _row_gather_start(p2_ref, tile * tm, tm, out_hbm, gbuf.at[slot, 1], sem.at[slot])

    @pl.when(i == 0)
    def _():
        start(0, 0)

    @pl.when(i + 1 < pl.num_programs(0))
    def _():
        start(i + 1, (i + 1) % 2)

    cur = i % 2
    _row_gather_wait(gbuf.at[cur], sem.at[cur])
    wts = wts_ref[...]
    y = x_ref[...] + wts[:, 0:1] * _load_row_tiles(gbuf.at[cur, 0]) + wts[:, 1:2] * _load_row_tiles(gbuf.at[cur, 1])
    if final:
        _store_batch_major(y_o, ys_scr, _rms(y, nrm_ref[...]), b)
    else:
        y_o[...] = y


def _moe_combine(x, wts, out, pos1, pos2, nrm, final, b, t, tm=256):
    n = x.shape[0]
    tile = lambda w: pl.BlockSpec((tm, w), lambda i, p1, p2: (i, 0))
    if final:
        yshape, yspec = _time_major_spec(b, t, D_MODEL, tm)
    else:
        yshape, yspec = (n, D_MODEL), tile(D_MODEL)
    grid_spec = pltpu.PrefetchScalarGridSpec(
        num_scalar_prefetch=2,
        grid=(n // tm,),
        in_specs=[tile(D_MODEL), tile(128), pl.BlockSpec((1, D_MODEL), lambda i, p1, p2: (0, 0)),
                  pl.BlockSpec(memory_space=pl.ANY)],
        out_specs=yspec,
        scratch_shapes=[pltpu.VMEM((2, 2, tm * ROW_CHUNKS, 128), F32), pltpu.SemaphoreType.DMA((2,)),
                        pltpu.VMEM((D_MODEL // 128, tm, 128), F32)],
    )
    return pl.pallas_call(
        functools.partial(_combine_kernel, final=final, b=b),
        grid_spec=grid_spec,
        out_shape=jax.ShapeDtypeStruct(yshape, F32),
        compiler_params=_cparams("arbitrary"),
        name="moe_combine_final" if final else "moe_combine",
    )(pos1, pos2, x, wts, nrm, out)


def _gelu_tanh(x):
    return 0.5 * x * (1.0 + jnp.tanh(0.7978845608028654 * (x + 0.044715 * (x * x * x))))


def _mix1_kernel(x_ref, nmix_ref, bre_ref, bim_ref, are_ref, aim_ref, cre_ref, cim_ref,
                 dsk_ref, wo_ref, h0r_ref, h0i_ref, nrm_ref, wr_ref, br_ref,
                 x2_o, xn_o, idx_o, wts_o, cnt_o, hr_o, hi_o,
                 bur, bui, hr_scr, hi_scr, cnt_scr, *, b, cw):
    @pl.when(pl.program_id(0) == 0)
    def _():
        cnt_scr[...] = jnp.zeros_like(cnt_scr)
        hr_scr[...] = h0r_ref[...]
        hi_scr[...] = h0i_ref[...]

    x = x_ref[...]
    u = _rms(x, nmix_ref[...])
    ub = u.astype(BF16)
    nblk = bre_ref.shape[0]
    kin = D_MODEL // nblk
    kst = S5_STATE // nblk
    for cb in range(nblk):
        ucb = ub[:, cb * kin:(cb + 1) * kin]
        bur[:, cb * kst:(cb + 1) * kst] = _dot(ucb, bre_ref[cb])
        bui[:, cb * kst:(cb + 1) * kst] = _dot(ucb, bim_ref[cb])

    tc = x.shape[0] // b
    for c0 in range(0, S5_STATE, cw):
        cs = slice(c0, c0 + cw)
        ar = jnp.broadcast_to(are_ref[:, cs], (b, cw))
        ai = jnp.broadcast_to(aim_ref[:, cs], (b, cw))

        def step(s, carry, cs=cs, ar=ar, ai=ai):
            hr, hi = carry
            rows = pl.ds(pl.multiple_of(s * b, b), b)
            nr = ar * hr - ai * hi + bur[rows, cs]
            ni = ar * hi + ai * hr + bui[rows, cs]
            bur[rows, cs] = nr
            bui[rows, cs] = ni
            return nr, ni

        hr, hi = lax.fori_loop(0, tc, step, (hr_scr[:, cs], hi_scr[:, cs]), unroll=True)
        hr_scr[:, cs] = hr
        hi_scr[:, cs] = hi

    ych = []
    for cb in range(nblk):
        ss = slice(cb * kst, (cb + 1) * kst)
        ych.append(_dot(bur[:, ss].astype(BF16), cre_ref[cb]) - _dot(bui[:, ss].astype(BF16), cim_ref[cb]))
    y = jnp.concatenate(ych, axis=1) + dsk_ref[...] * u
    z = _dot(_gelu_tanh(y).astype(BF16), wo_ref[...])
    x2 = x + z[:, :D_MODEL] * _sigmoid(z[:, D_MODEL:])
    x2_o[...] = x2
    _route_tile(x2, nrm_ref, wr_ref, br_ref, cnt_scr, xn_o, idx_o, wts_o)
    cnt_o[...] = cnt_scr[...]
    hr_o[...] = hr_scr[...]
    hi_o[...] = hi_scr[...]


def _mix1(x, sp, rp, h0r, h0i, b, tr):
    n = x.shape[0]
    cw = 1024 if b == 8 else 128
    tile = lambda w: pl.BlockSpec((tr, w), lambda i: (i, 0))
    row = lambda w: pl.BlockSpec((1, w), lambda i: (0, 0))
    full = lambda *s: pl.BlockSpec(s, lambda i: (0,) * len(s))
    nblk = sp['b_re'].shape[0]
    return pl.pallas_call(
        functools.partial(_mix1_kernel, b=b, cw=cw),
        grid=(n // tr,),
        in_specs=[tile(D_MODEL), row(D_MODEL),
                  full(nblk, D_MODEL // nblk, S5_STATE // nblk), full(nblk, D_MODEL // nblk, S5_STATE // nblk),
                  row(S5_STATE), row(S5_STATE),
                  full(nblk, S5_STATE // nblk, D_MODEL // nblk), full(nblk, S5_STATE // nblk, D_MODEL // nblk),
                  row(D_MODEL), full(D_MODEL, 2 * D_MODEL), full(b, S5_STATE), full(b, S5_STATE)] + _route_in_specs(),
        out_specs=[tile(D_MODEL)] + _route_out_specs(tr) + [full(b, S5_STATE), full(b, S5_STATE)],
        out_shape=[jax.ShapeDtypeStruct((n, D_MODEL), F32)] + _route_out_shapes(n)
                  + [jax.ShapeDtypeStruct((b, S5_STATE), F32)] * 2,
        scratch_shapes=[pltpu.VMEM((tr, S5_STATE), F32), pltpu.VMEM((tr, S5_STATE), F32),
                        pltpu.VMEM((b, S5_STATE), F32), pltpu.VMEM((b, S5_STATE), F32),
                        pltpu.VMEM((8, 128), F32)],
        compiler_params=_cparams("arbitrary"),
        name="mix1",
    )(x, sp['norm'], sp['b_re'], sp['b_im'], sp['a_re'], sp['a_im'], sp['c_re'], sp['c_im'],
      sp['d'], sp['w_out'], h0r, h0i, rp['norm'], rp['w'], rp['b'])


def _router_params(norm, w_rc, b_rc, w_rf, b_rf):
    w = jnp.concatenate([w_rc, w_rf.reshape(D_MODEL, N_EXPERTS), jnp.zeros((D_MODEL, 12), F32)], axis=1)
    hi = w.astype(BF16)
    lo = (w - hi.astype(F32)).astype(BF16)
    wcat = jnp.concatenate([hi, lo, jnp.zeros((D_MODEL, 64), BF16)], axis=1)
    bias = jnp.concatenate([b_rc, b_rf.reshape(-1), jnp.zeros((12,), F32)]).reshape(1, 32)
    return {'norm': norm.reshape(1, D_MODEL), 'w': wcat, 'b': bias}


def _expert_params(wg, wu, wd):
    return {'wg': wg, 'wu': wu, 'wd': wd}


def _s5_params(norm, a_re, a_im, log_dt, b_re, b_im, c_re, c_im, d_skip, w_out, nblk=8):
    dt = jnp.exp(log_dt)
    mag = jnp.exp(dt * a_re)
    ab_re, ab_im = mag * jnp.cos(dt * a_im), mag * jnp.sin(dt * a_im)
    den = a_re * a_re + a_im * a_im
    f_re = ((ab_re - 1.0) * a_re + ab_im * a_im) / den
    f_im = (ab_im * a_re - (ab_re - 1.0) * a_im) / den
    bb_re = f_re[..., None] * b_re - f_im[..., None] * b_im
    bb_im = f_re[..., None] * b_im + f_im[..., None] * b_re
    gpb = S5_GROUPS // nblk
    eye = jnp.eye(gpb, dtype=F32)

    def in_blocks(bb):
        bb = bb.reshape(nblk, gpb, S5_P, S5_CH)
        w = jnp.einsum('ngpc,gh->ngchp', bb, eye)
        return w.reshape(nblk, gpb * S5_CH, gpb * S5_P).astype(BF16)

    def out_blocks(cc):
        cc = cc.reshape(nblk, gpb, S5_CH, S5_P)
        w = jnp.einsum('ngcp,gh->ngphc', cc, eye)
        return w.reshape(nblk, gpb * S5_P, gpb * S5_CH).astype(BF16)

    return {'norm': norm.reshape(1, D_MODEL), 'b_re': in_blocks(bb_re), 'b_im': in_blocks(bb_im),
            'a_re': ab_re.reshape(1, S5_STATE), 'a_im': ab_im.reshape(1, S5_STATE),
            'c_re': out_blocks(c_re), 'c_im': out_blocks(c_im), 'd': d_skip.reshape(1, D_MODEL),
            'w_out': w_out.astype(BF16)}


def _head_block(value):
    hid = jnp.arange(D_B, dtype=I32) // HEAD_DIM
    return jnp.where(hid[:, None] == hid[None, :], value, 0.0).astype(BF16)


def _run_group(x, cache_k, cache_v, shift0, wkv0, h0r, h0i, pr):
    b, t = x.shape[0], x.shape[1]
    n = b * t
    prompt = cache_k is None
    tm = 256 if prompt else 128
    if not prompt:
        x = x.transpose(1, 0, 2)
    q, kv, pb = _in_proj(x, pr['l0_norm'], pr['l0_w_in'], b, t, tm)

    if prompt:
        attn = _attn_prompt(q, kv, pr['sinks'], b, t)
        kv3 = kv[t - WINDOW:].reshape(WINDOW, b, 2, N_KV_A, HEAD_DIM)
        new_k = kv3[:, :, 0].transpose(1, 0, 2, 3)
        new_v = kv3[:, :, 1].transpose(1, 0, 2, 3)
        init = jnp.zeros((b, D_B_IN), F32)
    else:
        qs = q.reshape(t, b, Q_COLS).transpose(1, 0, 2)
        kvs = kv.reshape(t, b, 2 * KV_COLS).transpose(1, 0, 2)
        kn, vn = kvs[..., :KV_COLS], kvs[..., KV_COLS:]
        ck = cache_k.reshape(b, WINDOW, KV_COLS)
        cv = cache_v.reshape(b, WINDOW, KV_COLS)
        attn = _attn_sample(qs, kn, vn, ck, cv, pr['sinks'])
        attn = attn.transpose(1, 0, 2).reshape(n, Q_COLS)
        new_k = jnp.concatenate([ck[:, t:], kn], axis=1).reshape(b, WINDOW, N_KV_A, HEAD_DIM)
        new_v = jnp.concatenate([cv[:, t:], vn], axis=1).reshape(b, WINDOW, N_KV_A, HEAD_DIM)
        init = shift0
    new_shift = pb[n - b:]

    r, w, k, v, kk, nkka, bonus, g = _rwkv_prep(pb, init, pr['rw'], b, tm)
    tc = 64 if prompt else t
    s0 = jnp.zeros((b // SEQ_PER_GROUP, HEAD_DIM, HEAD_DIM // 2, 128), F32) if prompt else _state_to_scan(wkv0, b)
    o, s_fin = _wkv_scan(r, w, k, kk, nkka, v, s0, b, t, tc)
    new_wkv = _state_from_scan(s_fin, b)

    x1, xn, idx, wts, cnt = _mix0_out(o, bonus, g, attn, x, pr['rw'], pr['l0_route'], b, t, 256)
    out, pos1, pos2 = _moe(xn, idx, cnt, pr['l0_exp'])
    x1 = _moe_combine(x1, wts, out, pos1, pos2, pr['final_norm'], False, b, t)

    x2, xn, idx, wts, cnt, hr, hi = _mix1(x1, pr['s5'], pr['l1_route'], h0r, h0i, b, 256)
    out, pos1, pos2 = _moe(xn, idx, cnt, pr['l1_exp'])
    y = _moe_combine(x2, wts, out, pos1, pos2, pr['final_norm'], True, b, t)
    y = y if prompt else y.reshape(t, b, D_MODEL).transpose(1, 0, 2)
    return (y, new_k, new_v, new_shift, new_wkv,
            hr.reshape(b, S5_GROUPS, S5_P), hi.reshape(b, S5_GROUPS, S5_P))


def kernel(x_prompt, x_sample, cache_win_k, cache_win_v, state_shift, state_wkv, state_s5_re, state_s5_im,
           l0_norm_mix, l0_w_in, l0_sinks, l0_mu, l0_w0, l0_w_lora_up, l0_a0, l0_a_lora_up, l0_g_lora_up,
           l0_k_k, l0_k_a, l0_r_k, l0_ln_w, l0_ln_b, l0_w_out,
           l0_norm_ffn, l0_router_coarse, l0_bias_coarse, l0_router_fine, l0_bias_fine,
           l0_exp_gate, l0_exp_up, l0_exp_down,
           l1_norm_mix, l1_s5_a_re, l1_s5_a_im, l1_s5_log_dt, l1_s5_b_re, l1_s5_b_im, l1_s5_c_re, l1_s5_c_im,
           l1_s5_d, l1_w_out,
           l1_norm_ffn, l1_router_coarse, l1_bias_coarse, l1_router_fine, l1_bias_fine,
           l1_exp_gate, l1_exp_up, l1_exp_down,
           final_norm):
    row = lambda z: z.reshape(1, -1)
    pr = {
        'l0_norm': row(l0_norm_mix), 'l0_w_in': l0_w_in.astype(BF16), 'sinks': l0_sinks,
        'rw': {'mu': row(l0_mu), 'w0': row(l0_w0), 'w_up': l0_w_lora_up.astype(BF16), 'a0': row(l0_a0),
               'a_up': l0_a_lora_up.astype(BF16), 'g_up': l0_g_lora_up.astype(BF16), 'k_k': row(l0_k_k),
               'k_a': row(l0_k_a), 'r_k': row(l0_r_k), 'ln_w': row(l0_ln_w), 'ln_b': row(l0_ln_b),
               'head_ones': _head_block(1.0), 'head_avg': _head_block(1.0 / HEAD_DIM),
               'w_out': l0_w_out.astype(BF16)},
        'l0_route': _router_params(l0_norm_ffn, l0_router_coarse, l0_bias_coarse, l0_router_fine, l0_bias_fine),
        'l0_exp': _expert_params(l0_exp_gate, l0_exp_up, l0_exp_down),
        's5': _s5_params(l1_norm_mix, l1_s5_a_re, l1_s5_a_im, l1_s5_log_dt, l1_s5_b_re, l1_s5_b_im,
                         l1_s5_c_re, l1_s5_c_im, l1_s5_d, l1_w_out),
        'l1_route': _router_params(l1_norm_ffn, l1_router_coarse, l1_bias_coarse, l1_router_fine, l1_bias_fine),
        'l1_exp': _expert_params(l1_exp_gate, l1_exp_up, l1_exp_down),
        'final_norm': row(final_norm),
    }
    bp, bs = x_prompt.shape[0], x_sample.shape[0]
    zero_state = jnp.zeros((bp, S5_STATE), F32)
    yp, pk, pv, psh, pwkv, pre, pim = _run_group(x_prompt, None, None, None, None, zero_state, zero_state, pr)
    ys, sk, sv, ssh, swkv, sre, sim = _run_group(
        x_sample, cache_win_k, cache_win_v, state_shift, state_wkv,
        state_s5_re.reshape(bs, S5_STATE), state_s5_im.reshape(bs, S5_STATE), pr)
    return (yp, ys, pk, pv, psh, pwkv, pre, pim, sk, sv, ssh, swkv, sre, sim)
```

```python
import functools

import jax
import jax.numpy as jnp
from jax import lax
from jax.experimental import pallas as pl
from jax.experimental.pallas import tpu as pltpu

F32 = jnp.float32
BF16 = jnp.bfloat16
I32 = jnp.int32

D_MODEL = 1024
HEAD_DIM = 64
N_HEADS_A = 8
N_KV_A = 2
GQA_GROUP = 4
WINDOW = 128
Q_COLS = 512
KV_COLS = 128
D_A_IN = 768
N_HEADS_B = 8
D_B = 512
D_LORA_W = 64
D_LORA_A = 64
D_LORA_G = 128
D_B_IN = 1792
D_IN0 = 2560
RWKV_GN_EPS = 64e-5
S5_CH = 16
S5_GROUPS = 64
S5_P = 64
S5_STATE = S5_GROUPS * S5_P
N_EGROUPS = 4
EXP_PER_GROUP = 4
N_EXPERTS = 16
D_FF_E = 512
RMS_EPS = 1e-5
NEG_BIG = -1e30
PAIRS = 64
SEQ_PER_GROUP = PAIRS // N_HEADS_B
MOE_TILE = 512
ROW_CHUNKS = D_MODEL // 128
VMEM_LIMIT = 56 * 1024 * 1024


def _cparams(*sem):
    return pltpu.CompilerParams(dimension_semantics=sem, vmem_limit_bytes=VMEM_LIMIT)


def _dot(a, b):
    return jnp.dot(a, b, preferred_element_type=F32)


def _split_bf16(x):
    hi = x.astype(BF16)
    lo = (x - hi.astype(F32)).astype(BF16)
    return hi, lo


def _dot2(x, w):
    hi, lo = _split_bf16(x)
    return _dot(hi, w) + _dot(lo, w)


def _rms(x, g):
    return x * lax.rsqrt(jnp.mean(x * x, axis=-1, keepdims=True) + RMS_EPS) * g


def _sigmoid(x):
    return 1.0 / (1.0 + jnp.exp(-x))


def _store_row_tiles(ref, x):
    rows = x.shape[0]
    for c in range(ROW_CHUNKS):
        ref[pl.ds(c, rows, stride=ROW_CHUNKS), :] = x[:, c * 128:(c + 1) * 128]


def _time_major_spec(b, t, d, tm):
    if b == SEQ_PER_GROUP:
        return (b, t, d), pl.BlockSpec((b, tm // b, d), lambda i, *_: (0, i, 0))
    return (b * t, d), pl.BlockSpec((tm, d), lambda i, *_: (i, 0))


def _interleave_rows(pieces, scr):
    nb, steps = len(pieces), pieces[0].shape[0]
    chunks = pieces[0].shape[1] // 128
    for s, p in enumerate(pieces):
        for c in range(chunks):
            scr[c, pl.ds(s, steps, stride=nb), :] = p[:, c * 128:(c + 1) * 128].astype(scr.dtype)
    return jnp.concatenate([scr[c] for c in range(chunks)], axis=1)


def _deinterleave_rows(x, nb, scr):
    steps = x.shape[0] // nb
    chunks = x.shape[1] // 128
    for c in range(chunks):
        scr[c] = x[:, c * 128:(c + 1) * 128].astype(scr.dtype)
    return [jnp.concatenate([scr[c, pl.ds(s, steps, stride=nb), :] for c in range(chunks)], axis=1)
            for s in range(nb)]


def _load_time_major(x_ref, scr, b):
    if len(x_ref.shape) == 3:
        return _interleave_rows([x_ref[s] for s in range(b)], scr)
    return x_ref[...]


def _store_batch_major(y_ref, scr, y, b):
    if len(y_ref.shape) == 3:
        for s, p in enumerate(_deinterleave_rows(y, b, scr)):
            y_ref[s] = p
    else:
        y_ref[...] = y


def _load_row_tiles(ref):
    rows = ref.shape[0] // ROW_CHUNKS
    return jnp.concatenate([ref[pl.ds(c, rows, stride=ROW_CHUNKS), :] for c in range(ROW_CHUNKS)], axis=1)


def _in_proj_kernel(x_ref, g_ref, w_ref, q_ref, kv_ref, pb_ref, xs_scr, q_scr, kv_scr, *, b):
    x = _load_time_major(x_ref, xs_scr, b)
    xn = _rms(x, g_ref[...]).astype(BF16)
    q = _dot(xn, w_ref[:, :Q_COLS])
    kv = _dot(xn, w_ref[:, Q_COLS:D_A_IN])
    pb_ref[...] = _dot(xn, w_ref[:, D_A_IN:])
    if len(x_ref.shape) == 3:
        q_ref[...] = jnp.concatenate(_deinterleave_rows(q, b, q_scr), axis=1).astype(BF16)
        kv_ref[...] = jnp.concatenate(_deinterleave_rows(kv, b, kv_scr), axis=1)
    else:
        q_ref[...] = q
        kv_ref[...] = kv


def _in_proj(x, g, w_bf16, b, t, tm):
    n = b * t
    xshape, xspec = _time_major_spec(b, t, D_MODEL, tm)
    slab = len(xshape) == 3
    steps = tm // b
    if slab:
        qkv_specs = [pl.BlockSpec((steps, b * Q_COLS), lambda i: (i, 0)),
                     pl.BlockSpec((steps, b * 2 * KV_COLS), lambda i: (i, 0))]
        qkv_shapes = [jax.ShapeDtypeStruct((t, b * Q_COLS), BF16), jax.ShapeDtypeStruct((t, b * 2 * KV_COLS), F32)]
    else:
        qkv_specs = [pl.BlockSpec((tm, Q_COLS), lambda i: (i, 0)), pl.BlockSpec((tm, 2 * KV_COLS), lambda i: (i, 0))]
        qkv_shapes = [jax.ShapeDtypeStruct((n, Q_COLS), F32), jax.ShapeDtypeStruct((n, 2 * KV_COLS), F32)]
    return pl.pallas_call(
        functools.partial(_in_proj_kernel, b=b),
        grid=(n // tm,),
        in_specs=[xspec,
                  pl.BlockSpec((1, D_MODEL), lambda i: (0, 0)),
                  pl.BlockSpec((D_MODEL, D_IN0), lambda i: (0, 0))],
        out_specs=qkv_specs + [pl.BlockSpec((tm, D_B_IN), lambda i: (i, 0))],
        out_shape=qkv_shapes + [jax.ShapeDtypeStruct((n, D_B_IN), F32)],
        scratch_shapes=[pltpu.VMEM((D_MODEL // 128, tm, 128), F32), pltpu.VMEM((Q_COLS // 128, tm, 128), F32),
                        pltpu.VMEM((2 * KV_COLS // 128, tm, 128), F32)],
        compiler_params=_cparams("parallel"),
        name="in_proj",
    )(x.reshape(xshape), g, w_bf16)


def _attn_prompt_kernel(sinks_ref, q_ref, kc_ref, kp_ref, vc_ref, vp_ref, o_ref):
    j = pl.program_id(1)
    qi = lax.broadcasted_iota(I32, (WINDOW, 2 * WINDOW), 0)
    kj = lax.broadcasted_iota(I32, (WINDOW, 2 * WINDOW), 1)
    valid = jnp.logical_and(kj > qi, kj <= qi + WINDOW)
    valid = jnp.logical_and(valid, jnp.logical_or(kj >= WINDOW, j > 0))
    dist = (WINDOW + qi - kj).astype(F32)
    for n in range(N_KV_A):
        cs = slice(n * HEAD_DIM, (n + 1) * HEAD_DIM)
        kb = jnp.concatenate([kp_ref[:, cs], kc_ref[:, cs]], axis=0).astype(BF16)
        vb = jnp.concatenate([vp_ref[:, cs], vc_ref[:, cs]], axis=0).astype(BF16)
        for g in range(GQA_GROUP):
            h = n * GQA_GROUP + g
            hs = slice(h * HEAD_DIM, (h + 1) * HEAD_DIM)
            s = lax.dot_general(q_ref[:, hs], kb, (((1,), (1,)), ((), ())), preferred_element_type=F32)
            s = s * (HEAD_DIM ** -0.5) - (2.0 ** -(h + 1)) * dist
            s = jnp.where(valid, s, NEG_BIG)
            sink = sinks_ref[h]
            m = jnp.maximum(jnp.max(s, axis=1, keepdims=True), sink)
            p = jnp.exp(s - m)
            l = jnp.sum(p, axis=1, keepdims=True) + jnp.exp(sink - m)
            o = _dot(p.astype(BF16), vb) / l
            o_ref[:, hs] = o.astype(BF16)


def _attn_prompt(q2, kv2, sinks, b, t):
    prev = lambda bi, j: jnp.maximum(j - 1, 0)
    return pl.pallas_call(
        _attn_prompt_kernel,
        grid=(b, t // WINDOW),
        in_specs=[pl.BlockSpec(memory_space=pltpu.SMEM),
                  pl.BlockSpec((WINDOW, Q_COLS), lambda bi, j: (j, bi)),
                  pl.BlockSpec((WINDOW, KV_COLS), lambda bi, j: (j, 2 * bi)),
                  pl.BlockSpec((WINDOW, KV_COLS), lambda bi, j: (prev(bi, j), 2 * bi)),
                  pl.BlockSpec((WINDOW, KV_COLS), lambda bi, j: (j, 2 * bi + 1)),
                  pl.BlockSpec((WINDOW, KV_COLS), lambda bi, j: (prev(bi, j), 2 * bi + 1))],
        out_specs=pl.BlockSpec((WINDOW, Q_COLS), lambda bi, j: (j, bi)),
        out_shape=jax.ShapeDtypeStruct((t, b * Q_COLS), BF16),
        compiler_params=_cparams("parallel", "parallel"),
        name="attn_prompt",
    )(sinks, q2, kv2, kv2, kv2, kv2)


def _attn_sample_kernel(sinks_ref, q_ref, kn_ref, vn_ref, ck_ref, cv_ref, o_ref):
    bs, t = q_ref.shape[0], q_ref.shape[1]
    assert t & (t - 1) == 0
    nq, nk = GQA_GROUP * t, 2 * WINDOW
    r = lax.broadcasted_iota(I32, (nq, nk), 0)
    kj = lax.broadcasted_iota(I32, (nq, nk), 1)
    tq = jnp.bitwise_and(r, t - 1)
    valid = jnp.logical_and(kj > tq, kj <= tq + WINDOW)
    dist = (WINDOW + tq - kj).astype(F32)
    grp = jnp.right_shift(lax.broadcasted_iota(I32, (nq, 1), 0), t.bit_length() - 1)
    pad = jnp.zeros((bs, nk - WINDOW - t, HEAD_DIM), F32)
    for n in range(N_KV_A):
        cs = slice(n * HEAD_DIM, (n + 1) * HEAD_DIM)
        kb = jnp.concatenate([ck_ref[:, :, cs], kn_ref[:, :, cs], pad], axis=1).astype(BF16)
        vb = jnp.concatenate([cv_ref[:, :, cs], vn_ref[:, :, cs], pad], axis=1).astype(BF16)
        qn = jnp.concatenate([q_ref[:, :, (n * GQA_GROUP + g) * HEAD_DIM:(n * GQA_GROUP + g + 1) * HEAD_DIM]
                              for g in range(GQA_GROUP)], axis=1).astype(BF16)
        slope = jnp.zeros((nq, 1), F32)
        sink = jnp.zeros((nq, 1), F32)
        for g in range(GQA_GROUP):
            h = n * GQA_GROUP + g
            slope = jnp.where(grp == g, 2.0 ** -(h + 1), slope)
            sink = jnp.where(grp == g, sinks_ref[h], sink)
        s = jnp.einsum('bqd,bkd->bqk', qn, kb, preferred_element_type=F32)
        s = s * (HEAD_DIM ** -0.5) - (slope * dist)[None]
        s = jnp.where(valid[None], s, NEG_BIG)
        m = jnp.maximum(jnp.max(s, axis=2, keepdims=True), sink[None])
        p = jnp.exp(s - m)
        l = jnp.sum(p, axis=2, keepdims=True) + jnp.exp(sink[None] - m)
        o = jnp.einsum('bqk,bkd->bqd', p.astype(BF16), vb, preferred_element_type=F32) / l
        for g in range(GQA_GROUP):
            h = n * GQA_GROUP + g
            o_ref[:, :, h * HEAD_DIM:(h + 1) * HEAD_DIM] = o[:, g * t:(g + 1) * t, :]


def _attn_sample(q, kn, vn, ck, cv, sinks, bs=16):
    db, t = q.shape[0], q.shape[1]
    seq3 = lambda w: pl.BlockSpec((bs, t, w), lambda i: (i, 0, 0))
    cache = pl.BlockSpec((bs, WINDOW, KV_COLS), lambda i: (i, 0, 0))
    return pl.pallas_call(
        _attn_sample_kernel,
        grid=(db // bs,),
        in_specs=[pl.BlockSpec(memory_space=pltpu.SMEM), seq3(Q_COLS), seq3(KV_COLS), seq3(KV_COLS), cache, cache],
        out_specs=seq3(Q_COLS),
        out_shape=jax.ShapeDtypeStruct((db, t, Q_COLS), F32),
        compiler_params=_cparams("parallel"),
        name="attn_sample",
    )(sinks, q, kn, vn, ck, cv)


def _rwkv_prep_kernel(pb_ref, halo_ref, init_ref, mu_ref, w0_ref, wup_ref, a0_ref, aup_ref, gup_ref,
                      kk_ref, ka_ref, rk_ref, ones_ref,
                      r_o, w_o, k_o, v_o, kk_o, nkka_o, bonus_o, g_o, *, b):
    i = pl.program_id(0)
    pb = pb_ref[...]
    tm = pb.shape[0]
    halo = jnp.where(i == 0, init_ref[...], halo_ref[...])
    prev = halo if tm == b else jnp.concatenate([halo, pb[:tm - b]], axis=0)
    xs = pb + (prev - pb) * mu_ref[...]
    r = xs[:, :D_B]
    k = xs[:, D_B:2 * D_B]
    v = xs[:, 2 * D_B:3 * D_B]
    o1 = 3 * D_B
    wd = xs[:, o1:o1 + D_LORA_W]
    ad = xs[:, o1 + D_LORA_W:o1 + D_LORA_W + D_LORA_A]
    gd = xs[:, o1 + D_LORA_W + D_LORA_A:]
    z = -(w0_ref[...] + _dot(jnp.tanh(wd).astype(BF16), wup_ref[...]))
    softplus = jnp.maximum(z, 0.0) + jnp.log(1.0 + jnp.exp(-jnp.abs(z)))
    decay = jnp.exp(-jnp.exp(-softplus - 0.5))
    a = _sigmoid(a0_ref[...] + _dot(ad.astype(BF16), aup_ref[...]))
    g_o[...] = _dot(_sigmoid(gd).astype(BF16), gup_ref[...])
    ones = ones_ref[...]
    kk = k * kk_ref[...]
    kk = kk * lax.rsqrt(jnp.maximum(_dot2(kk * kk, ones), 1e-24))
    k2 = k * (1.0 + (a - 1.0) * ka_ref[...])
    bonus_o[...] = _dot2(r * k2 * rk_ref[...], ones) * v

    half = HEAD_DIM // 2
    lane8 = lax.broadcasted_iota(I32, (SEQ_PER_GROUP, 128), 1)
    low8 = lane8 < HEAD_DIM
    first_copy = jnp.bitwise_and(lax.broadcasted_iota(I32, (half, 128), 1), SEQ_PER_GROUP) == 0
    pairs = ((r, decay, r_o, w_o), (k2, kk, k_o, kk_o), (-(kk * a), v, nkka_o, None))
    for u in range(tm // SEQ_PER_GROUP):
        rows = slice(u * SEQ_PER_GROUP, (u + 1) * SEQ_PER_GROUP)
        for x, y, x_o, y_o in pairs:
            xu, yu = x[rows], y[rows]
            pieces = []
            for h in range(N_HEADS_B):
                cs = slice((h // 2) * 128, (h // 2 + 1) * 128)
                if h % 2 == 0:
                    p = jnp.where(low8, xu[:, cs], pltpu.roll(yu[:, cs], HEAD_DIM, 1))
                else:
                    p = jnp.where(low8, pltpu.roll(xu[:, cs], HEAD_DIM, 1), yu[:, cs])
                pieces += [p, p]
            tr = jnp.concatenate(pieces, axis=0).T
            x_o[u] = tr[:HEAD_DIM]
            if y_o is not None:
                y_o[u] = tr[HEAD_DIM:]
            else:
                v_o[u] = jnp.where(first_copy, tr[HEAD_DIM:HEAD_DIM + half], tr[HEAD_DIM + half:])


def _rwkv_prep(pb, init, p, b, tm):
    n = pb.shape[0]
    units = n // SEQ_PER_GROUP
    tu = tm // SEQ_PER_GROUP
    half = HEAD_DIM // 2
    row = lambda w: pl.BlockSpec((1, w), lambda i: (0, 0))
    full = lambda r, c: pl.BlockSpec((r, c), lambda i: (0, 0))
    tile = pl.BlockSpec((tm, D_B), lambda i: (i, 0))
    kspec = pl.BlockSpec((tu, HEAD_DIM, 128), lambda i: (i, 0, 0))
    vspec = pl.BlockSpec((tu, half, 128), lambda i: (i, 0, 0))
    kshape = jax.ShapeDtypeStruct((units, HEAD_DIM, 128), F32)
    halo_blocks = tm // b
    return pl.pallas_call(
        functools.partial(_rwkv_prep_kernel, b=b),
        grid=(n // tm,),
        in_specs=[pl.BlockSpec((tm, D_B_IN), lambda i: (i, 0)),
                  pl.BlockSpec((b, D_B_IN), lambda i: (jnp.maximum(i * halo_blocks - 1, 0), 0)),
                  full(b, D_B_IN), row(D_B_IN), row(D_B), full(D_LORA_W, D_B), row(D_B), full(D_LORA_A, D_B),
                  full(D_LORA_G, D_B), row(D_B), row(D_B), row(D_B), full(D_B, D_B)],
        out_specs=[kspec, kspec, kspec, vspec, kspec, kspec, tile, tile],
        out_shape=[kshape, kshape, kshape, jax.ShapeDtypeStruct((units, half, 128), F32), kshape, kshape,
                   jax.ShapeDtypeStruct((n, D_B), F32), jax.ShapeDtypeStruct((n, D_B), F32)],
        compiler_params=_cparams("arbitrary"),
        name="rwkv_prep",
    )(pb, pb, init, p['mu'], p['w0'], p['w_up'], p['a0'], p['a_up'], p['g_up'], p['k_k'], p['k_a'], p['r_k'],
      p['head_ones'])


def _wkv_scan_kernel(r_ref, w_ref, k_ref, kk_ref, nkka_ref, v_ref, s0_ref, o_ref, st_ref, s_scr):
    j = pl.program_id(1)

    @pl.when(j == 0)
    def _():
        s_scr[...] = s0_ref[0]

    tc = r_ref.shape[0]
    nsub = (HEAD_DIM // 2) // 8

    def bcast(ref, s, kx):
        return jnp.broadcast_to(ref[s, pl.ds(kx, 1), :], (8, 128))

    acc0 = [[jnp.zeros((8, 128), F32) for _ in range(2)] for _ in range(nsub)]
    for kx in range(HEAD_DIM):
        kkr = bcast(kk_ref, 0, kx)
        for i in range(nsub):
            acc0[i][kx % 2] = acc0[i][kx % 2] + s_scr[kx, 8 * i:8 * i + 8, :] * kkr

    def step(s, sa):
        nxt = jnp.minimum(s + 1, tc - 1)
        vv = [v_ref[s, 8 * i:8 * i + 8, :] for i in range(nsub)]
        oacc = [[jnp.zeros((8, 128), F32) for _ in range(2)] for _ in range(nsub)]
        nacc = [[jnp.zeros((8, 128), F32) for _ in range(2)] for _ in range(nsub)]
        for kx in range(HEAD_DIM):
            rr, wr, kr = bcast(r_ref, s, kx), bcast(w_ref, s, kx), bcast(k_ref, s, kx)
            nk, kkn = bcast(nkka_ref, s, kx), bcast(kk_ref, nxt, kx)
            for i in range(nsub):
                rows = slice(8 * i, 8 * i + 8)
                sk = s_scr[kx, rows, :] * wr + sa[i] * nk + vv[i] * kr
                s_scr[kx, rows, :] = sk
                oacc[i][kx % 2] = oacc[i][kx % 2] + sk * rr
                nacc[i][kx % 2] = nacc[i][kx % 2] + sk * kkn
        o_ref[s] = jnp.concatenate([a[0] + a[1] for a in oacc], axis=0)
        return [a[0] + a[1] for a in nacc]

    lax.fori_loop(0, tc, step, [a[0] + a[1] for a in acc0])

    @pl.when(j == pl.num_programs(1) - 1)
    def _():
        st_ref[0] = s_scr[...]


def _wkv_scan(r, w, k, kk, nkka, v, s0, b, t, tc):
    g = b // SEQ_PER_GROUP
    half = HEAD_DIM // 2
    kview = lambda z: z.reshape(t, g, HEAD_DIM, 128)
    kspec = pl.BlockSpec((tc, None, HEAD_DIM, 128), lambda gi, j: (j, gi, 0, 0))
    vspec = pl.BlockSpec((tc, None, half, 128), lambda gi, j: (j, gi, 0, 0))
    sspec = pl.BlockSpec((1, HEAD_DIM, half, 128), lambda gi, j: (gi, 0, 0, 0))
    o, st = pl.pallas_call(
        _wkv_scan_kernel,
        grid=(g, t // tc),
        in_specs=[kspec] * 5 + [vspec, sspec],
        out_specs=[vspec, sspec],
        out_shape=[jax.ShapeDtypeStruct((t, g, half, 128), F32),
                   jax.ShapeDtypeStruct((g, HEAD_DIM, half, 128), F32)],
        scratch_shapes=[pltpu.VMEM((HEAD_DIM, half, 128), F32)],
        compiler_params=_cparams("parallel", "arbitrary"),
        name="wkv_scan",
    )(kview(r), kview(w), kview(k), kview(kk), kview(nkka), v.reshape(t, g, half, 128), s0)
    return o.reshape(t * g, half, 128), st


def _state_to_scan(s, b):
    g = b // SEQ_PER_GROUP
    s = s.reshape(g, SEQ_PER_GROUP, N_HEADS_B, 2, HEAD_DIM // 2, HEAD_DIM).transpose(0, 5, 4, 2, 3, 1)
    return s.reshape(g, HEAD_DIM, HEAD_DIM // 2, 128)


def _state_from_scan(s, b):
    g = b // SEQ_PER_GROUP
    s = s.reshape(g, HEAD_DIM, HEAD_DIM // 2, N_HEADS_B, 2, SEQ_PER_GROUP).transpose(0, 5, 3, 4, 2, 1)
    return s.reshape(b, N_HEADS_B, HEAD_DIM, HEAD_DIM)


def _route_tile(x, nrm_ref, wr_ref, br_ref, cnt_scr, xn_o, idx_o, wts_o):
    tm = x.shape[0]
    xn = _rms(x, nrm_ref[...])
    hi, lo = _split_bf16(xn)
    _store_row_tiles(xn_o, xn)
    wr = wr_ref[...]
    pa = _dot(hi, wr)
    pb = _dot(lo, wr)
    lg = pa[:, 0:32] + pa[:, 32:64] + pb[:, 0:32] + pb[:, 32:64] + br_ref[...]
    col = lambda c: lg[:, c:c + 1]
    c = [col(gx) for gx in range(N_EGROUPS)]
    m = jnp.maximum(jnp.maximum(c[0], c[1]), jnp.maximum(c[2], c[3]))
    den = jnp.exp(c[0] - m) + jnp.exp(c[1] - m) + jnp.exp(c[2] - m) + jnp.exp(c[3] - m)
    pg = 1.0 / den
    gi = jnp.where(c[0] >= m, 0, jnp.where(c[1] >= m, 1, jnp.where(c[2] >= m, 2, 3))).astype(I32)
    sel = []
    for e in range(EXP_PER_GROUP):
        sel.append(jnp.where(gi == 0, col(4 + e), jnp.where(gi == 1, col(8 + e),
                                                            jnp.where(gi == 2, col(12 + e), col(16 + e)))))
    v1 = jnp.maximum(jnp.maximum(sel[0], sel[1]), jnp.maximum(sel[2], sel[3]))
    i1 = jnp.where(sel[0] >= v1, 0, jnp.where(sel[1] >= v1, 1, jnp.where(sel[2] >= v1, 2, 3))).astype(I32)
    rest = [jnp.where(i1 == e, -jnp.inf, sel[e]) for e in range(EXP_PER_GROUP)]
    v2 = jnp.maximum(jnp.maximum(rest[0], rest[1]), jnp.maximum(rest[2], rest[3]))
    i2 = jnp.where(rest[0] >= v2, 0, jnp.where(rest[1] >= v2, 1, jnp.where(rest[2] >= v2, 2, 3))).astype(I32)
    tt = jnp.exp(v2 - v1)
    w1 = pg / (1.0 + tt)
    w2 = pg * tt / (1.0 + tt)
    e1 = gi * EXP_PER_GROUP + i1
    e2 = gi * EXP_PER_GROUP + i2
    lane = lax.broadcasted_iota(I32, (tm, N_EXPERTS), 1)
    oh1 = lane == e1
    oh2 = lane == e2
    oh = jnp.where(jnp.logical_or(oh1, oh2), 1.0, 0.0)
    ri = lax.broadcasted_iota(I32, (tm, tm), 0)
    ci = lax.broadcasted_iota(I32, (tm, tm), 1)
    ltri = jnp.where(ri > ci, 1.0, 0.0).astype(BF16)
    cnt = cnt_scr[0:1, 0:N_EXPERTS]
    pre = _dot(ltri, oh.astype(BF16)) + cnt
    rank1 = jnp.sum(jnp.where(oh1, pre, 0.0), axis=1, keepdims=True)
    rank2 = jnp.sum(jnp.where(oh2, pre, 0.0), axis=1, keepdims=True)
    cnt_scr[0:1, 0:N_EXPERTS] = cnt + jnp.sum(oh, axis=0, keepdims=True)
    lw = lax.broadcasted_iota(I32, (tm, 128), 1)
    wts_o[...] = jnp.where(lw == 0, w1, jnp.where(lw == 1, w2, 0.0))
    cols = jnp.where(lw == 0, e1.astype(F32), jnp.where(lw == 1, e2.astype(F32),
                                                         jnp.where(lw == 2, rank1, jnp.where(lw == 3, rank2, 0.0))))
    idx_o[...] = cols.T[0:8, :].astype(I32)


def _route_out_specs(tm):
    return [pl.BlockSpec((tm * ROW_CHUNKS, 128), lambda i: (i, 0)),
            pl.BlockSpec((8, tm), lambda i: (0, i)),
            pl.BlockSpec((tm, 128), lambda i: (i, 0)),
            pl.BlockSpec((8, 128), lambda i: (0, 0))]


def _route_out_shapes(n):
    return [jax.ShapeDtypeStruct((n * ROW_CHUNKS, 128), F32),
            jax.ShapeDtypeStruct((8, n), I32),
            jax.ShapeDtypeStruct((n, 128), F32),
            jax.ShapeDtypeStruct((8, 128), F32)]


def _route_in_specs():
    return [pl.BlockSpec((1, D_MODEL), lambda i: (0, 0)),
            pl.BlockSpec((D_MODEL, 128), lambda i: (0, 0)),
            pl.BlockSpec((1, 32), lambda i: (0, 0))]


def _mix0_out_kernel(o_ref, bonus_ref, g_ref, attn_ref, x_ref, lnw_ref, lnb_ref, avg_ref, wo_ref,
                     nrm_ref, wr_ref, br_ref,
                     x1_o, xn_o, idx_o, wts_o, cnt_o, cnt_scr, xs_scr, at_scr, *, b):
    @pl.when(pl.program_id(0) == 0)
    def _():
        cnt_scr[...] = jnp.zeros_like(cnt_scr)

    avg = avg_ref[...]
    half = HEAD_DIM // 2
    lane8 = lax.broadcasted_iota(I32, (SEQ_PER_GROUP, 128), 1)
    unit_rows = []
    for u in range(o_ref.shape[0]):
        ot = jnp.concatenate([o_ref[u], jnp.zeros((128 - half, 128), F32)], axis=0).T
        cols = []
        for jj in range(N_HEADS_B // 2):
            q = [ot[(4 * jj + i) * SEQ_PER_GROUP:(4 * jj + i + 1) * SEQ_PER_GROUP] for i in range(4)]
            c = jnp.where(lane8 < half, q[0], pltpu.roll(q[1], half, 1))
            c = jnp.where(lane8 < 2 * half, c, pltpu.roll(q[2], 2 * half, 1))
            c = jnp.where(lane8 < 3 * half, c, pltpu.roll(q[3], 3 * half, 1))
            cols.append(c)
        unit_rows.append(jnp.concatenate(cols, axis=1))
    o = jnp.concatenate(unit_rows, axis=0)
    d = o - _dot2(o, avg)
    var = _dot2(d * d, avg)
    on = d * lax.rsqrt(var + RWKV_GN_EPS) * lnw_ref[...] + lnb_ref[...]
    rout = ((on + bonus_ref[...]) * g_ref[...]).astype(BF16)
    x = _load_time_major(x_ref, xs_scr, b)
    if len(x_ref.shape) == 3:
        attn = _interleave_rows([attn_ref[:, s * Q_COLS:(s + 1) * Q_COLS] for s in range(b)], at_scr)
    else:
        attn = attn_ref[...]
    y = _dot(attn.astype(BF16), wo_ref[:Q_COLS, :]) + _dot(rout, wo_ref[Q_COLS:, :]) + x
    x1_o[...] = y
    _route_tile(y, nrm_ref, wr_ref, br_ref, cnt_scr, xn_o, idx_o, wts_o)
    cnt_o[...] = cnt_scr[...]


def _mix0_out(o, bonus, g, attn, x, p, rp, b, t, tm):
    n = b * t
    xshape, xspec = _time_major_spec(b, t, D_MODEL, tm)
    aspec = (pl.BlockSpec((tm // b, b * Q_COLS), lambda i: (i, 0)) if len(xshape) == 3
             else pl.BlockSpec((tm, Q_COLS), lambda i: (i, 0)))
    tile = lambda w: pl.BlockSpec((tm, w), lambda i: (i, 0))
    row = lambda w: pl.BlockSpec((1, w), lambda i: (0, 0))
    full = lambda r, c: pl.BlockSpec((r, c), lambda i: (0, 0))
    return pl.pallas_call(
        functools.partial(_mix0_out_kernel, b=b),
        grid=(n // tm,),
        in_specs=[pl.BlockSpec((tm // SEQ_PER_GROUP, HEAD_DIM // 2, 128), lambda i: (i, 0, 0)),
                  tile(D_B), tile(D_B), aspec, xspec, row(D_B), row(D_B),
                  full(D_B, D_B), full(D_MODEL, D_MODEL)] + _route_in_specs(),
        out_specs=[tile(D_MODEL)] + _route_out_specs(tm),
        out_shape=[jax.ShapeDtypeStruct((n, D_MODEL), F32)] + _route_out_shapes(n),
        scratch_shapes=[pltpu.VMEM((8, 128), F32), pltpu.VMEM((D_MODEL // 128, tm, 128), F32),
                        pltpu.VMEM((Q_COLS // 128, tm, 128), F32)],
        compiler_params=_cparams("arbitrary"),
        name="mix0_out",
    )(o, bonus, g, attn, x.reshape(xshape), p['ln_w'], p['ln_b'], p['head_avg'], p['w_out'], rp['norm'], rp['w'], rp['b'])


def _row_dma_start(idx_ref, pos, r, src_hbm, dst, sem, priority):
    src_row = pl.multiple_of(idx_ref[pos] * ROW_CHUNKS, ROW_CHUNKS)
    dst_row = pl.multiple_of(r * ROW_CHUNKS, ROW_CHUNKS)
    pltpu.make_async_copy(src_hbm.at[pl.ds(src_row, ROW_CHUNKS)], dst.at[pl.ds(dst_row, ROW_CHUNKS)],
                          sem).start(priority=priority)


def _row_gather_start(idx_ref, base, n_rows, src_hbm, dst, sem):
    def body(r2, carry):
        for p in range(2):
            _row_dma_start(idx_ref, base + 2 * r2 + p, 2 * r2 + p, src_hbm, dst, sem, p)
        return carry

    lax.fori_loop(0, n_rows // 2, body, 0, unroll=4)


def _row_gather_wait(dst, sem):
    pltpu.make_async_copy(dst, dst, sem).wait()


def _expert_kernel(te_ref, nu_ref, pad_ref, p1_ref, p2_ref, x_hbm, wg_ref, wu_ref, wd_ref, o_ref,
                   src_ref, xbuf, sem, wgb, wub, wdb):
    i = pl.program_id(0)
    nu = nu_ref[0]

    @pl.when(i == 0)
    def _():
        def fill_pad(e, carry):
            start = pad_ref[2 * e]

            def body(r, c):
                src_ref[r] = jnp.minimum(r - start, p1_ref.shape[0] - 1)
                return c

            lax.fori_loop(start, pad_ref[2 * e + 1], body, 0)
            return carry

        lax.fori_loop(0, N_EXPERTS + 1, fill_pad, 0)

        def invert(n, carry):
            src_ref[p1_ref[n]] = n
            src_ref[p2_ref[n]] = n
            return carry

        lax.fori_loop(0, p1_ref.shape[0], invert, 0, unroll=8)
        _row_gather_start(src_ref, 0, MOE_TILE, x_hbm, xbuf.at[0], sem.at[0])

    new_expert = jnp.logical_or(i == 0, te_ref[i] != te_ref[jnp.maximum(i - 1, 0)])

    @pl.when(jnp.logical_and(i < nu, new_expert))
    def _():
        wgb[...] = wg_ref[0].astype(BF16)
        wub[...] = wu_ref[0].astype(BF16)
        wdb[...] = wd_ref[0].astype(BF16)

    @pl.when(i < nu)
    def _():
        cur = i % 2
        nxt = 1 - cur
        base = (i + 1) * MOE_TILE
        pieces = 2 * (D_FF_E // 256) * (D_MODEL // 256) + (D_MODEL // 256) * (D_FF_E // 256)
        per = -(-MOE_TILE // pieces)
        issued = [0]

        def start_some():
            for r in range(issued[0], min(issued[0] + per, MOE_TILE)):
                _row_dma_start(src_ref, base + r, r, x_hbm, xbuf.at[nxt], sem.at[nxt], r % 2)
            issued[0] = min(issued[0] + per, MOE_TILE)

        def block_dot(a, w_ref, c):
            acc = None
            for kt in range(a.shape[1] // 256):
                part = _dot(a[:, kt * 256:(kt + 1) * 256], w_ref[kt * 256:(kt + 1) * 256, c * 256:(c + 1) * 256])
                acc = part if acc is None else acc + part
                start_some()
            return acc

        _row_gather_wait(xbuf.at[cur], sem.at[cur])
        x = _load_row_tiles(xbuf.at[cur]).astype(BF16)
        hg = jnp.concatenate([block_dot(x, wgb, c) for c in range(D_FF_E // 256)], axis=1)
        hu = jnp.concatenate([block_dot(x, wub, c) for c in range(D_FF_E // 256)], axis=1)
        h = ((hg * _sigmoid(hg)) * hu).astype(BF16)
        for c in range(D_MODEL // 256):
            res = block_dot(h, wdb, c)
            for cc in range(2):
                o_ref[pl.ds(2 * c + cc, MOE_TILE, stride=ROW_CHUNKS), :] = res[:, cc * 128:(cc + 1) * 128]
        assert issued[0] == MOE_TILE

    @pl.when(i == nu)
    def _():
        _row_gather_wait(xbuf.at[i % 2], sem.at[i % 2])

    @pl.when(i >= nu)
    def _():
        o_ref[...] = jnp.zeros_like(o_ref)


def _experts(xn, r, pos1, pos2, tile_expert, n_used, pad_ranges, wg, wu, wd):
    wspec = lambda a, b: pl.BlockSpec((1, a, b), lambda i, te, nu, pad, p1, p2: (te[i], 0, 0))
    grid_spec = pltpu.PrefetchScalarGridSpec(
        num_scalar_prefetch=5,
        grid=(r // MOE_TILE,),
        in_specs=[pl.BlockSpec(memory_space=pl.ANY),
                  wspec(D_MODEL, D_FF_E), wspec(D_MODEL, D_FF_E), wspec(D_FF_E, D_MODEL)],
        out_specs=pl.BlockSpec((MOE_TILE * ROW_CHUNKS, 128), lambda i, te, nu, pad, p1, p2: (i, 0)),
        scratch_shapes=[pltpu.SMEM((r,), I32), pltpu.VMEM((2, MOE_TILE * ROW_CHUNKS, 128), F32),
                        pltpu.SemaphoreType.DMA((2,)),
                        pltpu.VMEM((D_MODEL, D_FF_E), BF16), pltpu.VMEM((D_MODEL, D_FF_E), BF16),
                        pltpu.VMEM((D_FF_E, D_MODEL), BF16)],
    )
    return pl.pallas_call(
        _expert_kernel,
        grid_spec=grid_spec,
        out_shape=jax.ShapeDtypeStruct((r * ROW_CHUNKS, 128), F32),
        compiler_params=_cparams("arbitrary"),
        name="moe_experts",
    )(tile_expert, n_used, pad_ranges, pos1, pos2, xn, wg, wu, wd)


def _moe(xn, idx, cnt, ep):
    n = xn.shape[0] // ROW_CHUNKS
    rows = 2 * n + N_EXPERTS * MOE_TILE
    counts = cnt[0, :N_EXPERTS].astype(I32)
    padded = ((counts + MOE_TILE - 1) // MOE_TILE) * MOE_TILE
    ends = jnp.cumsum(padded)
    offs = ends - padded
    pos1 = offs[idx[0]] + idx[2]
    pos2 = offs[idx[1]] + idx[3]
    pad_ranges = jnp.stack([jnp.append(offs + counts, ends[-1]), jnp.append(ends, ends[-1] + MOE_TILE)], axis=1)
    pad_ranges = pad_ranges.reshape(-1).astype(I32)
    n_used = (ends[-1] // MOE_TILE).astype(I32)
    starts = jnp.arange(rows // MOE_TILE, dtype=I32) * MOE_TILE
    starts = jnp.minimum(starts, ends[-1] - 1)
    tile_expert = jnp.sum((starts[:, None] >= ends[None, :]).astype(I32), axis=1)
    tile_expert = jnp.minimum(tile_expert, N_EXPERTS - 1).astype(I32)
    out = _experts(xn, rows, pos1, pos2, tile_expert, n_used.reshape(1), pad_ranges, ep['wg'], ep['wu'], ep['wd'])
    return out, pos1, pos2


def _combine_kernel(p1_ref, p2_ref, x_ref, wts_ref, nrm_ref, out_hbm, y_o, gbuf, sem, ys_scr, *, final, b):
    i = pl.program_id(0)
    tm = x_ref.shape[0]

    def start(tile, slot):
        _row_gather_start(p1_ref, tile * tm, tm, out_hbm, gbuf.at[slot, 0], sem.at[slot])
        _row_gather_start(p2_ref, tile * tm, tm, out_hbm, gbuf.at[slot, 1], sem.at[slot])

    @pl.when(i == 0)
    def _():
        start(0, 0)

    @pl.when(i + 1 < pl.num_programs(0))
    def _():
        start(i + 1, (i + 1) % 2)

    cur = i % 2
    _row_gather_wait(gbuf.at[cur], sem.at[cur])
    wts = wts_ref[...]
    y = x_ref[...] + wts[:, 0:1] * _load_row_tiles(gbuf.at[cur, 0]) + wts[:, 1:2] * _load_row_tiles(gbuf.at[cur, 1])
    if final:
        _store_batch_major(y_o, ys_scr, _rms(y, nrm_ref[...]), b)
    else:
        y_o[...] = y


def _moe_combine(x, wts, out, pos1, pos2, nrm, final, b, t, tm=256):
    n = x.shape[0]
    tile = lambda w: pl.BlockSpec((tm, w), lambda i, p1, p2: (i, 0))
    if final:
        yshape, yspec = _time_major_spec(b, t, D_MODEL, tm)
    else:
        yshape, yspec = (n, D_MODEL), tile(D_MODEL)
    grid_spec = pltpu.PrefetchScalarGridSpec(
        num_scalar_prefetch=2,
        grid=(n // tm,),
        in_specs=[tile(D_MODEL), tile(128), pl.BlockSpec((1, D_MODEL), lambda i, p1, p2: (0, 0)),
                  pl.BlockSpec(memory_space=pl.ANY)],
        out_specs=yspec,
        scratch_shapes=[pltpu.VMEM((2, 2, tm * ROW_CHUNKS, 128), F32), pltpu.SemaphoreType.DMA((2,)),
                        pltpu.VMEM((D_MODEL // 128, tm, 128), F32)],
    )
    return pl.pallas_call(
        functools.partial(_combine_kernel, final=final, b=b),
        grid_spec=grid_spec,
        out_shape=jax.ShapeDtypeStruct(yshape, F32),
        compiler_params=_cparams("arbitrary"),
        name="moe_combine_final" if final else "moe_combine",
    )(pos1, pos2, x, wts, nrm, out)


def _gelu_tanh(x):
    return 0.5 * x * (1.0 + jnp.tanh(0.7978845608028654 * (x + 0.044715 * (x * x * x))))


def _mix1_kernel(x_ref, nmix_ref, bre_ref, bim_ref, are_ref, aim_ref, cre_ref, cim_ref,
                 dsk_ref, wo_ref, h0r_ref, h0i_ref, nrm_ref, wr_ref, br_ref,
                 x2_o, xn_o, idx_o, wts_o, cnt_o, hr_o, hi_o,
                 bur, bui, hr_scr, hi_scr, cnt_scr, *, b, cw):
    @pl.when(pl.program_id(0) == 0)
    def _():
        cnt_scr[...] = jnp.zeros_like(cnt_scr)
        hr_scr[...] = h0r_ref[...]
        hi_scr[...] = h0i_ref[...]

    x = x_ref[...]
    u = _rms(x, nmix_ref[...])
    ub = u.astype(BF16)
    nblk = bre_ref.shape[0]
    kin = D_MODEL // nblk
    kst = S5_STATE // nblk
    for cb in range(nblk):
        ucb = ub[:, cb * kin:(cb + 1) * kin]
        bur[:, cb * kst:(cb + 1) * kst] = _dot(ucb, bre_ref[cb])
        bui[:, cb * kst:(cb + 1) * kst] = _dot(ucb, bim_ref[cb])

    tc = x.shape[0] // b
    for c0 in range(0, S5_STATE, cw):
        cs = slice(c0, c0 + cw)
        ar = jnp.broadcast_to(are_ref[:, cs], (b, cw))
        ai = jnp.broadcast_to(aim_ref[:, cs], (b, cw))

        def step(s, carry, cs=cs, ar=ar, ai=ai):
            hr, hi = carry
            rows = pl.ds(pl.multiple_of(s * b, b), b)
            nr = ar * hr - ai * hi + bur[rows, cs]
            ni = ar * hi + ai * hr + bui[rows, cs]
            bur[rows, cs] = nr
            bui[rows, cs] = ni
            return nr, ni

        hr, hi = lax.fori_loop(0, tc, step, (hr_scr[:, cs], hi_scr[:, cs]), unroll=True)
        hr_scr[:, cs] = hr
        hi_scr[:, cs] = hi

    ych = []
    for cb in range(nblk):
        ss = slice(cb * kst, (cb + 1) * kst)
        ych.append(_dot(bur[:, ss].astype(BF16), cre_ref[cb]) - _dot(bui[:, ss].astype(BF16), cim_ref[cb]))
    y = jnp.concatenate(ych, axis=1) + dsk_ref[...] * u
    z = _dot(_gelu_tanh(y).astype(BF16), wo_ref[...])
    x2 = x + z[:, :D_MODEL] * _sigmoid(z[:, D_MODEL:])
    x2_o[...] = x2
    _route_tile(x2, nrm_ref, wr_ref, br_ref, cnt_scr, xn_o, idx_o, wts_o)
    cnt_o[...] = cnt_scr[...]
    hr_o[...] = hr_scr[...]
    hi_o[...] = hi_scr[...]


def _mix1(x, sp, rp, h0r, h0i, b, tr):
    n = x.shape[0]
    cw = 1024 if b == 8 else 128
    tile = lambda w: pl.BlockSpec((tr, w), lambda i: (i, 0))
    row = lambda w: pl.BlockSpec((1, w), lambda i: (0, 0))
    full = lambda *s: pl.BlockSpec(s, lambda i: (0,) * len(s))
    nblk = sp['b_re'].shape[0]
    return pl.pallas_call(
        functools.partial(_mix1_kernel, b=b, cw=cw),
        grid=(n // tr,),
        in_specs=[tile(D_MODEL), row(D_MODEL),
                  full(nblk, D_MODEL // nblk, S5_STATE // nblk), full(nblk, D_MODEL // nblk, S5_STATE // nblk),
                  row(S5_STATE), row(S5_STATE),
                  full(nblk, S5_STATE // nblk, D_MODEL // nblk), full(nblk, S5_STATE // nblk, D_MODEL // nblk),
                  row(D_MODEL), full(D_MODEL, 2 * D_MODEL), full(b, S5_STATE), full(b, S5_STATE)] + _route_in_specs(),
        out_specs=[tile(D_MODEL)] + _route_out_specs(tr) + [full(b, S5_STATE), full(b, S5_STATE)],
        out_shape=[jax.ShapeDtypeStruct((n, D_MODEL), F32)] + _route_out_shapes(n)
                  + [jax.ShapeDtypeStruct((b, S5_STATE), F32)] * 2,
        scratch_shapes=[pltpu.VMEM((tr, S5_STATE), F32), pltpu.VMEM((tr, S5_STATE), F32),
                        pltpu.VMEM((b, S5_STATE), F32), pltpu.VMEM((b, S5_STATE), F32),
                        pltpu.VMEM((8, 128), F32)],
        compiler_params=_cparams("arbitrary"),
        name="mix1",
    )(x, sp['norm'], sp['b_re'], sp['b_im'], sp['a_re'], sp['a_im'], sp['c_re'], sp['c_im'],
      sp['d'], sp['w_out'], h0r, h0i, rp['norm'], rp['w'], rp['b'])


def _router_params(norm, w_rc, b_rc, w_rf, b_rf):
    w = jnp.concatenate([w_rc, w_rf.reshape(D_MODEL, N_EXPERTS), jnp.zeros((D_MODEL, 12), F32)], axis=1)
    hi = w.astype(BF16)
    lo = (w - hi.astype(F32)).astype(BF16)
    wcat = jnp.concatenate([hi, lo, jnp.zeros((D_MODEL, 64), BF16)], axis=1)
    bias = jnp.concatenate([b_rc, b_rf.reshape(-1), jnp.zeros((12,), F32)]).reshape(1, 32)
    return {'norm': norm.reshape(1, D_MODEL), 'w': wcat, 'b': bias}


def _expert_params(wg, wu, wd):
    return {'wg': wg, 'wu': wu, 'wd': wd}


def _s5_params(norm, a_re, a_im, log_dt, b_re, b_im, c_re, c_im, d_skip, w_out, nblk=8):
    dt = jnp.exp(log_dt)
    mag = jnp.exp(dt * a_re)
    ab_re, ab_im = mag * jnp.cos(dt * a_im), mag * jnp.sin(dt * a_im)
    den = a_re * a_re + a_im * a_im
    f_re = ((ab_re - 1.0) * a_re + ab_im * a_im) / den
    f_im = (ab_im * a_re - (ab_re - 1.0) * a_im) / den
    bb_re = f_re[..., None] * b_re - f_im[..., None] * b_im
    bb_im = f_re[..., None] * b_im + f_im[..., None] * b_re
    gpb = S5_GROUPS // nblk
    eye = jnp.eye(gpb, dtype=F32)

    def in_blocks(bb):
        bb = bb.reshape(nblk, gpb, S5_P, S5_CH)
        w = jnp.einsum('ngpc,gh->ngchp', bb, eye)
        return w.reshape(nblk, gpb * S5_CH, gpb * S5_P).astype(BF16)

    def out_blocks(cc):
        cc = cc.reshape(nblk, gpb, S5_CH, S5_P)
        w = jnp.einsum('ngcp,gh->ngphc', cc, eye)
        return w.reshape(nblk, gpb * S5_P, gpb * S5_CH).astype(BF16)

    return {'norm': norm.reshape(1, D_MODEL), 'b_re': in_blocks(bb_re), 'b_im': in_blocks(bb_im),
            'a_re': ab_re.reshape(1, S5_STATE), 'a_im': ab_im.reshape(1, S5_STATE),
            'c_re': out_blocks(c_re), 'c_im': out_blocks(c_im), 'd': d_skip.reshape(1, D_MODEL),
            'w_out': w_out.astype(BF16)}


def _head_block(value):
    hid = jnp.arange(D_B, dtype=I32) // HEAD_DIM
    return jnp.where(hid[:, None] == hid[None, :], value, 0.0).astype(BF16)


def _run_group(x, cache_k, cache_v, shift0, wkv0, h0r, h0i, pr):
    b, t = x.shape[0], x.shape[1]
    n = b * t
    prompt = cache_k is None
    tm = 256 if prompt else 128
    if not prompt:
        x = x.transpose(1, 0, 2)
    q, kv, pb = _in_proj(x, pr['l0_norm'], pr['l0_w_in'], b, t, tm)

    if prompt:
        attn = _attn_prompt(q, kv, pr['sinks'], b, t)
        kv3 = kv[t - WINDOW:].reshape(WINDOW, b, 2, N_KV_A, HEAD_DIM)
        new_k = kv3[:, :, 0].transpose(1, 0, 2, 3)
        new_v = kv3[:, :, 1].transpose(1, 0, 2, 3)
        init = jnp.zeros((b, D_B_IN), F32)
    else:
        qs = q.reshape(t, b, Q_COLS).transpose(1, 0, 2)
        kvs = kv.reshape(t, b, 2 * KV_COLS).transpose(1, 0, 2)
        kn, vn = kvs[..., :KV_COLS], kvs[..., KV_COLS:]
        ck = cache_k.reshape(b, WINDOW, KV_COLS)
        cv = cache_v.reshape(b, WINDOW, KV_COLS)
        attn = _attn_sample(qs, kn, vn, ck, cv, pr['sinks'])
        attn = attn.transpose(1, 0, 2).reshape(n, Q_COLS)
        new_k = jnp.concatenate([ck[:, t:], kn], axis=1).reshape(b, WINDOW, N_KV_A, HEAD_DIM)
        new_v = jnp.concatenate([cv[:, t:], vn], axis=1).reshape(b, WINDOW, N_KV_A, HEAD_DIM)
        init = shift0
    new_shift = pb[n - b:]

    r, w, k, v, kk, nkka, bonus, g = _rwkv_prep(pb, init, pr['rw'], b, tm)
    tc = 64 if prompt else t
    s0 = jnp.zeros((b // SEQ_PER_GROUP, HEAD_DIM, HEAD_DIM // 2, 128), F32) if prompt else _state_to_scan(wkv0, b)
    o, s_fin = _wkv_scan(r, w, k, kk, nkka, v, s0, b, t, tc)
    new_wkv = _state_from_scan(s_fin, b)

    x1, xn, idx, wts, cnt = _mix0_out(o, bonus, g, attn, x, pr['rw'], pr['l0_route'], b, t, 256)
    out, pos1, pos2 = _moe(xn, idx, cnt, pr['l0_exp'])
    x1 = _moe_combine(x1, wts, out, pos1, pos2, pr['final_norm'], False, b, t)

    x2, xn, idx, wts, cnt, hr, hi = _mix1(x1, pr['s5'], pr['l1_route'], h0r, h0i, b, 256)
    out, pos1, pos2 = _moe(xn, idx, cnt, pr['l1_exp'])
    y = _moe_combine(x2, wts, out, pos1, pos2, pr['final_norm'], True, b, t)
    y = y if prompt else y.reshape(t, b, D_MODEL).transpose(1, 0, 2)
    return (y, new_k, new_v, new_shift, new_wkv,
            hr.reshape(b, S5_GROUPS, S5_P), hi.reshape(b, S5_GROUPS, S5_P))


def kernel(x_prompt, x_sample, cache_win_k, cache_win_v, state_shift, state_wkv, state_s5_re, state_s5_im,
           l0_norm_mix, l0_w_in, l0_sinks, l0_mu, l0_w0, l0_w_lora_up, l0_a0, l0_a_lora_up, l0_g_lora_up,
           l0_k_k, l0_k_a, l0_r_k, l0_ln_w, l0_ln_b, l0_w_out,
           l0_norm_ffn, l0_router_coarse, l0_bias_coarse, l0_router_fine, l0_bias_fine,
           l0_exp_gate, l0_exp_up, l0_exp_down,
           l1_norm_mix, l1_s5_a_re, l1_s5_a_im, l1_s5_log_dt, l1_s5_b_re, l1_s5_b_im, l1_s5_c_re, l1_s5_c_im,
           l1_s5_d, l1_w_out,
           l1_norm_ffn, l1_router_coarse, l1_bias_coarse, l1_router_fine, l1_bias_fine,
           l1_exp_gate, l1_exp_up, l1_exp_down,
           final_norm):
    row = lambda z: z.reshape(1, -1)
    pr = {
        'l0_norm': row(l0_norm_mix), 'l0_w_in': l0_w_in.astype(BF16), 'sinks': l0_sinks,
        'rw': {'mu': row(l0_mu), 'w0': row(l0_w0), 'w_up': l0_w_lora_up.astype(BF16), 'a0': row(l0_a0),
               'a_up': l0_a_lora_up.astype(BF16), 'g_up': l0_g_lora_up.astype(BF16), 'k_k': row(l0_k_k),
               'k_a': row(l0_k_a), 'r_k': row(l0_r_k), 'ln_w': row(l0_ln_w), 'ln_b': row(l0_ln_b),
               'head_ones': _head_block(1.0), 'head_avg': _head_block(1.0 / HEAD_DIM),
               'w_out': l0_w_out.astype(BF16)},
        'l0_route': _router_params(l0_norm_ffn, l0_router_coarse, l0_bias_coarse, l0_router_fine, l0_bias_fine),
        'l0_exp': _expert_params(l0_exp_gate, l0_exp_up, l0_exp_down),
        's5': _s5_params(l1_norm_mix, l1_s5_a_re, l1_s5_a_im, l1_s5_log_dt, l1_s5_b_re, l1_s5_b_im,
                         l1_s5_c_re, l1_s5_c_im, l1_s5_d, l1_w_out),
        'l1_route': _router_params(l1_norm_ffn, l1_router_coarse, l1_bias_coarse, l1_router_fine, l1_bias_fine),
        'l1_exp': _expert_params(l1_exp_gate, l1_exp_up, l1_exp_down),
        'final_norm': row(final_norm),
    }
    bp, bs = x_prompt.shape[0], x_sample.shape[0]
    zero_state = jnp.zeros((bp, S5_STATE), F32)
    yp, pk, pv, psh, pwkv, pre, pim = _run_group(x_prompt, None, None, None, None, zero_state, zero_state, pr)
    ys, sk, sv, ssh, swkv, sre, sim = _run_group(
        x_sample, cache_win_k, cache_win_v, state_shift, state_wkv,
        state_s5_re.reshape(bs, S5_STATE), state_s5_im.reshape(bs, S5_STATE), pr)
    return (yp, ys, pk, pv, psh, pwkv, pre, pim, sk, sv, ssh, swkv, sre, sim)
```

```python
import functools

import jax
import jax.numpy as jnp
from jax import lax
from jax.experimental import pallas as pl
from jax.experimental.pallas import tpu as pltpu

F32 = jnp.float32
BF16 = jnp.bfloat16
I32 = jnp.int32

D_MODEL = 1024
HEAD_DIM = 64
N_HEADS_A = 8
N_KV_A = 2
GQA_GROUP = 4
WINDOW = 128
Q_COLS = 512
KV_COLS = 128
D_A_IN = 768
N_HEADS_B = 8
D_B = 512
D_LORA_W = 64
D_LORA_A = 64
D_LORA_G = 128
D_B_IN = 1792
D_IN0 = 2560
RWKV_GN_EPS = 64e-5
S5_CH = 16
S5_GROUPS = 64
S5_P = 64
S5_STATE = S5_GROUPS * S5_P
N_EGROUPS = 4
EXP_PER_GROUP = 4
N_EXPERTS = 16
D_FF_E = 512
RMS_EPS = 1e-5
NEG_BIG = -1e30
PAIRS = 64
SEQ_PER_GROUP = PAIRS // N_HEADS_B
ROW_CHUNKS = D_MODEL // 128
VMEM_LIMIT = 56 * 1024 * 1024


def _cparams(*sem):
    return pltpu.CompilerParams(dimension_semantics=sem, vmem_limit_bytes=VMEM_LIMIT)


def _dot(a, b):
    return jnp.dot(a, b, preferred_element_type=F32)


def _split_bf16(x):
    hi = x.astype(BF16)
    lo = (x - hi.astype(F32)).astype(BF16)
    return hi, lo


def _dot2(x, w):
    hi, lo = _split_bf16(x)
    return _dot(hi, w) + _dot(lo, w)


def _rms(x, g):
    return x * lax.rsqrt(jnp.mean(x * x, axis=-1, keepdims=True) + RMS_EPS) * g


def _sigmoid(x):
    return 1.0 / (1.0 + jnp.exp(-x))


def _store_row_tiles(ref, x):
    rows = x.shape[0]
    for c in range(ROW_CHUNKS):
        ref[pl.ds(c, rows, stride=ROW_CHUNKS), :] = x[:, c * 128:(c + 1) * 128]


def _time_major_spec(b, t, d, tm):
    if b == SEQ_PER_GROUP:
        return (b, t, d), pl.BlockSpec((b, tm // b, d), lambda i, *_: (0, i, 0))
    return (b * t, d), pl.BlockSpec((tm, d), lambda i, *_: (i, 0))


def _interleave_rows(pieces, scr):
    nb, steps = len(pieces), pieces[0].shape[0]
    chunks = pieces[0].shape[1] // 128
    for s, p in enumerate(pieces):
        for c in range(chunks):
            scr[c, pl.ds(s, steps, stride=nb), :] = p[:, c * 128:(c + 1) * 128].astype(scr.dtype)
    return jnp.concatenate([scr[c] for c in range(chunks)], axis=1)


def _deinterleave_rows(x, nb, scr):
    steps = x.shape[0] // nb
    chunks = x.shape[1] // 128
    for c in range(chunks):
        scr[c] = x[:, c * 128:(c + 1) * 128].astype(scr.dtype)
    return [jnp.concatenate([scr[c, pl.ds(s, steps, stride=nb), :] for c in range(chunks)], axis=1)
            for s in range(nb)]


def _load_time_major(x_ref, scr, b):
    if len(x_ref.shape) == 3:
        return _interleave_rows([x_ref[s] for s in range(b)], scr)
    return x_ref[...]


def _store_batch_major(y_ref, scr, y, b):
    if len(y_ref.shape) == 3:
        for s, p in enumerate(_deinterleave_rows(y, b, scr)):
            y_ref[s] = p
    else:
        y_ref[...] = y


def _load_row_tiles(ref):
    rows = ref.shape[0] // ROW_CHUNKS
    return jnp.concatenate([ref[pl.ds(c, rows, stride=ROW_CHUNKS), :] for c in range(ROW_CHUNKS)], axis=1)


def _in_proj_kernel(x_ref, g_ref, w_ref, q_ref, kv_ref, pb_ref, xs_scr, q_scr, kv_scr, *, b):
    x = _load_time_major(x_ref, xs_scr, b)
    xn = _rms(x, g_ref[...]).astype(BF16)
    q = _dot(xn, w_ref[:, :Q_COLS])
    kv = _dot(xn, w_ref[:, Q_COLS:D_A_IN])
    pb_ref[...] = _dot(xn, w_ref[:, D_A_IN:])
    if len(x_ref.shape) == 3:
        q_ref[...] = jnp.concatenate(_deinterleave_rows(q, b, q_scr), axis=1).astype(BF16)
        kv_ref[...] = jnp.concatenate(_deinterleave_rows(kv, b, kv_scr), axis=1)
    else:
        q_ref[...] = q
        kv_ref[...] = kv


def _in_proj(x, g, w_bf16, b, t, tm):
    n = b * t
    xshape, xspec = _time_major_spec(b, t, D_MODEL, tm)
    slab = len(xshape) == 3
    steps = tm // b
    if slab:
        qkv_specs = [pl.BlockSpec((steps, b * Q_COLS), lambda i: (i, 0)),
                     pl.BlockSpec((steps, b * 2 * KV_COLS), lambda i: (i, 0))]
        qkv_shapes = [jax.ShapeDtypeStruct((t, b * Q_COLS), BF16), jax.ShapeDtypeStruct((t, b * 2 * KV_COLS), F32)]
    else:
        qkv_specs = [pl.BlockSpec((tm, Q_COLS), lambda i: (i, 0)), pl.BlockSpec((tm, 2 * KV_COLS), lambda i: (i, 0))]
        qkv_shapes = [jax.ShapeDtypeStruct((n, Q_COLS), F32), jax.ShapeDtypeStruct((n, 2 * KV_COLS), F32)]
    return pl.pallas_call(
        functools.partial(_in_proj_kernel, b=b),
        grid=(n // tm,),
        in_specs=[xspec,
                  pl.BlockSpec((1, D_MODEL), lambda i: (0, 0)),
                  pl.BlockSpec((D_MODEL, D_IN0), lambda i: (0, 0))],
        out_specs=qkv_specs + [pl.BlockSpec((tm, D_B_IN), lambda i: (i, 0))],
        out_shape=qkv_shapes + [jax.ShapeDtypeStruct((n, D_B_IN), F32)],
        scratch_shapes=[pltpu.VMEM((D_MODEL // 128, tm, 128), F32), pltpu.VMEM((Q_COLS // 128, tm, 128), F32),
                        pltpu.VMEM((2 * KV_COLS // 128, tm, 128), F32)],
        compiler_params=_cparams("parallel"),
        name="in_proj",
    )(x.reshape(xshape), g, w_bf16)


def _attn_prompt_kernel(sinks_ref, q_ref, kc_ref, kp_ref, vc_ref, vp_ref, o_ref):
    j = pl.program_id(1)
    qi = lax.broadcasted_iota(I32, (WINDOW, 2 * WINDOW), 0)
    kj = lax.broadcasted_iota(I32, (WINDOW, 2 * WINDOW), 1)
    valid = jnp.logical_and(kj > qi, kj <= qi + WINDOW)
    valid = jnp.logical_and(valid, jnp.logical_or(kj >= WINDOW, j > 0))
    dist = (WINDOW + qi - kj).astype(F32)
    for n in range(N_KV_A):
        cs = slice(n * HEAD_DIM, (n + 1) * HEAD_DIM)
        kb = jnp.concatenate([kp_ref[:, cs], kc_ref[:, cs]], axis=0).astype(BF16)
        vb = jnp.concatenate([vp_ref[:, cs], vc_ref[:, cs]], axis=0).astype(BF16)
        for g in range(GQA_GROUP):
            h = n * GQA_GROUP + g
            hs = slice(h * HEAD_DIM, (h + 1) * HEAD_DIM)
            s = lax.dot_general(q_ref[:, hs], kb, (((1,), (1,)), ((), ())), preferred_element_type=F32)
            s = s * (HEAD_DIM ** -0.5) - (2.0 ** -(h + 1)) * dist
            s = jnp.where(valid, s, NEG_BIG)
            sink = sinks_ref[h]
            m = jnp.maximum(jnp.max(s, axis=1, keepdims=True), sink)
            p = jnp.exp(s - m)
            l = jnp.sum(p, axis=1, keepdims=True) + jnp.exp(sink - m)
            o = _dot(p.astype(BF16), vb) / l
            o_ref[:, hs] = o.astype(BF16)


def _attn_prompt(q2, kv2, sinks, b, t):
    prev = lambda bi, j: jnp.maximum(j - 1, 0)
    return pl.pallas_call(
        _attn_prompt_kernel,
        grid=(b, t // WINDOW),
        in_specs=[pl.BlockSpec(memory_space=pltpu.SMEM),
                  pl.BlockSpec((WINDOW, Q_COLS), lambda bi, j: (j, bi)),
                  pl.BlockSpec((WINDOW, KV_COLS), lambda bi, j: (j, 2 * bi)),
                  pl.BlockSpec((WINDOW, KV_COLS), lambda bi, j: (prev(bi, j), 2 * bi)),
                  pl.BlockSpec((WINDOW, KV_COLS), lambda bi, j: (j, 2 * bi + 1)),
                  pl.BlockSpec((WINDOW, KV_COLS), lambda bi, j: (prev(bi, j), 2 * bi + 1))],
        out_specs=pl.BlockSpec((WINDOW, Q_COLS), lambda bi, j: (j, bi)),
        out_shape=jax.ShapeDtypeStruct((t, b * Q_COLS), BF16),
        compiler_params=_cparams("parallel", "parallel"),
        name="attn_prompt",
    )(sinks, q2, kv2, kv2, kv2, kv2)


def _attn_sample_kernel(sinks_ref, q_ref, kn_ref, vn_ref, ck_ref, cv_ref, o_ref):
    bs, t = q_ref.shape[0], q_ref.shape[1]
    assert t & (t - 1) == 0
    nq, nk = GQA_GROUP * t, 2 * WINDOW
    r = lax.broadcasted_iota(I32, (nq, nk), 0)
    kj = lax.broadcasted_iota(I32, (nq, nk), 1)
    tq = jnp.bitwise_and(r, t - 1)
    valid = jnp.logical_and(kj > tq, kj <= tq + WINDOW)
    dist = (WINDOW + tq - kj).astype(F32)
    grp = jnp.right_shift(lax.broadcasted_iota(I32, (nq, 1), 0), t.bit_length() - 1)
    pad = jnp.zeros((bs, nk - WINDOW - t, HEAD_DIM), F32)
    for n in range(N_KV_A):
        cs = slice(n * HEAD_DIM, (n + 1) * HEAD_DIM)
        kb = jnp.concatenate([ck_ref[:, :, cs], kn_ref[:, :, cs], pad], axis=1).astype(BF16)
        vb = jnp.concatenate([cv_ref[:, :, cs], vn_ref[:, :, cs], pad], axis=1).astype(BF16)
        qn = jnp.concatenate([q_ref[:, :, (n * GQA_GROUP + g) * HEAD_DIM:(n * GQA_GROUP + g + 1) * HEAD_DIM]
                              for g in range(GQA_GROUP)], axis=1).astype(BF16)
        slope = jnp.zeros((nq, 1), F32)
        sink = jnp.zeros((nq, 1), F32)
        for g in range(GQA_GROUP):
            h = n * GQA_GROUP + g
            slope = jnp.where(grp == g, 2.0 ** -(h + 1), slope)
            sink = jnp.where(grp == g, sinks_ref[h], sink)
        s = jnp.einsum('bqd,bkd->bqk', qn, kb, preferred_element_type=F32)
        s = s * (HEAD_DIM ** -0.5) - (slope * dist)[None]
        s = jnp.where(valid[None], s, NEG_BIG)
        m = jnp.maximum(jnp.max(s, axis=2, keepdims=True), sink[None])
        p = jnp.exp(s - m)
        l = jnp.sum(p, axis=2, keepdims=True) + jnp.exp(sink[None] - m)
        o = jnp.einsum('bqk,bkd->bqd', p.astype(BF16), vb, preferred_element_type=F32) / l
        for g in range(GQA_GROUP):
            h = n * GQA_GROUP + g
            o_ref[:, :, h * HEAD_DIM:(h + 1) * HEAD_DIM] = o[:, g * t:(g + 1) * t, :]


def _attn_sample(q, kn, vn, ck, cv, sinks, bs=16):
    db, t = q.shape[0], q.shape[1]
    seq3 = lambda w: pl.BlockSpec((bs, t, w), lambda i: (i, 0, 0))
    cache = pl.BlockSpec((bs, WINDOW, KV_COLS), lambda i: (i, 0, 0))
    return pl.pallas_call(
        _attn_sample_kernel,
        grid=(db // bs,),
        in_specs=[pl.BlockSpec(memory_space=pltpu.SMEM), seq3(Q_COLS), seq3(KV_COLS), seq3(KV_COLS), cache, cache],
        out_specs=seq3(Q_COLS),
        out_shape=jax.ShapeDtypeStruct((db, t, Q_COLS), F32),
        compiler_params=_cparams("parallel"),
        name="attn_sample",
    )(sinks, q, kn, vn, ck, cv)


def _rwkv_prep_kernel(pb_ref, halo_ref, init_ref, mu_ref, w0_ref, wup_ref, a0_ref, aup_ref, gup_ref,
                      kk_ref, ka_ref, rk_ref, ones_ref,
                      r_o, w_o, k_o, v_o, kk_o, nkka_o, bonus_o, g_o, *, b):
    i = pl.program_id(0)
    pb = pb_ref[...]
    tm = pb.shape[0]
    halo = jnp.where(i == 0, init_ref[...], halo_ref[...])
    prev = halo if tm == b else jnp.concatenate([halo, pb[:tm - b]], axis=0)
    xs = pb + (prev - pb) * mu_ref[...]
    r = xs[:, :D_B]
    k = xs[:, D_B:2 * D_B]
    v = xs[:, 2 * D_B:3 * D_B]
    o1 = 3 * D_B
    wd = xs[:, o1:o1 + D_LORA_W]
    ad = xs[:, o1 + D_LORA_W:o1 + D_LORA_W + D_LORA_A]
    gd = xs[:, o1 + D_LORA_W + D_LORA_A:]
    z = -(w0_ref[...] + _dot(jnp.tanh(wd).astype(BF16), wup_ref[...]))
    softplus = jnp.maximum(z, 0.0) + jnp.log(1.0 + jnp.exp(-jnp.abs(z)))
    decay = jnp.exp(-jnp.exp(-softplus - 0.5))
    a = _sigmoid(a0_ref[...] + _dot(ad.astype(BF16), aup_ref[...]))
    g_o[...] = _dot(_sigmoid(gd).astype(BF16), gup_ref[...])
    ones = ones_ref[...]
    kk = k * kk_ref[...]
    kk = kk * lax.rsqrt(jnp.maximum(_dot2(kk * kk, ones), 1e-24))
    k2 = k * (1.0 + (a - 1.0) * ka_ref[...])
    bonus_o[...] = _dot2(r * k2 * rk_ref[...], ones) * v

    half = HEAD_DIM // 2
    lane8 = lax.broadcasted_iota(I32, (SEQ_PER_GROUP, 128), 1)
    low8 = lane8 < HEAD_DIM
    first_copy = jnp.bitwise_and(lax.broadcasted_iota(I32, (half, 128), 1), SEQ_PER_GROUP) == 0
    pairs = ((r, decay, r_o, w_o), (k2, kk, k_o, kk_o), (-(kk * a), v, nkka_o, None))
    for u in range(tm // SEQ_PER_GROUP):
        rows = slice(u * SEQ_PER_GROUP, (u + 1) * SEQ_PER_GROUP)
        for x, y, x_o, y_o in pairs:
            xu, yu = x[rows], y[rows]
            pieces = []
            for h in range(N_HEADS_B):
                cs = slice((h // 2) * 128, (h // 2 + 1) * 128)
                if h % 2 == 0:
                    p = jnp.where(low8, xu[:, cs], pltpu.roll(yu[:, cs], HEAD_DIM, 1))
                else:
                    p = jnp.where(low8, pltpu.roll(xu[:, cs], HEAD_DIM, 1), yu[:, cs])
                pieces += [p, p]
            tr = jnp.concatenate(pieces, axis=0).T
            x_o[u] = tr[:HEAD_DIM]
            if y_o is not None:
                y_o[u] = tr[HEAD_DIM:]
            else:
                v_o[u] = jnp.where(first_copy, tr[HEAD_DIM:HEAD_DIM + half], tr[HEAD_DIM + half:])


def _rwkv_prep(pb, init, p, b, tm):
    n = pb.shape[0]
    units = n // SEQ_PER_GROUP
    tu = tm // SEQ_PER_GROUP
    half = HEAD_DIM // 2
    row = lambda w: pl.BlockSpec((1, w), lambda i: (0, 0))
    full = lambda r, c: pl.BlockSpec((r, c), lambda i: (0, 0))
    tile = pl.BlockSpec((tm, D_B), lambda i: (i, 0))
    kspec = pl.BlockSpec((tu, HEAD_DIM, 128), lambda i: (i, 0, 0))
    vspec = pl.BlockSpec((tu, half, 128), lambda i: (i, 0, 0))
    kshape = jax.ShapeDtypeStruct((units, HEAD_DIM, 128), F32)
    halo_blocks = tm // b
    return pl.pallas_call(
        functools.partial(_rwkv_prep_kernel, b=b),
        grid=(n // tm,),
        in_specs=[pl.BlockSpec((tm, D_B_IN), lambda i: (i, 0)),
                  pl.BlockSpec((b, D_B_IN), lambda i: (jnp.maximum(i * halo_blocks - 1, 0), 0)),
                  full(b, D_B_IN), row(D_B_IN), row(D_B), full(D_LORA_W, D_B), row(D_B), full(D_LORA_A, D_B),
                  full(D_LORA_G, D_B), row(D_B), row(D_B), row(D_B), full(D_B, D_B)],
        out_specs=[kspec, kspec, kspec, vspec, kspec, kspec, tile, tile],
        out_shape=[kshape, kshape, kshape, jax.ShapeDtypeStruct((units, half, 128), F32), kshape, kshape,
                   jax.ShapeDtypeStruct((n, D_B), F32), jax.ShapeDtypeStruct((n, D_B), F32)],
        compiler_params=_cparams("arbitrary"),
        name="rwkv_prep",
    )(pb, pb, init, p['mu'], p['w0'], p['w_up'], p['a0'], p['a_up'], p['g_up'], p['k_k'], p['k_a'], p['r_k'],
      p['head_ones'])


def _wkv_scan_kernel(r_ref, w_ref, k_ref, kk_ref, nkka_ref, v_ref, s0_ref, o_ref, st_ref, s_scr):
    j = pl.program_id(1)

    @pl.when(j == 0)
    def _():
        s_scr[...] = s0_ref[0]

    tc = r_ref.shape[0]
    nsub = (HEAD_DIM // 2) // 8

    def bcast(ref, s, kx):
        return jnp.broadcast_to(ref[s, pl.ds(kx, 1), :], (8, 128))

    acc0 = [[jnp.zeros((8, 128), F32) for _ in range(2)] for _ in range(nsub)]
    for kx in range(HEAD_DIM):
        kkr = bcast(kk_ref, 0, kx)
        for i in range(nsub):
            acc0[i][kx % 2] = acc0[i][kx % 2] + s_scr[kx, 8 * i:8 * i + 8, :] * kkr

    def step(s, sa):
        nxt = jnp.minimum(s + 1, tc - 1)
        vv = [v_ref[s, 8 * i:8 * i + 8, :] for i in range(nsub)]
        oacc = [[jnp.zeros((8, 128), F32) for _ in range(2)] for _ in range(nsub)]
        nacc = [[jnp.zeros((8, 128), F32) for _ in range(2)] for _ in range(nsub)]
        for kx in range(HEAD_DIM):
            rr, wr, kr = bcast(r_ref, s, kx), bcast(w_ref, s, kx), bcast(k_ref, s, kx)
            nk, kkn = bcast(nkka_ref, s, kx), bcast(kk_ref, nxt, kx)
            for i in range(nsub):
                rows = slice(8 * i, 8 * i + 8)
                sk = s_scr[kx, rows, :] * wr + sa[i] * nk + vv[i] * kr
                s_scr[kx, rows, :] = sk
                oacc[i][kx % 2] = oacc[i][kx % 2] + sk * rr
                nacc[i][kx % 2] = nacc[i][kx % 2] + sk * kkn
        o_ref[s] = jnp.concatenate([a[0] + a[1] for a in oacc], axis=0)
        return [a[0] + a[1] for a in nacc]

    lax.fori_loop(0, tc, step, [a[0] + a[1] for a in acc0])

    @pl.when(j == pl.num_programs(1) - 1)
    def _():
        st_ref[0] = s_scr[...]


def _wkv_scan(r, w, k, kk, nkka, v, s0, b, t, tc):
    g = b // SEQ_PER_GROUP
    half = HEAD_DIM // 2
    kview = lambda z: z.reshape(t, g, HEAD_DIM, 128)
    kspec = pl.BlockSpec((tc, None, HEAD_DIM, 128), lambda gi, j: (j, gi, 0, 0))
    vspec = pl.BlockSpec((tc, None, half, 128), lambda gi, j: (j, gi, 0, 0))
    sspec = pl.BlockSpec((1, HEAD_DIM, half, 128), lambda gi, j: (gi, 0, 0, 0))
    o, st = pl.pallas_call(
        _wkv_scan_kernel,
        grid=(g, t // tc),
        in_specs=[kspec] * 5 + [vspec, sspec],
        out_specs=[vspec, sspec],
        out_shape=[jax.ShapeDtypeStruct((t, g, half, 128), F32),
                   jax.ShapeDtypeStruct((g, HEAD_DIM, half, 128), F32)],
        scratch_shapes=[pltpu.VMEM((HEAD_DIM, half, 128), F32)],
        compiler_params=_cparams("parallel", "arbitrary"),
        name="wkv_scan",
    )(kview(r), kview(w), kview(k), kview(kk), kview(nkka), v.reshape(t, g, half, 128), s0)
    return o.reshape(t * g, half, 128), st


def _state_to_scan(s, b):
    g = b // SEQ_PER_GROUP
    s = s.reshape(g, SEQ_PER_GROUP, N_HEADS_B, 2, HEAD_DIM // 2, HEAD_DIM).transpose(0, 5, 4, 2, 3, 1)
    return s.reshape(g, HEAD_DIM, HEAD_DIM // 2, 128)


def _state_from_scan(s, b):
    g = b // SEQ_PER_GROUP
    s = s.reshape(g, HEAD_DIM, HEAD_DIM // 2, N_HEADS_B, 2, SEQ_PER_GROUP).transpose(0, 5, 3, 4, 2, 1)
    return s.reshape(b, N_HEADS_B, HEAD_DIM, HEAD_DIM)


def _route_tile(x, nrm_ref, wr_ref, br_ref, cnt_scr, xn_o, idx_o, wts_o):
    tm = x.shape[0]
    xn = _rms(x, nrm_ref[...])
    hi, lo = _split_bf16(xn)
    _store_row_tiles(xn_o, xn)
    wr = wr_ref[...]
    pa = _dot(hi, wr)
    pb = _dot(lo, wr)
    lg = pa[:, 0:32] + pa[:, 32:64] + pb[:, 0:32] + pb[:, 32:64] + br_ref[...]
    col = lambda c: lg[:, c:c + 1]
    c = [col(gx) for gx in range(N_EGROUPS)]
    m = jnp.maximum(jnp.maximum(c[0], c[1]), jnp.maximum(c[2], c[3]))
    den = jnp.exp(c[0] - m) + jnp.exp(c[1] - m) + jnp.exp(c[2] - m) + jnp.exp(c[3] - m)
    pg = 1.0 / den
    gi = jnp.where(c[0] >= m, 0, jnp.where(c[1] >= m, 1, jnp.where(c[2] >= m, 2, 3))).astype(I32)
    sel = []
    for e in range(EXP_PER_GROUP):
        sel.append(jnp.where(gi == 0, col(4 + e), jnp.where(gi == 1, col(8 + e),
                                                            jnp.where(gi == 2, col(12 + e), col(16 + e)))))
    v1 = jnp.maximum(jnp.maximum(sel[0], sel[1]), jnp.maximum(sel[2], sel[3]))
    i1 = jnp.where(sel[0] >= v1, 0, jnp.where(sel[1] >= v1, 1, jnp.where(sel[2] >= v1, 2, 3))).astype(I32)
    rest = [jnp.where(i1 == e, -jnp.inf, sel[e]) for e in range(EXP_PER_GROUP)]
    v2 = jnp.maximum(jnp.maximum(rest[0], rest[1]), jnp.maximum(rest[2], rest[3]))
    i2 = jnp.where(rest[0] >= v2, 0, jnp.where(rest[1] >= v2, 1, jnp.where(rest[2] >= v2, 2, 3))).astype(I32)
    tt = jnp.exp(v2 - v1)
    w1 = pg / (1.0 + tt)
    w2 = pg * tt / (1.0 + tt)
    e1 = gi * EXP_PER_GROUP + i1
    e2 = gi * EXP_PER_GROUP + i2
    lane = lax.broadcasted_iota(I32, (tm, N_EXPERTS), 1)
    oh1 = lane == e1
    oh2 = lane == e2
    oh = jnp.where(jnp.logical_or(oh1, oh2), 1.0, 0.0)
    ri = lax.broadcasted_iota(I32, (tm, tm), 0)
    ci = lax.broadcasted_iota(I32, (tm, tm), 1)
    ltri = jnp.where(ri > ci, 1.0, 0.0).astype(BF16)
    cnt = cnt_scr[0:1, 0:N_EXPERTS]
    pre = _dot(ltri, oh.astype(BF16)) + cnt
    rank1 = jnp.sum(jnp.where(oh1, pre, 0.0), axis=1, keepdims=True)
    rank2 = jnp.sum(jnp.where(oh2, pre, 0.0), axis=1, keepdims=True)
    cnt_scr[0:1, 0:N_EXPERTS] = cnt + jnp.sum(oh, axis=0, keepdims=True)
    lw = lax.broadcasted_iota(I32, (tm, 128), 1)
    wts_o[...] = jnp.where(lw == 0, w1, jnp.where(lw == 1, w2, 0.0))
    cols = jnp.where(lw == 0, e1.astype(F32), jnp.where(lw == 1, e2.astype(F32),
                                                         jnp.where(lw == 2, rank1, jnp.where(lw == 3, rank2, 0.0))))
    idx_o[...] = cols.T[0:8, :].astype(I32)


def _route_out_specs(tm):
    return [pl.BlockSpec((tm * ROW_CHUNKS, 128), lambda i: (i, 0)),
            pl.BlockSpec((8, tm), lambda i: (0, i)),
            pl.BlockSpec((tm, 128), lambda i: (i, 0)),
            pl.BlockSpec((8, 128), lambda i: (0, 0))]


def _route_out_shapes(n):
    return [jax.ShapeDtypeStruct((n * ROW_CHUNKS, 128), F32),
            jax.ShapeDtypeStruct((8, n), I32),
            jax.ShapeDtypeStruct((n, 128), F32),
            jax.ShapeDtypeStruct((8, 128), F32)]


def _route_in_specs():
    return [pl.BlockSpec((1, D_MODEL), lambda i: (0, 0)),
            pl.BlockSpec((D_MODEL, 128), lambda i: (0, 0)),
            pl.BlockSpec((1, 32), lambda i: (0, 0))]


def _mix0_out_kernel(o_ref, bonus_ref, g_ref, attn_ref, x_ref, lnw_ref, lnb_ref, avg_ref, wo_ref,
                     nrm_ref, wr_ref, br_ref,
                     x1_o, xn_o, idx_o, wts_o, cnt_o, cnt_scr, xs_scr, at_scr, *, b):
    @pl.when(pl.program_id(0) == 0)
    def _():
        cnt_scr[...] = jnp.zeros_like(cnt_scr)

    avg = avg_ref[...]
    half = HEAD_DIM // 2
    lane8 = lax.broadcasted_iota(I32, (SEQ_PER_GROUP, 128), 1)
    unit_rows = []
    for u in range(o_ref.shape[0]):
        ot = jnp.concatenate([o_ref[u], jnp.zeros((128 - half, 128), F32)], axis=0).T
        cols = []
        for jj in range(N_HEADS_B // 2):
            q = [ot[(4 * jj + i) * SEQ_PER_GROUP:(4 * jj + i + 1) * SEQ_PER_GROUP] for i in range(4)]
            c = jnp.where(lane8 < half, q[0], pltpu.roll(q[1], half, 1))
            c = jnp.where(lane8 < 2 * half, c, pltpu.roll(q[2], 2 * half, 1))
            c = jnp.where(lane8 < 3 * half, c, pltpu.roll(q[3], 3 * half, 1))
            cols.append(c)
        unit_rows.append(jnp.concatenate(cols, axis=1))
    o = jnp.concatenate(unit_rows, axis=0)
    d = o - _dot2(o, avg)
    var = _dot2(d * d, avg)
    on = d * lax.rsqrt(var + RWKV_GN_EPS) * lnw_ref[...] + lnb_ref[...]
    rout = ((on + bonus_ref[...]) * g_ref[...]).astype(BF16)
    x = _load_time_major(x_ref, xs_scr, b)
    if len(x_ref.shape) == 3:
        attn = _interleave_rows([attn_ref[:, s * Q_COLS:(s + 1) * Q_COLS] for s in range(b)], at_scr)
    else:
        attn = attn_ref[...]
    y = _dot(attn.astype(BF16), wo_ref[:Q_COLS, :]) + _dot(rout, wo_ref[Q_COLS:, :]) + x
    x1_o[...] = y
    _route_tile(y, nrm_ref, wr_ref, br_ref, cnt_scr, xn_o, idx_o, wts_o)
    cnt_o[...] = cnt_scr[...]


def _mix0_out(o, bonus, g, attn, x, p, rp, b, t, tm):
    n = b * t
    xshape, xspec = _time_major_spec(b, t, D_MODEL, tm)
    aspec = (pl.BlockSpec((tm // b, b * Q_COLS), lambda i: (i, 0)) if len(xshape) == 3
             else pl.BlockSpec((tm, Q_COLS), lambda i: (i, 0)))
    tile = lambda w: pl.BlockSpec((tm, w), lambda i: (i, 0))
    row = lambda w: pl.BlockSpec((1, w), lambda i: (0, 0))
    full = lambda r, c: pl.BlockSpec((r, c), lambda i: (0, 0))
    return pl.pallas_call(
        functools.partial(_mix0_out_kernel, b=b),
        grid=(n // tm,),
        in_specs=[pl.BlockSpec((tm // SEQ_PER_GROUP, HEAD_DIM // 2, 128), lambda i: (i, 0, 0)),
                  tile(D_B), tile(D_B), aspec, xspec, row(D_B), row(D_B),
                  full(D_B, D_B), full(D_MODEL, D_MODEL)] + _route_in_specs(),
        out_specs=[tile(D_MODEL)] + _route_out_specs(tm),
        out_shape=[jax.ShapeDtypeStruct((n, D_MODEL), F32)] + _route_out_shapes(n),
        scratch_shapes=[pltpu.VMEM((8, 128), F32), pltpu.VMEM((D_MODEL // 128, tm, 128), F32),
                        pltpu.VMEM((Q_COLS // 128, tm, 128), F32)],
        compiler_params=_cparams("arbitrary"),
        name="mix0_out",
    )(o, bonus, g, attn, x.reshape(xshape), p['ln_w'], p['ln_b'], p['head_avg'], p['w_out'], rp['norm'], rp['w'], rp['b'])


def _row_dma_start(idx_ref, pos, r, src_hbm, dst, sem, priority):
    src_row = pl.multiple_of(idx_ref[pos] * ROW_CHUNKS, ROW_CHUNKS)
    dst_row = pl.multiple_of(r * ROW_CHUNKS, ROW_CHUNKS)
    pltpu.make_async_copy(src_hbm.at[pl.ds(src_row, ROW_CHUNKS)], dst.at[pl.ds(dst_row, ROW_CHUNKS)],
                          sem).start(priority=priority)


def _row_gather_start(idx_ref, base, n_rows, src_hbm, dst, sem):
    def body(r2, carry):
        for p in range(2):
            _row_dma_start(idx_ref, base + 2 * r2 + p, 2 * r2 + p, src_hbm, dst, sem, p)
        return carry

    lax.fori_loop(0, n_rows // 2, body, 0, unroll=4)


def _row_gather_wait(dst, sem):
    pltpu.make_async_copy(dst, dst, sem).wait()


def _expert_kernel(te_ref, nu_ref, pad_ref, p1_ref, p2_ref, x_hbm, wg_ref, wu_ref, wd_ref, o_ref,
                   src_ref, xbuf, sem, wgb, wub, wdb):
    i = pl.program_id(0)
    nu = nu_ref[0]
    mt = xbuf.shape[1] // ROW_CHUNKS

    @pl.when(i == 0)
    def _():
        def fill_pad(e, carry):
            start = pad_ref[2 * e]

            def body(r, c):
                src_ref[r] = jnp.minimum(r - start, p1_ref.shape[0] - 1)
                return c

            lax.fori_loop(start, pad_ref[2 * e + 1], body, 0)
            return carry

        lax.fori_loop(0, N_EXPERTS + 1, fill_pad, 0)

        def invert(n, carry):
            src_ref[p1_ref[n]] = n
            src_ref[p2_ref[n]] = n
            return carry

        lax.fori_loop(0, p1_ref.shape[0], invert, 0, unroll=8)
        _row_gather_start(src_ref, 0, mt, x_hbm, xbuf.at[0], sem.at[0])

    new_expert = jnp.logical_or(i == 0, te_ref[i] != te_ref[jnp.maximum(i - 1, 0)])

    @pl.when(jnp.logical_and(i < nu, new_expert))
    def _():
        wgb[...] = wg_ref[0].astype(BF16)
        wub[...] = wu_ref[0].astype(BF16)
        wdb[...] = wd_ref[0].astype(BF16)

    @pl.when(i < nu)
    def _():
        cur = i % 2
        nxt = 1 - cur
        base = (i + 1) * mt
        pieces = 2 * (D_FF_E // 256) * (D_MODEL // 256) + (D_MODEL // 256) * (D_FF_E // 256)
        per = -(-mt // pieces)
        issued = [0]

        def start_some():
            for r in range(issued[0], min(issued[0] + per, mt)):
                _row_dma_start(src_ref, base + r, r, x_hbm, xbuf.at[nxt], sem.at[nxt], r % 2)
            issued[0] = min(issued[0] + per, mt)

        def block_dot(a, w_ref, c):
            acc = None
            for kt in range(a.shape[1] // 256):
                part = _dot(a[:, kt * 256:(kt + 1) * 256], w_ref[kt * 256:(kt + 1) * 256, c * 256:(c + 1) * 256])
                acc = part if acc is None else acc + part
                start_some()
            return acc

        _row_gather_wait(xbuf.at[cur], sem.at[cur])
        x = _load_row_tiles(xbuf.at[cur]).astype(BF16)
        hg = jnp.concatenate([block_dot(x, wgb, c) for c in range(D_FF_E // 256)], axis=1)
        hu = jnp.concatenate([block_dot(x, wub, c) for c in range(D_FF_E // 256)], axis=1)
        h = ((hg * _sigmoid(hg)) * hu).astype(BF16)
        for c in range(D_MODEL // 256):
            res = block_dot(h, wdb, c)
            for cc in range(2):
                o_ref[pl.ds(2 * c + cc, mt, stride=ROW_CHUNKS), :] = res[:, cc * 128:(cc + 1) * 128]
        assert issued[0] == mt

    @pl.when(i == nu)
    def _():
        _row_gather_wait(xbuf.at[i % 2], sem.at[i % 2])

    @pl.when(i >= nu)
    def _():
        o_ref[...] = jnp.zeros_like(o_ref)


def _experts(xn, r, mt, pos1, pos2, tile_expert, n_used, pad_ranges, wg, wu, wd):
    wspec = lambda a, b: pl.BlockSpec((1, a, b), lambda i, te, nu, pad, p1, p2: (te[i], 0, 0))
    grid_spec = pltpu.PrefetchScalarGridSpec(
        num_scalar_prefetch=5,
        grid=(r // mt,),
        in_specs=[pl.BlockSpec(memory_space=pl.ANY),
                  wspec(D_MODEL, D_FF_E), wspec(D_MODEL, D_FF_E), wspec(D_FF_E, D_MODEL)],
        out_specs=pl.BlockSpec((mt * ROW_CHUNKS, 128), lambda i, te, nu, pad, p1, p2: (i, 0)),
        scratch_shapes=[pltpu.SMEM((r,), I32), pltpu.VMEM((2, mt * ROW_CHUNKS, 128), F32),
                        pltpu.SemaphoreType.DMA((2,)),
                        pltpu.VMEM((D_MODEL, D_FF_E), BF16), pltpu.VMEM((D_MODEL, D_FF_E), BF16),
                        pltpu.VMEM((D_FF_E, D_MODEL), BF16)],
    )
    return pl.pallas_call(
        _expert_kernel,
        grid_spec=grid_spec,
        out_shape=jax.ShapeDtypeStruct((r * ROW_CHUNKS, 128), F32),
        compiler_params=_cparams("arbitrary"),
        name="moe_experts",
    )(tile_expert, n_used, pad_ranges, pos1, pos2, xn, wg, wu, wd)


def _moe(xn, idx, cnt, ep):
    n = xn.shape[0] // ROW_CHUNKS
    mt = 512 if n >= 8192 else 128
    rows = 2 * n + N_EXPERTS * mt
    counts = cnt[0, :N_EXPERTS].astype(I32)
    padded = ((counts + mt - 1) // mt) * mt
    ends = jnp.cumsum(padded)
    offs = ends - padded
    pos1 = offs[idx[0]] + idx[2]
    pos2 = offs[idx[1]] + idx[3]
    pad_ranges = jnp.stack([jnp.append(offs + counts, ends[-1]), jnp.append(ends, ends[-1] + mt)], axis=1)
    pad_ranges = pad_ranges.reshape(-1).astype(I32)
    n_used = (ends[-1] // mt).astype(I32)
    starts = jnp.arange(rows // mt, dtype=I32) * mt
    starts = jnp.minimum(starts, ends[-1] - 1)
    tile_expert = jnp.sum((starts[:, None] >= ends[None, :]).astype(I32), axis=1)
    tile_expert = jnp.minimum(tile_expert, N_EXPERTS - 1).astype(I32)
    out = _experts(xn, rows, mt, pos1, pos2, tile_expert, n_used.reshape(1), pad_ranges, ep['wg'], ep['wu'], ep['wd'])
    return out, pos1, pos2


def _combine_kernel(p1_ref, p2_ref, x_ref, wts_ref, nrm_ref, out_hbm, y_o, gbuf, sem, ys_scr, *, final, b):
    i = pl.program_id(0)
    tm = x_ref.shape[0]

    def start(tile, slot):
        _row_gather_start(p1_ref, tile * tm, tm, out_hbm, gbuf.at[slot, 0], sem.at[slot])
        _row_gather_start(p2_ref, tile * tm, tm, out_hbm, gbuf.at[slot, 1], sem.at[slot])

    @pl.when(i == 0)
    def _():
        start(0, 0)

    @pl.when(i + 1 < pl.num_programs(0))
    def _():
        start(i + 1, (i + 1) % 2)

    cur = i % 2
    _row_gather_wait(gbuf.at[cur], sem.at[cur])
    wts = wts_ref[...]
    y = x_ref[...] + wts[:, 0:1] * _load_row_tiles(gbuf.at[cur, 0]) + wts[:, 1:2] * _load_row_tiles(gbuf.at[cur, 1])
    if final:
        _store_batch_major(y_o, ys_scr, _rms(y, nrm_ref[...]), b)
    else:
        y_o[...] = y


def _moe_combine(x, wts, out, pos1, pos2, nrm, final, b, t, tm=256):
    n = x.shape[0]
    tile = lambda w: pl.BlockSpec((tm, w), lambda i, p1, p2: (i, 0))
    if final:
        yshape, yspec = _time_major_spec(b, t, D_MODEL, tm)
    else:
        yshape, yspec = (n, D_MODEL), tile(D_MODEL)
    grid_spec = pltpu.PrefetchScalarGridSpec(
        num_scalar_prefetch=2,
        grid=(n // tm,),
        in_specs=[tile(D_MODEL), tile(128), pl.BlockSpec((1, D_MODEL), lambda i, p1, p2: (0, 0)),
                  pl.BlockSpec(memory_space=pl.ANY)],
        out_specs=yspec,
        scratch_shapes=[pltpu.VMEM((2, 2, tm * ROW_CHUNKS, 128), F32), pltpu.SemaphoreType.DMA((2,)),
                        pltpu.VMEM((D_MODEL // 128, tm, 128), F32)],
    )
    return pl.pallas_call(
        functools.partial(_combine_kernel, final=final, b=b),
        grid_spec=grid_spec,
        out_shape=jax.ShapeDtypeStruct(yshape, F32),
        compiler_params=_cparams("arbitrary"),
        name="moe_combine_final" if final else "moe_combine",
    )(pos1, pos2, x, wts, nrm, out)


def _gelu_tanh(x):
    return 0.5 * x * (1.0 + jnp.tanh(0.7978845608028654 * (x + 0.044715 * (x * x * x))))


def _mix1_kernel(x_ref, nmix_ref, bre_ref, bim_ref, are_ref, aim_ref, cre_ref, cim_ref,
                 dsk_ref, wo_ref, h0r_ref, h0i_ref, nrm_ref, wr_ref, br_ref,
                 x2_o, xn_o, idx_o, wts_o, cnt_o, hr_o, hi_o,
                 bur, bui, hr_scr, hi_scr, cnt_scr, *, b, cw):
    @pl.when(pl.program_id(0) == 0)
    def _():
        cnt_scr[...] = jnp.zeros_like(cnt_scr)
        hr_scr[...] = h0r_ref[...]
        hi_scr[...] = h0i_ref[...]

    x = x_ref[...]
    u = _rms(x, nmix_ref[...])
    ub = u.astype(BF16)
    nblk = bre_ref.shape[0]
    kin = D_MODEL // nblk
    kst = S5_STATE // nblk
    for cb in range(nblk):
        ucb = ub[:, cb * kin:(cb + 1) * kin]
        bur[:, cb * kst:(cb + 1) * kst] = _dot(ucb, bre_ref[cb])
        bui[:, cb * kst:(cb + 1) * kst] = _dot(ucb, bim_ref[cb])

    tc = x.shape[0] // b
    for c0 in range(0, S5_STATE, cw):
        cs = slice(c0, c0 + cw)
        ar = jnp.broadcast_to(are_ref[:, cs], (b, cw))
        ai = jnp.broadcast_to(aim_ref[:, cs], (b, cw))

        def step(s, carry, cs=cs, ar=ar, ai=ai):
            hr, hi = carry
            rows = pl.ds(pl.multiple_of(s * b, b), b)
            nr = ar * hr - ai * hi + bur[rows, cs]
            ni = ar * hi + ai * hr + bui[rows, cs]
            bur[rows, cs] = nr
            bui[rows, cs] = ni
            return nr, ni

        hr, hi = lax.fori_loop(0, tc, step, (hr_scr[:, cs], hi_scr[:, cs]), unroll=True)
        hr_scr[:, cs] = hr
        hi_scr[:, cs] = hi

    ych = []
    for cb in range(nblk):
        ss = slice(cb * kst, (cb + 1) * kst)
        ych.append(_dot(bur[:, ss].astype(BF16), cre_ref[cb]) - _dot(bui[:, ss].astype(BF16), cim_ref[cb]))
    y = jnp.concatenate(ych, axis=1) + dsk_ref[...] * u
    z = _dot(_gelu_tanh(y).astype(BF16), wo_ref[...])
    x2 = x + z[:, :D_MODEL] * _sigmoid(z[:, D_MODEL:])
    x2_o[...] = x2
    _route_tile(x2, nrm_ref, wr_ref, br_ref, cnt_scr, xn_o, idx_o, wts_o)
    cnt_o[...] = cnt_scr[...]
    hr_o[...] = hr_scr[...]
    hi_o[...] = hi_scr[...]


def _mix1(x, sp, rp, h0r, h0i, b, tr):
    n = x.shape[0]
    cw = 1024 if b == 8 else 128
    tile = lambda w: pl.BlockSpec((tr, w), lambda i: (i, 0))
    row = lambda w: pl.BlockSpec((1, w), lambda i: (0, 0))
    full = lambda *s: pl.BlockSpec(s, lambda i: (0,) * len(s))
    nblk = sp['b_re'].shape[0]
    return pl.pallas_call(
        functools.partial(_mix1_kernel, b=b, cw=cw),
        grid=(n // tr,),
        in_specs=[tile(D_MODEL), row(D_MODEL),
                  full(nblk, D_MODEL // nblk, S5_STATE // nblk), full(nblk, D_MODEL // nblk, S5_STATE // nblk),
                  row(S5_STATE), row(S5_STATE),
                  full(nblk, S5_STATE // nblk, D_MODEL // nblk), full(nblk, S5_STATE // nblk, D_MODEL // nblk),
                  row(D_MODEL), full(D_MODEL, 2 * D_MODEL), full(b, S5_STATE), full(b, S5_STATE)] + _route_in_specs(),
        out_specs=[tile(D_MODEL)] + _route_out_specs(tr) + [full(b, S5_STATE), full(b, S5_STATE)],
        out_shape=[jax.ShapeDtypeStruct((n, D_MODEL), F32)] + _route_out_shapes(n)
                  + [jax.ShapeDtypeStruct((b, S5_STATE), F32)] * 2,
        scratch_shapes=[pltpu.VMEM((tr, S5_STATE), F32), pltpu.VMEM((tr, S5_STATE), F32),
                        pltpu.VMEM((b, S5_STATE), F32), pltpu.VMEM((b, S5_STATE), F32),
                        pltpu.VMEM((8, 128), F32)],
        compiler_params=_cparams("arbitrary"),
        name="mix1",
    )(x, sp['norm'], sp['b_re'], sp['b_im'], sp['a_re'], sp['a_im'], sp['c_re'], sp['c_im'],
      sp['d'], sp['w_out'], h0r, h0i, rp['norm'], rp['w'], rp['b'])


def _router_params(norm, w_rc, b_rc, w_rf, b_rf):
    w = jnp.concatenate([w_rc, w_rf.reshape(D_MODEL, N_EXPERTS), jnp.zeros((D_MODEL, 12), F32)], axis=1)
    hi = w.astype(BF16)
    lo = (w - hi.astype(F32)).astype(BF16)
    wcat = jnp.concatenate([hi, lo, jnp.zeros((D_MODEL, 64), BF16)], axis=1)
    bias = jnp.concatenate([b_rc, b_rf.reshape(-1), jnp.zeros((12,), F32)]).reshape(1, 32)
    return {'norm': norm.reshape(1, D_MODEL), 'w': wcat, 'b': bias}


def _expert_params(wg, wu, wd):
    return {'wg': wg, 'wu': wu, 'wd': wd}


def _s5_params(norm, a_re, a_im, log_dt, b_re, b_im, c_re, c_im, d_skip, w_out, nblk=8):
    dt = jnp.exp(log_dt)
    mag = jnp.exp(dt * a_re)
    ab_re, ab_im = mag * jnp.cos(dt * a_im), mag * jnp.sin(dt * a_im)
    den = a_re * a_re + a_im * a_im
    f_re = ((ab_re - 1.0) * a_re + ab_im * a_im) / den
    f_im = (ab_im * a_re - (ab_re - 1.0) * a_im) / den
    bb_re = f_re[..., None] * b_re - f_im[..., None] * b_im
    bb_im = f_re[..., None] * b_im + f_im[..., None] * b_re
    gpb = S5_GROUPS // nblk
    eye = jnp.eye(gpb, dtype=F32)

    def in_blocks(bb):
        bb = bb.reshape(nblk, gpb, S5_P, S5_CH)
        w = jnp.einsum('ngpc,gh->ngchp', bb, eye)
        return w.reshape(nblk, gpb * S5_CH, gpb * S5_P).astype(BF16)

    def out_blocks(cc):
        cc = cc.reshape(nblk, gpb, S5_CH, S5_P)
        w = jnp.einsum('ngcp,gh->ngphc', cc, eye)
        return w.reshape(nblk, gpb * S5_P, gpb * S5_CH).astype(BF16)

    return {'norm': norm.reshape(1, D_MODEL), 'b_re': in_blocks(bb_re), 'b_im': in_blocks(bb_im),
            'a_re': ab_re.reshape(1, S5_STATE), 'a_im': ab_im.reshape(1, S5_STATE),
            'c_re': out_blocks(c_re), 'c_im': out_blocks(c_im), 'd': d_skip.reshape(1, D_MODEL),
            'w_out': w_out.astype(BF16)}


def _head_block(value):
    hid = jnp.arange(D_B, dtype=I32) // HEAD_DIM
    return jnp.where(hid[:, None] == hid[None, :], value, 0.0).astype(BF16)


def _run_group(x, cache_k, cache_v, shift0, wkv0, h0r, h0i, pr):
    b, t = x.shape[0], x.shape[1]
    n = b * t
    prompt = cache_k is None
    tm = 256 if prompt else 128
    if not prompt:
        x = x.transpose(1, 0, 2)
    q, kv, pb = _in_proj(x, pr['l0_norm'], pr['l0_w_in'], b, t, tm)

    if prompt:
        attn = _attn_prompt(q, kv, pr['sinks'], b, t)
        kv3 = kv[t - WINDOW:].reshape(WINDOW, b, 2, N_KV_A, HEAD_DIM)
        new_k = kv3[:, :, 0].transpose(1, 0, 2, 3)
        new_v = kv3[:, :, 1].transpose(1, 0, 2, 3)
        init = jnp.zeros((b, D_B_IN), F32)
    else:
        qs = q.reshape(t, b, Q_COLS).transpose(1, 0, 2)
        kvs = kv.reshape(t, b, 2 * KV_COLS).transpose(1, 0, 2)
        kn, vn = kvs[..., :KV_COLS], kvs[..., KV_COLS:]
        ck = cache_k.reshape(b, WINDOW, KV_COLS)
        cv = cache_v.reshape(b, WINDOW, KV_COLS)
        attn = _attn_sample(qs, kn, vn, ck, cv, pr['sinks'])
        attn = attn.transpose(1, 0, 2).reshape(n, Q_COLS)
        new_k = jnp.concatenate([ck[:, t:], kn], axis=1).reshape(b, WINDOW, N_KV_A, HEAD_DIM)
        new_v = jnp.concatenate([cv[:, t:], vn], axis=1).reshape(b, WINDOW, N_KV_A, HEAD_DIM)
        init = shift0
    new_shift = pb[n - b:]

    r, w, k, v, kk, nkka, bonus, g = _rwkv_prep(pb, init, pr['rw'], b, tm)
    tc = 64 if prompt else t
    s0 = jnp.zeros((b // SEQ_PER_GROUP, HEAD_DIM, HEAD_DIM // 2, 128), F32) if prompt else _state_to_scan(wkv0, b)
    o, s_fin = _wkv_scan(r, w, k, kk, nkka, v, s0, b, t, tc)
    new_wkv = _state_from_scan(s_fin, b)

    x1, xn, idx, wts, cnt = _mix0_out(o, bonus, g, attn, x, pr['rw'], pr['l0_route'], b, t, 256)
    out, pos1, pos2 = _moe(xn, idx, cnt, pr['l0_exp'])
    x1 = _moe_combine(x1, wts, out, pos1, pos2, pr['final_norm'], False, b, t)

    x2, xn, idx, wts, cnt, hr, hi = _mix1(x1, pr['s5'], pr['l1_route'], h0r, h0i, b, 256)
    out, pos1, pos2 = _moe(xn, idx, cnt, pr['l1_exp'])
    y = _moe_combine(x2, wts, out, pos1, pos2, pr['final_norm'], True, b, t)
    y = y if prompt else y.reshape(t, b, D_MODEL).transpose(1, 0, 2)
    return (y, new_k, new_v, new_shift, new_wkv,
            hr.reshape(b, S5_GROUPS, S5_P), hi.reshape(b, S5_GROUPS, S5_P))


def kernel(x_prompt, x_sample, cache_win_k, cache_win_v, state_shift, state_wkv, state_s5_re, state_s5_im,
           l0_norm_mix, l0_w_in, l0_sinks, l0_mu, l0_w0, l0_w_lora_up, l0_a0, l0_a_lora_up, l0_g_lora_up,
           l0_k_k, l0_k_a, l0_r_k, l0_ln_w, l0_ln_b, l0_w_out,
           l0_norm_ffn, l0_router_coarse, l0_bias_coarse, l0_router_fine, l0_bias_fine,
           l0_exp_gate, l0_exp_up, l0_exp_down,
           l1_norm_mix, l1_s5_a_re, l1_s5_a_im, l1_s5_log_dt, l1_s5_b_re, l1_s5_b_im, l1_s5_c_re, l1_s5_c_im,
           l1_s5_d, l1_w_out,
           l1_norm_ffn, l1_router_coarse, l1_bias_coarse, l1_router_fine, l1_bias_fine,
           l1_exp_gate, l1_exp_up, l1_exp_down,
           final_norm):
    row = lambda z: z.reshape(1, -1)
    pr = {
        'l0_norm': row(l0_norm_mix), 'l0_w_in': l0_w_in.astype(BF16), 'sinks': l0_sinks,
        'rw': {'mu': row(l0_mu), 'w0': row(l0_w0), 'w_up': l0_w_lora_up.astype(BF16), 'a0': row(l0_a0),
               'a_up': l0_a_lora_up.astype(BF16), 'g_up': l0_g_lora_up.astype(BF16), 'k_k': row(l0_k_k),
               'k_a': row(l0_k_a), 'r_k': row(l0_r_k), 'ln_w': row(l0_ln_w), 'ln_b': row(l0_ln_b),
               'head_ones': _head_block(1.0), 'head_avg': _head_block(1.0 / HEAD_DIM),
               'w_out': l0_w_out.astype(BF16)},
        'l0_route': _router_params(l0_norm_ffn, l0_router_coarse, l0_bias_coarse, l0_router_fine, l0_bias_fine),
        'l0_exp': _expert_params(l0_exp_gate, l0_exp_up, l0_exp_down),
        's5': _s5_params(l1_norm_mix, l1_s5_a_re, l1_s5_a_im, l1_s5_log_dt, l1_s5_b_re, l1_s5_b_im,
                         l1_s5_c_re, l1_s5_c_im, l1_s5_d, l1_w_out),
        'l1_route': _router_params(l1_norm_ffn, l1_router_coarse, l1_bias_coarse, l1_router_fine, l1_bias_fine),
        'l1_exp': _expert_params(l1_exp_gate, l1_exp_up, l1_exp_down),
        'final_norm': row(final_norm),
    }
    bp, bs = x_prompt.shape[0], x_sample.shape[0]
    zero_state = jnp.zeros((bp, S5_STATE), F32)
    yp, pk, pv, psh, pwkv, pre, pim = _run_group(x_prompt, None, None, None, None, zero_state, zero_state, pr)
    ys, sk, sv, ssh, swkv, sre, sim = _run_group(
        x_sample, cache_win_k, cache_win_v, state_shift, state_wkv,
        state_s5_re.reshape(bs, S5_STATE), state_s5_im.reshape(bs, S5_STATE), pr)
    return (yp, ys, pk, pv, psh, pwkv, pre, pim, sk, sv, ssh, swkv, sre, sim)
```

```python
import functools

import jax
import jax.numpy as jnp
from jax import lax
from jax.experimental import pallas as pl
from jax.experimental.pallas import tpu as pltpu

F32 = jnp.float32
BF16 = jnp.bfloat16
I32 = jnp.int32

D_MODEL = 1024
HEAD_DIM = 64
N_HEADS_A = 8
N_KV_A = 2
GQA_GROUP = 4
WINDOW = 128
Q_COLS = 512
KV_COLS = 128
D_A_IN = 768
N_HEADS_B = 8
D_B = 512
D_LORA_W = 64
D_LORA_A = 64
D_LORA_G = 128
D_B_IN = 1792
D_IN0 = 2560
RWKV_GN_EPS = 64e-5
S5_CH = 16
S5_GROUPS = 64
S5_P = 64
S5_STATE = S5_GROUPS * S5_P
N_EGROUPS = 4
EXP_PER_GROUP = 4
N_EXPERTS = 16
D_FF_E = 512
RMS_EPS = 1e-5
NEG_BIG = -1e30
PAIRS = 64
SEQ_PER_GROUP = PAIRS // N_HEADS_B
ROW_CHUNKS = D_MODEL // 128
VMEM_LIMIT = 56 * 1024 * 1024


def _cparams(*sem):
    return pltpu.CompilerParams(dimension_semantics=sem, vmem_limit_bytes=VMEM_LIMIT)


def _dot(a, b):
    return jnp.dot(a, b, preferred_element_type=F32)


def _split_bf16(x):
    hi = x.astype(BF16)
    lo = (x - hi.astype(F32)).astype(BF16)
    return hi, lo


def _dot2(x, w):
    hi, lo = _split_bf16(x)
    return _dot(hi, w) + _dot(lo, w)


def _rms(x, g):
    return x * lax.rsqrt(jnp.mean(x * x, axis=-1, keepdims=True) + RMS_EPS) * g


def _sigmoid(x):
    return 1.0 / (1.0 + jnp.exp(-x))


def _store_row_tiles(ref, x):
    rows = x.shape[0]
    for c in range(ROW_CHUNKS):
        ref[pl.ds(c, rows, stride=ROW_CHUNKS), :] = x[:, c * 128:(c + 1) * 128]


def _time_major_spec(b, t, d, tm):
    if b == SEQ_PER_GROUP:
        return (b, t, d), pl.BlockSpec((b, tm // b, d), lambda i, *_: (0, i, 0))
    return (b * t, d), pl.BlockSpec((tm, d), lambda i, *_: (i, 0))


def _interleave_rows(pieces, scr):
    nb, steps = len(pieces), pieces[0].shape[0]
    chunks = pieces[0].shape[1] // 128
    for s, p in enumerate(pieces):
        for c in range(chunks):
            scr[c, pl.ds(s, steps, stride=nb), :] = p[:, c * 128:(c + 1) * 128].astype(scr.dtype)
    return jnp.concatenate([scr[c] for c in range(chunks)], axis=1)


def _deinterleave_rows(x, nb, scr):
    steps = x.shape[0] // nb
    chunks = x.shape[1] // 128
    for c in range(chunks):
        scr[c] = x[:, c * 128:(c + 1) * 128].astype(scr.dtype)
    return [jnp.concatenate([scr[c, pl.ds(s, steps, stride=nb), :] for c in range(chunks)], axis=1)
            for s in range(nb)]


def _load_time_major(x_ref, scr, b):
    if len(x_ref.shape) == 3:
        return _interleave_rows([x_ref[s] for s in range(b)], scr)
    return x_ref[...]


def _store_batch_major(y_ref, scr, y, b):
    if len(y_ref.shape) == 3:
        for s, p in enumerate(_deinterleave_rows(y, b, scr)):
            y_ref[s] = p
    else:
        y_ref[...] = y


def _load_row_tiles(ref):
    rows = ref.shape[0] // ROW_CHUNKS
    return jnp.concatenate([ref[pl.ds(c, rows, stride=ROW_CHUNKS), :] for c in range(ROW_CHUNKS)], axis=1)


def _in_proj_kernel(x_ref, g_ref, w_ref, q_ref, kv_ref, pb_ref, xs_scr, q_scr, kv_scr, *, b):
    x = _load_time_major(x_ref, xs_scr, b)
    xn = _rms(x, g_ref[...]).astype(BF16)
    q = _dot(xn, w_ref[:, :Q_COLS])
    kv = _dot(xn, w_ref[:, Q_COLS:D_A_IN])
    pb_ref[...] = _dot(xn, w_ref[:, D_A_IN:])
    if len(x_ref.shape) == 3:
        q_ref[...] = jnp.concatenate(_deinterleave_rows(q, b, q_scr), axis=1).astype(BF16)
        kv_ref[...] = jnp.concatenate(_deinterleave_rows(kv, b, kv_scr), axis=1)
    else:
        q_ref[...] = q
        kv_ref[...] = kv


def _in_proj(x, g, w_bf16, b, t, tm):
    n = b * t
    xshape, xspec = _time_major_spec(b, t, D_MODEL, tm)
    slab = len(xshape) == 3
    steps = tm // b
    if slab:
        qkv_specs = [pl.BlockSpec((steps, b * Q_COLS), lambda i: (i, 0)),
                     pl.BlockSpec((steps, b * 2 * KV_COLS), lambda i: (i, 0))]
        qkv_shapes = [jax.ShapeDtypeStruct((t, b * Q_COLS), BF16), jax.ShapeDtypeStruct((t, b * 2 * KV_COLS), F32)]
    else:
        qkv_specs = [pl.BlockSpec((tm, Q_COLS), lambda i: (i, 0)), pl.BlockSpec((tm, 2 * KV_COLS), lambda i: (i, 0))]
        qkv_shapes = [jax.ShapeDtypeStruct((n, Q_COLS), F32), jax.ShapeDtypeStruct((n, 2 * KV_COLS), F32)]
    return pl.pallas_call(
        functools.partial(_in_proj_kernel, b=b),
        grid=(n // tm,),
        in_specs=[xspec,
                  pl.BlockSpec((1, D_MODEL), lambda i: (0, 0)),
                  pl.BlockSpec((D_MODEL, D_IN0), lambda i: (0, 0))],
        out_specs=qkv_specs + [pl.BlockSpec((tm, D_B_IN), lambda i: (i, 0))],
        out_shape=qkv_shapes + [jax.ShapeDtypeStruct((n, D_B_IN), F32)],
        scratch_shapes=[pltpu.VMEM((D_MODEL // 128, tm, 128), F32), pltpu.VMEM((Q_COLS // 128, tm, 128), F32),
                        pltpu.VMEM((2 * KV_COLS // 128, tm, 128), F32)],
        compiler_params=_cparams("parallel"),
        name="in_proj",
    )(x.reshape(xshape), g, w_bf16)


def _attn_prompt_kernel(sinks_ref, q_ref, kc_ref, kp_ref, vc_ref, vp_ref, o_ref):
    j = pl.program_id(1)
    qi = lax.broadcasted_iota(I32, (WINDOW, 2 * WINDOW), 0)
    kj = lax.broadcasted_iota(I32, (WINDOW, 2 * WINDOW), 1)
    valid = jnp.logical_and(kj > qi, kj <= qi + WINDOW)
    valid = jnp.logical_and(valid, jnp.logical_or(kj >= WINDOW, j > 0))
    dist = (WINDOW + qi - kj).astype(F32)
    for n in range(N_KV_A):
        cs = slice(n * HEAD_DIM, (n + 1) * HEAD_DIM)
        kb = jnp.concatenate([kp_ref[:, cs], kc_ref[:, cs]], axis=0).astype(BF16)
        vb = jnp.concatenate([vp_ref[:, cs], vc_ref[:, cs]], axis=0).astype(BF16)
        for g in range(GQA_GROUP):
            h = n * GQA_GROUP + g
            hs = slice(h * HEAD_DIM, (h + 1) * HEAD_DIM)
            s = lax.dot_general(q_ref[:, hs], kb, (((1,), (1,)), ((), ())), preferred_element_type=F32)
            s = s * (HEAD_DIM ** -0.5) - (2.0 ** -(h + 1)) * dist
            s = jnp.where(valid, s, NEG_BIG)
            sink = sinks_ref[h]
            m = jnp.maximum(jnp.max(s, axis=1, keepdims=True), sink)
            p = jnp.exp(s - m)
            l = jnp.sum(p, axis=1, keepdims=True) + jnp.exp(sink - m)
            o = _dot(p.astype(BF16), vb) / l
            o_ref[:, hs] = o.astype(BF16)


def _attn_prompt(q2, kv2, sinks, b, t):
    prev = lambda bi, j: jnp.maximum(j - 1, 0)
    return pl.pallas_call(
        _attn_prompt_kernel,
        grid=(b, t // WINDOW),
        in_specs=[pl.BlockSpec(memory_space=pltpu.SMEM),
                  pl.BlockSpec((WINDOW, Q_COLS), lambda bi, j: (j, bi)),
                  pl.BlockSpec((WINDOW, KV_COLS), lambda bi, j: (j, 2 * bi)),
                  pl.BlockSpec((WINDOW, KV_COLS), lambda bi, j: (prev(bi, j), 2 * bi)),
                  pl.BlockSpec((WINDOW, KV_COLS), lambda bi, j: (j, 2 * bi + 1)),
                  pl.BlockSpec((WINDOW, KV_COLS), lambda bi, j: (prev(bi, j), 2 * bi + 1))],
        out_specs=pl.BlockSpec((WINDOW, Q_COLS), lambda bi, j: (j, bi)),
        out_shape=jax.ShapeDtypeStruct((t, b * Q_COLS), BF16),
        compiler_params=_cparams("parallel", "parallel"),
        name="attn_prompt",
    )(sinks, q2, kv2, kv2, kv2, kv2)


def _attn_sample_kernel(sinks_ref, q_ref, kn_ref, vn_ref, ck_ref, cv_ref, o_ref):
    bs, t = q_ref.shape[0], q_ref.shape[1]
    assert t & (t - 1) == 0
    nq, nk = GQA_GROUP * t, 2 * WINDOW
    r = lax.broadcasted_iota(I32, (nq, nk), 0)
    kj = lax.broadcasted_iota(I32, (nq, nk), 1)
    tq = jnp.bitwise_and(r, t - 1)
    valid = jnp.logical_and(kj > tq, kj <= tq + WINDOW)
    dist = (WINDOW + tq - kj).astype(F32)
    grp = jnp.right_shift(lax.broadcasted_iota(I32, (nq, 1), 0), t.bit_length() - 1)
    pad = jnp.zeros((bs, nk - WINDOW - t, HEAD_DIM), F32)
    for n in range(N_KV_A):
        cs = slice(n * HEAD_DIM, (n + 1) * HEAD_DIM)
        kb = jnp.concatenate([ck_ref[:, :, cs], kn_ref[:, :, cs], pad], axis=1).astype(BF16)
        vb = jnp.concatenate([cv_ref[:, :, cs], vn_ref[:, :, cs], pad], axis=1).astype(BF16)
        qn = jnp.concatenate([q_ref[:, :, (n * GQA_GROUP + g) * HEAD_DIM:(n * GQA_GROUP + g + 1) * HEAD_DIM]
                              for g in range(GQA_GROUP)], axis=1).astype(BF16)
        slope = jnp.zeros((nq, 1), F32)
        sink = jnp.zeros((nq, 1), F32)
        for g in range(GQA_GROUP):
            h = n * GQA_GROUP + g
            slope = jnp.where(grp == g, 2.0 ** -(h + 1), slope)
            sink = jnp.where(grp == g, sinks_ref[h], sink)
        s = jnp.einsum('bqd,bkd->bqk', qn, kb, preferred_element_type=F32)
        s = s * (HEAD_DIM ** -0.5) - (slope * dist)[None]
        s = jnp.where(valid[None], s, NEG_BIG)
        m = jnp.maximum(jnp.max(s, axis=2, keepdims=True), sink[None])
        p = jnp.exp(s - m)
        l = jnp.sum(p, axis=2, keepdims=True) + jnp.exp(sink[None] - m)
        o = jnp.einsum('bqk,bkd->bqd', p.astype(BF16), vb, preferred_element_type=F32) / l
        for g in range(GQA_GROUP):
            h = n * GQA_GROUP + g
            o_ref[:, :, h * HEAD_DIM:(h + 1) * HEAD_DIM] = o[:, g * t:(g + 1) * t, :]


def _attn_sample(q, kn, vn, ck, cv, sinks, bs=16):
    db, t = q.shape[0], q.shape[1]
    seq3 = lambda w: pl.BlockSpec((bs, t, w), lambda i: (i, 0, 0))
    cache = pl.BlockSpec((bs, WINDOW, KV_COLS), lambda i: (i, 0, 0))
    return pl.pallas_call(
        _attn_sample_kernel,
        grid=(db // bs,),
        in_specs=[pl.BlockSpec(memory_space=pltpu.SMEM), seq3(Q_COLS), seq3(KV_COLS), seq3(KV_COLS), cache, cache],
        out_specs=seq3(Q_COLS),
        out_shape=jax.ShapeDtypeStruct((db, t, Q_COLS), F32),
        compiler_params=_cparams("parallel"),
        name="attn_sample",
    )(sinks, q, kn, vn, ck, cv)


def _rwkv_prep_kernel(pb_ref, halo_ref, init_ref, mu_ref, w0_ref, wup_ref, a0_ref, aup_ref, gup_ref,
                      kk_ref, ka_ref, rk_ref, ones_ref,
                      r_o, w_o, k_o, v_o, kk_o, nkka_o, bonus_o, g_o, *, b):
    i = pl.program_id(0)
    pb = pb_ref[...]
    tm = pb.shape[0]
    halo = jnp.where(i == 0, init_ref[...], halo_ref[...])
    prev = halo if tm == b else jnp.concatenate([halo, pb[:tm - b]], axis=0)
    xs = pb + (prev - pb) * mu_ref[...]
    r = xs[:, :D_B]
    k = xs[:, D_B:2 * D_B]
    v = xs[:, 2 * D_B:3 * D_B]
    o1 = 3 * D_B
    wd = xs[:, o1:o1 + D_LORA_W]
    ad = xs[:, o1 + D_LORA_W:o1 + D_LORA_W + D_LORA_A]
    gd = xs[:, o1 + D_LORA_W + D_LORA_A:]
    z = -(w0_ref[...] + _dot(jnp.tanh(wd).astype(BF16), wup_ref[...]))
    softplus = jnp.maximum(z, 0.0) + jnp.log(1.0 + jnp.exp(-jnp.abs(z)))
    decay = jnp.exp(-jnp.exp(-softplus - 0.5))
    a = _sigmoid(a0_ref[...] + _dot(ad.astype(BF16), aup_ref[...]))
    g_o[...] = _dot(_sigmoid(gd).astype(BF16), gup_ref[...])
    ones = ones_ref[...]
    kk = k * kk_ref[...]
    kk = kk * lax.rsqrt(jnp.maximum(_dot2(kk * kk, ones), 1e-24))
    k2 = k * (1.0 + (a - 1.0) * ka_ref[...])
    bonus_o[...] = _dot2(r * k2 * rk_ref[...], ones) * v

    half = HEAD_DIM // 2
    lane8 = lax.broadcasted_iota(I32, (SEQ_PER_GROUP, 128), 1)
    low8 = lane8 < HEAD_DIM
    first_copy = jnp.bitwise_and(lax.broadcasted_iota(I32, (half, 128), 1), SEQ_PER_GROUP) == 0
    pairs = ((r, decay, r_o, w_o), (k2, kk, k_o, kk_o), (-(kk * a), v, nkka_o, None))
    for u in range(tm // SEQ_PER_GROUP):
        rows = slice(u * SEQ_PER_GROUP, (u + 1) * SEQ_PER_GROUP)
        for x, y, x_o, y_o in pairs:
            xu, yu = x[rows], y[rows]
            pieces = []
            for h in range(N_HEADS_B):
                cs = slice((h // 2) * 128, (h // 2 + 1) * 128)
                if h % 2 == 0:
                    p = jnp.where(low8, xu[:, cs], pltpu.roll(yu[:, cs], HEAD_DIM, 1))
                else:
                    p = jnp.where(low8, pltpu.roll(xu[:, cs], HEAD_DIM, 1), yu[:, cs])
                pieces += [p, p]
            tr = jnp.concatenate(pieces, axis=0).T
            x_o[u] = tr[:HEAD_DIM]
            if y_o is not None:
                y_o[u] = tr[HEAD_DIM:]
            else:
                v_o[u] = jnp.where(first_copy, tr[HEAD_DIM:HEAD_DIM + half], tr[HEAD_DIM + half:])


def _rwkv_prep(pb, init, p, b, tm):
    n = pb.shape[0]
    units = n // SEQ_PER_GROUP
    tu = tm // SEQ_PER_GROUP
    half = HEAD_DIM // 2
    row = lambda w: pl.BlockSpec((1, w), lambda i: (0, 0))
    full = lambda r, c: pl.BlockSpec((r, c), lambda i: (0, 0))
    tile = pl.BlockSpec((tm, D_B), lambda i: (i, 0))
    kspec = pl.BlockSpec((tu, HEAD_DIM, 128), lambda i: (i, 0, 0))
    vspec = pl.BlockSpec((tu, half, 128), lambda i: (i, 0, 0))
    kshape = jax.ShapeDtypeStruct((units, HEAD_DIM, 128), F32)
    halo_blocks = tm // b
    return pl.pallas_call(
        functools.partial(_rwkv_prep_kernel, b=b),
        grid=(n // tm,),
        in_specs=[pl.BlockSpec((tm, D_B_IN), lambda i: (i, 0)),
                  pl.BlockSpec((b, D_B_IN), lambda i: (jnp.maximum(i * halo_blocks - 1, 0), 0)),
                  full(b, D_B_IN), row(D_B_IN), row(D_B), full(D_LORA_W, D_B), row(D_B), full(D_LORA_A, D_B),
                  full(D_LORA_G, D_B), row(D_B), row(D_B), row(D_B), full(D_B, D_B)],
        out_specs=[kspec, kspec, kspec, vspec, kspec, kspec, tile, tile],
        out_shape=[kshape, kshape, kshape, jax.ShapeDtypeStruct((units, half, 128), F32), kshape, kshape,
                   jax.ShapeDtypeStruct((n, D_B), F32), jax.ShapeDtypeStruct((n, D_B), F32)],
        compiler_params=_cparams("arbitrary"),
        name="rwkv_prep",
    )(pb, pb, init, p['mu'], p['w0'], p['w_up'], p['a0'], p['a_up'], p['g_up'], p['k_k'], p['k_a'], p['r_k'],
      p['head_ones'])


def _wkv_scan_kernel(r_ref, w_ref, k_ref, kk_ref, nkka_ref, v_ref, s0_ref, o_ref, st_ref, s_scr):
    j = pl.program_id(1)

    @pl.when(j == 0)
    def _():
        s_scr[...] = s0_ref[0]

    tc = r_ref.shape[0]
    nsub = (HEAD_DIM // 2) // 8

    def bcast(ref, s, kx):
        return jnp.broadcast_to(ref[s, pl.ds(kx, 1), :], (8, 128))

    acc0 = [[jnp.zeros((8, 128), F32) for _ in range(2)] for _ in range(nsub)]
    for kx in range(HEAD_DIM):
        kkr = bcast(kk_ref, 0, kx)
        for i in range(nsub):
            acc0[i][kx % 2] = acc0[i][kx % 2] + s_scr[kx, 8 * i:8 * i + 8, :] * kkr

    def step(s, sa):
        nxt = jnp.minimum(s + 1, tc - 1)
        vv = [v_ref[s, 8 * i:8 * i + 8, :] for i in range(nsub)]
        oacc = [[jnp.zeros((8, 128), F32) for _ in range(2)] for _ in range(nsub)]
        nacc = [[jnp.zeros((8, 128), F32) for _ in range(2)] for _ in range(nsub)]
        for kx in range(HEAD_DIM):
            rr, wr, kr = bcast(r_ref, s, kx), bcast(w_ref, s, kx), bcast(k_ref, s, kx)
            nk, kkn = bcast(nkka_ref, s, kx), bcast(kk_ref, nxt, kx)
            for i in range(nsub):
                rows = slice(8 * i, 8 * i + 8)
                sk = s_scr[kx, rows, :] * wr + sa[i] * nk + vv[i] * kr
                s_scr[kx, rows, :] = sk
                oacc[i][kx % 2] = oacc[i][kx % 2] + sk * rr
                nacc[i][kx % 2] = nacc[i][kx % 2] + sk * kkn
        o_ref[s] = jnp.concatenate([a[0] + a[1] for a in oacc], axis=0)
        return [a[0] + a[1] for a in nacc]

    lax.fori_loop(0, tc, step, [a[0] + a[1] for a in acc0])

    @pl.when(j == pl.num_programs(1) - 1)
    def _():
        st_ref[0] = s_scr[...]


def _wkv_scan(r, w, k, kk, nkka, v, s0, b, t, tc):
    g = b // SEQ_PER_GROUP
    half = HEAD_DIM // 2
    kview = lambda z: z.reshape(t, g, HEAD_DIM, 128)
    kspec = pl.BlockSpec((tc, None, HEAD_DIM, 128), lambda gi, j: (j, gi, 0, 0))
    vspec = pl.BlockSpec((tc, None, half, 128), lambda gi, j: (j, gi, 0, 0))
    sspec = pl.BlockSpec((1, HEAD_DIM, half, 128), lambda gi, j: (gi, 0, 0, 0))
    o, st = pl.pallas_call(
        _wkv_scan_kernel,
        grid=(g, t // tc),
        in_specs=[kspec] * 5 + [vspec, sspec],
        out_specs=[vspec, sspec],
        out_shape=[jax.ShapeDtypeStruct((t, g, half, 128), F32),
                   jax.ShapeDtypeStruct((g, HEAD_DIM, half, 128), F32)],
        scratch_shapes=[pltpu.VMEM((HEAD_DIM, half, 128), F32)],
        compiler_params=_cparams("parallel", "arbitrary"),
        name="wkv_scan",
    )(kview(r), kview(w), kview(k), kview(kk), kview(nkka), v.reshape(t, g, half, 128), s0)
    return o.reshape(t * g, half, 128), st


def _state_to_scan(s, b):
    g = b // SEQ_PER_GROUP
    s = s.reshape(g, SEQ_PER_GROUP, N_HEADS_B, 2, HEAD_DIM // 2, HEAD_DIM).transpose(0, 5, 4, 2, 3, 1)
    return s.reshape(g, HEAD_DIM, HEAD_DIM // 2, 128)


def _state_from_scan(s, b):
    g = b // SEQ_PER_GROUP
    s = s.reshape(g, HEAD_DIM, HEAD_DIM // 2, N_HEADS_B, 2, SEQ_PER_GROUP).transpose(0, 5, 3, 4, 2, 1)
    return s.reshape(b, N_HEADS_B, HEAD_DIM, HEAD_DIM)


def _route_tile(x, nrm_ref, wr_ref, br_ref, cnt_scr, xn_o, idx_o, wts_o):
    tm = x.shape[0]
    xn = _rms(x, nrm_ref[...])
    hi, lo = _split_bf16(xn)
    _store_row_tiles(xn_o, xn)
    wr = wr_ref[...]
    pa = _dot(hi, wr)
    pb = _dot(lo, wr)
    lg = pa[:, 0:32] + pa[:, 32:64] + pb[:, 0:32] + pb[:, 32:64] + br_ref[...]
    col = lambda c: lg[:, c:c + 1]
    c = [col(gx) for gx in range(N_EGROUPS)]
    m = jnp.maximum(jnp.maximum(c[0], c[1]), jnp.maximum(c[2], c[3]))
    den = jnp.exp(c[0] - m) + jnp.exp(c[1] - m) + jnp.exp(c[2] - m) + jnp.exp(c[3] - m)
    pg = 1.0 / den
    gi = jnp.where(c[0] >= m, 0, jnp.where(c[1] >= m, 1, jnp.where(c[2] >= m, 2, 3))).astype(I32)
    sel = []
    for e in range(EXP_PER_GROUP):
        sel.append(jnp.where(gi == 0, col(4 + e), jnp.where(gi == 1, col(8 + e),
                                                            jnp.where(gi == 2, col(12 + e), col(16 + e)))))
    v1 = jnp.maximum(jnp.maximum(sel[0], sel[1]), jnp.maximum(sel[2], sel[3]))
    i1 = jnp.where(sel[0] >= v1, 0, jnp.where(sel[1] >= v1, 1, jnp.where(sel[2] >= v1, 2, 3))).astype(I32)
    rest = [jnp.where(i1 == e, -jnp.inf, sel[e]) for e in range(EXP_PER_GROUP)]
    v2 = jnp.maximum(jnp.maximum(rest[0], rest[1]), jnp.maximum(rest[2], rest[3]))
    i2 = jnp.where(rest[0] >= v2, 0, jnp.where(rest[1] >= v2, 1, jnp.where(rest[2] >= v2, 2, 3))).astype(I32)
    tt = jnp.exp(v2 - v1)
    w1 = pg / (1.0 + tt)
    w2 = pg * tt / (1.0 + tt)
    e1 = gi * EXP_PER_GROUP + i1
    e2 = gi * EXP_PER_GROUP + i2
    lane = lax.broadcasted_iota(I32, (tm, N_EXPERTS), 1)
    oh1 = lane == e1
    oh2 = lane == e2
    oh = jnp.where(jnp.logical_or(oh1, oh2), 1.0, 0.0)
    ri = lax.broadcasted_iota(I32, (tm, tm), 0)
    ci = lax.broadcasted_iota(I32, (tm, tm), 1)
    ltri = jnp.where(ri > ci, 1.0, 0.0).astype(BF16)
    cnt = cnt_scr[0:1, 0:N_EXPERTS]
    pre = _dot(ltri, oh.astype(BF16)) + cnt
    rank1 = jnp.sum(jnp.where(oh1, pre, 0.0), axis=1, keepdims=True)
    rank2 = jnp.sum(jnp.where(oh2, pre, 0.0), axis=1, keepdims=True)
    cnt_scr[0:1, 0:N_EXPERTS] = cnt + jnp.sum(oh, axis=0, keepdims=True)
    lw = lax.broadcasted_iota(I32, (tm, 128), 1)
    wts_o[...] = jnp.where(lw == 0, w1, jnp.where(lw == 1, w2, 0.0))
    cols = jnp.where(lw == 0, e1.astype(F32), jnp.where(lw == 1, e2.astype(F32),
                                                         jnp.where(lw == 2, rank1, jnp.where(lw == 3, rank2, 0.0))))
    idx_o[...] = cols.T[0:8, :].astype(I32)


def _route_out_specs(tm):
    return [pl.BlockSpec((tm * ROW_CHUNKS, 128), lambda i: (i, 0)),
            pl.BlockSpec((8, tm), lambda i: (0, i)),
            pl.BlockSpec((tm, 128), lambda i: (i, 0)),
            pl.BlockSpec((8, 128), lambda i: (0, 0))]


def _route_out_shapes(n):
    return [jax.ShapeDtypeStruct((n * ROW_CHUNKS, 128), F32),
            jax.ShapeDtypeStruct((8, n), I32),
            jax.ShapeDtypeStruct((n, 128), F32),
            jax.ShapeDtypeStruct((8, 128), F32)]


def _route_in_specs():
    return [pl.BlockSpec((1, D_MODEL), lambda i: (0, 0)),
            pl.BlockSpec((D_MODEL, 128), lambda i: (0, 0)),
            pl.BlockSpec((1, 32), lambda i: (0, 0))]


def _mix0_out_kernel(o_ref, bonus_ref, g_ref, attn_ref, x_ref, lnw_ref, lnb_ref, avg_ref, wo_ref,
                     nrm_ref, wr_ref, br_ref,
                     x1_o, xn_o, idx_o, wts_o, cnt_o, cnt_scr, xs_scr, at_scr, *, b):
    @pl.when(pl.program_id(0) == 0)
    def _():
        cnt_scr[...] = jnp.zeros_like(cnt_scr)

    avg = avg_ref[...]
    half = HEAD_DIM // 2
    lane8 = lax.broadcasted_iota(I32, (SEQ_PER_GROUP, 128), 1)
    unit_rows = []
    for u in range(o_ref.shape[0]):
        ot = jnp.concatenate([o_ref[u], jnp.zeros((128 - half, 128), F32)], axis=0).T
        cols = []
        for jj in range(N_HEADS_B // 2):
            q = [ot[(4 * jj + i) * SEQ_PER_GROUP:(4 * jj + i + 1) * SEQ_PER_GROUP] for i in range(4)]
            c = jnp.where(lane8 < half, q[0], pltpu.roll(q[1], half, 1))
            c = jnp.where(lane8 < 2 * half, c, pltpu.roll(q[2], 2 * half, 1))
            c = jnp.where(lane8 < 3 * half, c, pltpu.roll(q[3], 3 * half, 1))
            cols.append(c)
        unit_rows.append(jnp.concatenate(cols, axis=1))
    o = jnp.concatenate(unit_rows, axis=0)
    d = o - _dot2(o, avg)
    var = _dot2(d * d, avg)
    on = d * lax.rsqrt(var + RWKV_GN_EPS) * lnw_ref[...] + lnb_ref[...]
    rout = ((on + bonus_ref[...]) * g_ref[...]).astype(BF16)
    x = _load_time_major(x_ref, xs_scr, b)
    if len(x_ref.shape) == 3:
        attn = _interleave_rows([attn_ref[:, s * Q_COLS:(s + 1) * Q_COLS] for s in range(b)], at_scr)
    else:
        attn = attn_ref[...]
    y = _dot(attn.astype(BF16), wo_ref[:Q_COLS, :]) + _dot(rout, wo_ref[Q_COLS:, :]) + x
    x1_o[...] = y
    _route_tile(y, nrm_ref, wr_ref, br_ref, cnt_scr, xn_o, idx_o, wts_o)
    cnt_o[...] = cnt_scr[...]


def _mix0_out(o, bonus, g, attn, x, p, rp, b, t, tm):
    n = b * t
    xshape, xspec = _time_major_spec(b, t, D_MODEL, tm)
    aspec = (pl.BlockSpec((tm // b, b * Q_COLS), lambda i: (i, 0)) if len(xshape) == 3
             else pl.BlockSpec((tm, Q_COLS), lambda i: (i, 0)))
    tile = lambda w: pl.BlockSpec((tm, w), lambda i: (i, 0))
    row = lambda w: pl.BlockSpec((1, w), lambda i: (0, 0))
    full = lambda r, c: pl.BlockSpec((r, c), lambda i: (0, 0))
    return pl.pallas_call(
        functools.partial(_mix0_out_kernel, b=b),
        grid=(n // tm,),
        in_specs=[pl.BlockSpec((tm // SEQ_PER_GROUP, HEAD_DIM // 2, 128), lambda i: (i, 0, 0)),
                  tile(D_B), tile(D_B), aspec, xspec, row(D_B), row(D_B),
                  full(D_B, D_B), full(D_MODEL, D_MODEL)] + _route_in_specs(),
        out_specs=[tile(D_MODEL)] + _route_out_specs(tm),
        out_shape=[jax.ShapeDtypeStruct((n, D_MODEL), F32)] + _route_out_shapes(n),
        scratch_shapes=[pltpu.VMEM((8, 128), F32), pltpu.VMEM((D_MODEL // 128, tm, 128), F32),
                        pltpu.VMEM((Q_COLS // 128, tm, 128), F32)],
        compiler_params=_cparams("arbitrary"),
        name="mix0_out",
    )(o, bonus, g, attn, x.reshape(xshape), p['ln_w'], p['ln_b'], p['head_avg'], p['w_out'], rp['norm'], rp['w'], rp['b'])


def _row_dma_start(idx_ref, pos, r, src_hbm, dst, sem, priority):
    src_row = pl.multiple_of(idx_ref[pos] * ROW_CHUNKS, ROW_CHUNKS)
    dst_row = pl.multiple_of(r * ROW_CHUNKS, ROW_CHUNKS)
    pltpu.make_async_copy(src_hbm.at[pl.ds(src_row, ROW_CHUNKS)], dst.at[pl.ds(dst_row, ROW_CHUNKS)],
                          sem).start(priority=priority)


def _row_gather_start(idx_ref, base, n_rows, src_hbm, dst, sem):
    def body(r2, carry):
        for p in range(2):
            _row_dma_start(idx_ref, base + 2 * r2 + p, 2 * r2 + p, src_hbm, dst, sem, p)
        return carry

    lax.fori_loop(0, n_rows // 2, body, 0, unroll=4)


def _row_gather_wait(dst, sem):
    pltpu.make_async_copy(dst, dst, sem).wait()


def _expert_kernel(te_ref, nu_ref, pad_ref, p1_ref, p2_ref, x_hbm, wg_ref, wu_ref, wd_ref, o_ref,
                   src_ref, xbuf, sem, wgb, wub, wdb):
    i = pl.program_id(0)
    nu = nu_ref[0]
    mt = xbuf.shape[1] // ROW_CHUNKS

    @pl.when(i == 0)
    def _():
        def fill_pad(e, carry):
            start = pad_ref[2 * e]

            def body(r, c):
                src_ref[r] = jnp.minimum(r - start, p1_ref.shape[0] - 1)
                return c

            lax.fori_loop(start, pad_ref[2 * e + 1], body, 0)
            return carry

        lax.fori_loop(0, N_EXPERTS + 1, fill_pad, 0)

        def invert(n, carry):
            src_ref[p1_ref[n]] = n
            src_ref[p2_ref[n]] = n
            return carry

        lax.fori_loop(0, p1_ref.shape[0], invert, 0, unroll=8)
        _row_gather_start(src_ref, 0, mt, x_hbm, xbuf.at[0], sem.at[0])

    new_expert = jnp.logical_or(i == 0, te_ref[i] != te_ref[jnp.maximum(i - 1, 0)])

    @pl.when(jnp.logical_and(i < nu, new_expert))
    def _():
        wgb[...] = wg_ref[0].astype(BF16)
        wub[...] = wu_ref[0].astype(BF16)
        wdb[...] = wd_ref[0].astype(BF16)

    @pl.when(i < nu)
    def _():
        cur = i % 2
        nxt = 1 - cur
        _row_gather_start(src_ref, (i + 1) * mt, mt, x_hbm, xbuf.at[nxt], sem.at[nxt])
        _row_gather_wait(xbuf.at[cur], sem.at[cur])
        x = _load_row_tiles(xbuf.at[cur]).astype(BF16)
        hg = _dot(x, wgb[...])
        hu = _dot(x, wub[...])
        h = ((hg * _sigmoid(hg)) * hu).astype(BF16)
        _store_row_tiles(o_ref, _dot(h, wdb[...]))

    @pl.when(i == nu)
    def _():
        _row_gather_wait(xbuf.at[i % 2], sem.at[i % 2])

    @pl.when(i >= nu)
    def _():
        o_ref[...] = jnp.zeros_like(o_ref)


def _experts(xn, r, mt, pos1, pos2, tile_expert, n_used, pad_ranges, wg, wu, wd):
    wspec = lambda a, b: pl.BlockSpec((1, a, b), lambda i, te, nu, pad, p1, p2: (te[i], 0, 0))
    grid_spec = pltpu.PrefetchScalarGridSpec(
        num_scalar_prefetch=5,
        grid=(r // mt,),
        in_specs=[pl.BlockSpec(memory_space=pl.ANY),
                  wspec(D_MODEL, D_FF_E), wspec(D_MODEL, D_FF_E), wspec(D_FF_E, D_MODEL)],
        out_specs=pl.BlockSpec((mt * ROW_CHUNKS, 128), lambda i, te, nu, pad, p1, p2: (i, 0)),
        scratch_shapes=[pltpu.SMEM((r,), I32), pltpu.VMEM((2, mt * ROW_CHUNKS, 128), F32),
                        pltpu.SemaphoreType.DMA((2,)),
                        pltpu.VMEM((D_MODEL, D_FF_E), BF16), pltpu.VMEM((D_MODEL, D_FF_E), BF16),
                        pltpu.VMEM((D_FF_E, D_MODEL), BF16)],
    )
    return pl.pallas_call(
        _expert_kernel,
        grid_spec=grid_spec,
        out_shape=jax.ShapeDtypeStruct((r * ROW_CHUNKS, 128), F32),
        compiler_params=_cparams("arbitrary"),
        name="moe_experts",
    )(tile_expert, n_used, pad_ranges, pos1, pos2, xn, wg, wu, wd)


def _moe(xn, idx, cnt, ep):
    n = xn.shape[0] // ROW_CHUNKS
    mt = 512 if n >= 8192 else 128
    rows = 2 * n + N_EXPERTS * mt
    counts = cnt[0, :N_EXPERTS].astype(I32)
    padded = ((counts + mt - 1) // mt) * mt
    ends = jnp.cumsum(padded)
    offs = ends - padded
    pos1 = offs[idx[0]] + idx[2]
    pos2 = offs[idx[1]] + idx[3]
    pad_ranges = jnp.stack([jnp.append(offs + counts, ends[-1]), jnp.append(ends, ends[-1] + mt)], axis=1)
    pad_ranges = pad_ranges.reshape(-1).astype(I32)
    n_used = (ends[-1] // mt).astype(I32)
    starts = jnp.arange(rows // mt, dtype=I32) * mt
    starts = jnp.minimum(starts, ends[-1] - 1)
    tile_expert = jnp.sum((starts[:, None] >= ends[None, :]).astype(I32), axis=1)
    tile_expert = jnp.minimum(tile_expert, N_EXPERTS - 1).astype(I32)
    out = _experts(xn, rows, mt, pos1, pos2, tile_expert, n_used.reshape(1), pad_ranges, ep['wg'], ep['wu'], ep['wd'])
    return out, pos1, pos2


def _combine_kernel(p1_ref, p2_ref, x_ref, wts_ref, nrm_ref, out_hbm, y_o, gbuf, sem, ys_scr, *, final, b):
    i = pl.program_id(0)
    tm = x_ref.shape[0]

    def start(tile, slot):
        _row_gather_start(p1_ref, tile * tm, tm, out_hbm, gbuf.at[slot, 0], sem.at[slot])
        _row_gather_start(p2_ref, tile * tm, tm, out_hbm, gbuf.at[slot, 1], sem.at[slot])

    @pl.when(i == 0)
    def _():
        start(0, 0)

    @pl.when(i + 1 < pl.num_programs(0))
    def _():
        start(i + 1, (i + 1) % 2)

    cur = i % 2
    _row_gather_wait(gbuf.at[cur], sem.at[cur])
    wts = wts_ref[...]
    y = x_ref[...] + wts[:, 0:1] * _load_row_tiles(gbuf.at[cur, 0]) + wts[:, 1:2] * _load_row_tiles(gbuf.at[cur, 1])
    if final:
        _store_batch_major(y_o, ys_scr, _rms(y, nrm_ref[...]), b)
    else:
        y_o[...] = y


def _moe_combine(x, wts, out, pos1, pos2, nrm, final, b, t, tm=256):
    n = x.shape[0]
    tile = lambda w: pl.BlockSpec((tm, w), lambda i, p1, p2: (i, 0))
    if final:
        yshape, yspec = _time_major_spec(b, t, D_MODEL, tm)
    else:
        yshape, yspec = (n, D_MODEL), tile(D_MODEL)
    grid_spec = pltpu.PrefetchScalarGridSpec(
        num_scalar_prefetch=2,
        grid=(n // tm,),
        in_specs=[tile(D_MODEL), tile(128), pl.BlockSpec((1, D_MODEL), lambda i, p1, p2: (0, 0)),
                  pl.BlockSpec(memory_space=pl.ANY)],
        out_specs=yspec,
        scratch_shapes=[pltpu.VMEM((2, 2, tm * ROW_CHUNKS, 128), F32), pltpu.SemaphoreType.DMA((2,)),
                        pltpu.VMEM((D_MODEL // 128, tm, 128), F32)],
    )
    return pl.pallas_call(
        functools.partial(_combine_kernel, final=final, b=b),
        grid_spec=grid_spec,
        out_shape=jax.ShapeDtypeStruct(yshape, F32),
        compiler_params=_cparams("arbitrary"),
        name="moe_combine_final" if final else "moe_combine",
    )(pos1, pos2, x, wts, nrm, out)


def _gelu_tanh(x):
    return 0.5 * x * (1.0 + jnp.tanh(0.7978845608028654 * (x + 0.044715 * (x * x * x))))


def _mix1_kernel(x_ref, nmix_ref, bre_ref, bim_ref, are_ref, aim_ref, cre_ref, cim_ref,
                 dsk_ref, wo_ref, h0r_ref, h0i_ref, nrm_ref, wr_ref, br_ref,
                 x2_o, xn_o, idx_o, wts_o, cnt_o, hr_o, hi_o,
                 bur, bui, hr_scr, hi_scr, cnt_scr, *, b, cw):
    @pl.when(pl.program_id(0) == 0)
    def _():
        cnt_scr[...] = jnp.zeros_like(cnt_scr)
        hr_scr[...] = h0r_ref[...]
        hi_scr[...] = h0i_ref[...]

    x = x_ref[...]
    u = _rms(x, nmix_ref[...])
    ub = u.astype(BF16)
    nblk = bre_ref.shape[0]
    kin = D_MODEL // nblk
    kst = S5_STATE // nblk
    for cb in range(nblk):
        ucb = ub[:, cb * kin:(cb + 1) * kin]
        bur[:, cb * kst:(cb + 1) * kst] = _dot(ucb, bre_ref[cb])
        bui[:, cb * kst:(cb + 1) * kst] = _dot(ucb, bim_ref[cb])

    tc = x.shape[0] // b
    for c0 in range(0, S5_STATE, cw):
        cs = slice(c0, c0 + cw)
        ar = jnp.broadcast_to(are_ref[:, cs], (b, cw))
        ai = jnp.broadcast_to(aim_ref[:, cs], (b, cw))

        def step(s, carry, cs=cs, ar=ar, ai=ai):
            hr, hi = carry
            rows = pl.ds(pl.multiple_of(s * b, b), b)
            nr = ar * hr - ai * hi + bur[rows, cs]
            ni = ar * hi + ai * hr + bui[rows, cs]
            bur[rows, cs] = nr
            bui[rows, cs] = ni
            return nr, ni

        hr, hi = lax.fori_loop(0, tc, step, (hr_scr[:, cs], hi_scr[:, cs]), unroll=True)
        hr_scr[:, cs] = hr
        hi_scr[:, cs] = hi

    ych = []
    for cb in range(nblk):
        ss = slice(cb * kst, (cb + 1) * kst)
        ych.append(_dot(bur[:, ss].astype(BF16), cre_ref[cb]) - _dot(bui[:, ss].astype(BF16), cim_ref[cb]))
    y = jnp.concatenate(ych, axis=1) + dsk_ref[...] * u
    z = _dot(_gelu_tanh(y).astype(BF16), wo_ref[...])
    x2 = x + z[:, :D_MODEL] * _sigmoid(z[:, D_MODEL:])
    x2_o[...] = x2
    _route_tile(x2, nrm_ref, wr_ref, br_ref, cnt_scr, xn_o, idx_o, wts_o)
    cnt_o[...] = cnt_scr[...]
    hr_o[...] = hr_scr[...]
    hi_o[...] = hi_scr[...]


def _mix1(x, sp, rp, h0r, h0i, b, tr):
    n = x.shape[0]
    cw = 1024 if b == 8 else 128
    tile = lambda w: pl.BlockSpec((tr, w), lambda i: (i, 0))
    row = lambda w: pl.BlockSpec((1, w), lambda i: (0, 0))
    full = lambda *s: pl.BlockSpec(s, lambda i: (0,) * len(s))
    nblk = sp['b_re'].shape[0]
    return pl.pallas_call(
        functools.partial(_mix1_kernel, b=b, cw=cw),
        grid=(n // tr,),
        in_specs=[tile(D_MODEL), row(D_MODEL),
                  full(nblk, D_MODEL // nblk, S5_STATE // nblk), full(nblk, D_MODEL // nblk, S5_STATE // nblk),
                  row(S5_STATE), row(S5_STATE),
                  full(nblk, S5_STATE // nblk, D_MODEL // nblk), full(nblk, S5_STATE // nblk, D_MODEL // nblk),
                  row(D_MODEL), full(D_MODEL, 2 * D_MODEL), full(b, S5_STATE), full(b, S5_STATE)] + _route_in_specs(),
        out_specs=[tile(D_MODEL)] + _route_out_specs(tr) + [full(b, S5_STATE), full(b, S5_STATE)],
        out_shape=[jax.ShapeDtypeStruct((n, D_MODEL), F32)] + _route_out_shapes(n)
                  + [jax.ShapeDtypeStruct((b, S5_STATE), F32)] * 2,
        scratch_shapes=[pltpu.VMEM((tr, S5_STATE), F32), pltpu.VMEM((tr, S5_STATE), F32),
                        pltpu.VMEM((b, S5_STATE), F32), pltpu.VMEM((b, S5_STATE), F32),
                        pltpu.VMEM((8, 128), F32)],
        compiler_params=_cparams("arbitrary"),
        name="mix1",
    )(x, sp['norm'], sp['b_re'], sp['b_im'], sp['a_re'], sp['a_im'], sp['c_re'], sp['c_im'],
      sp['d'], sp['w_out'], h0r, h0i, rp['norm'], rp['w'], rp['b'])


def _router_params(norm, w_rc, b_rc, w_rf, b_rf):
    w = jnp.concatenate([w_rc, w_rf.reshape(D_MODEL, N_EXPERTS), jnp.zeros((D_MODEL, 12), F32)], axis=1)
    hi = w.astype(BF16)
    lo = (w - hi.astype(F32)).astype(BF16)
    wcat = jnp.concatenate([hi, lo, jnp.zeros((D_MODEL, 64), BF16)], axis=1)
    bias = jnp.concatenate([b_rc, b_rf.reshape(-1), jnp.zeros((12,), F32)]).reshape(1, 32)
    return {'norm': norm.reshape(1, D_MODEL), 'w': wcat, 'b': bias}


def _expert_params(wg, wu, wd):
    return {'wg': wg, 'wu': wu, 'wd': wd}


def _s5_params(norm, a_re, a_im, log_dt, b_re, b_im, c_re, c_im, d_skip, w_out, nblk=8):
    dt = jnp.exp(log_dt)
    mag = jnp.exp(dt * a_re)
    ab_re, ab_im = mag * jnp.cos(dt * a_im), mag * jnp.sin(dt * a_im)
    den = a_re * a_re + a_im * a_im
    f_re = ((ab_re - 1.0) * a_re + ab_im * a_im) / den
    f_im = (ab_im * a_re - (ab_re - 1.0) * a_im) / den
    bb_re = f_re[..., None] * b_re - f_im[..., None] * b_im
    bb_im = f_re[..., None] * b_im + f_im[..., None] * b_re
    gpb = S5_GROUPS // nblk
    eye = jnp.eye(gpb, dtype=F32)

    def in_blocks(bb):
        bb = bb.reshape(nblk, gpb, S5_P, S5_CH)
        w = jnp.einsum('ngpc,gh->ngchp', bb, eye)
        return w.reshape(nblk, gpb * S5_CH, gpb * S5_P).astype(BF16)

    def out_blocks(cc):
        cc = cc.reshape(nblk, gpb, S5_CH, S5_P)
        w = jnp.einsum('ngcp,gh->ngphc', cc, eye)
        return w.reshape(nblk, gpb * S5_P, gpb * S5_CH).astype(BF16)

    return {'norm': norm.reshape(1, D_MODEL), 'b_re': in_blocks(bb_re), 'b_im': in_blocks(bb_im),
            'a_re': ab_re.reshape(1, S5_STATE), 'a_im': ab_im.reshape(1, S5_STATE),
            'c_re': out_blocks(c_re), 'c_im': out_blocks(c_im), 'd': d_skip.reshape(1, D_MODEL),
            'w_out': w_out.astype(BF16)}


def _head_block(value):
    hid = jnp.arange(D_B, dtype=I32) // HEAD_DIM
    return jnp.where(hid[:, None] == hid[None, :], value, 0.0).astype(BF16)


def _run_group(x, cache_k, cache_v, shift0, wkv0, h0r, h0i, pr):
    b, t = x.shape[0], x.shape[1]
    n = b * t
    prompt = cache_k is None
    tm = 256 if prompt else 128
    if not prompt:
        x = x.transpose(1, 0, 2)
    q, kv, pb = _in_proj(x, pr['l0_norm'], pr['l0_w_in'], b, t, tm)

    if prompt:
        attn = _attn_prompt(q, kv, pr['sinks'], b, t)
        kv3 = kv[t - WINDOW:].reshape(WINDOW, b, 2, N_KV_A, HEAD_DIM)
        new_k = kv3[:, :, 0].transpose(1, 0, 2, 3)
        new_v = kv3[:, :, 1].transpose(1, 0, 2, 3)
        init = jnp.zeros((b, D_B_IN), F32)
    else:
        qs = q.reshape(t, b, Q_COLS).transpose(1, 0, 2)
        kvs = kv.reshape(t, b, 2 * KV_COLS).transpose(1, 0, 2)
        kn, vn = kvs[..., :KV_COLS], kvs[..., KV_COLS:]
        ck = cache_k.reshape(b, WINDOW, KV_COLS)
        cv = cache_v.reshape(b, WINDOW, KV_COLS)
        attn = _attn_sample(qs, kn, vn, ck, cv, pr['sinks'])
        attn = attn.transpose(1, 0, 2).reshape(n, Q_COLS)
        new_k = jnp.concatenate([ck[:, t:], kn], axis=1).reshape(b, WINDOW, N_KV_A, HEAD_DIM)
        new_v = jnp.concatenate([cv[:, t:], vn], axis=1).reshape(b, WINDOW, N_KV_A, HEAD_DIM)
        init = shift0
    new_shift = pb[n - b:]

    r, w, k, v, kk, nkka, bonus, g = _rwkv_prep(pb, init, pr['rw'], b, tm)
    tc = 64 if prompt else t
    s0 = jnp.zeros((b // SEQ_PER_GROUP, HEAD_DIM, HEAD_DIM // 2, 128), F32) if prompt else _state_to_scan(wkv0, b)
    o, s_fin = _wkv_scan(r, w, k, kk, nkka, v, s0, b, t, tc)
    new_wkv = _state_from_scan(s_fin, b)

    x1, xn, idx, wts, cnt = _mix0_out(o, bonus, g, attn, x, pr['rw'], pr['l0_route'], b, t, 256)
    out, pos1, pos2 = _moe(xn, idx, cnt, pr['l0_exp'])
    x1 = _moe_combine(x1, wts, out, pos1, pos2, pr['final_norm'], False, b, t)

    x2, xn, idx, wts, cnt, hr, hi = _mix1(x1, pr['s5'], pr['l1_route'], h0r, h0i, b, 256)
    out, pos1, pos2 = _moe(xn, idx, cnt, pr['l1_exp'])
    y = _moe_combine(x2, wts, out, pos1, pos2, pr['final_norm'], True, b, t)
    y = y if prompt else y.reshape(t, b, D_MODEL).transpose(1, 0, 2)
    return (y, new_k, new_v, new_shift, new_wkv,
            hr.reshape(b, S5_GROUPS, S5_P), hi.reshape(b, S5_GROUPS, S5_P))


def kernel(x_prompt, x_sample, cache_win_k, cache_win_v, state_shift, state_wkv, state_s5_re, state_s5_im,
           l0_norm_mix, l0_w_in, l0_sinks, l0_mu, l0_w0, l0_w_lora_up, l0_a0, l0_a_lora_up, l0_g_lora_up,
           l0_k_k, l0_k_a, l0_r_k, l0_ln_w, l0_ln_b, l0_w_out,
           l0_norm_ffn, l0_router_coarse, l0_bias_coarse, l0_router_fine, l0_bias_fine,
           l0_exp_gate, l0_exp_up, l0_exp_down,
           l1_norm_mix, l1_s5_a_re, l1_s5_a_im, l1_s5_log_dt, l1_s5_b_re, l1_s5_b_im, l1_s5_c_re, l1_s5_c_im,
           l1_s5_d, l1_w_out,
           l1_norm_ffn, l1_router_coarse, l1_bias_coarse, l1_router_fine, l1_bias_fine,
           l1_exp_gate, l1_exp_up, l1_exp_down,
           final_norm):
    row = lambda z: z.reshape(1, -1)
    pr = {
        'l0_norm': row(l0_norm_mix), 'l0_w_in': l0_w_in.astype(BF16), 'sinks': l0_sinks,
        'rw': {'mu': row(l0_mu), 'w0': row(l0_w0), 'w_up': l0_w_lora_up.astype(BF16), 'a0': row(l0_a0),
               'a_up': l0_a_lora_up.astype(BF16), 'g_up': l0_g_lora_up.astype(BF16), 'k_k': row(l0_k_k),
               'k_a': row(l0_k_a), 'r_k': row(l0_r_k), 'ln_w': row(l0_ln_w), 'ln_b': row(l0_ln_b),
               'head_ones': _head_block(1.0), 'head_avg': _head_block(1.0 / HEAD_DIM),
               'w_out': l0_w_out.astype(BF16)},
        'l0_route': _router_params(l0_norm_ffn, l0_router_coarse, l0_bias_coarse, l0_router_fine, l0_bias_fine),
        'l0_exp': _expert_params(l0_exp_gate, l0_exp_up, l0_exp_down),
        's5': _s5_params(l1_norm_mix, l1_s5_a_re, l1_s5_a_im, l1_s5_log_dt, l1_s5_b_re, l1_s5_b_im,
                         l1_s5_c_re, l1_s5_c_im, l1_s5_d, l1_w_out),
        'l1_route': _router_params(l1_norm_ffn, l1_router_coarse, l1_bias_coarse, l1_router_fine, l1_bias_fine),
        'l1_exp': _expert_params(l1_exp_gate, l1_exp_up, l1_exp_down),
        'final_norm': row(final_norm),
    }
    bp, bs = x_prompt.shape[0], x_sample.shape[0]
    zero_state = jnp.zeros((bp, S5_STATE), F32)
    yp, pk, pv, psh, pwkv, pre, pim = _run_group(x_prompt, None, None, None, None, zero_state, zero_state, pr)
    ys, sk, sv, ssh, swkv, sre, sim = _run_group(
        x_sample, cache_win_k, cache_win_v, state_shift, state_wkv,
        state_s5_re.reshape(bs, S5_STATE), state_s5_im.reshape(bs, S5_STATE), pr)
    return (yp, ys, pk, pv, psh, pwkv, pre, pim, sk, sv, ssh, swkv, sre, sim)
```

```python
import functools

import jax
import jax.numpy as jnp
from jax import lax
from jax.experimental import pallas as pl
from jax.experimental.pallas import tpu as pltpu

F32 = jnp.float32
BF16 = jnp.bfloat16
I32 = jnp.int32

D_MODEL = 1024
HEAD_DIM = 64
N_HEADS_A = 8
N_KV_A = 2
GQA_GROUP = 4
WINDOW = 128
Q_COLS = 512
KV_COLS = 128
D_A_IN = 768
N_HEADS_B = 8
D_B = 512
D_LORA_W = 64
D_LORA_A = 64
D_LORA_G = 128
D_B_IN = 1792
D_IN0 = 2560
RWKV_GN_EPS = 64e-5
S5_CH = 16
S5_GROUPS = 64
S5_P = 64
S5_STATE = S5_GROUPS * S5_P
N_EGROUPS = 4
EXP_PER_GROUP = 4
N_EXPERTS = 16
D_FF_E = 512
RMS_EPS = 1e-5
NEG_BIG = -1e30
PAIRS = 64
SEQ_PER_GROUP = PAIRS // N_HEADS_B
ROW_CHUNKS = D_MODEL // 128
VMEM_LIMIT = 56 * 1024 * 1024


def _cparams(*sem):
    return pltpu.CompilerParams(dimension_semantics=sem, vmem_limit_bytes=VMEM_LIMIT)


def _dot(a, b):
    return jnp.dot(a, b, preferred_element_type=F32)


def _split_bf16(x):
    hi = x.astype(BF16)
    lo = (x - hi.astype(F32)).astype(BF16)
    return hi, lo


def _dot2(x, w):
    hi, lo = _split_bf16(x)
    return _dot(hi, w) + _dot(lo, w)


def _rms(x, g):
    return x * lax.rsqrt(jnp.mean(x * x, axis=-1, keepdims=True) + RMS_EPS) * g


def _sigmoid(x):
    return 1.0 / (1.0 + jnp.exp(-x))


def _store_row_tiles(ref, x):
    rows = x.shape[0]
    for c in range(ROW_CHUNKS):
        ref[pl.ds(c, rows, stride=ROW_CHUNKS), :] = x[:, c * 128:(c + 1) * 128]


def _time_major_spec(b, t, d, tm):
    if b == SEQ_PER_GROUP:
        return (b, t, d), pl.BlockSpec((b, tm // b, d), lambda i, *_: (0, i, 0))
    return (b * t, d), pl.BlockSpec((tm, d), lambda i, *_: (i, 0))


def _interleave_rows(pieces, scr):
    nb, steps = len(pieces), pieces[0].shape[0]
    chunks = pieces[0].shape[1] // 128
    for s, p in enumerate(pieces):
        for c in range(chunks):
            scr[c, pl.ds(s, steps, stride=nb), :] = p[:, c * 128:(c + 1) * 128].astype(scr.dtype)
    return jnp.concatenate([scr[c] for c in range(chunks)], axis=1)


def _deinterleave_rows(x, nb, scr):
    steps = x.shape[0] // nb
    chunks = x.shape[1] // 128
    for c in range(chunks):
        scr[c] = x[:, c * 128:(c + 1) * 128].astype(scr.dtype)
    return [jnp.concatenate([scr[c, pl.ds(s, steps, stride=nb), :] for c in range(chunks)], axis=1)
            for s in range(nb)]


def _load_time_major(x_ref, scr, b):
    if len(x_ref.shape) == 3:
        return _interleave_rows([x_ref[s] for s in range(b)], scr)
    return x_ref[...]


def _store_batch_major(y_ref, scr, y, b):
    if len(y_ref.shape) == 3:
        for s, p in enumerate(_deinterleave_rows(y, b, scr)):
            y_ref[s] = p
    else:
        y_ref[...] = y


def _load_row_tiles(ref):
    rows = ref.shape[0] // ROW_CHUNKS
    return jnp.concatenate([ref[pl.ds(c, rows, stride=ROW_CHUNKS), :] for c in range(ROW_CHUNKS)], axis=1)


def _in_proj_kernel(x_ref, g_ref, w_ref, q_ref, kv_ref, pb_ref, xs_scr, q_scr, kv_scr, *, b):
    x = _load_time_major(x_ref, xs_scr, b)
    xn = _rms(x, g_ref[...]).astype(BF16)
    q = _dot(xn, w_ref[:, :Q_COLS])
    kv = _dot(xn, w_ref[:, Q_COLS:D_A_IN])
    pb_ref[...] = _dot(xn, w_ref[:, D_A_IN:])
    if len(x_ref.shape) == 3:
        q_ref[...] = jnp.concatenate(_deinterleave_rows(q, b, q_scr), axis=1).astype(BF16)
        kv_ref[...] = jnp.concatenate(_deinterleave_rows(kv, b, kv_scr), axis=1)
    else:
        q_ref[...] = q
        kv_ref[...] = kv


def _in_proj(x, g, w_bf16, b, t, tm):
    n = b * t
    xshape, xspec = _time_major_spec(b, t, D_MODEL, tm)
    slab = len(xshape) == 3
    steps = tm // b
    if slab:
        qkv_specs = [pl.BlockSpec((steps, b * Q_COLS), lambda i: (i, 0)),
                     pl.BlockSpec((steps, b * 2 * KV_COLS), lambda i: (i, 0))]
        qkv_shapes = [jax.ShapeDtypeStruct((t, b * Q_COLS), BF16), jax.ShapeDtypeStruct((t, b * 2 * KV_COLS), F32)]
    else:
        qkv_specs = [pl.BlockSpec((tm, Q_COLS), lambda i: (i, 0)), pl.BlockSpec((tm, 2 * KV_COLS), lambda i: (i, 0))]
        qkv_shapes = [jax.ShapeDtypeStruct((n, Q_COLS), F32), jax.ShapeDtypeStruct((n, 2 * KV_COLS), F32)]
    return pl.pallas_call(
        functools.partial(_in_proj_kernel, b=b),
        grid=(n // tm,),
        in_specs=[xspec,
                  pl.BlockSpec((1, D_MODEL), lambda i: (0, 0)),
                  pl.BlockSpec((D_MODEL, D_IN0), lambda i: (0, 0))],
        out_specs=qkv_specs + [pl.BlockSpec((tm, D_B_IN), lambda i: (i, 0))],
        out_shape=qkv_shapes + [jax.ShapeDtypeStruct((n, D_B_IN), F32)],
        scratch_shapes=[pltpu.VMEM((D_MODEL // 128, tm, 128), F32), pltpu.VMEM((Q_COLS // 128, tm, 128), F32),
                        pltpu.VMEM((2 * KV_COLS // 128, tm, 128), F32)],
        compiler_params=_cparams("parallel"),
        name="in_proj",
    )(x.reshape(xshape), g, w_bf16)


def _attn_prompt_kernel(sinks_ref, q_ref, kc_ref, kp_ref, vc_ref, vp_ref, o_ref):
    j = pl.program_id(1)
    qi = lax.broadcasted_iota(I32, (WINDOW, 2 * WINDOW), 0)
    kj = lax.broadcasted_iota(I32, (WINDOW, 2 * WINDOW), 1)
    valid = jnp.logical_and(kj > qi, kj <= qi + WINDOW)
    valid = jnp.logical_and(valid, jnp.logical_or(kj >= WINDOW, j > 0))
    dist = (WINDOW + qi - kj).astype(F32)
    for n in range(N_KV_A):
        cs = slice(n * HEAD_DIM, (n + 1) * HEAD_DIM)
        kb = jnp.concatenate([kp_ref[:, cs], kc_ref[:, cs]], axis=0).astype(BF16)
        vb = jnp.concatenate([vp_ref[:, cs], vc_ref[:, cs]], axis=0).astype(BF16)
        for g in range(GQA_GROUP):
            h = n * GQA_GROUP + g
            hs = slice(h * HEAD_DIM, (h + 1) * HEAD_DIM)
            s = lax.dot_general(q_ref[:, hs], kb, (((1,), (1,)), ((), ())), preferred_element_type=F32)
            s = s * (HEAD_DIM ** -0.5) - (2.0 ** -(h + 1)) * dist
            s = jnp.where(valid, s, NEG_BIG)
            sink = sinks_ref[h]
            m = jnp.maximum(jnp.max(s, axis=1, keepdims=True), sink)
            p = jnp.exp(s - m)
            l = jnp.sum(p, axis=1, keepdims=True) + jnp.exp(sink - m)
            o = _dot(p.astype(BF16), vb) / l
            o_ref[:, hs] = o.astype(BF16)


def _attn_prompt(q2, kv2, sinks, b, t):
    prev = lambda bi, j: jnp.maximum(j - 1, 0)
    return pl.pallas_call(
        _attn_prompt_kernel,
        grid=(b, t // WINDOW),
        in_specs=[pl.BlockSpec(memory_space=pltpu.SMEM),
                  pl.BlockSpec((WINDOW, Q_COLS), lambda bi, j: (j, bi)),
                  pl.BlockSpec((WINDOW, KV_COLS), lambda bi, j: (j, 2 * bi)),
                  pl.BlockSpec((WINDOW, KV_COLS), lambda bi, j: (prev(bi, j), 2 * bi)),
                  pl.BlockSpec((WINDOW, KV_COLS), lambda bi, j: (j, 2 * bi + 1)),
                  pl.BlockSpec((WINDOW, KV_COLS), lambda bi, j: (prev(bi, j), 2 * bi + 1))],
        out_specs=pl.BlockSpec((WINDOW, Q_COLS), lambda bi, j: (j, bi)),
        out_shape=jax.ShapeDtypeStruct((t, b * Q_COLS), BF16),
        compiler_params=_cparams("parallel", "parallel"),
        name="attn_prompt",
    )(sinks, q2, kv2, kv2, kv2, kv2)


def _attn_sample_kernel(sinks_ref, q_ref, kn_ref, vn_ref, ck_ref, cv_ref, o_ref):
    bs, t = q_ref.shape[0], q_ref.shape[1]
    assert t & (t - 1) == 0
    nq, nk = GQA_GROUP * t, 2 * WINDOW
    r = lax.broadcasted_iota(I32, (nq, nk), 0)
    kj = lax.broadcasted_iota(I32, (nq, nk), 1)
    tq = jnp.bitwise_and(r, t - 1)
    valid = jnp.logical_and(kj > tq, kj <= tq + WINDOW)
    dist = (WINDOW + tq - kj).astype(F32)
    grp = jnp.right_shift(lax.broadcasted_iota(I32, (nq, 1), 0), t.bit_length() - 1)
    pad = jnp.zeros((bs, nk - WINDOW - t, HEAD_DIM), F32)
    for n in range(N_KV_A):
        cs = slice(n * HEAD_DIM, (n + 1) * HEAD_DIM)
        kb = jnp.concatenate([ck_ref[:, :, cs], kn_ref[:, :, cs], pad], axis=1).astype(BF16)
        vb = jnp.concatenate([cv_ref[:, :, cs], vn_ref[:, :, cs], pad], axis=1).astype(BF16)
        qn = jnp.concatenate([q_ref[:, :, (n * GQA_GROUP + g) * HEAD_DIM:(n * GQA_GROUP + g + 1) * HEAD_DIM]
                              for g in range(GQA_GROUP)], axis=1).astype(BF16)
        slope = jnp.zeros((nq, 1), F32)
        sink = jnp.zeros((nq, 1), F32)
        for g in range(GQA_GROUP):
            h = n * GQA_GROUP + g
            slope = jnp.where(grp == g, 2.0 ** -(h + 1), slope)
            sink = jnp.where(grp == g, sinks_ref[h], sink)
        s = jnp.einsum('bqd,bkd->bqk', qn, kb, preferred_element_type=F32)
        s = s * (HEAD_DIM ** -0.5) - (slope * dist)[None]
        s = jnp.where(valid[None], s, NEG_BIG)
        m = jnp.maximum(jnp.max(s, axis=2, keepdims=True), sink[None])
        p = jnp.exp(s - m)
        l = jnp.sum(p, axis=2, keepdims=True) + jnp.exp(sink[None] - m)
        o = jnp.einsum('bqk,bkd->bqd', p.astype(BF16), vb, preferred_element_type=F32) / l
        for g in range(GQA_GROUP):
            h = n * GQA_GROUP + g
            o_ref[:, :, h * HEAD_DIM:(h + 1) * HEAD_DIM] = o[:, g * t:(g + 1) * t, :]


def _attn_sample(q, kn, vn, ck, cv, sinks, bs=16):
    db, t = q.shape[0], q.shape[1]
    seq3 = lambda w: pl.BlockSpec((bs, t, w), lambda i: (i, 0, 0))
    cache = pl.BlockSpec((bs, WINDOW, KV_COLS), lambda i: (i, 0, 0))
    return pl.pallas_call(
        _attn_sample_kernel,
        grid=(db // bs,),
        in_specs=[pl.BlockSpec(memory_space=pltpu.SMEM), seq3(Q_COLS), seq3(KV_COLS), seq3(KV_COLS), cache, cache],
        out_specs=seq3(Q_COLS),
        out_shape=jax.ShapeDtypeStruct((db, t, Q_COLS), F32),
        compiler_params=_cparams("parallel"),
        name="attn_sample",
    )(sinks, q, kn, vn, ck, cv)


def _rwkv_prep_kernel(pb_ref, halo_ref, init_ref, mu_ref, w0_ref, wup_ref, a0_ref, aup_ref, gup_ref,
                      kk_ref, ka_ref, rk_ref, ones_ref,
                      r_o, w_o, k_o, v_o, kk_o, nkka_o, bonus_o, g_o, *, b):
    i = pl.program_id(0)
    pb = pb_ref[...]
    tm = pb.shape[0]
    halo = jnp.where(i == 0, init_ref[...], halo_ref[...])
    prev = halo if tm == b else jnp.concatenate([halo, pb[:tm - b]], axis=0)
    xs = pb + (prev - pb) * mu_ref[...]
    r = xs[:, :D_B]
    k = xs[:, D_B:2 * D_B]
    v = xs[:, 2 * D_B:3 * D_B]
    o1 = 3 * D_B
    wd = xs[:, o1:o1 + D_LORA_W]
    ad = xs[:, o1 + D_LORA_W:o1 + D_LORA_W + D_LORA_A]
    gd = xs[:, o1 + D_LORA_W + D_LORA_A:]
    z = -(w0_ref[...] + _dot(jnp.tanh(wd).astype(BF16), wup_ref[...]))
    softplus = jnp.maximum(z, 0.0) + jnp.log(1.0 + jnp.exp(-jnp.abs(z)))
    decay = jnp.exp(-jnp.exp(-softplus - 0.5))
    a = _sigmoid(a0_ref[...] + _dot(ad.astype(BF16), aup_ref[...]))
    g_o[...] = _dot(_sigmoid(gd).astype(BF16), gup_ref[...])
    ones = ones_ref[...]
    kk = k * kk_ref[...]
    kk = kk * lax.rsqrt(jnp.maximum(_dot2(kk * kk, ones), 1e-24))
    k2 = k * (1.0 + (a - 1.0) * ka_ref[...])
    bonus_o[...] = _dot2(r * k2 * rk_ref[...], ones) * v

    half = HEAD_DIM // 2
    lane8 = lax.broadcasted_iota(I32, (SEQ_PER_GROUP, 128), 1)
    low8 = lane8 < HEAD_DIM
    first_copy = jnp.bitwise_and(lax.broadcasted_iota(I32, (half, 128), 1), SEQ_PER_GROUP) == 0
    pairs = ((r, decay, r_o, w_o), (k2, kk, k_o, kk_o), (-(kk * a), v, nkka_o, None))
    for u in range(tm // SEQ_PER_GROUP):
        rows = slice(u * SEQ_PER_GROUP, (u + 1) * SEQ_PER_GROUP)
        for x, y, x_o, y_o in pairs:
            xu, yu = x[rows], y[rows]
            pieces = []
            for h in range(N_HEADS_B):
                cs = slice((h // 2) * 128, (h // 2 + 1) * 128)
                if h % 2 == 0:
                    p = jnp.where(low8, xu[:, cs], pltpu.roll(yu[:, cs], HEAD_DIM, 1))
                else:
                    p = jnp.where(low8, pltpu.roll(xu[:, cs], HEAD_DIM, 1), yu[:, cs])
                pieces += [p, p]
            tr = jnp.concatenate(pieces, axis=0).T
            x_o[u] = tr[:HEAD_DIM]
            if y_o is not None:
                y_o[u] = tr[HEAD_DIM:]
            else:
                v_o[u] = jnp.where(first_copy, tr[HEAD_DIM:HEAD_DIM + half], tr[HEAD_DIM + half:])


def _rwkv_prep(pb, init, p, b, tm):
    n = pb.shape[0]
    units = n // SEQ_PER_GROUP
    tu = tm // SEQ_PER_GROUP
    half = HEAD_DIM // 2
    row = lambda w: pl.BlockSpec((1, w), lambda i: (0, 0))
    full = lambda r, c: pl.BlockSpec((r, c), lambda i: (0, 0))
    tile = pl.BlockSpec((tm, D_B), lambda i: (i, 0))
    kspec = pl.BlockSpec((tu, HEAD_DIM, 128), lambda i: (i, 0, 0))
    vspec = pl.BlockSpec((tu, half, 128), lambda i: (i, 0, 0))
    kshape = jax.ShapeDtypeStruct((units, HEAD_DIM, 128), F32)
    halo_blocks = tm // b
    return pl.pallas_call(
        functools.partial(_rwkv_prep_kernel, b=b),
        grid=(n // tm,),
        in_specs=[pl.BlockSpec((tm, D_B_IN), lambda i: (i, 0)),
                  pl.BlockSpec((b, D_B_IN), lambda i: (jnp.maximum(i * halo_blocks - 1, 0), 0)),
                  full(b, D_B_IN), row(D_B_IN), row(D_B), full(D_LORA_W, D_B), row(D_B), full(D_LORA_A, D_B),
                  full(D_LORA_G, D_B), row(D_B), row(D_B), row(D_B), full(D_B, D_B)],
        out_specs=[kspec, kspec, kspec, vspec, kspec, kspec, tile, tile],
        out_shape=[kshape, kshape, kshape, jax.ShapeDtypeStruct((units, half, 128), F32), kshape, kshape,
                   jax.ShapeDtypeStruct((n, D_B), F32), jax.ShapeDtypeStruct((n, D_B), F32)],
        compiler_params=_cparams("arbitrary"),
        name="rwkv_prep",
    )(pb, pb, init, p['mu'], p['w0'], p['w_up'], p['a0'], p['a_up'], p['g_up'], p['k_k'], p['k_a'], p['r_k'],
      p['head_ones'])


def _wkv_scan_kernel(r_ref, w_ref, k_ref, kk_ref, nkka_ref, v_ref, s0_ref, o_ref, st_ref, s_scr):
    j = pl.program_id(1)

    @pl.when(j == 0)
    def _():
        s_scr[...] = s0_ref[0]

    tc = r_ref.shape[0]
    nsub = (HEAD_DIM // 2) // 8

    def bcast(ref, s, kx):
        return jnp.broadcast_to(ref[s, pl.ds(kx, 1), :], (8, 128))

    acc0 = [[jnp.zeros((8, 128), F32) for _ in range(2)] for _ in range(nsub)]
    for kx in range(HEAD_DIM):
        kkr = bcast(kk_ref, 0, kx)
        for i in range(nsub):
            acc0[i][kx % 2] = acc0[i][kx % 2] + s_scr[kx, 8 * i:8 * i + 8, :] * kkr

    def step(s, sa):
        nxt = jnp.minimum(s + 1, tc - 1)
        vv = [v_ref[s, 8 * i:8 * i + 8, :] for i in range(nsub)]
        oacc = [[jnp.zeros((8, 128), F32) for _ in range(2)] for _ in range(nsub)]
        nacc = [[jnp.zeros((8, 128), F32) for _ in range(2)] for _ in range(nsub)]
        for kx in range(HEAD_DIM):
            rr, wr, kr = bcast(r_ref, s, kx), bcast(w_ref, s, kx), bcast(k_ref, s, kx)
            nk, kkn = bcast(nkka_ref, s, kx), bcast(kk_ref, nxt, kx)
            for i in range(nsub):
                rows = slice(8 * i, 8 * i + 8)
                sk = s_scr[kx, rows, :] * wr + sa[i] * nk + vv[i] * kr
                s_scr[kx, rows, :] = sk
                oacc[i][kx % 2] = oacc[i][kx % 2] + sk * rr
                nacc[i][kx % 2] = nacc[i][kx % 2] + sk * kkn
        o_ref[s] = jnp.concatenate([a[0] + a[1] for a in oacc], axis=0)
        return [a[0] + a[1] for a in nacc]

    lax.fori_loop(0, tc, step, [a[0] + a[1] for a in acc0])

    @pl.when(j == pl.num_programs(1) - 1)
    def _():
        st_ref[0] = s_scr[...]


def _wkv_scan(r, w, k, kk, nkka, v, s0, b, t, tc):
    g = b // SEQ_PER_GROUP
    half = HEAD_DIM // 2
    kview = lambda z: z.reshape(t, g, HEAD_DIM, 128)
    kspec = pl.BlockSpec((tc, None, HEAD_DIM, 128), lambda gi, j: (j, gi, 0, 0))
    vspec = pl.BlockSpec((tc, None, half, 128), lambda gi, j: (j, gi, 0, 0))
    sspec = pl.BlockSpec((1, HEAD_DIM, half, 128), lambda gi, j: (gi, 0, 0, 0))
    o, st = pl.pallas_call(
        _wkv_scan_kernel,
        grid=(g, t // tc),
        in_specs=[kspec] * 5 + [vspec, sspec],
        out_specs=[vspec, sspec],
        out_shape=[jax.ShapeDtypeStruct((t, g, half, 128), F32),
                   jax.ShapeDtypeStruct((g, HEAD_DIM, half, 128), F32)],
        scratch_shapes=[pltpu.VMEM((HEAD_DIM, half, 128), F32)],
        compiler_params=_cparams("parallel", "arbitrary"),
        name="wkv_scan",
    )(kview(r), kview(w), kview(k), kview(kk), kview(nkka), v.reshape(t, g, half, 128), s0)
    return o.reshape(t * g, half, 128), st


def _state_to_scan(s, b):
    g = b // SEQ_PER_GROUP
    s = s.reshape(g, SEQ_PER_GROUP, N_HEADS_B, 2, HEAD_DIM // 2, HEAD_DIM).transpose(0, 5, 4, 2, 3, 1)
    return s.reshape(g, HEAD_DIM, HEAD_DIM // 2, 128)


def _state_from_scan(s, b):
    g = b // SEQ_PER_GROUP
    s = s.reshape(g, HEAD_DIM, HEAD_DIM // 2, N_HEADS_B, 2, SEQ_PER_GROUP).transpose(0, 5, 3, 4, 2, 1)
    return s.reshape(b, N_HEADS_B, HEAD_DIM, HEAD_DIM)


def _route_tile(x, nrm_ref, wr_ref, br_ref, cnt_scr, xn_o, idx_o, wts_o):
    tm = x.shape[0]
    xn = _rms(x, nrm_ref[...])
    hi, lo = _split_bf16(xn)
    _store_row_tiles(xn_o, xn)
    wr = wr_ref[...]
    pa = _dot(hi, wr)
    pb = _dot(lo, wr)
    lg = pa[:, 0:32] + pa[:, 32:64] + pb[:, 0:32] + pb[:, 32:64] + br_ref[...]
    col = lambda c: lg[:, c:c + 1]
    c = [col(gx) for gx in range(N_EGROUPS)]
    m = jnp.maximum(jnp.maximum(c[0], c[1]), jnp.maximum(c[2], c[3]))
    den = jnp.exp(c[0] - m) + jnp.exp(c[1] - m) + jnp.exp(c[2] - m) + jnp.exp(c[3] - m)
    pg = 1.0 / den
    gi = jnp.where(c[0] >= m, 0, jnp.where(c[1] >= m, 1, jnp.where(c[2] >= m, 2, 3))).astype(I32)
    sel = []
    for e in range(EXP_PER_GROUP):
        sel.append(jnp.where(gi == 0, col(4 + e), jnp.where(gi == 1, col(8 + e),
                                                            jnp.where(gi == 2, col(12 + e), col(16 + e)))))
    v1 = jnp.maximum(jnp.maximum(sel[0], sel[1]), jnp.maximum(sel[2], sel[3]))
    i1 = jnp.where(sel[0] >= v1, 0, jnp.where(sel[1] >= v1, 1, jnp.where(sel[2] >= v1, 2, 3))).astype(I32)
    rest = [jnp.where(i1 == e, -jnp.inf, sel[e]) for e in range(EXP_PER_GROUP)]
    v2 = jnp.maximum(jnp.maximum(rest[0], rest[1]), jnp.maximum(rest[2], rest[3]))
    i2 = jnp.where(rest[0] >= v2, 0, jnp.where(rest[1] >= v2, 1, jnp.where(rest[2] >= v2, 2, 3))).astype(I32)
    tt = jnp.exp(v2 - v1)
    w1 = pg / (1.0 + tt)
    w2 = pg * tt / (1.0 + tt)
    e1 = gi * EXP_PER_GROUP + i1
    e2 = gi * EXP_PER_GROUP + i2
    lane = lax.broadcasted_iota(I32, (tm, N_EXPERTS), 1)
    oh1 = lane == e1
    oh2 = lane == e2
    oh = jnp.where(jnp.logical_or(oh1, oh2), 1.0, 0.0)
    ri = lax.broadcasted_iota(I32, (tm, tm), 0)
    ci = lax.broadcasted_iota(I32, (tm, tm), 1)
    ltri = jnp.where(ri > ci, 1.0, 0.0).astype(BF16)
    cnt = cnt_scr[0:1, 0:N_EXPERTS]
    pre = _dot(ltri, oh.astype(BF16)) + cnt
    rank1 = jnp.sum(jnp.where(oh1, pre, 0.0), axis=1, keepdims=True)
    rank2 = jnp.sum(jnp.where(oh2, pre, 0.0), axis=1, keepdims=True)
    cnt_scr[0:1, 0:N_EXPERTS] = cnt + jnp.sum(oh, axis=0, keepdims=True)
    lw = lax.broadcasted_iota(I32, (tm, 128), 1)
    wts_o[...] = jnp.where(lw == 0, w1, jnp.where(lw == 1, w2, 0.0))
    cols = jnp.where(lw == 0, e1.astype(F32), jnp.where(lw == 1, e2.astype(F32),
                                                         jnp.where(lw == 2, rank1, jnp.where(lw == 3, rank2, 0.0))))
    idx_o[...] = cols.T[0:8, :].astype(I32)


def _route_out_specs(tm):
    return [pl.BlockSpec((tm * ROW_CHUNKS, 128), lambda i: (i, 0)),
            pl.BlockSpec((8, tm), lambda i: (0, i)),
            pl.BlockSpec((tm, 128), lambda i: (i, 0)),
            pl.BlockSpec((8, 128), lambda i: (0, 0))]


def _route_out_shapes(n):
    return [jax.ShapeDtypeStruct((n * ROW_CHUNKS, 128), F32),
            jax.ShapeDtypeStruct((8, n), I32),
            jax.ShapeDtypeStruct((n, 128), F32),
            jax.ShapeDtypeStruct((8, 128), F32)]


def _route_in_specs():
    return [pl.BlockSpec((1, D_MODEL), lambda i: (0, 0)),
            pl.BlockSpec((D_MODEL, 128), lambda i: (0, 0)),
            pl.BlockSpec((1, 32), lambda i: (0, 0))]


def _mix0_out_kernel(o_ref, bonus_ref, g_ref, attn_ref, x_ref, lnw_ref, lnb_ref, avg_ref, wo_ref,
                     nrm_ref, wr_ref, br_ref,
                     x1_o, xn_o, idx_o, wts_o, cnt_o, cnt_scr, xs_scr, at_scr, *, b):
    @pl.when(pl.program_id(0) == 0)
    def _():
        cnt_scr[...] = jnp.zeros_like(cnt_scr)

    avg = avg_ref[...]
    half = HEAD_DIM // 2
    lane8 = lax.broadcasted_iota(I32, (SEQ_PER_GROUP, 128), 1)
    unit_rows = []
    for u in range(o_ref.shape[0]):
        ot = jnp.concatenate([o_ref[u], jnp.zeros((128 - half, 128), F32)], axis=0).T
        cols = []
        for jj in range(N_HEADS_B // 2):
            q = [ot[(4 * jj + i) * SEQ_PER_GROUP:(4 * jj + i + 1) * SEQ_PER_GROUP] for i in range(4)]
            c = jnp.where(lane8 < half, q[0], pltpu.roll(q[1], half, 1))
            c = jnp.where(lane8 < 2 * half, c, pltpu.roll(q[2], 2 * half, 1))
            c = jnp.where(lane8 < 3 * half, c, pltpu.roll(q[3], 3 * half, 1))
            cols.append(c)
        unit_rows.append(jnp.concatenate(cols, axis=1))
    o = jnp.concatenate(unit_rows, axis=0)
    d = o - _dot2(o, avg)
    var = _dot2(d * d, avg)
    on = d * lax.rsqrt(var + RWKV_GN_EPS) * lnw_ref[...] + lnb_ref[...]
    rout = ((on + bonus_ref[...]) * g_ref[...]).astype(BF16)
    x = _load_time_major(x_ref, xs_scr, b)
    if len(x_ref.shape) == 3:
        attn = _interleave_rows([attn_ref[:, s * Q_COLS:(s + 1) * Q_COLS] for s in range(b)], at_scr)
    else:
        attn = attn_ref[...]
    y = _dot(attn.astype(BF16), wo_ref[:Q_COLS, :]) + _dot(rout, wo_ref[Q_COLS:, :]) + x
    x1_o[...] = y
    _route_tile(y, nrm_ref, wr_ref, br_ref, cnt_scr, xn_o, idx_o, wts_o)
    cnt_o[...] = cnt_scr[...]


def _mix0_out(o, bonus, g, attn, x, p, rp, b, t, tm):
    n = b * t
    xshape, xspec = _time_major_spec(b, t, D_MODEL, tm)
    aspec = (pl.BlockSpec((tm // b, b * Q_COLS), lambda i: (i, 0)) if len(xshape) == 3
             else pl.BlockSpec((tm, Q_COLS), lambda i: (i, 0)))
    tile = lambda w: pl.BlockSpec((tm, w), lambda i: (i, 0))
    row = lambda w: pl.BlockSpec((1, w), lambda i: (0, 0))
    full = lambda r, c: pl.BlockSpec((r, c), lambda i: (0, 0))
    return pl.pallas_call(
        functools.partial(_mix0_out_kernel, b=b),
        grid=(n // tm,),
        in_specs=[pl.BlockSpec((tm // SEQ_PER_GROUP, HEAD_DIM // 2, 128), lambda i: (i, 0, 0)),
                  tile(D_B), tile(D_B), aspec, xspec, row(D_B), row(D_B),
                  full(D_B, D_B), full(D_MODEL, D_MODEL)] + _route_in_specs(),
        out_specs=[tile(D_MODEL)] + _route_out_specs(tm),
        out_shape=[jax.ShapeDtypeStruct((n, D_MODEL), F32)] + _route_out_shapes(n),
        scratch_shapes=[pltpu.VMEM((8, 128), F32), pltpu.VMEM((D_MODEL // 128, tm, 128), F32),
                        pltpu.VMEM((Q_COLS // 128, tm, 128), F32)],
        compiler_params=_cparams("arbitrary"),
        name="mix0_out",
    )(o, bonus, g, attn, x.reshape(xshape), p['ln_w'], p['ln_b'], p['head_avg'], p['w_out'], rp['norm'], rp['w'], rp['b'])


def _row_dma_start(idx_ref, pos, r, src_hbm, dst, sem, priority):
    src_row = pl.multiple_of(idx_ref[pos] * ROW_CHUNKS, ROW_CHUNKS)
    dst_row = pl.multiple_of(r * ROW_CHUNKS, ROW_CHUNKS)
    pltpu.make_async_copy(src_hbm.at[pl.ds(src_row, ROW_CHUNKS)], dst.at[pl.ds(dst_row, ROW_CHUNKS)],
                          sem).start(priority=priority)


def _row_gather_start(idx_ref, base, n_rows, src_hbm, dst, sem):
    def body(r2, carry):
        for p in range(2):
            _row_dma_start(idx_ref, base + 2 * r2 + p, 2 * r2 + p, src_hbm, dst, sem, p)
        return carry

    lax.fori_loop(0, n_rows // 2, body, 0, unroll=4)


def _row_gather_wait(dst, sem):
    pltpu.make_async_copy(dst, dst, sem).wait()


def _expert_kernel(te_ref, nu_ref, pad_ref, p1_ref, p2_ref, x_hbm, wg_ref, wu_ref, wd_ref, o_ref,
                   src_ref, xbuf, sem, wgb, wub, wdb):
    i = pl.program_id(0)
    nu = nu_ref[0]
    mt = xbuf.shape[1] // ROW_CHUNKS

    @pl.when(i == 0)
    def _():
        def fill_pad(e, carry):
            start = pad_ref[2 * e]

            def body(r, c):
                src_ref[r] = jnp.minimum(r - start, p1_ref.shape[0] - 1)
                return c

            lax.fori_loop(start, pad_ref[2 * e + 1], body, 0)
            return carry

        lax.fori_loop(0, N_EXPERTS + 1, fill_pad, 0)

        def invert(n, carry):
            src_ref[p1_ref[n]] = n
            src_ref[p2_ref[n]] = n
            return carry

        lax.fori_loop(0, p1_ref.shape[0], invert, 0, unroll=8)
        _row_gather_start(src_ref, 0, mt, x_hbm, xbuf.at[0], sem.at[0])

    new_expert = jnp.logical_or(i == 0, te_ref[i] != te_ref[jnp.maximum(i - 1, 0)])

    @pl.when(jnp.logical_and(i < nu, new_expert))
    def _():
        wgb[...] = wg_ref[0].astype(BF16)
        wub[...] = wu_ref[0].astype(BF16)
        wdb[...] = wd_ref[0].astype(BF16)

    @pl.when(i < nu)
    def _():
        cur = i % 2
        nxt = 1 - cur
        base = (i + 1) * mt
        pieces = 2 * (D_FF_E // 256) * (D_MODEL // 256) + (D_MODEL // 256) * (D_FF_E // 256)
        per = -(-mt // pieces)
        issued = [0]

        def start_some():
            for r in range(issued[0], min(issued[0] + per, mt)):
                _row_dma_start(src_ref, base + r, r, x_hbm, xbuf.at[nxt], sem.at[nxt], r % 2)
            issued[0] = min(issued[0] + per, mt)

        def block_dot(a, w_ref, c):
            acc = None
            for kt in range(a.shape[1] // 256):
                part = _dot(a[:, kt * 256:(kt + 1) * 256], w_ref[kt * 256:(kt + 1) * 256, c * 256:(c + 1) * 256])
                acc = part if acc is None else acc + part
                start_some()
            return acc

        _row_gather_wait(xbuf.at[cur], sem.at[cur])
        x = _load_row_tiles(xbuf.at[cur]).astype(BF16)
        hg = jnp.concatenate([block_dot(x, wgb, c) for c in range(D_FF_E // 256)], axis=1)
        hu = jnp.concatenate([block_dot(x, wub, c) for c in range(D_FF_E // 256)], axis=1)
        h = ((hg * _sigmoid(hg)) * hu).astype(BF16)
        for c in range(D_MODEL // 256):
            res = block_dot(h, wdb, c)
            for cc in range(2):
                o_ref[pl.ds(2 * c + cc, mt, stride=ROW_CHUNKS), :] = res[:, cc * 128:(cc + 1) * 128]
        assert issued[0] == mt

    @pl.when(i == nu)
    def _():
        _row_gather_wait(xbuf.at[i % 2], sem.at[i % 2])

    @pl.when(i >= nu)
    def _():
        o_ref[...] = jnp.zeros_like(o_ref)


def _experts(xn, r, mt, pos1, pos2, tile_expert, n_used, pad_ranges, wg, wu, wd):
    wspec = lambda a, b: pl.BlockSpec((1, a, b), lambda i, te, nu, pad, p1, p2: (te[i], 0, 0))
    grid_spec = pltpu.PrefetchScalarGridSpec(
        num_scalar_prefetch=5,
        grid=(r // mt,),
        in_specs=[pl.BlockSpec(memory_space=pl.ANY),
                  wspec(D_MODEL, D_FF_E), wspec(D_MODEL, D_FF_E), wspec(D_FF_E, D_MODEL)],
        out_specs=pl.BlockSpec((mt * ROW_CHUNKS, 128), lambda i, te, nu, pad, p1, p2: (i, 0)),
        scratch_shapes=[pltpu.SMEM((r,), I32), pltpu.VMEM((2, mt * ROW_CHUNKS, 128), F32),
                        pltpu.SemaphoreType.DMA((2,)),
                        pltpu.VMEM((D_MODEL, D_FF_E), BF16), pltpu.VMEM((D_MODEL, D_FF_E), BF16),
                        pltpu.VMEM((D_FF_E, D_MODEL), BF16)],
    )
    return pl.pallas_call(
        _expert_kernel,
        grid_spec=grid_spec,
        out_shape=jax.ShapeDtypeStruct((r * ROW_CHUNKS, 128), F32),
        compiler_params=_cparams("arbitrary"),
        name="moe_experts",
    )(tile_expert, n_used, pad_ranges, pos1, pos2, xn, wg, wu, wd)


def _moe(xn, idx, cnt, ep):
    n = xn.shape[0] // ROW_CHUNKS
    mt = 512 if n >= 8192 else 128
    rows = 2 * n + N_EXPERTS * mt
    counts = cnt[0, :N_EXPERTS].astype(I32)
    padded = ((counts + mt - 1) // mt) * mt
    ends = jnp.cumsum(padded)
    offs = ends - padded
    pos1 = offs[idx[0]] + idx[2]
    pos2 = offs[idx[1]] + idx[3]
    pad_ranges = jnp.stack([jnp.append(offs + counts, ends[-1]), jnp.append(ends, ends[-1] + mt)], axis=1)
    pad_ranges = pad_ranges.reshape(-1).astype(I32)
    n_used = (ends[-1] // mt).astype(I32)
    starts = jnp.arange(rows // mt, dtype=I32) * mt
    starts = jnp.minimum(starts, ends[-1] - 1)
    tile_expert = jnp.sum((starts[:, None] >= ends[None, :]).astype(I32), axis=1)
    tile_expert = jnp.minimum(tile_expert, N_EXPERTS - 1).astype(I32)
    out = _experts(xn, rows, mt, pos1, pos2, tile_expert, n_used.reshape(1), pad_ranges, ep['wg'], ep['wu'], ep['wd'])
    return out, pos1, pos2


def _combine_kernel(p1_ref, p2_ref, x_ref, wts_ref, nrm_ref, out_hbm, y_o, gbuf, sem, ys_scr, *, final, b):
    i = pl.program_id(0)
    tm = x_ref.shape[0]

    def start(tile, slot):
        _row_gather_start(p1_ref, tile * tm, tm, out_hbm, gbuf.at[slot, 0], sem.at[slot])
        _row_gather_start(p2_ref, tile * tm, tm, out_hbm, gbuf.at[slot, 1], sem.at[slot])

    @pl.when(i == 0)
    def _():
        start(0, 0)

    @pl.when(i + 1 < pl.num_programs(0))
    def _():
        start(i + 1, (i + 1) % 2)

    cur = i % 2
    _row_gather_wait(gbuf.at[cur], sem.at[cur])
    wts = wts_ref[...]
    y = x_ref[...] + wts[:, 0:1] * _load_row_tiles(gbuf.at[cur, 0]) + wts[:, 1:2] * _load_row_tiles(gbuf.at[cur, 1])
    if final:
        _store_batch_major(y_o, ys_scr, _rms(y, nrm_ref[...]), b)
    else:
        y_o[...] = y


def _moe_combine(x, wts, out, pos1, pos2, nrm, final, b, t, tm=256):
    n = x.shape[0]
    tile = lambda w: pl.BlockSpec((tm, w), lambda i, p1, p2: (i, 0))
    if final:
        yshape, yspec = _time_major_spec(b, t, D_MODEL, tm)
    else:
        yshape, yspec = (n, D_MODEL), tile(D_MODEL)
    grid_spec = pltpu.PrefetchScalarGridSpec(
        num_scalar_prefetch=2,
        grid=(n // tm,),
        in_specs=[tile(D_MODEL), tile(128), pl.BlockSpec((1, D_MODEL), lambda i, p1, p2: (0, 0)),
                  pl.BlockSpec(memory_space=pl.ANY)],
        out_specs=yspec,
        scratch_shapes=[pltpu.VMEM((2, 2, tm * ROW_CHUNKS, 128), F32), pltpu.SemaphoreType.DMA((2,)),
                        pltpu.VMEM((D_MODEL // 128, tm, 128), F32)],
    )
    return pl.pallas_call(
        functools.partial(_combine_kernel, final=final, b=b),
        grid_spec=grid_spec,
        out_shape=jax.ShapeDtypeStruct(yshape, F32),
        compiler_params=_cparams("arbitrary"),
        name="moe_combine_final" if final else "moe_combine",
    )(pos1, pos2, x, wts, nrm, out)


def _gelu_tanh(x):
    return 0.5 * x * (1.0 + jnp.tanh(0.7978845608028654 * (x + 0.044715 * (x * x * x))))


def _mix1_kernel(x_ref, nmix_ref, bre_ref, bim_ref, are_ref, aim_ref, cre_ref, cim_ref,
                 dsk_ref, wo_ref, h0r_ref, h0i_ref, nrm_ref, wr_ref, br_ref,
                 x2_o, xn_o, idx_o, wts_o, cnt_o, hr_o, hi_o,
                 bur, bui, hr_scr, hi_scr, cnt_scr, *, b, cw):
    @pl.when(pl.program_id(0) == 0)
    def _():
        cnt_scr[...] = jnp.zeros_like(cnt_scr)
        hr_scr[...] = h0r_ref[...]
        hi_scr[...] = h0i_ref[...]

    x = x_ref[...]
    u = _rms(x, nmix_ref[...])
    ub = u.astype(BF16)
    nblk = bre_ref.shape[0]
    kin = D_MODEL // nblk
    kst = S5_STATE // nblk
    for cb in range(nblk):
        ucb = ub[:, cb * kin:(cb + 1) * kin]
        bur[:, cb * kst:(cb + 1) * kst] = _dot(ucb, bre_ref[cb])
        bui[:, cb * kst:(cb + 1) * kst] = _dot(ucb, bim_ref[cb])

    tc = x.shape[0] // b
    for c0 in range(0, S5_STATE, cw):
        cs = slice(c0, c0 + cw)
        ar = jnp.broadcast_to(are_ref[:, cs], (b, cw))
        ai = jnp.broadcast_to(aim_ref[:, cs], (b, cw))

        def step(s, carry, cs=cs, ar=ar, ai=ai):
            hr, hi = carry
            rows = pl.ds(pl.multiple_of(s * b, b), b)
            nr = ar * hr - ai * hi + bur[rows, cs]
            ni = ar * hi + ai * hr + bui[rows, cs]
            bur[rows, cs] = nr
            bui[rows, cs] = ni
            return nr, ni

        hr, hi = lax.fori_loop(0, tc, step, (hr_scr[:, cs], hi_scr[:, cs]), unroll=True)
        hr_scr[:, cs] = hr
        hi_scr[:, cs] = hi

    ych = []
    for cb in range(nblk):
        ss = slice(cb * kst, (cb + 1) * kst)
        ych.append(_dot(bur[:, ss].astype(BF16), cre_ref[cb]) - _dot(bui[:, ss].astype(BF16), cim_ref[cb]))
    y = jnp.concatenate(ych, axis=1) + dsk_ref[...] * u
    z = _dot(_gelu_tanh(y).astype(BF16), wo_ref[...])
    x2 = x + z[:, :D_MODEL] * _sigmoid(z[:, D_MODEL:])
    x2_o[...] = x2
    _route_tile(x2, nrm_ref, wr_ref, br_ref, cnt_scr, xn_o, idx_o, wts_o)
    cnt_o[...] = cnt_scr[...]
    hr_o[...] = hr_scr[...]
    hi_o[...] = hi_scr[...]


def _mix1(x, sp, rp, h0r, h0i, b, tr):
    n = x.shape[0]
    cw = 1024 if b == 8 else 128
    tile = lambda w: pl.BlockSpec((tr, w), lambda i: (i, 0))
    row = lambda w: pl.BlockSpec((1, w), lambda i: (0, 0))
    full = lambda *s: pl.BlockSpec(s, lambda i: (0,) * len(s))
    nblk = sp['b_re'].shape[0]
    return pl.pallas_call(
        functools.partial(_mix1_kernel, b=b, cw=cw),
        grid=(n // tr,),
        in_specs=[tile(D_MODEL), row(D_MODEL),
                  full(nblk, D_MODEL // nblk, S5_STATE // nblk), full(nblk, D_MODEL // nblk, S5_STATE // nblk),
                  row(S5_STATE), row(S5_STATE),
                  full(nblk, S5_STATE // nblk, D_MODEL // nblk), full(nblk, S5_STATE // nblk, D_MODEL // nblk),
                  row(D_MODEL), full(D_MODEL, 2 * D_MODEL), full(b, S5_STATE), full(b, S5_STATE)] + _route_in_specs(),
        out_specs=[tile(D_MODEL)] + _route_out_specs(tr) + [full(b, S5_STATE), full(b, S5_STATE)],
        out_shape=[jax.ShapeDtypeStruct((n, D_MODEL), F32)] + _route_out_shapes(n)
                  + [jax.ShapeDtypeStruct((b, S5_STATE), F32)] * 2,
        scratch_shapes=[pltpu.VMEM((tr, S5_STATE), F32), pltpu.VMEM((tr, S5_STATE), F32),
                        pltpu.VMEM((b, S5_STATE), F32), pltpu.VMEM((b, S5_STATE), F32),
                        pltpu.VMEM((8, 128), F32)],
        compiler_params=_cparams("arbitrary"),
        name="mix1",
    )(x, sp['norm'], sp['b_re'], sp['b_im'], sp['a_re'], sp['a_im'], sp['c_re'], sp['c_im'],
      sp['d'], sp['w_out'], h0r, h0i, rp['norm'], rp['w'], rp['b'])


def _router_params(norm, w_rc, b_rc, w_rf, b_rf):
    w = jnp.concatenate([w_rc, w_rf.reshape(D_MODEL, N_EXPERTS), jnp.zeros((D_MODEL, 12), F32)], axis=1)
    hi = w.astype(BF16)
    lo = (w - hi.astype(F32)).astype(BF16)
    wcat = jnp.concatenate([hi, lo, jnp.zeros((D_MODEL, 64), BF16)], axis=1)
    bias = jnp.concatenate([b_rc, b_rf.reshape(-1), jnp.zeros((12,), F32)]).reshape(1, 32)
    return {'norm': norm.reshape(1, D_MODEL), 'w': wcat, 'b': bias}


def _expert_params(wg, wu, wd):
    return {'wg': wg, 'wu': wu, 'wd': wd}


def _s5_params(norm, a_re, a_im, log_dt, b_re, b_im, c_re, c_im, d_skip, w_out, nblk=8):
    dt = jnp.exp(log_dt)
    mag = jnp.exp(dt * a_re)
    ab_re, ab_im = mag * jnp.cos(dt * a_im), mag * jnp.sin(dt * a_im)
    den = a_re * a_re + a_im * a_im
    f_re = ((ab_re - 1.0) * a_re + ab_im * a_im) / den
    f_im = (ab_im * a_re - (ab_re - 1.0) * a_im) / den
    bb_re = f_re[..., None] * b_re - f_im[..., None] * b_im
    bb_im = f_re[..., None] * b_im + f_im[..., None] * b_re
    gpb = S5_GROUPS // nblk
    eye = jnp.eye(gpb, dtype=F32)

    def in_blocks(bb):
        bb = bb.reshape(nblk, gpb, S5_P, S5_CH)
        w = jnp.einsum('ngpc,gh->ngchp', bb, eye)
        return w.reshape(nblk, gpb * S5_CH, gpb * S5_P).astype(BF16)

    def out_blocks(cc):
        cc = cc.reshape(nblk, gpb, S5_CH, S5_P)
        w = jnp.einsum('ngcp,gh->ngphc', cc, eye)
        return w.reshape(nblk, gpb * S5_P, gpb * S5_CH).astype(BF16)

    return {'norm': norm.reshape(1, D_MODEL), 'b_re': in_blocks(bb_re), 'b_im': in_blocks(bb_im),
            'a_re': ab_re.reshape(1, S5_STATE), 'a_im': ab_im.reshape(1, S5_STATE),
            'c_re': out_blocks(c_re), 'c_im': out_blocks(c_im), 'd': d_skip.reshape(1, D_MODEL),
            'w_out': w_out.astype(BF16)}


def _head_block(value):
    hid = jnp.arange(D_B, dtype=I32) // HEAD_DIM
    return jnp.where(hid[:, None] == hid[None, :], value, 0.0).astype(BF16)


def _run_group(x, cache_k, cache_v, shift0, wkv0, h0r, h0i, pr):
    b, t = x.shape[0], x.shape[1]
    n = b * t
    prompt = cache_k is None
    tm = 256 if prompt else 128
    if not prompt:
        x = x.transpose(1, 0, 2)
    q, kv, pb = _in_proj(x, pr['l0_norm'], pr['l0_w_in'], b, t, 2 * tm if prompt else tm)

    if prompt:
        attn = _attn_prompt(q, kv, pr['sinks'], b, t)
        kv3 = kv[t - WINDOW:].reshape(WINDOW, b, 2, N_KV_A, HEAD_DIM)
        new_k = kv3[:, :, 0].transpose(1, 0, 2, 3)
        new_v = kv3[:, :, 1].transpose(1, 0, 2, 3)
        init = jnp.zeros((b, D_B_IN), F32)
    else:
        qs = q.reshape(t, b, Q_COLS).transpose(1, 0, 2)
        kvs = kv.reshape(t, b, 2 * KV_COLS).transpose(1, 0, 2)
        kn, vn = kvs[..., :KV_COLS], kvs[..., KV_COLS:]
        ck = cache_k.reshape(b, WINDOW, KV_COLS)
        cv = cache_v.reshape(b, WINDOW, KV_COLS)
        attn = _attn_sample(qs, kn, vn, ck, cv, pr['sinks'])
        attn = attn.transpose(1, 0, 2).reshape(n, Q_COLS)
        new_k = jnp.concatenate([ck[:, t:], kn], axis=1).reshape(b, WINDOW, N_KV_A, HEAD_DIM)
        new_v = jnp.concatenate([cv[:, t:], vn], axis=1).reshape(b, WINDOW, N_KV_A, HEAD_DIM)
        init = shift0
    new_shift = pb[n - b:]

    r, w, k, v, kk, nkka, bonus, g = _rwkv_prep(pb, init, pr['rw'], b, tm)
    tc = 64 if prompt else t
    s0 = jnp.zeros((b // SEQ_PER_GROUP, HEAD_DIM, HEAD_DIM // 2, 128), F32) if prompt else _state_to_scan(wkv0, b)
    o, s_fin = _wkv_scan(r, w, k, kk, nkka, v, s0, b, t, tc)
    new_wkv = _state_from_scan(s_fin, b)

    x1, xn, idx, wts, cnt = _mix0_out(o, bonus, g, attn, x, pr['rw'], pr['l0_route'], b, t, 256)
    out, pos1, pos2 = _moe(xn, idx, cnt, pr['l0_exp'])
    ct = 512 if prompt else 256
    x1 = _moe_combine(x1, wts, out, pos1, pos2, pr['final_norm'], False, b, t, ct)

    x2, xn, idx, wts, cnt, hr, hi = _mix1(x1, pr['s5'], pr['l1_route'], h0r, h0i, b, 256)
    out, pos1, pos2 = _moe(xn, idx, cnt, pr['l1_exp'])
    y = _moe_combine(x2, wts, out, pos1, pos2, pr['final_norm'], True, b, t, ct)
    y = y if prompt else y.reshape(t, b, D_MODEL).transpose(1, 0, 2)
    return (y, new_k, new_v, new_shift, new_wkv,
            hr.reshape(b, S5_GROUPS, S5_P), hi.reshape(b, S5_GROUPS, S5_P))


def kernel(x_prompt, x_sample, cache_win_k, cache_win_v, state_shift, state_wkv, state_s5_re, state_s5_im,
           l0_norm_mix, l0_w_in, l0_sinks, l0_mu, l0_w0, l0_w_lora_up, l0_a0, l0_a_lora_up, l0_g_lora_up,
           l0_k_k, l0_k_a, l0_r_k, l0_ln_w, l0_ln_b, l0_w_out,
           l0_norm_ffn, l0_router_coarse, l0_bias_coarse, l0_router_fine, l0_bias_fine,
           l0_exp_gate, l0_exp_up, l0_exp_down,
           l1_norm_mix, l1_s5_a_re, l1_s5_a_im, l1_s5_log_dt, l1_s5_b_re, l1_s5_b_im, l1_s5_c_re, l1_s5_c_im,
           l1_s5_d, l1_w_out,
           l1_norm_ffn, l1_router_coarse, l1_bias_coarse, l1_router_fine, l1_bias_fine,
           l1_exp_gate, l1_exp_up, l1_exp_down,
           final_norm):
    row = lambda z: z.reshape(1, -1)
    pr = {
        'l0_norm': row(l0_norm_mix), 'l0_w_in': l0_w_in.astype(BF16), 'sinks': l0_sinks,
        'rw': {'mu': row(l0_mu), 'w0': row(l0_w0), 'w_up': l0_w_lora_up.astype(BF16), 'a0': row(l0_a0),
               'a_up': l0_a_lora_up.astype(BF16), 'g_up': l0_g_lora_up.astype(BF16), 'k_k': row(l0_k_k),
               'k_a': row(l0_k_a), 'r_k': row(l0_r_k), 'ln_w': row(l0_ln_w), 'ln_b': row(l0_ln_b),
               'head_ones': _head_block(1.0), 'head_avg': _head_block(1.0 / HEAD_DIM),
               'w_out': l0_w_out.astype(BF16)},
        'l0_route': _router_params(l0_norm_ffn, l0_router_coarse, l0_bias_coarse, l0_router_fine, l0_bias_fine),
        'l0_exp': _expert_params(l0_exp_gate, l0_exp_up, l0_exp_down),
        's5': _s5_params(l1_norm_mix, l1_s5_a_re, l1_s5_a_im, l1_s5_log_dt, l1_s5_b_re, l1_s5_b_im,
                         l1_s5_c_re, l1_s5_c_im, l1_s5_d, l1_w_out),
        'l1_route': _router_params(l1_norm_ffn, l1_router_coarse, l1_bias_coarse, l1_router_fine, l1_bias_fine),
        'l1_exp': _expert_params(l1_exp_gate, l1_exp_up, l1_exp_down),
        'final_norm': row(final_norm),
    }
    bp, bs = x_prompt.shape[0], x_sample.shape[0]
    zero_state = jnp.zeros((bp, S5_STATE), F32)
    yp, pk, pv, psh, pwkv, pre, pim = _run_group(x_prompt, None, None, None, None, zero_state, zero_state, pr)
    ys, sk, sv, ssh, swkv, sre, sim = _run_group(
        x_sample, cache_win_k, cache_win_v, state_shift, state_wkv,
        state_s5_re.reshape(bs, S5_STATE), state_s5_im.reshape(bs, S5_STATE), pr)
    return (yp, ys, pk, pv, psh, pwkv, pre, pim, sk, sv, ssh, swkv, sre, sim)
```
